```python
import math
import jax, jax.numpy as jnp
from jax import lax
import numpy as np

D_MODEL = 2048
BATCH = 2
SEQ = 4096
DEPTH = 2

GRID_W = 64
N_MIXERS = 2
N_SSD_LAYERS = (DEPTH + 1) // 2
N_NA_LAYERS = DEPTH // 2
EPS = 1e-6

SSD_EXPAND = 2
D_INNER = SSD_EXPAND * D_MODEL
SSD_HEAD_DIM = 64
SSD_HEADS = D_INNER // SSD_HEAD_DIM
SSD_GROUPS = 8
SSD_HEADS_PER_GROUP = SSD_HEADS // SSD_GROUPS
SSD_STATE = 128
SSD_CONV = 5
SSD_CHUNK = 128
CONV_DIM = D_INNER + 2 * SSD_GROUPS * SSD_STATE
SSD_IN_DIM = D_INNER + CONV_DIM + 2 * SSD_HEADS

NA_HEAD_DIM = 64
NA_HEADS = D_MODEL // NA_HEAD_DIM
WIN_H = 8
WIN_W = 16

MOE_GROUPS = 4
MOE_EXPERTS_PER_GROUP = 8
N_EXPERTS = MOE_GROUPS * MOE_EXPERTS_PER_GROUP
MOE_TOP_K = 2
MOE_D_FF = D_MODEL // 4
MOE_BLOCK = 128

kernel_name = 'bidir_hybrid_ssd_natten_hmoe'


def rmsnorm(x, w):
    xf = x.astype(jnp.float32)
    y = xf * lax.rsqrt(jnp.mean(xf * xf, axis=-1, keepdims=True) + EPS)
    return (y * w).astype(x.dtype)


def segsum(a):
    q = a.shape[-1]
    cs = jnp.cumsum(a, axis=-1)
    diff = cs[..., :, None] - cs[..., None, :]
    mask = jnp.tril(jnp.ones((q, q), dtype=bool))
    return jnp.where(mask, diff, -jnp.inf)


def ssd_chunked(xdt, a_dt, bm, cm):
    b, L, G, R, P = xdt.shape
    N = bm.shape[-1]
    Q = SSD_CHUNK
    nc = L // Q
    x_c = xdt.reshape(b, nc, Q, G, R, P)
    a_c = jnp.moveaxis(a_dt.reshape(b, nc, Q, G, R), 2, -1)
    b_c = bm.reshape(b, nc, Q, G, N)
    c_c = cm.reshape(b, nc, Q, G, N)
    a_cum = jnp.cumsum(a_c, axis=-1)
    l_mat = jnp.exp(segsum(a_c))
    cb = jnp.einsum('bclgn,bcsgn->bcgls', c_c, b_c)
    y_diag = jnp.einsum('bcgrls,bcsgrp->bclgrp', cb[:, :, :, None] * l_mat, x_c)
    decay_states = jnp.exp(a_cum[..., -1:] - a_cum)
    states = jnp.einsum('bclgn,bclgrp->bcgrpn', b_c,
                        x_c * jnp.moveaxis(decay_states, -1, 2)[..., None])
    a_last = jnp.moveaxis(a_cum[..., -1], 1, -1)
    decay_chunk = jnp.exp(segsum(jnp.pad(a_last, ((0, 0), (0, 0), (0, 0), (1, 0)))))
    states = jnp.concatenate([jnp.zeros_like(states[:, :1]), states], axis=1)
    states = jnp.einsum('bgrzc,bcgrpn->bzgrpn', decay_chunk, states)[:, :-1]
    y_off = jnp.einsum('bclgn,bcgrpn->bclgrp', c_c, states) * jnp.moveaxis(jnp.exp(a_cum), -1, 2)[..., None]
    return (y_diag + y_off).reshape(b, L, G, R, P)


def dwconv_centred(u, w, bias):
    k = w.shape[0]
    out = lax.conv_general_dilated(u, w[:, None, :], window_strides=(1,), padding=[(k // 2, k // 2)],
                                   dimension_numbers=('NWC', 'WIO', 'NWC'), feature_group_count=u.shape[-1])
    return out + bias


def ssd_mixer(h, w_in, conv_w, conv_b, a_log, dt_bias, d_skip, norm_w, w_out):
    b, L, _ = h.shape
    G, R, P, N = SSD_GROUPS, SSD_HEADS_PER_GROUP, SSD_HEAD_DIM, SSD_STATE
    proj = h @ w_in
    z = proj[..., :D_INNER]
    xbc = proj[..., D_INNER:D_INNER + CONV_DIM]
    dt_raw = proj[..., D_INNER + CONV_DIM:]
    xbc = jax.nn.silu(dwconv_centred(xbc, conv_w, conv_b))
    xs = xbc[..., :D_INNER].reshape(b, L, G, R, P)
    bm = xbc[..., D_INNER:D_INNER + G * N].reshape(b, L, G, N)
    cm = xbc[..., D_INNER + G * N:].reshape(b, L, G, N)
    dt = jax.nn.softplus(dt_raw.astype(jnp.float32).reshape(b, L, 2, SSD_HEADS) + dt_bias)
    a = -jnp.exp(a_log.astype(jnp.float32))
    y = xs * d_skip.reshape(G, R)[:, :, None]
    for d in range(2):
        dt_d = dt[:, :, d].reshape(b, L, G, R)
        a_dt = dt_d * a[d].reshape(G, R)
        xdt = xs * dt_d[..., None]
        bd, cd = bm, cm
        if d == 1:
            xdt, a_dt, bd, cd = (jnp.flip(xdt, 1), jnp.flip(a_dt, 1), jnp.flip(bd, 1), jnp.flip(cd, 1))
        y_d = ssd_chunked(xdt, a_dt, bd, cd)
        if d == 1:
            y_d = jnp.flip(y_d, 1)
        y = y + y_d.astype(y.dtype)
    yf = (y.reshape(b, L, D_INNER) * jax.nn.silu(z)).astype(jnp.float32).reshape(b, L, SSD_GROUPS, -1)
    yf = yf * lax.rsqrt(jnp.mean(yf * yf, axis=-1, keepdims=True) + EPS)
    yn = (yf.reshape(b, L, D_INNER) * norm_w).astype(h.dtype)
    return yn @ w_out


def na_mixer(h, w_qkv, rpb, w_o):
    b, L, _ = h.shape
    rows = L // GRID_W
    kh = min(WIN_H, rows)
    scale = NA_HEAD_DIM ** -0.5
    qkv = (h @ w_qkv).reshape(b, rows, GRID_W, 3, NA_HEADS, NA_HEAD_DIM)
    q = (qkv[:, :, :, 0] * scale).transpose(1, 0, 3, 2, 4)
    k = qkv[:, :, :, 1].transpose(0, 3, 1, 2, 4)
    v = qkv[:, :, :, 2].transpose(0, 3, 1, 2, 4)
    j = jnp.arange(GRID_W)
    c0 = jnp.clip(j - WIN_W // 2, 0, GRID_W - WIN_W)
    col_idx = c0[:, None] + jnp.arange(WIN_W)[None, :]
    dxi = col_idx - j[:, None] + (WIN_W - 1)

    def row_attend(args):
        r, q_r = args
        r0 = jnp.clip(r - WIN_H // 2, 0, rows - kh)
        k_rows = lax.dynamic_slice_in_dim(k, r0, kh, axis=2)
        v_rows = lax.dynamic_slice_in_dim(v, r0, kh, axis=2)
        k_win = k_rows[:, :, :, col_idx]
        v_win = v_rows[:, :, :, col_idx]
        s = jnp.einsum('bhjd,bhajkd->bhjak', q_r, k_win).astype(jnp.float32)
        dyi = r0 + jnp.arange(kh) - r + (WIN_H - 1)
        bias = rpb[:, dyi[None, :, None], dxi[:, None, :]]
        s = s + bias[None].astype(jnp.float32)
        p = jax.nn.softmax(s.reshape(b, NA_HEADS, GRID_W, kh * WIN_W), axis=-1)
        p = p.reshape(b, NA_HEADS, GRID_W, kh, WIN_W).astype(v.dtype)
        return jnp.einsum('bhjak,bhajkd->bhjd', p, v_win)

    o = lax.map(row_attend, (jnp.arange(rows), q))
    o = o.transpose(1, 0, 3, 2, 4).reshape(b, L, D_MODEL)
    return o @ w_o


def hier_moe(h, w_group, w_expert, w1, w3, w2):
    b, L, D = h.shape
    T = b * L
    xt = h.reshape(T, D)
    g_logits = (xt @ w_group).astype(jnp.float32)
    g_prob = jax.nn.softmax(g_logits, axis=-1)
    g_sel = jnp.argmax(g_logits, axis=-1)
    g_w = jnp.take_along_axis(g_prob, g_sel[:, None], axis=1)[:, 0]
    e_logits = (xt @ w_expert).astype(jnp.float32).reshape(T, MOE_GROUPS, MOE_EXPERTS_PER_GROUP)
    e_logits = jnp.take_along_axis(e_logits, g_sel[:, None, None], axis=1)[:, 0]
    e_prob = jax.nn.softmax(e_logits, axis=-1)
    top_p, top_i = lax.top_k(e_prob, MOE_TOP_K)
    top_p = top_p / jnp.sum(top_p, axis=-1, keepdims=True)
    weights = (g_w[:, None] * top_p).astype(h.dtype)
    expert_id = (g_sel[:, None] * MOE_EXPERTS_PER_GROUP + top_i).astype(jnp.int32)
    A = T * MOE_TOP_K
    eid = expert_id.reshape(A)
    tid = jnp.repeat(jnp.arange(T, dtype=jnp.int32), MOE_TOP_K)
    wts = weights.reshape(A)
    order = jnp.argsort(eid)
    eid_s, tid_s, w_s = eid[order], tid[order], wts[order]
    counts = jnp.zeros((N_EXPERTS,), jnp.int32).at[eid].add(1)
    padded = ((counts + MOE_BLOCK - 1) // MOE_BLOCK) * MOE_BLOCK
    start = jnp.cumsum(counts) - counts
    pend = jnp.cumsum(padded)
    pstart = pend - padded
    dest = pstart[eid_s] + (jnp.arange(A, dtype=jnp.int32) - start[eid_s])
    nb = (A + N_EXPERTS * (MOE_BLOCK - 1) + MOE_BLOCK - 1) // MOE_BLOCK
    P = nb * MOE_BLOCK
    tok_buf = jnp.full((P,), T, jnp.int32).at[dest].set(tid_s)
    w_buf = jnp.zeros((P,), h.dtype).at[dest].set(w_s)
    blk_expert = jnp.minimum(jnp.searchsorted(pend, jnp.arange(nb, dtype=jnp.int32) * MOE_BLOCK, side='right'),
                             N_EXPERTS - 1)
    x_pad = jnp.concatenate([xt, jnp.zeros((1, D), xt.dtype)], axis=0)
    xb = x_pad[tok_buf].reshape(nb, MOE_BLOCK, D)

    def run_block(args):
        xb_i, e = args
        return (jax.nn.silu(xb_i @ w1[e]) * (xb_i @ w3[e])) @ w2[e]

    yb = lax.map(run_block, (xb, blk_expert)).reshape(P, D)
    out = jax.ops.segment_sum(yb * w_buf[:, None], tok_buf, num_segments=T + 1)[:T]
    return out.reshape(b, L, D)


def setup_inputs(seed: int = 0) -> dict:
    key = jax.random.key(seed)
    ks = jax.random.split(key, 24)
    f32 = jnp.float32

    def nrm(k, shape, scale):
        return jax.random.normal(k, shape, f32) * scale

    x = nrm(ks[0], (BATCH, SEQ, D_MODEL), 1.0)
    c = nrm(ks[1], (BATCH, D_MODEL), 1.0)
    ada_w = nrm(ks[2], (DEPTH, D_MODEL, 6 * D_MODEL), 0.5 * D_MODEL ** -0.5)
    ada_b = nrm(ks[3], (DEPTH, 6 * D_MODEL), 0.02)
    norm_mix = 1.0 + nrm(ks[4], (DEPTH, D_MODEL), 0.02)
    norm_ffn = 1.0 + nrm(ks[5], (DEPTH, D_MODEL), 0.02)
    ssd_w_in = nrm(ks[6], (N_SSD_LAYERS, D_MODEL, SSD_IN_DIM), D_MODEL ** -0.5)
    ssd_conv_w = nrm(ks[7], (N_SSD_LAYERS, SSD_CONV, CONV_DIM), SSD_CONV ** -0.5)
    ssd_conv_b = nrm(ks[8], (N_SSD_LAYERS, CONV_DIM), 0.02)
    ssd_a_log = jnp.log(jax.random.uniform(ks[9], (N_SSD_LAYERS, 2, SSD_HEADS), f32, 1.0, 16.0))
    dt0 = jnp.exp(jax.random.uniform(ks[10], (N_SSD_LAYERS, 2, SSD_HEADS), f32, math.log(1e-3), math.log(1e-1)))
    ssd_dt_bias = dt0 + jnp.log(-jnp.expm1(-dt0))
    ssd_d = 1.0 + nrm(ks[11], (N_SSD_LAYERS, SSD_HEADS), 0.1)
    ssd_norm_w = 1.0 + nrm(ks[12], (N_SSD_LAYERS, D_INNER), 0.02)
    ssd_w_out = nrm(ks[13], (N_SSD_LAYERS, D_INNER, D_MODEL), D_INNER ** -0.5)
    na_w_qkv = nrm(ks[14], (N_NA_LAYERS, D_MODEL, 3 * D_MODEL), D_MODEL ** -0.5)
    na_rpb = nrm(ks[15], (N_NA_LAYERS, NA_HEADS, 2 * WIN_H - 1, 2 * WIN_W - 1), 0.1)
    na_w_o = nrm(ks[16], (N_NA_LAYERS, D_MODEL, D_MODEL), D_MODEL ** -0.5)
    moe_w_group = nrm(ks[17], (DEPTH, D_MODEL, MOE_GROUPS), D_MODEL ** -0.5)
    moe_w_expert = nrm(ks[18], (DEPTH, D_MODEL, N_EXPERTS), D_MODEL ** -0.5)
    moe_w1 = nrm(ks[19], (DEPTH, N_EXPERTS, D_MODEL, MOE_D_FF), D_MODEL ** -0.5)
    moe_w3 = nrm(ks[20], (DEPTH, N_EXPERTS, D_MODEL, MOE_D_FF), D_MODEL ** -0.5)
    moe_w2 = nrm(ks[21], (DEPTH, N_EXPERTS, MOE_D_FF, D_MODEL), MOE_D_FF ** -0.5)
    final_norm = 1.0 + nrm(ks[22], (D_MODEL,), 0.02)
    return {'x': x, 'c': c, 'ada_w': ada_w, 'ada_b': ada_b, 'norm_mix': norm_mix, 'norm_ffn': norm_ffn,
            'ssd_w_in': ssd_w_in, 'ssd_conv_w': ssd_conv_w, 'ssd_conv_b': ssd_conv_b, 'ssd_a_log': ssd_a_log,
            'ssd_dt_bias': ssd_dt_bias, 'ssd_d': ssd_d, 'ssd_norm_w': ssd_norm_w, 'ssd_w_out': ssd_w_out,
            'na_w_qkv': na_w_qkv, 'na_rpb': na_rpb, 'na_w_o': na_w_o,
            'moe_w_group': moe_w_group, 'moe_w_expert': moe_w_expert, 'moe_w1': moe_w1, 'moe_w3': moe_w3,
            'moe_w2': moe_w2, 'final_norm': final_norm}


def reference(x, c, ada_w, ada_b, norm_mix, norm_ffn, ssd_w_in, ssd_conv_w, ssd_conv_b, ssd_a_log,
              ssd_dt_bias, ssd_d, ssd_norm_w, ssd_w_out, na_w_qkv, na_rpb, na_w_o,
              moe_w_group, moe_w_expert, moe_w1, moe_w3, moe_w2, final_norm):
    c_act = jax.nn.silu(c)
    for i in range(DEPTH):
        mod = (c_act @ ada_w[i] + ada_b[i])[:, None, :]
        sh1, sc1, g1, sh2, sc2, g2 = jnp.split(mod, 6, axis=-1)
        h = rmsnorm(x, norm_mix[i]) * (1.0 + sc1) + sh1
        j = i // N_MIXERS
        if i % N_MIXERS == 0:
            y = ssd_mixer(h, ssd_w_in[j], ssd_conv_w[j], ssd_conv_b[j], ssd_a_log[j], ssd_dt_bias[j],
                          ssd_d[j], ssd_norm_w[j], ssd_w_out[j])
        else:
            y = na_mixer(h, na_w_qkv[j], na_rpb[j], na_w_o[j])
        x = x + g1 * y
        h = rmsnorm(x, norm_ffn[i]) * (1.0 + sc2) + sh2
        x = x + g2 * hier_moe(h, moe_w_group[i], moe_w_expert[i], moe_w1[i], moe_w3[i], moe_w2[i])
    return rmsnorm(x, final_norm)
```

```python
import functools

import jax
import jax.numpy as jnp
from jax import lax
from jax.experimental import pallas as pl
from jax.experimental.pallas import tpu as pltpu

F32 = jnp.float32
BF16 = jnp.bfloat16
I32 = jnp.int32

EPS = 1e-6
NEG = -1e30

D_MODEL = 2048
GRID_W = 64
SSD_HEAD_DIM = 64
SSD_GROUPS = 8
SSD_HEADS_PER_GROUP = 8
SSD_STATE = 128
SSD_CONV = 5
D_INNER = 2 * D_MODEL
GROUP_W = SSD_HEADS_PER_GROUP * SSD_HEAD_DIM
SSD_Q = 128
CONV_HALO = 16
NA_HEAD_DIM = 64
NA_HEADS = D_MODEL // NA_HEAD_DIM
NA_PAIRS = NA_HEADS // 2
WIN_H = 8
WIN_W = 16
NA_ROWS = 4
MOE_GROUPS = 4
MOE_EPG = 8
N_EXPERTS = MOE_GROUPS * MOE_EPG
MOE_D_FF = D_MODEL // 4
MOE_TB = 256

VMEM_LIMIT = 56 * 1024 * 1024
LANES = 128


def _cparams(sem):
    return pltpu.CompilerParams(dimension_semantics=sem, vmem_limit_bytes=VMEM_LIMIT)


def _silu(v):
    return v / (1.0 + jnp.exp(-v))


def _softplus(v):
    return jnp.maximum(v, 0.0) + jnp.log1p(jnp.exp(-jnp.abs(v)))


def _split3(v):
    hi = v.astype(BF16)
    r1 = v - hi.astype(F32)
    mid = r1.astype(BF16)
    lo = (r1 - mid.astype(F32)).astype(BF16)
    return hi, mid, lo


def _dot(a, b):
    return jnp.dot(a, b, preferred_element_type=F32)


def _dot_nt(a, b):
    return lax.dot_general(a, b, (((1,), (1,)), ((), ())), preferred_element_type=F32)


def _dot3_left(v, sel):
    hi, mid, lo = _split3(v)
    return _dot(hi, sel) + _dot(mid, sel) + _dot(lo, sel)


def _dot3_right(sel, v):
    hi, mid, lo = _split3(v)
    return _dot(sel, hi) + _dot(sel, mid) + _dot(sel, lo)


def _normmod(x, nw, sc, sh):
    ms = jnp.mean(x * x, axis=-1, keepdims=True)
    return (x * lax.rsqrt(ms + EPS) * nw) * (1.0 + sc) + sh


def _ada_kernel(c_ref, w_ref, b_ref, o_ref):
    ca = _silu(c_ref[...])
    o_ref[0] = jnp.dot(ca, w_ref[0], precision=lax.Precision.HIGHEST,
                       preferred_element_type=F32) + b_ref[0]


def _ada(c_pad, ada_w, ada_b):
    depth, d, n = ada_w.shape
    tn = 1024
    return pl.pallas_call(
        _ada_kernel,
        out_shape=jax.ShapeDtypeStruct((depth, 8, n), F32),
        grid=(depth, n // tn),
        in_specs=[pl.BlockSpec((8, d), lambda i, j: (0, 0)),
                  pl.BlockSpec((1, d, tn), lambda i, j: (i, 0, j)),
                  pl.BlockSpec((1, 1, tn), lambda i, j: (i, 0, j))],
        out_specs=pl.BlockSpec((1, 8, tn), lambda i, j: (i, 0, j)),
        compiler_params=_cparams(("arbitrary", "arbitrary")),
        name="ada_mod",
    )(c_pad, ada_w, ada_b.reshape(depth, 1, n))


def _nm_mm_kernel(x_ref, nw_ref, sc_ref, sh_ref, w_ref, o_ref, h_ref, *, pair_major):
    @pl.when(pl.program_id(1) == 0)
    def _():
        h_ref[...] = _normmod(x_ref[...], nw_ref[...], sc_ref[0], sh_ref[0]).astype(BF16)

    r = _dot(h_ref[...], w_ref[...].astype(BF16))
    if pair_major:
        for c in range(o_ref.shape[0]):
            o_ref[c] = r[:, c * LANES:(c + 1) * LANES].astype(o_ref.dtype)
    else:
        o_ref[...] = r.astype(o_ref.dtype)


def _nm_matmul(x, nw, sc, sh, w, *, col0, ncols, tn, out_dtype, rows_per_batch, pair_major=False):
    t, d = x.shape
    tm = 1024
    tiles_per_batch = rows_per_batch // tm
    nj = ncols // tn
    jb = col0 // tn
    if pair_major:
        out_shape = jax.ShapeDtypeStruct((ncols // LANES, t, LANES), out_dtype)
        out_spec = pl.BlockSpec((tn // LANES, tm, LANES), lambda i, j: (j, i, 0))
    else:
        out_shape = jax.ShapeDtypeStruct((t, ncols), out_dtype)
        out_spec = pl.BlockSpec((tm, tn), lambda i, j: (i, j))
    return pl.pallas_call(
        functools.partial(_nm_mm_kernel, pair_major=pair_major),
        out_shape=out_shape,
        grid=(t // tm, nj),
        in_specs=[pl.BlockSpec((tm, d), lambda i, j: (i, 0)),
                  pl.BlockSpec((1, d), lambda i, j: (0, 0)),
                  pl.BlockSpec((1, 1, d), lambda i, j: (i // tiles_per_batch, 0, 0)),
                  pl.BlockSpec((1, 1, d), lambda i, j: (i // tiles_per_batch, 0, 0)),
                  pl.BlockSpec((d, tn), lambda i, j: (0, jb + j))],
        out_specs=out_spec,
        scratch_shapes=[pltpu.VMEM((tm, d), BF16)],
        compiler_params=_cparams(("arbitrary", "arbitrary")),
        name="norm_mod_matmul",
    )(x, nw, sc, sh, w)


def _mm_resid_kernel(a_ref, w_ref, x_ref, g_ref, o_ref, *, pair_major):
    if pair_major:
        a = jnp.concatenate([a_ref[c] for c in range(a_ref.shape[0])], axis=1)
    else:
        a = a_ref[...]
    o_ref[...] = x_ref[...] + g_ref[0] * _dot(a, w_ref[...].astype(BF16))


def _mm_resid(a, w, x, g, *, rows_per_batch, pair_major=False):
    t, n = x.shape
    k = w.shape[0]
    tm, tn = 1024, 512
    tiles_per_batch = rows_per_batch // tm
    if pair_major:
        a_spec = pl.BlockSpec((k // LANES, tm, LANES), lambda i, j: (0, i, 0))
    else:
        a_spec = pl.BlockSpec((tm, k), lambda i, j: (i, 0))
    return pl.pallas_call(
        functools.partial(_mm_resid_kernel, pair_major=pair_major),
        out_shape=jax.ShapeDtypeStruct((t, n), F32),
        grid=(t // tm, n // tn),
        in_specs=[a_spec,
                  pl.BlockSpec((k, tn), lambda i, j: (0, j)),
                  pl.BlockSpec((tm, tn), lambda i, j: (i, j)),
                  pl.BlockSpec((1, 1, tn), lambda i, j: (i // tiles_per_batch, 0, j))],
        out_specs=pl.BlockSpec((tm, tn), lambda i, j: (i, j)),
        compiler_params=_cparams(("arbitrary", "arbitrary")),
        name="matmul_resid",
    )(a, w, x, g)


def _ssd_kernel(z_ref, x_ref, b_ref, c_ref, dtc_ref, dtr_ref,
                cwx_ref, cwb_ref, cwc_ref, cbx_ref, cbb_ref, cbc_ref,
                alc_ref, alr_ref, dbc_ref, dbr_ref, dsk_ref, nw_ref,
                o_ref,
                xc_s, bc_s, cc_s, yacc_s, st_s):
    seq = x_ref.shape[0]
    q = SSD_Q
    nc = seq // q
    halo = CONV_HALO
    nst = SSD_STATE

    def conv_piece(j, base, src_ref, w_ref, bias_ref, dst_ref, lo):
        cols = slice(lo, lo + LANES)
        main = src_ref[pl.ds(base, q), cols].astype(F32)
        pstart = pl.multiple_of(jnp.maximum(base - halo, 0), halo)
        prev = jnp.where(j > 0, src_ref[pl.ds(pstart, halo), cols].astype(F32), 0.0)
        nstart = pl.multiple_of(jnp.minimum(base + q, seq - halo), halo)
        nxt = jnp.where(j < nc - 1, src_ref[pl.ds(nstart, halo), cols].astype(F32), 0.0)
        cat = jnp.concatenate([prev, main, nxt], axis=0)
        rows = q + 2 * halo
        acc = jnp.broadcast_to(bias_ref[:, cols], (q, LANES))
        for k in range(SSD_CONV):
            shift = (SSD_CONV // 2 - k) % rows
            rolled = cat if shift == 0 else pltpu.roll(cat, shift, axis=0)
            acc = acc + w_ref[k:k + 1, cols] * rolled[halo:halo + q]
        dst_ref[pl.ds(base, q), cols] = _silu(acc).astype(BF16)

    def conv_chunk(j, carry):
        base = pl.multiple_of(j * q, q)
        for lo in range(0, GROUP_W, LANES):
            conv_piece(j, base, x_ref, cwx_ref, cbx_ref, xc_s, lo)
        conv_piece(j, base, b_ref, cwb_ref, cbb_ref, bc_s, 0)
        conv_piece(j, base, c_ref, cwc_ref, cbc_ref, cc_s, 0)
        return carry

    lax.fori_loop(0, nc, conv_chunk, 0)

    row_i = lax.broadcasted_iota(I32, (q, q), 0)
    col_i = lax.broadcasted_iota(I32, (q, q), 1)
    lower = row_i >= col_i
    upper = row_i <= col_i
    lower_b = lower.astype(BF16)
    upper_b = upper.astype(BF16)
    lane_w = lax.broadcasted_iota(I32, (q, GROUP_W), 1)
    even_head = (lane_w & (LANES - 1)) < SSD_HEAD_DIM
    a_col = -jnp.exp(alc_ref[...])
    a_row = -jnp.exp(alr_ref[...])

    def scan_pass(direction):
        hoff = direction * SSD_HEADS_PER_GROUP
        e_row = lax.broadcasted_iota(I32, (LANES, GROUP_W), 0)
        e_col = lax.broadcasted_iota(I32, (LANES, GROUP_W), 1)
        expand = (e_row == hoff + (e_col >> 6)).astype(BF16)
        mask = lower if direction == 0 else upper
        st_s[...] = jnp.zeros_like(st_s)

        def chunk(t, carry):
            c = t if direction == 0 else nc - 1 - t
            base = pl.multiple_of(c * q, q)
            dt_c = _softplus(dtc_ref[pl.ds(base, q), :] + dbc_ref[...])
            adt_c = dt_c * a_col
            dt_r = _softplus(dtr_ref[c] + dbr_ref[...])
            adt_r = dt_r * a_row
            if direction == 0:
                cum_c = _dot3_right(lower_b, adt_c)
                cum_r = _dot3_left(adt_r, upper_b)
                edge = q - 1
            else:
                cum_c = _dot3_right(upper_b, adt_c)
                cum_r = _dot3_left(adt_r, lower_b)
                edge = 0
            scale_c = jnp.exp(cum_c)
            decay_c = jnp.exp(cum_c[edge:edge + 1, :] - cum_c)
            ex = _dot3_left(jnp.concatenate([dt_c, decay_c, scale_c], axis=0), expand)
            dt_e, dec_e, sc_e = ex[0:q], ex[q:2 * q], ex[2 * q:3 * q]

            xcf = xc_s[pl.ds(base, q), :].astype(F32)
            xdt = xcf * dt_e
            xdt_b = xdt.astype(BF16)
            xd_b = (xdt * dec_e).astype(BF16)
            bm = bc_s[pl.ds(base, q), :]
            cm = cc_s[pl.ds(base, q), :]
            cb = _dot_nt(cm, bm)
            zero_b = jnp.zeros_like(xdt_b)
            xdt_even = jnp.where(even_head, xdt_b, zero_b)
            xdt_odd = jnp.where(even_head, zero_b, xdt_b)

            pieces = []
            for pp in range(SSD_HEADS_PER_GROUP // 2):
                cols = slice(pp * LANES, (pp + 1) * LANES)
                acc = None
                for par, src in ((0, xdt_even), (1, xdt_odd)):
                    j = hoff + 2 * pp + par
                    diff = cum_c[:, j:j + 1] - cum_r[j:j + 1, :]
                    lmat = jnp.exp(jnp.where(mask, diff, NEG))
                    part = _dot((cb * lmat).astype(BF16), src[:, cols])
                    acc = part if acc is None else acc + part
                pieces.append(acc)
            y = jnp.concatenate(pieces, axis=1)

            s_in = st_s[...]
            y = y + _dot(cm, s_in.astype(BF16)) * sc_e
            bm_t = bm.astype(F32).T.astype(BF16)
            st_s[...] = s_in * sc_e[edge:edge + 1, :] + _dot(bm_t, xd_b)

            if direction == 0:
                yacc_s[pl.ds(base, q), :] = y
            else:
                total = yacc_s[pl.ds(base, q), :] + y + xcf * dsk_ref[...]
                gated = total * _silu(z_ref[pl.ds(base, q), :].astype(F32))
                ms = jnp.mean(gated * gated, axis=-1, keepdims=True)
                o_ref[pl.ds(base, q), :] = (gated * lax.rsqrt(ms + EPS) * nw_ref[...]).astype(BF16)
            return carry

        lax.fori_loop(0, nc, chunk, 0)

    scan_pass(0)
    scan_pass(1)


def _ssd_core(zx, dt_col, dt_row, conv_w, conv_b, al_col, al_row, db_col, db_row, dskip, norm_w,
              *, batch, seq):
    g = SSD_GROUPS
    nc = seq // SSD_Q
    xb = D_INNER // GROUP_W
    bb = (2 * D_INNER) // SSD_STATE
    cb = bb + g
    cwb = D_INNER // SSD_STATE
    cwc = cwb + g
    return pl.pallas_call(
        _ssd_kernel,
        out_shape=jax.ShapeDtypeStruct((batch * seq, D_INNER), BF16),
        grid=(batch, g),
        in_specs=[
            pl.BlockSpec((seq, GROUP_W), lambda b, i: (b, i)),
            pl.BlockSpec((seq, GROUP_W), lambda b, i: (b, xb + i)),
            pl.BlockSpec((seq, SSD_STATE), lambda b, i: (b, bb + i)),
            pl.BlockSpec((seq, SSD_STATE), lambda b, i: (b, cb + i)),
            pl.BlockSpec((None, None, seq, LANES), lambda b, i: (b, i, 0, 0)),
            pl.BlockSpec((None, None, nc, 2 * SSD_HEADS_PER_GROUP, SSD_Q), lambda b, i: (b, i, 0, 0, 0)),
            pl.BlockSpec((SSD_CONV, GROUP_W), lambda b, i: (0, i)),
            pl.BlockSpec((SSD_CONV, SSD_STATE), lambda b, i: (0, cwb + i)),
            pl.BlockSpec((SSD_CONV, SSD_STATE), lambda b, i: (0, cwc + i)),
            pl.BlockSpec((1, GROUP_W), lambda b, i: (0, i)),
            pl.BlockSpec((1, SSD_STATE), lambda b, i: (0, cwb + i)),
            pl.BlockSpec((1, SSD_STATE), lambda b, i: (0, cwc + i)),
            pl.BlockSpec((None, 1, LANES), lambda b, i: (i, 0, 0)),
            pl.BlockSpec((None, 2 * SSD_HEADS_PER_GROUP, 1), lambda b, i: (i, 0, 0)),
            pl.BlockSpec((None, 1, LANES), lambda b, i: (i, 0, 0)),
            pl.BlockSpec((None, 2 * SSD_HEADS_PER_GROUP, 1), lambda b, i: (i, 0, 0)),
            pl.BlockSpec((1, GROUP_W), lambda b, i: (0, i)),
            pl.BlockSpec((1, GROUP_W), lambda b, i: (0, i)),
        ],
        out_specs=pl.BlockSpec((seq, GROUP_W), lambda b, i: (b, i)),
        scratch_shapes=[pltpu.VMEM((seq, GROUP_W), BF16),
                        pltpu.VMEM((seq, SSD_STATE), BF16),
                        pltpu.VMEM((seq, SSD_STATE), BF16),
                        pltpu.VMEM((seq, GROUP_W), F32),
                        pltpu.VMEM((SSD_STATE, GROUP_W), F32)],
        compiler_params=_cparams(("arbitrary", "arbitrary")),
        name="ssd_core",
    )(zx, zx, zx, zx, dt_col, dt_row, conv_w, conv_w, conv_w, conv_b, conv_b, conv_b,
      al_col, al_row, db_col, db_row, dskip, norm_w)


def _ssd_mixer(x, mod_sc, mod_sh, mod_g, nw, w_in, conv_w, conv_b, a_log, dt_bias, d_skip, norm_w,
               w_out, *, batch, seq):
    g, r = SSD_GROUPS, SSD_HEADS_PER_GROUP
    conv_dim = conv_w.shape[1]
    zx = _nm_matmul(x, nw, mod_sc, mod_sh, w_in, col0=0, ncols=D_INNER + conv_dim, tn=1024,
                    out_dtype=BF16, rows_per_batch=seq)
    dt_raw = _nm_matmul(x, nw, mod_sc, mod_sh, w_in, col0=D_INNER + conv_dim, ncols=2 * g * r,
                        tn=2 * g * r, out_dtype=F32, rows_per_batch=seq)
    nc = seq // SSD_Q
    dt5 = dt_raw.reshape(batch, seq, 2, g, r)
    dt_col = dt5.transpose(0, 3, 1, 2, 4).reshape(batch, g, seq, 2 * r)
    dt_col = jnp.pad(dt_col, ((0, 0), (0, 0), (0, 0), (0, LANES - 2 * r)))
    dt_row = dt5.reshape(batch, nc, SSD_Q, 2, g, r).transpose(0, 4, 1, 3, 5, 2)
    dt_row = dt_row.reshape(batch, g, nc, 2 * r, SSD_Q)

    def per_group(p):
        return p.reshape(2, g, r).transpose(1, 0, 2).reshape(g, 2 * r)

    def col_form(p):
        return jnp.pad(per_group(p), ((0, 0), (0, LANES - 2 * r))).reshape(g, 1, LANES)

    def row_form(p):
        return per_group(p).reshape(g, 2 * r, 1)

    yn = _ssd_core(zx, dt_col, dt_row, conv_w, conv_b.reshape(1, conv_dim),
                   col_form(a_log), row_form(a_log), col_form(dt_bias), row_form(dt_bias),
                   jnp.repeat(d_skip, SSD_HEAD_DIM).reshape(1, D_INNER), norm_w.reshape(1, D_INNER),
                   batch=batch, seq=seq)
    return _mm_resid(yn, w_out, x, mod_g, rows_per_batch=seq)


def _bias_table_kernel(rpb_ref, o_ref):
    tn = o_ref.shape[1]
    col = lax.broadcasted_iota(I32, (1, tn), 1) + pl.program_id(0) * tn
    j = col >> 6
    c = col & (GRID_W - 1)
    c0 = jnp.clip(j - WIN_W // 2, 0, GRID_W - WIN_W)
    win = (c >= c0) & (c < c0 + WIN_W)
    want = jnp.where(win, c - j + (WIN_W - 1), -1)
    dx = lax.broadcasted_iota(I32, (LANES, tn), 0)
    onehot = (dx == want).astype(BF16)
    t = _dot3_left(rpb_ref[...], onehot)
    o_ref[...] = t + jnp.where(win, 0.0, NEG)


def _na_bias_table(rpb):
    h, ndy, ndx = rpb.shape
    rows = h * ndy
    rpb2 = jnp.pad(rpb.reshape(rows, ndx), ((0, 0), (0, LANES - ndx)))
    tn = 1024
    flat = pl.pallas_call(
        _bias_table_kernel,
        out_shape=jax.ShapeDtypeStruct((rows, GRID_W * GRID_W), F32),
        grid=(GRID_W * GRID_W // tn,),
        in_specs=[pl.BlockSpec((rows, LANES), lambda i: (0, 0))],
        out_specs=pl.BlockSpec((rows, tn), lambda i: (0, i)),
        compiler_params=_cparams(("arbitrary",)),
        name="na_bias_table",
    )(rpb2)
    t4 = flat.reshape(h, ndy, GRID_W, GRID_W)
    variants = [t4[:, d:d + WIN_H].transpose(0, 2, 1, 3).reshape(h, GRID_W, WIN_H * GRID_W)
                for d in range(WIN_H)]
    return jnp.stack(variants).reshape(WIN_H, h // 2, 2 * GRID_W, WIN_H * GRID_W)


def _na_kernel(q_ref, k0_ref, k1_ref, k2_ref, v0_ref, v1_ref, v2_ref, bias_ref, o_ref, kc_s, vc_s,
               *, n_row_blocks):
    rb = pl.program_id(2)
    npairs = q_ref.shape[0]
    blk = NA_ROWS * GRID_W
    nkeys = WIN_H * GRID_W
    for i, (kr, vr) in enumerate(((k0_ref, v0_ref), (k1_ref, v1_ref), (k2_ref, v2_ref))):
        kc_s[:, i * blk:(i + 1) * blk, :] = kr[...]
        vc_s[:, i * blk:(i + 1) * blk, :] = vr[...]
    first = rb == 0
    last = rb == n_row_blocks - 1
    edge = first | last
    lane = lax.broadcasted_iota(I32, (GRID_W, LANES), 1)
    left = lane < NA_HEAD_DIM
    scale = jnp.asarray(NA_HEAD_DIM ** -0.5, BF16)

    def pair_body(pp, carry):
        for qi in range(NA_ROWS):
            off = jnp.where(first, 0, jnp.where(last, blk, qi * GRID_W))
            off = pl.multiple_of(off, GRID_W)
            li = jnp.where(edge, NA_ROWS - 1 - qi, NA_ROWS - 1)
            q2 = q_ref[pp, qi * GRID_W:(qi + 1) * GRID_W, :] * scale
            zero = jnp.zeros_like(q2)
            qs = jnp.concatenate([jnp.where(left, q2, zero), jnp.where(left, zero, q2)], axis=0)
            kw = kc_s[pp, pl.ds(off, nkeys), :]
            s = _dot_nt(qs, kw) + bias_ref[li, pp]
            m = jnp.max(s, axis=-1, keepdims=True)
            p = jnp.exp(s - m)
            denom = jnp.sum(p, axis=-1, keepdims=True)
            pv = _dot(p.astype(BF16), vc_s[pp, pl.ds(off, nkeys), :]) / denom
            o = jnp.where(left, pv[0:GRID_W], pv[GRID_W:2 * GRID_W])
            o_ref[pp, qi * GRID_W:(qi + 1) * GRID_W, :] = o.astype(BF16)
        return carry

    lax.fori_loop(0, npairs, pair_body, 0)


def _na_attention(qkv_t, bias_tab, *, batch, seq):
    t = batch * seq
    blk = NA_ROWS * GRID_W
    nrb = seq // blk
    hp = NA_PAIRS // 2
    nsec = NA_PAIRS // hp

    def kv_spec(sec, i):
        def imap(hh, b, r):
            return (sec * nsec + hh, b * nrb + jnp.clip(r - 1, 0, nrb - 3) + i, 0)
        return pl.BlockSpec((hp, blk, LANES), imap)

    return pl.pallas_call(
        functools.partial(_na_kernel, n_row_blocks=nrb),
        out_shape=jax.ShapeDtypeStruct((NA_PAIRS, t, LANES), BF16),
        grid=(nsec, batch, nrb),
        in_specs=[pl.BlockSpec((hp, blk, LANES), lambda hh, b, r: (hh, b * nrb + r, 0)),
                  kv_spec(1, 0), kv_spec(1, 1), kv_spec(1, 2),
                  kv_spec(2, 0), kv_spec(2, 1), kv_spec(2, 2),
                  pl.BlockSpec((NA_ROWS, hp, 2 * GRID_W, WIN_H * GRID_W),
                               lambda hh, b, r: (jnp.where(r == 0, 1, 0), hh, 0, 0))],
        out_specs=pl.BlockSpec((hp, blk, LANES), lambda hh, b, r: (hh, b * nrb + r, 0)),
        scratch_shapes=[pltpu.VMEM((hp, 3 * blk, LANES), BF16),
                        pltpu.VMEM((hp, 3 * blk, LANES), BF16)],
        compiler_params=_cparams(("arbitrary", "arbitrary", "arbitrary")),
        name="na_attention",
    )(qkv_t, qkv_t, qkv_t, qkv_t, qkv_t, qkv_t, qkv_t, bias_tab)


def _na_mixer(x, mod_sc, mod_sh, mod_g, nw, w_qkv, rpb, w_o, *, batch, seq):
    qkv_t = _nm_matmul(x, nw, mod_sc, mod_sh, w_qkv, col0=0, ncols=3 * D_MODEL, tn=1024,
                       out_dtype=BF16, rows_per_batch=seq, pair_major=True)
    o_t = _na_attention(qkv_t, _na_bias_table(rpb), batch=batch, seq=seq)
    return _mm_resid(o_t, w_o, x, mod_g, rows_per_batch=seq, pair_major=True)


def _router_kernel(x_ref, nw_ref, sc_ref, sh_ref, wr_ref, h_ref, meta_ref, cnt_ref, carry_s):
    @pl.when(pl.program_id(0) == 0)
    def _():
        carry_s[...] = jnp.zeros_like(carry_s)

    h = _normmod(x_ref[...], nw_ref[...], sc_ref[0], sh_ref[0])
    h_ref[...] = h
    logits = jnp.dot(h, wr_ref[...], precision=lax.Precision.HIGHEST, preferred_element_type=F32)
    tm = logits.shape[0]
    lane_i = lax.broadcasted_iota(I32, logits.shape, 1)
    lane = lane_i.astype(F32)
    big = 1e9
    gl = jnp.where(lane_i < MOE_GROUPS, logits, NEG)
    gmax = jnp.max(gl, axis=1, keepdims=True)
    gsel = jnp.min(jnp.where(gl == gmax, lane, big), axis=1, keepdims=True)
    gw = 1.0 / jnp.sum(jnp.exp(gl - gmax), axis=1, keepdims=True)
    el = lane - MOE_GROUPS
    lo = gsel * MOE_EPG
    emask = (el >= lo) & (el < lo + MOE_EPG)
    e1 = jnp.where(emask, logits, NEG)
    m1 = jnp.max(e1, axis=1, keepdims=True)
    i1 = jnp.min(jnp.where(e1 == m1, el, big), axis=1, keepdims=True)
    e2 = jnp.where(emask & (el != i1), logits, NEG)
    m2 = jnp.max(e2, axis=1, keepdims=True)
    i2 = jnp.min(jnp.where(e2 == m2, el, big), axis=1, keepdims=True)
    tt = jnp.exp(m2 - m1)
    p1 = 1.0 / (1.0 + tt)
    w1 = gw * p1
    w2 = gw * (tt * p1)
    oh1 = el == i1
    oh2 = el == i2
    cnt = (oh1 | oh2).astype(F32)
    r_i = lax.broadcasted_iota(I32, (tm, tm), 0)
    c_i = lax.broadcasted_iota(I32, (tm, tm), 1)
    before = _dot((r_i > c_i).astype(BF16), cnt.astype(BF16)) + carry_s[...]
    rank1 = jnp.sum(jnp.where(oh1, before, 0.0), axis=1, keepdims=True)
    rank2 = jnp.sum(jnp.where(oh2, before, 0.0), axis=1, keepdims=True)
    carry_s[...] = carry_s[...] + jnp.sum(cnt, axis=0, keepdims=True)
    meta = jnp.zeros_like(logits)
    for pos, val in enumerate((i1, i2, w1, w2, rank1, rank2)):
        meta = jnp.where(lane_i == pos, val, meta)
    meta_ref[...] = meta
    cnt_ref[...] = jnp.broadcast_to(carry_s[...], cnt_ref.shape)


def _router(x, nw, sc, sh, wr, *, rows_per_batch):
    t, d = x.shape
    tm = 256
    tiles_per_batch = rows_per_batch // tm
    return pl.pallas_call(
        _router_kernel,
        out_shape=(jax.ShapeDtypeStruct((t, d), F32),
                   jax.ShapeDtypeStruct((t, LANES), F32),
                   jax.ShapeDtypeStruct((8, LANES), F32)),
        grid=(t // tm,),
        in_specs=[pl.BlockSpec((tm, d), lambda i: (i, 0)),
                  pl.BlockSpec((1, d), lambda i: (0, 0)),
                  pl.BlockSpec((1, 1, d), lambda i: (i // tiles_per_batch, 0, 0)),
                  pl.BlockSpec((1, 1, d), lambda i: (i // tiles_per_batch, 0, 0)),
                  pl.BlockSpec((d, LANES), lambda i: (0, 0))],
        out_specs=(pl.BlockSpec((tm, d), lambda i: (i, 0)),
                   pl.BlockSpec((tm, LANES), lambda i: (i, 0)),
                   pl.BlockSpec((8, LANES), lambda i: (0, 0))),
        scratch_shapes=[pltpu.VMEM((1, LANES), F32)],
        compiler_params=_cparams(("arbitrary",)),
        name="moe_router",
    )(x, nw, sc, sh, wr)


def _dispatch_kernel(dest_ref, h_ref, xs_in_ref, xs_ref, sem):
    del xs_in_ref
    tm = h_ref.shape[0]
    base = pl.program_id(0) * tm

    def copy(t, k):
        return pltpu.make_async_copy(h_ref.at[pl.ds(t, 1)],
                                     xs_ref.at[pl.ds(dest_ref[(base + t) * 2 + k], 1)], sem)

    def issue(t, carry):
        copy(t, 0).start()
        copy(t, 1).start()
        return carry

    def drain(t, carry):
        copy(t, 0).wait()
        copy(t, 1).wait()
        return carry

    lax.fori_loop(0, tm, issue, 0)
    lax.fori_loop(0, tm, drain, 0)


def _dispatch(dest, h, n_slots):
    t, d = h.shape
    tm = 256
    return pl.pallas_call(
        _dispatch_kernel,
        out_shape=jax.ShapeDtypeStruct((n_slots, d), F32),
        grid_spec=pltpu.PrefetchScalarGridSpec(
            num_scalar_prefetch=1,
            grid=(t // tm,),
            in_specs=[pl.BlockSpec((tm, d), lambda i, dest: (i, 0)),
                      pl.BlockSpec(memory_space=pl.ANY)],
            out_specs=pl.BlockSpec(memory_space=pl.ANY),
            scratch_shapes=[pltpu.SemaphoreType.DMA(())]),
        input_output_aliases={2: 0},
        compiler_params=_cparams(("arbitrary",)),
        name="moe_dispatch",
    )(dest, h, jnp.zeros((n_slots, d), F32))


def _ffn_kernel(be_ref, nu_ref, xs_ref, w1_ref, w3_ref, w2_ref, o_ref, w1_s, w3_s, w2_s):
    i = pl.program_id(0)

    @pl.when(i < nu_ref[0])
    def _():
        @pl.when((i == 0) | (be_ref[i] != be_ref[jnp.maximum(i - 1, 0)]))
        def _():
            w1_s[...] = w1_ref[0].astype(BF16)
            w3_s[...] = w3_ref[0].astype(BF16)
            w2_s[...] = w2_ref[0].astype(BF16)

        xb = xs_ref[...].astype(BF16)
        hmid = _silu(_dot(xb, w1_s[...])) * _dot(xb, w3_s[...])
        o_ref[...] = _dot(hmid.astype(BF16), w2_s[...])

    @pl.when(i >= nu_ref[0])
    def _():
        o_ref[...] = jnp.zeros_like(o_ref)


def _expert_ffn(blk_expert, n_used, xs, w1, w3, w2, layer):
    p, d = xs.shape
    f = w1.shape[3]
    tb = MOE_TB
    nb = p // tb

    def row_map(i, be, nu):
        return (jnp.minimum(i, nu[0] - 1), 0)

    def w_map(i, be, nu):
        return (layer, be[i], 0, 0)

    return pl.pallas_call(
        _ffn_kernel,
        out_shape=jax.ShapeDtypeStruct((p, d), F32),
        grid_spec=pltpu.PrefetchScalarGridSpec(
            num_scalar_prefetch=2,
            grid=(nb,),
            in_specs=[pl.BlockSpec((tb, d), row_map),
                      pl.BlockSpec((None, 1, d, f), w_map),
                      pl.BlockSpec((None, 1, d, f), w_map),
                      pl.BlockSpec((None, 1, f, d), w_map)],
            out_specs=pl.BlockSpec((tb, d), lambda i, be, nu: (i, 0)),
            scratch_shapes=[pltpu.VMEM((d, f), BF16), pltpu.VMEM((d, f), BF16),
                            pltpu.VMEM((f, d), BF16)]),
        compiler_params=_cparams(("arbitrary",)),
        name="moe_expert_ffn",
    )(blk_expert, n_used, xs, w1, w3, w2)


def _combine_kernel(dest_ref, x_ref, meta_ref, g_ref, fnw_ref, ys_ref, o_ref, buf, sem, *, final):
    tm = x_ref.shape[0]
    base = pl.program_id(0) * tm

    def copy(t, k):
        return pltpu.make_async_copy(ys_ref.at[pl.ds(dest_ref[(base + t) * 2 + k], 1)],
                                     buf.at[k, pl.ds(t, 1)], sem)

    def issue(t, carry):
        copy(t, 0).start()
        copy(t, 1).start()
        return carry

    def drain(t, carry):
        copy(t, 0).wait()
        copy(t, 1).wait()
        return carry

    lax.fori_loop(0, tm, issue, 0)
    lax.fori_loop(0, tm, drain, 0)
    meta = meta_ref[...]
    y = meta[:, 2:3] * buf[0] + meta[:, 3:4] * buf[1]
    xn = x_ref[...] + g_ref[0] * y
    if final:
        ms = jnp.mean(xn * xn, axis=-1, keepdims=True)
        xn = xn * lax.rsqrt(ms + EPS) * fnw_ref[...]
    o_ref[...] = xn


def _combine(dest, x, meta, g, fnw, ys, *, rows_per_batch, final):
    t, d = x.shape
    tm = 256
    tiles_per_batch = rows_per_batch // tm
    return pl.pallas_call(
        functools.partial(_combine_kernel, final=final),
        out_shape=jax.ShapeDtypeStruct((t, d), F32),
        grid_spec=pltpu.PrefetchScalarGridSpec(
            num_scalar_prefetch=1,
            grid=(t // tm,),
            in_specs=[pl.BlockSpec((tm, d), lambda i, dest: (i, 0)),
                      pl.BlockSpec((tm, LANES), lambda i, dest: (i, 0)),
                      pl.BlockSpec((1, 1, d), lambda i, dest: (i // tiles_per_batch, 0, 0)),
                      pl.BlockSpec((1, d), lambda i, dest: (0, 0)),
                      pl.BlockSpec(memory_space=pl.ANY)],
            out_specs=pl.BlockSpec((tm, d), lambda i, dest: (i, 0)),
            scratch_shapes=[pltpu.VMEM((2, tm, d), F32), pltpu.SemaphoreType.DMA(())]),
        compiler_params=_cparams(("arbitrary",)),
        name="moe_combine",
    )(dest, x, meta, g, fnw, ys)


def _hier_moe(x, mod_sc, mod_sh, mod_g, nw, w_group, w_expert, w1, w3, w2, layer, fnw, *,
              rows_per_batch, final):
    t, d = x.shape
    a = 2 * t
    tb = MOE_TB
    wr = jnp.concatenate([w_group, w_expert], axis=1)
    wr = jnp.pad(wr, ((0, 0), (0, LANES - wr.shape[1])))
    h, meta, cnt = _router(x, nw, mod_sc, mod_sh, wr, rows_per_batch=rows_per_batch)
    counts = cnt[0, MOE_GROUPS:MOE_GROUPS + N_EXPERTS].astype(I32)
    padded = ((counts + tb - 1) // tb) * tb
    pend = jnp.cumsum(padded)
    pstart = pend - padded
    eid = meta[:, 0:2].astype(I32)
    rank = meta[:, 4:6].astype(I32)
    dest = (pstart[eid] + rank).reshape(a)
    nb = (a + N_EXPERTS * (tb - 1) + tb - 1) // tb
    n_used = (pend[-1] // tb).astype(I32)
    blk = jnp.arange(nb, dtype=I32)
    be = jnp.minimum(jnp.searchsorted(pend, blk * tb, side='right'), N_EXPERTS - 1).astype(I32)
    be = jnp.where(blk < n_used, be, be[jnp.maximum(n_used - 1, 0)])
    xs = _dispatch(dest, h, nb * tb)
    ys = _expert_ffn(be, n_used.reshape(1), xs, w1, w3, w2, layer)
    return _combine(dest, x, meta, mod_g, fnw, ys, rows_per_batch=rows_per_batch, final=final)


def kernel(x, c, ada_w, ada_b, norm_mix, norm_ffn, ssd_w_in, ssd_conv_w, ssd_conv_b, ssd_a_log,
           ssd_dt_bias, ssd_d, ssd_norm_w, ssd_w_out, na_w_qkv, na_rpb, na_w_o,
           moe_w_group, moe_w_expert, moe_w1, moe_w3, moe_w2, final_norm):
    batch, seq, d = x.shape
    depth = ada_w.shape[0]
    xt = x.reshape(batch * seq, d)
    c_pad = jnp.pad(c, ((0, 8 - batch), (0, 0)))
    mod = _ada(c_pad, ada_w, ada_b)[:, :batch]
    fnw = final_norm.reshape(1, d)
    for i in range(depth):
        sh1, sc1, g1, sh2, sc2, g2 = [mod[i, :, k * d:(k + 1) * d].reshape(batch, 1, d)
                                      for k in range(6)]
        j = i // 2
        nw = norm_mix[i].reshape(1, d)
        if i % 2 == 0:
            xt = _ssd_mixer(xt, sc1, sh1, g1, nw, ssd_w_in[j], ssd_conv_w[j], ssd_conv_b[j],
                            ssd_a_log[j], ssd_dt_bias[j], ssd_d[j], ssd_norm_w[j], ssd_w_out[j],
                            batch=batch, seq=seq)
        else:
            xt = _na_mixer(xt, sc1, sh1, g1, nw, na_w_qkv[j], na_rpb[j], na_w_o[j],
                           batch=batch, seq=seq)
        xt = _hier_moe(xt, sc2, sh2, g2, norm_ffn[i].reshape(1, d), moe_w_group[i], moe_w_expert[i],
                       moe_w1, moe_w3, moe_w2, i, fnw, rows_per_batch=seq,
                       final=(i == depth - 1))
    return xt.reshape(batch, seq, d)
```

```python
import functools

import jax
import jax.numpy as jnp
from jax import lax
from jax.experimental import pallas as pl
from jax.experimental.pallas import tpu as pltpu

F32 = jnp.float32
BF16 = jnp.bfloat16
I32 = jnp.int32

EPS = 1e-6
NEG = -1e30

D_MODEL = 2048
GRID_W = 64
SSD_HEAD_DIM = 64
SSD_GROUPS = 8
SSD_HEADS_PER_GROUP = 8
SSD_STATE = 128
SSD_CONV = 5
D_INNER = 2 * D_MODEL
GROUP_W = SSD_HEADS_PER_GROUP * SSD_HEAD_DIM
SSD_Q = 128
CONV_HALO = 16
NA_HEAD_DIM = 64
NA_HEADS = D_MODEL // NA_HEAD_DIM
NA_PAIRS = NA_HEADS // 2
WIN_H = 8
WIN_W = 16
NA_ROWS = 4
MOE_GROUPS = 4
MOE_EPG = 8
N_EXPERTS = MOE_GROUPS * MOE_EPG
MOE_D_FF = D_MODEL // 4
MOE_TB = 256
DMA_UNROLL = 8

VMEM_LIMIT = 56 * 1024 * 1024
LANES = 128


def _cparams(sem):
    return pltpu.CompilerParams(dimension_semantics=sem, vmem_limit_bytes=VMEM_LIMIT)


def _silu(v):
    return v * pl.reciprocal(1.0 + jnp.exp(-v), approx=True)


def _softplus(v):
    return jnp.maximum(v, 0.0) + jnp.log1p(jnp.exp(-jnp.abs(v)))


def _split3(v):
    hi = v.astype(BF16)
    r1 = v - hi.astype(F32)
    mid = r1.astype(BF16)
    lo = (r1 - mid.astype(F32)).astype(BF16)
    return hi, mid, lo


def _dot(a, b):
    return jnp.dot(a, b, preferred_element_type=F32)


def _dot_nt(a, b):
    return lax.dot_general(a, b, (((1,), (1,)), ((), ())), preferred_element_type=F32)


def _dot3_left(v, sel):
    hi, mid, lo = _split3(v)
    return _dot(hi, sel) + _dot(mid, sel) + _dot(lo, sel)


def _normmod(x, nw, sc, sh):
    ms = jnp.mean(x * x, axis=-1, keepdims=True)
    return (x * lax.rsqrt(ms + EPS) * nw) * (1.0 + sc) + sh


def _ada_kernel(c_ref, w_ref, b_ref, o_ref):
    ca = _silu(c_ref[...])
    o_ref[0] = jnp.dot(ca, w_ref[0], precision=lax.Precision.HIGHEST,
                       preferred_element_type=F32) + b_ref[0]


def _ada(c_pad, ada_w, ada_b):
    depth, d, n = ada_w.shape
    tn = 1024
    return pl.pallas_call(
        _ada_kernel,
        out_shape=jax.ShapeDtypeStruct((depth, 8, n), F32),
        grid=(depth, n // tn),
        in_specs=[pl.BlockSpec((8, d), lambda i, j: (0, 0)),
                  pl.BlockSpec((1, d, tn), lambda i, j: (i, 0, j)),
                  pl.BlockSpec((1, 1, tn), lambda i, j: (i, 0, j))],
        out_specs=pl.BlockSpec((1, 8, tn), lambda i, j: (i, 0, j)),
        compiler_params=_cparams(("arbitrary", "arbitrary")),
        name="ada_mod",
    )(c_pad, ada_w, ada_b.reshape(depth, 1, n))


def _nm_mm_kernel(x_ref, nw_ref, sc_ref, sh_ref, w_ref, o_ref, h_ref, *, pair_major):
    @pl.when(pl.program_id(1) == 0)
    def _():
        h_ref[...] = _normmod(x_ref[...], nw_ref[...], sc_ref[0], sh_ref[0]).astype(BF16)

    r = _dot(h_ref[...], w_ref[...].astype(BF16))
    if pair_major:
        for c in range(o_ref.shape[0]):
            o_ref[c] = r[:, c * LANES:(c + 1) * LANES].astype(o_ref.dtype)
    else:
        o_ref[...] = r.astype(o_ref.dtype)


def _nm_matmul(x, nw, sc, sh, w, *, col0, ncols, tn, out_dtype, rows_per_batch, pair_major=False):
    t, d = x.shape
    tm = 1024
    tiles_per_batch = rows_per_batch // tm
    nj = ncols // tn
    jb = col0 // tn
    if pair_major:
        out_shape = jax.ShapeDtypeStruct((ncols // LANES, t, LANES), out_dtype)
        out_spec = pl.BlockSpec((tn // LANES, tm, LANES), lambda i, j: (j, i, 0))
    else:
        out_shape = jax.ShapeDtypeStruct((t, ncols), out_dtype)
        out_spec = pl.BlockSpec((tm, tn), lambda i, j: (i, j))
    return pl.pallas_call(
        functools.partial(_nm_mm_kernel, pair_major=pair_major),
        out_shape=out_shape,
        grid=(t // tm, nj),
        in_specs=[pl.BlockSpec((tm, d), lambda i, j: (i, 0)),
                  pl.BlockSpec((1, d), lambda i, j: (0, 0)),
                  pl.BlockSpec((1, 1, d), lambda i, j: (i // tiles_per_batch, 0, 0)),
                  pl.BlockSpec((1, 1, d), lambda i, j: (i // tiles_per_batch, 0, 0)),
                  pl.BlockSpec((d, tn), lambda i, j: (0, jb + j))],
        out_specs=out_spec,
        scratch_shapes=[pltpu.VMEM((tm, d), BF16)],
        compiler_params=_cparams(("arbitrary", "arbitrary")),
        name="norm_mod_matmul",
    )(x, nw, sc, sh, w)


def _mm_resid_kernel(a_ref, w_ref, x_ref, g_ref, o_ref, *, pair_major):
    if pair_major:
        a = jnp.concatenate([a_ref[c] for c in range(a_ref.shape[0])], axis=1)
    else:
        a = a_ref[...]
    o_ref[...] = x_ref[...] + g_ref[0] * _dot(a, w_ref[...].astype(BF16))


def _mm_resid(a, w, x, g, *, rows_per_batch, pair_major=False):
    t, n = x.shape
    k = w.shape[0]
    tm, tn = 1024, 512
    tiles_per_batch = rows_per_batch // tm
    if pair_major:
        a_spec = pl.BlockSpec((k // LANES, tm, LANES), lambda i, j: (0, i, 0))
    else:
        a_spec = pl.BlockSpec((tm, k), lambda i, j: (i, 0))
    return pl.pallas_call(
        functools.partial(_mm_resid_kernel, pair_major=pair_major),
        out_shape=jax.ShapeDtypeStruct((t, n), F32),
        grid=(t // tm, n // tn),
        in_specs=[a_spec,
                  pl.BlockSpec((k, tn), lambda i, j: (0, j)),
                  pl.BlockSpec((tm, tn), lambda i, j: (i, j)),
                  pl.BlockSpec((1, 1, tn), lambda i, j: (i // tiles_per_batch, 0, j))],
        out_specs=pl.BlockSpec((tm, tn), lambda i, j: (i, j)),
        compiler_params=_cparams(("arbitrary", "arbitrary")),
        name="matmul_resid",
    )(a, w, x, g)


def _ssd_kernel(z_ref, x_ref, b_ref, c_ref, dtr_ref,
                cwx_ref, cwb_ref, cwc_ref, cbx_ref, cbb_ref, cbc_ref,
                alr_ref, dbr_ref, dsk_ref, nw_ref,
                o_ref,
                xc_s, bc_s, cc_s, yacc_s, st_s, cv_s):
    seq = x_ref.shape[0]
    q = SSD_Q
    nc = seq // q
    halo = CONV_HALO
    nrow = 2 * SSD_HEADS_PER_GROUP

    def conv_piece(j, base, src_ref, w_ref, bias_ref, dst_ref, lo, stage):
        cols = slice(lo, lo + LANES)
        pstart = pl.multiple_of(jnp.maximum(base - halo, 0), halo)
        nstart = pl.multiple_of(jnp.minimum(base + q, seq - halo), halo)
        stage[0:halo, :] = jnp.where(j > 0, src_ref[pl.ds(pstart, halo), cols].astype(F32), 0.0)
        stage[halo:halo + q, :] = src_ref[pl.ds(base, q), cols].astype(F32)
        stage[halo + q:, :] = jnp.where(j < nc - 1, src_ref[pl.ds(nstart, halo), cols].astype(F32), 0.0)
        acc = jnp.broadcast_to(bias_ref[:, cols], (q, LANES))
        for k in range(SSD_CONV):
            first = halo - SSD_CONV // 2 + k
            acc = acc + w_ref[k:k + 1, cols] * stage[first:first + q, :]
        dst_ref[pl.ds(base, q), cols] = _silu(acc).astype(BF16)

    def conv_chunk(j, carry):
        base = pl.multiple_of(j * q, q)
        npx = GROUP_W // LANES
        for i in range(npx):
            conv_piece(j, base, x_ref, cwx_ref, cbx_ref, xc_s, i * LANES, cv_s.at[i])
        conv_piece(j, base, b_ref, cwb_ref, cbb_ref, bc_s, 0, cv_s.at[npx])
        conv_piece(j, base, c_ref, cwc_ref, cbc_ref, cc_s, 0, cv_s.at[npx + 1])
        return carry

    lax.fori_loop(0, nc, conv_chunk, 0)

    row_i = lax.broadcasted_iota(I32, (q, q), 0)
    col_i = lax.broadcasted_iota(I32, (q, q), 1)
    lower = row_i >= col_i
    upper = row_i <= col_i
    lower_b = lower.astype(BF16)
    upper_b = upper.astype(BF16)
    lane_w = lax.broadcasted_iota(I32, (q, GROUP_W), 1)
    even_head = (lane_w & (LANES - 1)) < SSD_HEAD_DIM
    left = lax.broadcasted_iota(I32, (q, LANES), 1) < SSD_HEAD_DIM
    a_row = -jnp.exp(alr_ref[...])
    pad_rows = jnp.zeros((LANES - nrow, q), F32)

    def scan_pass(direction):
        hoff = direction * SSD_HEADS_PER_GROUP
        mask = lower if direction == 0 else upper
        tri = upper_b if direction == 0 else lower_b
        edge = q - 1 if direction == 0 else 0
        st_s[...] = jnp.zeros_like(st_s)

        def chunk(t, carry):
            c = t if direction == 0 else nc - 1 - t
            base = pl.multiple_of(c * q, q)
            dt_r = _softplus(dtr_ref[c] + dbr_ref[...])
            cum_r = _dot3_left(dt_r * a_row, tri)
            cum_c = jnp.concatenate([cum_r, pad_rows], axis=0).T
            decdt_r = jnp.exp(cum_r[:, edge:edge + 1] - cum_r) * dt_r
            src_r = cum_r - jnp.log(dt_r)

            xcb = xc_s[pl.ds(base, q), :]
            zero_b = jnp.zeros_like(xcb)
            x_even = jnp.where(even_head, xcb, zero_b)
            x_odd = jnp.where(even_head, zero_b, xcb)
            bm = bc_s[pl.ds(base, q), :]
            cm = cc_s[pl.ds(base, q), :]
            cb = _dot_nt(cm, bm)
            bm_t = bm.astype(F32).T
            s_in = st_s[...]
            y_off = _dot(cm, s_in.astype(BF16))

            ys, sts = [], []
            for pp in range(SSD_HEADS_PER_GROUP // 2):
                cols = slice(pp * LANES, (pp + 1) * LANES)
                lhs_y, lhs_s, scales = [], [], []
                for par in range(2):
                    j = hoff + 2 * pp + par
                    cum_b = jnp.broadcast_to(cum_c[:, j:j + 1], (q, q))
                    lmat = jnp.exp(jnp.where(mask, cum_b - src_r[j:j + 1, :], NEG))
                    lhs_y.append((cb * lmat).astype(BF16))
                    lhs_s.append((bm_t * decdt_r[j:j + 1, :]).astype(BF16))
                    scales.append(jnp.exp(cum_b))
                x_rhs = jnp.concatenate([x_even[:, cols], x_odd[:, cols]], axis=0)
                sc_tile = jnp.where(left, scales[0], scales[1])
                ys.append(_dot(jnp.concatenate(lhs_y, axis=1), x_rhs) + y_off[:, cols] * sc_tile)
                sts.append(s_in[:, cols] * sc_tile[edge:edge + 1, :]
                           + _dot(jnp.concatenate(lhs_s, axis=1), x_rhs))
            y = jnp.concatenate(ys, axis=1)
            st_s[...] = jnp.concatenate(sts, axis=1)

            if direction == 0:
                yacc_s[pl.ds(base, q), :] = y
            else:
                total = yacc_s[pl.ds(base, q), :] + y + xcb.astype(F32) * dsk_ref[...]
                gated = total * _silu(z_ref[pl.ds(base, q), :].astype(F32))
                ms = jnp.mean(gated * gated, axis=-1, keepdims=True)
                o_ref[pl.ds(base, q), :] = (gated * lax.rsqrt(ms + EPS) * nw_ref[...]).astype(BF16)
            return carry

        lax.fori_loop(0, nc, chunk, 0, unroll=4)

    scan_pass(0)
    scan_pass(1)


def _ssd_core(zx, dt_row, conv_w, conv_b, al_row, db_row, dskip, norm_w, *, batch, seq):
    g = SSD_GROUPS
    nc = seq // SSD_Q
    xb = D_INNER // GROUP_W
    bb = (2 * D_INNER) // SSD_STATE
    cb = bb + g
    cwb = D_INNER // SSD_STATE
    cwc = cwb + g
    return pl.pallas_call(
        _ssd_kernel,
        out_shape=jax.ShapeDtypeStruct((batch * seq, D_INNER), BF16),
        grid=(batch, g),
        in_specs=[
            pl.BlockSpec((seq, GROUP_W), lambda b, i: (b, i)),
            pl.BlockSpec((seq, GROUP_W), lambda b, i: (b, xb + i)),
            pl.BlockSpec((seq, SSD_STATE), lambda b, i: (b, bb + i)),
            pl.BlockSpec((seq, SSD_STATE), lambda b, i: (b, cb + i)),
            pl.BlockSpec((None, None, nc, 2 * SSD_HEADS_PER_GROUP, SSD_Q), lambda b, i: (b, i, 0, 0, 0)),
            pl.BlockSpec((SSD_CONV, GROUP_W), lambda b, i: (0, i)),
            pl.BlockSpec((SSD_CONV, SSD_STATE), lambda b, i: (0, cwb + i)),
            pl.BlockSpec((SSD_CONV, SSD_STATE), lambda b, i: (0, cwc + i)),
            pl.BlockSpec((1, GROUP_W), lambda b, i: (0, i)),
            pl.BlockSpec((1, SSD_STATE), lambda b, i: (0, cwb + i)),
            pl.BlockSpec((1, SSD_STATE), lambda b, i: (0, cwc + i)),
            pl.BlockSpec((None, 2 * SSD_HEADS_PER_GROUP, 1), lambda b, i: (i, 0, 0)),
            pl.BlockSpec((None, 2 * SSD_HEADS_PER_GROUP, 1), lambda b, i: (i, 0, 0)),
            pl.BlockSpec((1, GROUP_W), lambda b, i: (0, i)),
            pl.BlockSpec((1, GROUP_W), lambda b, i: (0, i)),
        ],
        out_specs=pl.BlockSpec((seq, GROUP_W), lambda b, i: (b, i)),
        scratch_shapes=[pltpu.VMEM((seq, GROUP_W), BF16),
                        pltpu.VMEM((seq, SSD_STATE), BF16),
                        pltpu.VMEM((seq, SSD_STATE), BF16),
                        pltpu.VMEM((seq, GROUP_W), F32),
                        pltpu.VMEM((SSD_STATE, GROUP_W), F32),
                        pltpu.VMEM((GROUP_W // LANES + 2, SSD_Q + 2 * CONV_HALO, LANES), F32)],
        compiler_params=_cparams(("arbitrary", "arbitrary")),
        name="ssd_core",
    )(zx, zx, zx, zx, dt_row, conv_w, conv_w, conv_w, conv_b, conv_b, conv_b,
      al_row, db_row, dskip, norm_w)


def _ssd_mixer(x, mod_sc, mod_sh, mod_g, nw, w_in, conv_w, conv_b, a_log, dt_bias, d_skip, norm_w,
               w_out, *, batch, seq):
    g, r = SSD_GROUPS, SSD_HEADS_PER_GROUP
    conv_dim = conv_w.shape[1]
    zx = _nm_matmul(x, nw, mod_sc, mod_sh, w_in, col0=0, ncols=D_INNER + conv_dim, tn=1024,
                    out_dtype=BF16, rows_per_batch=seq)
    dt_raw = _nm_matmul(x, nw, mod_sc, mod_sh, w_in, col0=D_INNER + conv_dim, ncols=2 * g * r,
                        tn=2 * g * r, out_dtype=F32, rows_per_batch=seq)
    nc = seq // SSD_Q
    dt_row = dt_raw.reshape(batch, nc, SSD_Q, 2, g, r).transpose(0, 4, 1, 3, 5, 2)
    dt_row = dt_row.reshape(batch, g, nc, 2 * r, SSD_Q)

    def row_form(p):
        return p.reshape(2, g, r).transpose(1, 0, 2).reshape(g, 2 * r, 1)

    yn = _ssd_core(zx, dt_row, conv_w, conv_b.reshape(1, conv_dim), row_form(a_log), row_form(dt_bias),
                   jnp.repeat(d_skip, SSD_HEAD_DIM).reshape(1, D_INNER), norm_w.reshape(1, D_INNER),
                   batch=batch, seq=seq)
    return _mm_resid(yn, w_out, x, mod_g, rows_per_batch=seq)


def _bias_table_kernel(rpb_ref, o_ref):
    tn = o_ref.shape[1]
    col = lax.broadcasted_iota(I32, (1, tn), 1) + pl.program_id(0) * tn
    j = col >> 6
    c = col & (GRID_W - 1)
    c0 = jnp.clip(j - WIN_W // 2, 0, GRID_W - WIN_W)
    win = (c >= c0) & (c < c0 + WIN_W)
    want = jnp.where(win, c - j + (WIN_W - 1), -1)
    dx = lax.broadcasted_iota(I32, (LANES, tn), 0)
    onehot = (dx == want).astype(BF16)
    t = _dot3_left(rpb_ref[...], onehot)
    o_ref[...] = t + jnp.where(win, 0.0, NEG)


def _na_bias_table(rpb):
    h, ndy, ndx = rpb.shape
    rows = h * ndy
    rpb2 = jnp.pad(rpb.reshape(rows, ndx), ((0, 0), (0, LANES - ndx)))
    tn = 1024
    flat = pl.pallas_call(
        _bias_table_kernel,
        out_shape=jax.ShapeDtypeStruct((rows, GRID_W * GRID_W), F32),
        grid=(GRID_W * GRID_W // tn,),
        in_specs=[pl.BlockSpec((rows, LANES), lambda i: (0, 0))],
        out_specs=pl.BlockSpec((rows, tn), lambda i: (0, i)),
        compiler_params=_cparams(("arbitrary",)),
        name="na_bias_table",
    )(rpb2)
    t4 = flat.reshape(h, ndy, GRID_W, GRID_W)
    variants = [t4[:, d:d + WIN_H].transpose(0, 2, 1, 3).reshape(h, GRID_W, WIN_H * GRID_W)
                for d in range(WIN_H)]
    return jnp.stack(variants).reshape(WIN_H, h // 2, 2 * GRID_W, WIN_H * GRID_W)


def _na_kernel(q_ref, k0_ref, k1_ref, k2_ref, v0_ref, v1_ref, v2_ref, bias_ref, o_ref, kc_s, vc_s,
               s_s, p_s, r_s, *, n_row_blocks):
    rb = pl.program_id(2)
    npairs = q_ref.shape[0]
    blk = NA_ROWS * GRID_W
    nkeys = WIN_H * GRID_W
    for i, (kr, vr) in enumerate(((k0_ref, v0_ref), (k1_ref, v1_ref), (k2_ref, v2_ref))):
        kc_s[:, i * blk:(i + 1) * blk, :] = kr[...]
        vc_s[:, i * blk:(i + 1) * blk, :] = vr[...]
    first = rb == 0
    last = rb == n_row_blocks - 1
    edge = first | last
    lane = lax.broadcasted_iota(I32, (GRID_W, LANES), 1)
    left = lane < NA_HEAD_DIM
    scale = jnp.asarray(NA_HEAD_DIM ** -0.5, BF16)

    def window(qi):
        off = jnp.where(first, 0, jnp.where(last, blk, qi * GRID_W))
        li = jnp.where(edge, NA_ROWS - 1 - qi, NA_ROWS - 1)
        return pl.multiple_of(off, GRID_W), li

    def pair_body(pp, carry):
        for qi in range(NA_ROWS):
            off, li = window(qi)
            q2 = q_ref[pp, qi * GRID_W:(qi + 1) * GRID_W, :] * scale
            zero = jnp.zeros_like(q2)
            qs = jnp.concatenate([jnp.where(left, q2, zero), jnp.where(left, zero, q2)], axis=0)
            kw = kc_s[pp, pl.ds(off, nkeys), :]
            s_s[qi] = _dot_nt(qs, kw) + bias_ref[li, pp]
        for qi in range(NA_ROWS):
            s = s_s[qi]
            p = jnp.exp(s - jnp.max(s, axis=-1, keepdims=True))
            r_s[qi] = 1.0 / jnp.sum(p, axis=-1, keepdims=True)
            p_s[qi] = p.astype(BF16)
        for qi in range(NA_ROWS):
            off, _ = window(qi)
            pv = _dot(p_s[qi], vc_s[pp, pl.ds(off, nkeys), :]) * r_s[qi]
            o = jnp.where(left, pv[0:GRID_W], pv[GRID_W:2 * GRID_W])
            o_ref[pp, qi * GRID_W:(qi + 1) * GRID_W, :] = o.astype(BF16)
        return carry

    lax.fori_loop(0, npairs, pair_body, 0, unroll=4)


def _na_attention(qkv_t, bias_tab, *, batch, seq):
    t = batch * seq
    blk = NA_ROWS * GRID_W
    nrb = seq // blk
    hp = NA_PAIRS // 2
    nsec = NA_PAIRS // hp

    def kv_spec(sec, i):
        def imap(hh, b, r):
            return (sec * nsec + hh, b * nrb + jnp.clip(r - 1, 0, nrb - 3) + i, 0)
        return pl.BlockSpec((hp, blk, LANES), imap)

    return pl.pallas_call(
        functools.partial(_na_kernel, n_row_blocks=nrb),
        out_shape=jax.ShapeDtypeStruct((NA_PAIRS, t, LANES), BF16),
        grid=(nsec, batch, nrb),
        in_specs=[pl.BlockSpec((hp, blk, LANES), lambda hh, b, r: (hh, b * nrb + r, 0)),
                  kv_spec(1, 0), kv_spec(1, 1), kv_spec(1, 2),
                  kv_spec(2, 0), kv_spec(2, 1), kv_spec(2, 2),
                  pl.BlockSpec((NA_ROWS, hp, 2 * GRID_W, WIN_H * GRID_W),
                               lambda hh, b, r: (jnp.where(r == 0, 1, 0), hh, 0, 0))],
        out_specs=pl.BlockSpec((hp, blk, LANES), lambda hh, b, r: (hh, b * nrb + r, 0)),
        scratch_shapes=[pltpu.VMEM((hp, 3 * blk, LANES), BF16),
                        pltpu.VMEM((hp, 3 * blk, LANES), BF16),
                        pltpu.VMEM((NA_ROWS, 2 * GRID_W, WIN_H * GRID_W), F32),
                        pltpu.VMEM((NA_ROWS, 2 * GRID_W, WIN_H * GRID_W), BF16),
                        pltpu.VMEM((NA_ROWS, 2 * GRID_W, 1), F32)],
        compiler_params=_cparams(("arbitrary", "arbitrary", "arbitrary")),
        name="na_attention",
    )(qkv_t, qkv_t, qkv_t, qkv_t, qkv_t, qkv_t, qkv_t, bias_tab)


def _na_mixer(x, mod_sc, mod_sh, mod_g, nw, w_qkv, rpb, w_o, *, batch, seq):
    qkv_t = _nm_matmul(x, nw, mod_sc, mod_sh, w_qkv, col0=0, ncols=3 * D_MODEL, tn=1024,
                       out_dtype=BF16, rows_per_batch=seq, pair_major=True)
    o_t = _na_attention(qkv_t, _na_bias_table(rpb), batch=batch, seq=seq)
    return _mm_resid(o_t, w_o, x, mod_g, rows_per_batch=seq, pair_major=True)


U32 = jnp.uint32


def _pack_halves(vb):
    n = vb.shape[1] // 2
    bits = pltpu.bitcast(vb.astype(F32), U32)
    return (bits[:, :n] >> 16) | bits[:, n:]


def _unpack_halves(w):
    return pltpu.bitcast(w << 16, F32), pltpu.bitcast(w & U32(0xFFFF0000), F32)

def _router_kernel(x_ref, nw_ref, sc_ref, sh_ref, wr_ref, h_ref, meta_ref, cnt_ref, carry_s):
    @pl.when(pl.program_id(0) == 0)
    def _():
        carry_s[...] = jnp.zeros_like(carry_s)

    h = _normmod(x_ref[...], nw_ref[...], sc_ref[0], sh_ref[0])
    h_hi = h.astype(BF16)
    h_ref[...] = _pack_halves(h_hi)
    h_lo = (h - h_hi.astype(F32)).astype(BF16)
    wr = wr_ref[...]
    w_hi = wr.astype(BF16)
    w_lo = (wr - w_hi.astype(F32)).astype(BF16)
    logits = _dot(h_hi, w_hi) + (_dot(h_lo, w_hi) + _dot(h_hi, w_lo))
    tm = logits.shape[0]
    lane_i = lax.broadcasted_iota(I32, logits.shape, 1)
    lane = lane_i.astype(F32)
    big = 1e9
    gl = jnp.where(lane_i < MOE_GROUPS, logits, NEG)
    gmax = jnp.max(gl, axis=1, keepdims=True)
    gsel = jnp.min(jnp.where(gl == gmax, lane, big), axis=1, keepdims=True)
    gw = 1.0 / jnp.sum(jnp.exp(gl - gmax), axis=1, keepdims=True)
    el = lane - MOE_GROUPS
    lo = gsel * MOE_EPG
    emask = (el >= lo) & (el < lo + MOE_EPG)
    e1 = jnp.where(emask, logits, NEG)
    m1 = jnp.max(e1, axis=1, keepdims=True)
    i1 = jnp.min(jnp.where(e1 == m1, el, big), axis=1, keepdims=True)
    e2 = jnp.where(emask & (el != i1), logits, NEG)
    m2 = jnp.max(e2, axis=1, keepdims=True)
    i2 = jnp.min(jnp.where(e2 == m2, el, big), axis=1, keepdims=True)
    tt = jnp.exp(m2 - m1)
    p1 = 1.0 / (1.0 + tt)
    w1 = gw * p1
    w2 = gw * (tt * p1)
    oh1 = el == i1
    oh2 = el == i2
    cnt = (oh1 | oh2).astype(F32)
    r_i = lax.broadcasted_iota(I32, (tm, tm), 0)
    c_i = lax.broadcasted_iota(I32, (tm, tm), 1)
    before = _dot((r_i > c_i).astype(BF16), cnt.astype(BF16)) + carry_s[...]
    rank1 = jnp.sum(jnp.where(oh1, before, 0.0), axis=1, keepdims=True)
    rank2 = jnp.sum(jnp.where(oh2, before, 0.0), axis=1, keepdims=True)
    carry_s[...] = carry_s[...] + jnp.sum(cnt, axis=0, keepdims=True)
    meta = jnp.zeros_like(logits)
    for pos, val in enumerate((i1, i2, w1, w2, rank1, rank2)):
        meta = jnp.where(lane_i == pos, val, meta)
    meta_ref[...] = meta
    cnt_ref[...] = jnp.broadcast_to(carry_s[...], cnt_ref.shape)


def _router(x, nw, sc, sh, wr, *, rows_per_batch):
    t, d = x.shape
    tm = 256
    tiles_per_batch = rows_per_batch // tm
    return pl.pallas_call(
        _router_kernel,
        out_shape=(jax.ShapeDtypeStruct((t, d // 2), U32),
                   jax.ShapeDtypeStruct((t, LANES), F32),
                   jax.ShapeDtypeStruct((8, LANES), F32)),
        grid=(t // tm,),
        in_specs=[pl.BlockSpec((tm, d), lambda i: (i, 0)),
                  pl.BlockSpec((1, d), lambda i: (0, 0)),
                  pl.BlockSpec((1, 1, d), lambda i: (i // tiles_per_batch, 0, 0)),
                  pl.BlockSpec((1, 1, d), lambda i: (i // tiles_per_batch, 0, 0)),
                  pl.BlockSpec((d, LANES), lambda i: (0, 0))],
        out_specs=(pl.BlockSpec((tm, d // 2), lambda i: (i, 0)),
                   pl.BlockSpec((tm, LANES), lambda i: (i, 0)),
                   pl.BlockSpec((8, LANES), lambda i: (0, 0))),
        scratch_shapes=[pltpu.VMEM((1, LANES), F32)],
        compiler_params=_cparams(("arbitrary",)),
        name="moe_router",
    )(x, nw, sc, sh, wr)


def _dispatch_kernel(dest_ref, h_ref, xs_in_ref, xs_ref, sem):
    del xs_in_ref
    tm = h_ref.shape[0]
    base = pl.program_id(0) * tm

    def copy(t, k):
        return pltpu.make_async_copy(h_ref.at[pl.ds(t, 1)],
                                     xs_ref.at[pl.ds(dest_ref[(base + t) * 2 + k], 1)], sem)

    def issue(t, carry):
        copy(t, 0).start()
        copy(t, 1).start()
        return carry

    def drain(t, carry):
        copy(t, 0).wait()
        copy(t, 1).wait()
        return carry

    lax.fori_loop(0, tm, issue, 0, unroll=DMA_UNROLL)
    lax.fori_loop(0, tm, drain, 0, unroll=DMA_UNROLL)


def _dispatch(dest, h, n_slots):
    t, d = h.shape
    tm = 256
    return pl.pallas_call(
        _dispatch_kernel,
        out_shape=jax.ShapeDtypeStruct((n_slots, d), U32),
        grid_spec=pltpu.PrefetchScalarGridSpec(
            num_scalar_prefetch=1,
            grid=(t // tm,),
            in_specs=[pl.BlockSpec((tm, d), lambda i, dest: (i, 0)),
                      pl.BlockSpec(memory_space=pl.ANY)],
            out_specs=pl.BlockSpec(memory_space=pl.ANY),
            scratch_shapes=[pltpu.SemaphoreType.DMA(())]),
        input_output_aliases={2: 0},
        compiler_params=_cparams(("arbitrary",)),
        name="moe_dispatch",
    )(dest, h, jnp.zeros((n_slots, d), U32))


def _ffn_kernel(be_ref, nu_ref, xs_ref, w1_ref, w3_ref, w2_ref, o_ref, w1_s, w3_s, w2_s):
    i = pl.program_id(0)

    @pl.when(i < nu_ref[0])
    def _():
        @pl.when((i == 0) | (be_ref[i] != be_ref[jnp.maximum(i - 1, 0)]))
        def _():
            w1_s[...] = w1_ref[0].astype(BF16)
            w3_s[...] = w3_ref[0].astype(BF16)
            w2_s[...] = w2_ref[0].astype(BF16)

        lo, hi = _unpack_halves(xs_ref[...])
        xl, xh = lo.astype(BF16), hi.astype(BF16)
        half = xl.shape[1]
        a = _dot(xl, w1_s[0:half, :]) + _dot(xh, w1_s[half:2 * half, :])
        b = _dot(xl, w3_s[0:half, :]) + _dot(xh, w3_s[half:2 * half, :])
        hmid = _silu(a) * b
        o_ref[...] = _pack_halves(_dot(hmid.astype(BF16), w2_s[...]).astype(BF16))

    @pl.when(i >= nu_ref[0])
    def _():
        o_ref[...] = jnp.zeros_like(o_ref)


def _expert_ffn(blk_expert, n_used, xs, w1, w3, w2, layer):
    p, dh = xs.shape
    d = 2 * dh
    f = w1.shape[3]
    tb = MOE_TB
    nb = p // tb

    def row_map(i, be, nu):
        return (jnp.minimum(i, nu[0] - 1), 0)

    def w_map(i, be, nu):
        return (layer, be[i], 0, 0)

    return pl.pallas_call(
        _ffn_kernel,
        out_shape=jax.ShapeDtypeStruct((p, dh), U32),
        grid_spec=pltpu.PrefetchScalarGridSpec(
            num_scalar_prefetch=2,
            grid=(nb,),
            in_specs=[pl.BlockSpec((tb, dh), row_map),
                      pl.BlockSpec((None, 1, d, f), w_map),
                      pl.BlockSpec((None, 1, d, f), w_map),
                      pl.BlockSpec((None, 1, f, d), w_map)],
            out_specs=pl.BlockSpec((tb, dh), lambda i, be, nu: (i, 0)),
            scratch_shapes=[pltpu.VMEM((d, f), BF16), pltpu.VMEM((d, f), BF16),
                            pltpu.VMEM((f, d), BF16)]),
        compiler_params=_cparams(("arbitrary",)),
        name="moe_expert_ffn",
    )(blk_expert, n_used, xs, w1, w3, w2)


def _combine_kernel(dest_ref, x_ref, meta_ref, g_ref, fnw_ref, ys_ref, o_ref, buf, sem, *, final):
    tm = x_ref.shape[0]
    base = pl.program_id(0) * tm

    def copy(t, k):
        return pltpu.make_async_copy(ys_ref.at[pl.ds(dest_ref[(base + t) * 2 + k], 1)],
                                     buf.at[k, pl.ds(t, 1)], sem)

    def issue(t, carry):
        copy(t, 0).start()
        copy(t, 1).start()
        return carry

    def drain(t, carry):
        copy(t, 0).wait()
        copy(t, 1).wait()
        return carry

    lax.fori_loop(0, tm, issue, 0, unroll=DMA_UNROLL)
    lax.fori_loop(0, tm, drain, 0, unroll=DMA_UNROLL)
    meta = meta_ref[...]
    w1, w2 = meta[:, 2:3], meta[:, 3:4]
    lo1, hi1 = _unpack_halves(buf[0])
    lo2, hi2 = _unpack_halves(buf[1])
    y = jnp.concatenate([w1 * lo1 + w2 * lo2, w1 * hi1 + w2 * hi2], axis=1)
    xn = x_ref[...] + g_ref[0] * y
    if final:
        ms = jnp.mean(xn * xn, axis=-1, keepdims=True)
        xn = xn * lax.rsqrt(ms + EPS) * fnw_ref[...]
    o_ref[...] = xn


def _combine(dest, x, meta, g, fnw, ys, *, rows_per_batch, final):
    t, d = x.shape
    tm = 256
    tiles_per_batch = rows_per_batch // tm
    return pl.pallas_call(
        functools.partial(_combine_kernel, final=final),
        out_shape=jax.ShapeDtypeStruct((t, d), F32),
        grid_spec=pltpu.PrefetchScalarGridSpec(
            num_scalar_prefetch=1,
            grid=(t // tm,),
            in_specs=[pl.BlockSpec((tm, d), lambda i, dest: (i, 0)),
                      pl.BlockSpec((tm, LANES), lambda i, dest: (i, 0)),
                      pl.BlockSpec((1, 1, d), lambda i, dest: (i // tiles_per_batch, 0, 0)),
                      pl.BlockSpec((1, d), lambda i, dest: (0, 0)),
                      pl.BlockSpec(memory_space=pl.ANY)],
            out_specs=pl.BlockSpec((tm, d), lambda i, dest: (i, 0)),
            scratch_shapes=[pltpu.VMEM((2, tm, d // 2), U32), pltpu.SemaphoreType.DMA(())]),
        compiler_params=_cparams(("arbitrary",)),
        name="moe_combine",
    )(dest, x, meta, g, fnw, ys)


def _hier_moe(x, mod_sc, mod_sh, mod_g, nw, w_group, w_expert, w1, w3, w2, layer, fnw, *,
              rows_per_batch, final):
    t, d = x.shape
    a = 2 * t
    tb = MOE_TB
    wr = jnp.concatenate([w_group, w_expert], axis=1)
    wr = jnp.pad(wr, ((0, 0), (0, LANES - wr.shape[1])))
    h, meta, cnt = _router(x, nw, mod_sc, mod_sh, wr, rows_per_batch=rows_per_batch)
    counts = cnt[0, MOE_GROUPS:MOE_GROUPS + N_EXPERTS].astype(I32)
    padded = ((counts + tb - 1) // tb) * tb
    pend = jnp.cumsum(padded)
    pstart = pend - padded
    eid = meta[:, 0:2].astype(I32)
    rank = meta[:, 4:6].astype(I32)
    dest = (pstart[eid] + rank).reshape(a)
    nb = (a + N_EXPERTS * (tb - 1) + tb - 1) // tb
    n_used = (pend[-1] // tb).astype(I32)
    blk = jnp.arange(nb, dtype=I32)
    be = jnp.minimum(jnp.searchsorted(pend, blk * tb, side='right'), N_EXPERTS - 1).astype(I32)
    be = jnp.where(blk < n_used, be, be[jnp.maximum(n_used - 1, 0)])
    xs = _dispatch(dest, h, nb * tb)
    ys = _expert_ffn(be, n_used.reshape(1), xs, w1, w3, w2, layer)
    return _combine(dest, x, meta, mod_g, fnw, ys, rows_per_batch=rows_per_batch, final=final)


def kernel(x, c, ada_w, ada_b, norm_mix, norm_ffn, ssd_w_in, ssd_conv_w, ssd_conv_b, ssd_a_log,
           ssd_dt_bias, ssd_d, ssd_norm_w, ssd_w_out, na_w_qkv, na_rpb, na_w_o,
           moe_w_group, moe_w_expert, moe_w1, moe_w3, moe_w2, final_norm):
    batch, seq, d = x.shape
    depth = ada_w.shape[0]
    xt = x.reshape(batch * seq, d)
    c_pad = jnp.pad(c, ((0, 8 - batch), (0, 0)))
    mod = _ada(c_pad, ada_w, ada_b)[:, :batch]
    fnw = final_norm.reshape(1, d)
    for i in range(depth):
        sh1, sc1, g1, sh2, sc2, g2 = [mod[i, :, k * d:(k + 1) * d].reshape(batch, 1, d)
                                      for k in range(6)]
        j = i // 2
        nw = norm_mix[i].reshape(1, d)
        if i % 2 == 0:
            xt = _ssd_mixer(xt, sc1, sh1, g1, nw, ssd_w_in[j], ssd_conv_w[j], ssd_conv_b[j],
                            ssd_a_log[j], ssd_dt_bias[j], ssd_d[j], ssd_norm_w[j], ssd_w_out[j],
                            batch=batch, seq=seq)
        else:
            xt = _na_mixer(xt, sc1, sh1, g1, nw, na_w_qkv[j], na_rpb[j], na_w_o[j],
                           batch=batch, seq=seq)
        xt = _hier_moe(xt, sc2, sh2, g2, norm_ffn[i].reshape(1, d), moe_w_group[i], moe_w_expert[i],
                       moe_w1, moe_w3, moe_w2, i, fnw, rows_per_batch=seq,
                       final=(i == depth - 1))
    return xt.reshape(batch, seq, d)
```

```python
import functools

import jax
import jax.numpy as jnp
from jax import lax
from jax.experimental import pallas as pl
from jax.experimental.pallas import tpu as pltpu

F32 = jnp.float32
BF16 = jnp.bfloat16
I32 = jnp.int32

EPS = 1e-6
NEG = -1e30

D_MODEL = 2048
GRID_W = 64
SSD_HEAD_DIM = 64
SSD_GROUPS = 8
SSD_HEADS_PER_GROUP = 8
SSD_STATE = 128
SSD_CONV = 5
D_INNER = 2 * D_MODEL
GROUP_W = SSD_HEADS_PER_GROUP * SSD_HEAD_DIM
SSD_Q = 128
CONV_HALO = 16
NA_HEAD_DIM = 64
NA_HEADS = D_MODEL // NA_HEAD_DIM
NA_PAIRS = NA_HEADS // 2
WIN_H = 8
WIN_W = 16
NA_ROWS = 4
MOE_GROUPS = 4
MOE_EPG = 8
N_EXPERTS = MOE_GROUPS * MOE_EPG
MOE_D_FF = D_MODEL // 4
MOE_TB = 256
DMA_UNROLL = 8

VMEM_LIMIT = 56 * 1024 * 1024
LANES = 128


def _cparams(sem):
    return pltpu.CompilerParams(dimension_semantics=sem, vmem_limit_bytes=VMEM_LIMIT)


def _silu(v):
    return v * pl.reciprocal(1.0 + jnp.exp(-v), approx=True)


def _softplus(v):
    return jnp.maximum(v, 0.0) + jnp.log1p(jnp.exp(-jnp.abs(v)))


def _split3(v):
    hi = v.astype(BF16)
    r1 = v - hi.astype(F32)
    mid = r1.astype(BF16)
    lo = (r1 - mid.astype(F32)).astype(BF16)
    return hi, mid, lo


def _dot(a, b):
    return jnp.dot(a, b, preferred_element_type=F32)


def _dot_nt(a, b):
    return lax.dot_general(a, b, (((1,), (1,)), ((), ())), preferred_element_type=F32)


def _dot3_left(v, sel):
    hi, mid, lo = _split3(v)
    return _dot(hi, sel) + _dot(mid, sel) + _dot(lo, sel)


def _normmod(x, nw, sc, sh):
    ms = jnp.mean(x * x, axis=-1, keepdims=True)
    return (x * lax.rsqrt(ms + EPS) * nw) * (1.0 + sc) + sh


def _dot_split(a, b):
    a_hi = a.astype(BF16)
    a_lo = (a - a_hi.astype(F32)).astype(BF16)
    b_hi = b.astype(BF16)
    b_lo = (b - b_hi.astype(F32)).astype(BF16)
    return _dot(a_hi, b_hi) + (_dot(a_lo, b_hi) + _dot(a_hi, b_lo))


def _ada_kernel(c_ref, w_ref, b_ref, o_ref):
    c = c_ref[...]
    o_ref[0] = _dot_split(c / (1.0 + jnp.exp(-c)), w_ref[0]) + b_ref[0]


def _ada(c_pad, ada_w, ada_b):
    depth, d, n = ada_w.shape
    tn = 1024
    return pl.pallas_call(
        _ada_kernel,
        out_shape=jax.ShapeDtypeStruct((depth, 8, n), F32),
        grid=(depth, n // tn),
        in_specs=[pl.BlockSpec((8, d), lambda i, j: (0, 0)),
                  pl.BlockSpec((1, d, tn), lambda i, j: (i, 0, j)),
                  pl.BlockSpec((1, 1, tn), lambda i, j: (i, 0, j))],
        out_specs=pl.BlockSpec((1, 8, tn), lambda i, j: (i, 0, j)),
        compiler_params=_cparams(("arbitrary", "arbitrary")),
        name="ada_mod",
    )(c_pad, ada_w, ada_b.reshape(depth, 1, n))


def _nm_mm_kernel(x_ref, nw_ref, sc_ref, sh_ref, w_ref, o_ref, h_ref, *, pair_major):
    @pl.when(pl.program_id(1) == 0)
    def _():
        h_ref[...] = _normmod(x_ref[...], nw_ref[...], sc_ref[0], sh_ref[0]).astype(BF16)

    r = _dot(h_ref[...], w_ref[...].astype(BF16))
    if pair_major:
        for c in range(o_ref.shape[0]):
            o_ref[c] = r[:, c * LANES:(c + 1) * LANES].astype(o_ref.dtype)
    else:
        o_ref[...] = r.astype(o_ref.dtype)


def _nm_matmul(x, nw, sc, sh, w, *, col0, ncols, tn, out_dtype, rows_per_batch, pair_major=False):
    t, d = x.shape
    tm = 1024
    tiles_per_batch = rows_per_batch // tm
    nj = ncols // tn
    jb = col0 // tn
    if pair_major:
        out_shape = jax.ShapeDtypeStruct((ncols // LANES, t, LANES), out_dtype)
        out_spec = pl.BlockSpec((tn // LANES, tm, LANES), lambda i, j: (j, i, 0))
    else:
        out_shape = jax.ShapeDtypeStruct((t, ncols), out_dtype)
        out_spec = pl.BlockSpec((tm, tn), lambda i, j: (i, j))
    return pl.pallas_call(
        functools.partial(_nm_mm_kernel, pair_major=pair_major),
        out_shape=out_shape,
        grid=(t // tm, nj),
        in_specs=[pl.BlockSpec((tm, d), lambda i, j: (i, 0)),
                  pl.BlockSpec((1, d), lambda i, j: (0, 0)),
                  pl.BlockSpec((1, 1, d), lambda i, j: (i // tiles_per_batch, 0, 0)),
                  pl.BlockSpec((1, 1, d), lambda i, j: (i // tiles_per_batch, 0, 0)),
                  pl.BlockSpec((d, tn), lambda i, j: (0, jb + j))],
        out_specs=out_spec,
        scratch_shapes=[pltpu.VMEM((tm, d), BF16)],
        compiler_params=_cparams(("arbitrary", "arbitrary")),
        name="norm_mod_matmul",
    )(x, nw, sc, sh, w)


def _mm_resid_kernel(a_ref, w_ref, x_ref, g_ref, o_ref, *, pair_major):
    if pair_major:
        a = jnp.concatenate([a_ref[c] for c in range(a_ref.shape[0])], axis=1)
    else:
        a = a_ref[...]
    o_ref[...] = x_ref[...] + g_ref[0] * _dot(a, w_ref[...].astype(BF16))


def _mm_resid(a, w, x, g, *, rows_per_batch, pair_major=False):
    t, n = x.shape
    k = w.shape[0]
    tm, tn = 1024, 512
    tiles_per_batch = rows_per_batch // tm
    if pair_major:
        a_spec = pl.BlockSpec((k // LANES, tm, LANES), lambda i, j: (0, i, 0))
    else:
        a_spec = pl.BlockSpec((tm, k), lambda i, j: (i, 0))
    return pl.pallas_call(
        functools.partial(_mm_resid_kernel, pair_major=pair_major),
        out_shape=jax.ShapeDtypeStruct((t, n), F32),
        grid=(t // tm, n // tn),
        in_specs=[a_spec,
                  pl.BlockSpec((k, tn), lambda i, j: (0, j)),
                  pl.BlockSpec((tm, tn), lambda i, j: (i, j)),
                  pl.BlockSpec((1, 1, tn), lambda i, j: (i // tiles_per_batch, 0, j))],
        out_specs=pl.BlockSpec((tm, tn), lambda i, j: (i, j)),
        compiler_params=_cparams(("arbitrary", "arbitrary")),
        name="matmul_resid",
    )(a, w, x, g)


def _ssd_kernel(z_ref, x_ref, b_ref, c_ref, dtr_ref,
                cwx_ref, cwb_ref, cwc_ref, cbx_ref, cbb_ref, cbc_ref,
                alr_ref, dbr_ref, dsk_ref, nw_ref,
                o_ref,
                xc_s, bc_s, cc_s, yacc_s, st_s, cv_s):
    seq = x_ref.shape[0]
    q = SSD_Q
    nc = seq // q
    halo = CONV_HALO
    nrow = 2 * SSD_HEADS_PER_GROUP

    def conv_piece(j, base, src_ref, w_ref, bias_ref, dst_ref, lo, stage):
        cols = slice(lo, lo + LANES)
        pstart = pl.multiple_of(jnp.maximum(base - halo, 0), halo)
        nstart = pl.multiple_of(jnp.minimum(base + q, seq - halo), halo)
        stage[0:halo, :] = jnp.where(j > 0, src_ref[pl.ds(pstart, halo), cols].astype(F32), 0.0)
        stage[halo:halo + q, :] = src_ref[pl.ds(base, q), cols].astype(F32)
        stage[halo + q:, :] = jnp.where(j < nc - 1, src_ref[pl.ds(nstart, halo), cols].astype(F32), 0.0)
        acc = jnp.broadcast_to(bias_ref[:, cols], (q, LANES))
        for k in range(SSD_CONV):
            first = halo - SSD_CONV // 2 + k
            acc = acc + w_ref[k:k + 1, cols] * stage[first:first + q, :]
        dst_ref[pl.ds(base, q), cols] = _silu(acc).astype(BF16)

    def conv_chunk(j, carry):
        base = pl.multiple_of(j * q, q)
        npx = GROUP_W // LANES
        for i in range(npx):
            conv_piece(j, base, x_ref, cwx_ref, cbx_ref, xc_s, i * LANES, cv_s.at[i])
        conv_piece(j, base, b_ref, cwb_ref, cbb_ref, bc_s, 0, cv_s.at[npx])
        conv_piece(j, base, c_ref, cwc_ref, cbc_ref, cc_s, 0, cv_s.at[npx + 1])
        return carry

    lax.fori_loop(0, nc, conv_chunk, 0)

    row_i = lax.broadcasted_iota(I32, (q, q), 0)
    col_i = lax.broadcasted_iota(I32, (q, q), 1)
    lower = row_i >= col_i
    upper = row_i <= col_i
    lower_b = lower.astype(BF16)
    upper_b = upper.astype(BF16)
    lane_w = lax.broadcasted_iota(I32, (q, GROUP_W), 1)
    even_head = (lane_w & (LANES - 1)) < SSD_HEAD_DIM
    left = lax.broadcasted_iota(I32, (q, LANES), 1) < SSD_HEAD_DIM
    a_row = -jnp.exp(alr_ref[...])
    pad_rows = jnp.zeros((LANES - nrow, q), F32)

    def scan_pass(direction):
        hoff = direction * SSD_HEADS_PER_GROUP
        mask = lower if direction == 0 else upper
        tri = upper_b if direction == 0 else lower_b
        edge = q - 1 if direction == 0 else 0
        st_s[...] = jnp.zeros_like(st_s)

        def chunk(t, carry):
            c = t if direction == 0 else nc - 1 - t
            base = pl.multiple_of(c * q, q)
            dt_r = _softplus(dtr_ref[c] + dbr_ref[...])
            cum_r = _dot3_left(dt_r * a_row, tri)
            cum_c = jnp.concatenate([cum_r, pad_rows], axis=0).T
            decdt_r = jnp.exp(cum_r[:, edge:edge + 1] - cum_r) * dt_r
            src_r = cum_r - jnp.log(dt_r)

            xcb = xc_s[pl.ds(base, q), :]
            zero_b = jnp.zeros_like(xcb)
            x_even = jnp.where(even_head, xcb, zero_b)
            x_odd = jnp.where(even_head, zero_b, xcb)
            bm = bc_s[pl.ds(base, q), :]
            cm = cc_s[pl.ds(base, q), :]
            cb = _dot_nt(cm, bm)
            bm_t = bm.astype(F32).T
            s_in = st_s[...]
            y_off = _dot(cm, s_in.astype(BF16))

            ys, sts = [], []
            for pp in range(SSD_HEADS_PER_GROUP // 2):
                cols = slice(pp * LANES, (pp + 1) * LANES)
                lhs_y, lhs_s, scales = [], [], []
                for par in range(2):
                    j = hoff + 2 * pp + par
                    cum_b = jnp.broadcast_to(cum_c[:, j:j + 1], (q, q))
                    lmat = jnp.exp(jnp.where(mask, cum_b - src_r[j:j + 1, :], NEG))
                    lhs_y.append((cb * lmat).astype(BF16))
                    lhs_s.append((bm_t * decdt_r[j:j + 1, :]).astype(BF16))
                    scales.append(jnp.exp(cum_b))
                x_rhs = jnp.concatenate([x_even[:, cols], x_odd[:, cols]], axis=0)
                sc_tile = jnp.where(left, scales[0], scales[1])
                ys.append(_dot(jnp.concatenate(lhs_y, axis=1), x_rhs) + y_off[:, cols] * sc_tile)
                sts.append(s_in[:, cols] * sc_tile[edge:edge + 1, :]
                           + _dot(jnp.concatenate(lhs_s, axis=1), x_rhs))
            y = jnp.concatenate(ys, axis=1)
            st_s[...] = jnp.concatenate(sts, axis=1)

            if direction == 0:
                yacc_s[pl.ds(base, q), :] = y
            else:
                total = yacc_s[pl.ds(base, q), :] + y + xcb.astype(F32) * dsk_ref[...]
                gated = total * _silu(z_ref[pl.ds(base, q), :].astype(F32))
                ms = jnp.mean(gated * gated, axis=-1, keepdims=True)
                o_ref[pl.ds(base, q), :] = (gated * lax.rsqrt(ms + EPS) * nw_ref[...]).astype(BF16)
            return carry

        lax.fori_loop(0, nc, chunk, 0, unroll=4)

    scan_pass(0)
    scan_pass(1)


def _ssd_core(zx, dt_row, conv_w, conv_b, al_row, db_row, dskip, norm_w, *, batch, seq):
    g = SSD_GROUPS
    nc = seq // SSD_Q
    xb = D_INNER // GROUP_W
    bb = (2 * D_INNER) // SSD_STATE
    cb = bb + g
    cwb = D_INNER // SSD_STATE
    cwc = cwb + g
    return pl.pallas_call(
        _ssd_kernel,
        out_shape=jax.ShapeDtypeStruct((batch * seq, D_INNER), BF16),
        grid=(batch, g),
        in_specs=[
            pl.BlockSpec((seq, GROUP_W), lambda b, i: (b, i)),
            pl.BlockSpec((seq, GROUP_W), lambda b, i: (b, xb + i)),
            pl.BlockSpec((seq, SSD_STATE), lambda b, i: (b, bb + i)),
            pl.BlockSpec((seq, SSD_STATE), lambda b, i: (b, cb + i)),
            pl.BlockSpec((None, None, nc, 2 * SSD_HEADS_PER_GROUP, SSD_Q), lambda b, i: (b, i, 0, 0, 0)),
            pl.BlockSpec((SSD_CONV, GROUP_W), lambda b, i: (0, i)),
            pl.BlockSpec((SSD_CONV, SSD_STATE), lambda b, i: (0, cwb + i)),
            pl.BlockSpec((SSD_CONV, SSD_STATE), lambda b, i: (0, cwc + i)),
            pl.BlockSpec((1, GROUP_W), lambda b, i: (0, i)),
            pl.BlockSpec((1, SSD_STATE), lambda b, i: (0, cwb + i)),
            pl.BlockSpec((1, SSD_STATE), lambda b, i: (0, cwc + i)),
            pl.BlockSpec((None, 2 * SSD_HEADS_PER_GROUP, 1), lambda b, i: (i, 0, 0)),
            pl.BlockSpec((None, 2 * SSD_HEADS_PER_GROUP, 1), lambda b, i: (i, 0, 0)),
            pl.BlockSpec((1, GROUP_W), lambda b, i: (0, i)),
            pl.BlockSpec((1, GROUP_W), lambda b, i: (0, i)),
        ],
        out_specs=pl.BlockSpec((seq, GROUP_W), lambda b, i: (b, i)),
        scratch_shapes=[pltpu.VMEM((seq, GROUP_W), BF16),
                        pltpu.VMEM((seq, SSD_STATE), BF16),
                        pltpu.VMEM((seq, SSD_STATE), BF16),
                        pltpu.VMEM((seq, GROUP_W), F32),
                        pltpu.VMEM((SSD_STATE, GROUP_W), F32),
                        pltpu.VMEM((GROUP_W // LANES + 2, SSD_Q + 2 * CONV_HALO, LANES), F32)],
        compiler_params=_cparams(("arbitrary", "arbitrary")),
        name="ssd_core",
    )(zx, zx, zx, zx, dt_row, conv_w, conv_w, conv_w, conv_b, conv_b, conv_b,
      al_row, db_row, dskip, norm_w)


def _ssd_mixer(x, mod_sc, mod_sh, mod_g, nw, w_in, conv_w, conv_b, a_log, dt_bias, d_skip, norm_w,
               w_out, *, batch, seq):
    g, r = SSD_GROUPS, SSD_HEADS_PER_GROUP
    conv_dim = conv_w.shape[1]
    zx = _nm_matmul(x, nw, mod_sc, mod_sh, w_in, col0=0, ncols=D_INNER + conv_dim, tn=1024,
                    out_dtype=BF16, rows_per_batch=seq)
    dt_raw = _nm_matmul(x, nw, mod_sc, mod_sh, w_in, col0=D_INNER + conv_dim, ncols=2 * g * r,
                        tn=2 * g * r, out_dtype=F32, rows_per_batch=seq)
    nc = seq // SSD_Q
    dt_row = dt_raw.reshape(batch, nc, SSD_Q, 2, g, r).transpose(0, 4, 1, 3, 5, 2)
    dt_row = dt_row.reshape(batch, g, nc, 2 * r, SSD_Q)

    def row_form(p):
        return p.reshape(2, g, r).transpose(1, 0, 2).reshape(g, 2 * r, 1)

    yn = _ssd_core(zx, dt_row, conv_w, conv_b.reshape(1, conv_dim), row_form(a_log), row_form(dt_bias),
                   jnp.repeat(d_skip, SSD_HEAD_DIM).reshape(1, D_INNER), norm_w.reshape(1, D_INNER),
                   batch=batch, seq=seq)
    return _mm_resid(yn, w_out, x, mod_g, rows_per_batch=seq)


def _bias_table_kernel(rpb_ref, o_ref):
    dy0 = pl.program_id(1)
    lane = lax.broadcasted_iota(I32, (GRID_W, LANES), 1)
    j = lax.broadcasted_iota(I32, (GRID_W, LANES), 0)
    c = lane & (GRID_W - 1)
    c0 = jnp.clip(j - WIN_W // 2, 0, GRID_W - WIN_W)
    win = (c >= c0) & (c < c0 + WIN_W)
    left = lane < GRID_W
    for par in range(2):
        for m in range(WIN_H // 2):
            tiles = []
            for sub in range(2):
                row = rpb_ref[par, pl.ds(dy0 + 2 * m + sub, 1), :]
                shift = (sub * GRID_W - (WIN_W - 1)) % LANES
                tiles.append(pltpu.roll(jnp.broadcast_to(row, (GRID_W, LANES)), shift, 1,
                                        stride=1, stride_axis=0))
            tile = jnp.where(win, jnp.where(left, tiles[0], tiles[1]), NEG)
            o_ref[0, 0, par * GRID_W:(par + 1) * GRID_W, m * LANES:(m + 1) * LANES] = tile


def _na_bias_table(rpb):
    h, ndy, ndx = rpb.shape
    rpb_p = jnp.pad(rpb, ((0, 0), (0, 2 * WIN_H - ndy), (0, LANES - ndx)))
    return pl.pallas_call(
        _bias_table_kernel,
        out_shape=jax.ShapeDtypeStruct((WIN_H, h // 2, 2 * GRID_W, WIN_H * GRID_W), F32),
        grid=(h // 2, WIN_H),
        in_specs=[pl.BlockSpec((2, 2 * WIN_H, LANES), lambda p, d: (p, 0, 0))],
        out_specs=pl.BlockSpec((1, 1, 2 * GRID_W, WIN_H * GRID_W), lambda p, d: (d, p, 0, 0)),
        compiler_params=_cparams(("arbitrary", "arbitrary")),
        name="na_bias_table",
    )(rpb_p)


def _na_kernel(q_ref, k0_ref, k1_ref, k2_ref, v0_ref, v1_ref, v2_ref, bias_ref, o_ref, kc_s, vc_s,
               s_s, p_s, r_s, *, n_row_blocks):
    rb = pl.program_id(2)
    npairs = q_ref.shape[0]
    blk = NA_ROWS * GRID_W
    nkeys = WIN_H * GRID_W
    for i, (kr, vr) in enumerate(((k0_ref, v0_ref), (k1_ref, v1_ref), (k2_ref, v2_ref))):
        kc_s[:, i * blk:(i + 1) * blk, :] = kr[...]
        vc_s[:, i * blk:(i + 1) * blk, :] = vr[...]
    first = rb == 0
    last = rb == n_row_blocks - 1
    edge = first | last
    lane = lax.broadcasted_iota(I32, (GRID_W, LANES), 1)
    left = lane < NA_HEAD_DIM
    scale = jnp.asarray(NA_HEAD_DIM ** -0.5, BF16)

    def window(qi):
        off = jnp.where(first, 0, jnp.where(last, blk, qi * GRID_W))
        li = jnp.where(edge, NA_ROWS - 1 - qi, NA_ROWS - 1)
        return pl.multiple_of(off, GRID_W), li

    def pair_body(pp, carry):
        for qi in range(NA_ROWS):
            off, li = window(qi)
            q2 = q_ref[pp, qi * GRID_W:(qi + 1) * GRID_W, :] * scale
            zero = jnp.zeros_like(q2)
            qs = jnp.concatenate([jnp.where(left, q2, zero), jnp.where(left, zero, q2)], axis=0)
            kw = kc_s[pp, pl.ds(off, nkeys), :]
            s_s[qi] = _dot_nt(qs, kw) + bias_ref[li, pp]
        for qi in range(NA_ROWS):
            s = s_s[qi]
            p = jnp.exp(s - jnp.max(s, axis=-1, keepdims=True))
            r_s[qi] = 1.0 / jnp.sum(p, axis=-1, keepdims=True)
            p_s[qi] = p.astype(BF16)
        for qi in range(NA_ROWS):
            off, _ = window(qi)
            pv = _dot(p_s[qi], vc_s[pp, pl.ds(off, nkeys), :]) * r_s[qi]
            o = jnp.where(left, pv[0:GRID_W], pv[GRID_W:2 * GRID_W])
            o_ref[pp, qi * GRID_W:(qi + 1) * GRID_W, :] = o.astype(BF16)
        return carry

    lax.fori_loop(0, npairs, pair_body, 0, unroll=4)


def _na_attention(qkv_t, bias_tab, *, batch, seq):
    t = batch * seq
    blk = NA_ROWS * GRID_W
    nrb = seq // blk
    hp = NA_PAIRS // 2
    nsec = NA_PAIRS // hp

    def kv_spec(sec, i):
        def imap(hh, b, r):
            return (sec * nsec + hh, b * nrb + jnp.clip(r - 1, 0, nrb - 3) + i, 0)
        return pl.BlockSpec((hp, blk, LANES), imap)

    return pl.pallas_call(
        functools.partial(_na_kernel, n_row_blocks=nrb),
        out_shape=jax.ShapeDtypeStruct((NA_PAIRS, t, LANES), BF16),
        grid=(nsec, batch, nrb),
        in_specs=[pl.BlockSpec((hp, blk, LANES), lambda hh, b, r: (hh, b * nrb + r, 0)),
                  kv_spec(1, 0), kv_spec(1, 1), kv_spec(1, 2),
                  kv_spec(2, 0), kv_spec(2, 1), kv_spec(2, 2),
                  pl.BlockSpec((NA_ROWS, hp, 2 * GRID_W, WIN_H * GRID_W),
                               lambda hh, b, r: (jnp.where(r == 0, 1, 0), hh, 0, 0))],
        out_specs=pl.BlockSpec((hp, blk, LANES), lambda hh, b, r: (hh, b * nrb + r, 0)),
        scratch_shapes=[pltpu.VMEM((hp, 3 * blk, LANES), BF16),
                        pltpu.VMEM((hp, 3 * blk, LANES), BF16),
                        pltpu.VMEM((NA_ROWS, 2 * GRID_W, WIN_H * GRID_W), F32),
                        pltpu.VMEM((NA_ROWS, 2 * GRID_W, WIN_H * GRID_W), BF16),
                        pltpu.VMEM((NA_ROWS, 2 * GRID_W, 1), F32)],
        compiler_params=_cparams(("arbitrary", "arbitrary", "arbitrary")),
        name="na_attention",
    )(qkv_t, qkv_t, qkv_t, qkv_t, qkv_t, qkv_t, qkv_t, bias_tab)


def _na_mixer(x, mod_sc, mod_sh, mod_g, nw, w_qkv, rpb, w_o, *, batch, seq):
    qkv_t = _nm_matmul(x, nw, mod_sc, mod_sh, w_qkv, col0=0, ncols=3 * D_MODEL, tn=1024,
                       out_dtype=BF16, rows_per_batch=seq, pair_major=True)
    o_t = _na_attention(qkv_t, _na_bias_table(rpb), batch=batch, seq=seq)
    return _mm_resid(o_t, w_o, x, mod_g, rows_per_batch=seq, pair_major=True)


U32 = jnp.uint32


def _pack_halves(vb):
    n = vb.shape[1] // 2
    bits = pltpu.bitcast(vb.astype(F32), U32)
    return (bits[:, :n] >> 16) | bits[:, n:]


def _unpack_halves(w):
    return pltpu.bitcast(w << 16, F32), pltpu.bitcast(w & U32(0xFFFF0000), F32)

def _router_kernel(x_ref, nw_ref, sc_ref, sh_ref, wr_ref, h_ref, meta_ref, meta_t_ref, cnt_ref,
                   carry_s):
    @pl.when(pl.program_id(0) == 0)
    def _():
        carry_s[...] = jnp.zeros_like(carry_s)

    h = _normmod(x_ref[...], nw_ref[...], sc_ref[0], sh_ref[0])
    h_ref[...] = _pack_halves(h.astype(BF16))
    logits = _dot_split(h, wr_ref[...])
    tm = logits.shape[0]
    lane_i = lax.broadcasted_iota(I32, logits.shape, 1)
    lane = lane_i.astype(F32)
    big = 1e9
    gl = jnp.where(lane_i < MOE_GROUPS, logits, NEG)
    gmax = jnp.max(gl, axis=1, keepdims=True)
    gsel = jnp.min(jnp.where(gl == gmax, lane, big), axis=1, keepdims=True)
    gw = 1.0 / jnp.sum(jnp.exp(gl - gmax), axis=1, keepdims=True)
    el = lane - MOE_GROUPS
    lo = gsel * MOE_EPG
    emask = (el >= lo) & (el < lo + MOE_EPG)
    e1 = jnp.where(emask, logits, NEG)
    m1 = jnp.max(e1, axis=1, keepdims=True)
    i1 = jnp.min(jnp.where(e1 == m1, el, big), axis=1, keepdims=True)
    e2 = jnp.where(emask & (el != i1), logits, NEG)
    m2 = jnp.max(e2, axis=1, keepdims=True)
    i2 = jnp.min(jnp.where(e2 == m2, el, big), axis=1, keepdims=True)
    tt = jnp.exp(m2 - m1)
    p1 = 1.0 / (1.0 + tt)
    w1 = gw * p1
    w2 = gw * (tt * p1)
    oh1 = el == i1
    oh2 = el == i2
    cnt = (oh1 | oh2).astype(F32)
    r_i = lax.broadcasted_iota(I32, (tm, tm), 0)
    c_i = lax.broadcasted_iota(I32, (tm, tm), 1)
    before = _dot((r_i > c_i).astype(BF16), cnt.astype(BF16)) + carry_s[...]
    rank1 = jnp.sum(jnp.where(oh1, before, 0.0), axis=1, keepdims=True)
    rank2 = jnp.sum(jnp.where(oh2, before, 0.0), axis=1, keepdims=True)
    carry_s[...] = carry_s[...] + jnp.sum(cnt, axis=0, keepdims=True)
    meta = jnp.zeros_like(logits)
    for pos, val in enumerate((i1, i2, w1, w2, rank1, rank2)):
        meta = jnp.where(lane_i == pos, val, meta)
    meta_ref[...] = meta
    meta_t_ref[...] = meta.T[0:meta_t_ref.shape[0], :]
    cnt_ref[...] = jnp.broadcast_to(carry_s[...], cnt_ref.shape)


def _router(x, nw, sc, sh, wr, *, rows_per_batch):
    t, d = x.shape
    tm = 256
    tiles_per_batch = rows_per_batch // tm
    return pl.pallas_call(
        _router_kernel,
        out_shape=(jax.ShapeDtypeStruct((t, d // 2), U32),
                   jax.ShapeDtypeStruct((t, LANES), F32),
                   jax.ShapeDtypeStruct((8, t), F32),
                   jax.ShapeDtypeStruct((8, LANES), F32)),
        grid=(t // tm,),
        in_specs=[pl.BlockSpec((tm, d), lambda i: (i, 0)),
                  pl.BlockSpec((1, d), lambda i: (0, 0)),
                  pl.BlockSpec((1, 1, d), lambda i: (i // tiles_per_batch, 0, 0)),
                  pl.BlockSpec((1, 1, d), lambda i: (i // tiles_per_batch, 0, 0)),
                  pl.BlockSpec((d, LANES), lambda i: (0, 0))],
        out_specs=(pl.BlockSpec((tm, d // 2), lambda i: (i, 0)),
                   pl.BlockSpec((tm, LANES), lambda i: (i, 0)),
                   pl.BlockSpec((8, tm), lambda i: (0, i)),
                   pl.BlockSpec((8, LANES), lambda i: (0, 0))),
        scratch_shapes=[pltpu.VMEM((1, LANES), F32)],
        compiler_params=_cparams(("arbitrary",)),
        name="moe_router",
    )(x, nw, sc, sh, wr)


def _dispatch_kernel(dest_ref, h_ref, xs_in_ref, xs_ref, sem):
    del xs_in_ref
    tm = h_ref.shape[0]
    base = pl.program_id(0) * tm
    ntok = pl.num_programs(0) * tm

    def copy(t, k):
        return pltpu.make_async_copy(h_ref.at[pl.ds(t, 1)],
                                     xs_ref.at[pl.ds(dest_ref[k * ntok + base + t], 1)], sem)

    def issue(t, carry):
        copy(t, 0).start()
        copy(t, 1).start()
        return carry

    def drain(t, carry):
        copy(t, 0).wait()
        copy(t, 1).wait()
        return carry

    lax.fori_loop(0, tm, issue, 0, unroll=DMA_UNROLL)
    lax.fori_loop(0, tm, drain, 0, unroll=DMA_UNROLL)


def _dispatch(dest, h, n_slots):
    t, d = h.shape
    tm = 256
    return pl.pallas_call(
        _dispatch_kernel,
        out_shape=jax.ShapeDtypeStruct((n_slots, d), U32),
        grid_spec=pltpu.PrefetchScalarGridSpec(
            num_scalar_prefetch=1,
            grid=(t // tm,),
            in_specs=[pl.BlockSpec((tm, d), lambda i, dest: (i, 0)),
                      pl.BlockSpec(memory_space=pl.ANY)],
            out_specs=pl.BlockSpec(memory_space=pl.ANY),
            scratch_shapes=[pltpu.SemaphoreType.DMA(())]),
        input_output_aliases={2: 0},
        compiler_params=_cparams(("arbitrary",)),
        name="moe_dispatch",
    )(dest, h, jnp.zeros((n_slots, d), U32))


def _ffn_kernel(be_ref, nxt_ref, slot_ref, nu_ref, xs_ref, w1_hbm, w3_hbm, w2_hbm, o_ref,
                wb1, wb3, wb2, w1_s, w3_s, w2_s, sem, *, layer):
    i = pl.program_id(0)

    def fetch(e, s):
        return (pltpu.make_async_copy(w1_hbm.at[layer, e], wb1.at[s], sem.at[s, 0]),
                pltpu.make_async_copy(w3_hbm.at[layer, e], wb3.at[s], sem.at[s, 1]),
                pltpu.make_async_copy(w2_hbm.at[layer, e], wb2.at[s], sem.at[s, 2]))

    @pl.when(i < nu_ref[0])
    def _():
        e = be_ref[i]
        s = slot_ref[i]

        @pl.when((i == 0) | (e != be_ref[jnp.maximum(i - 1, 0)]))
        def _():
            @pl.when(i == 0)
            def _():
                for cp in fetch(e, s):
                    cp.start()

            for cp in fetch(e, s):
                cp.wait()

            @pl.when(nxt_ref[i] >= 0)
            def _():
                for cp in fetch(nxt_ref[i], 1 - s):
                    cp.start()

            w1_s[...] = wb1[s].astype(BF16)
            w3_s[...] = wb3[s].astype(BF16)
            w2_s[...] = wb2[s].astype(BF16)

        lo, hi = _unpack_halves(xs_ref[...])
        xl, xh = lo.astype(BF16), hi.astype(BF16)
        half = xl.shape[1]
        a = _dot(xl, w1_s[0:half, :]) + _dot(xh, w1_s[half:2 * half, :])
        b = _dot(xl, w3_s[0:half, :]) + _dot(xh, w3_s[half:2 * half, :])
        hmid = _silu(a) * b
        o_ref[...] = _pack_halves(_dot(hmid.astype(BF16), w2_s[...]).astype(BF16))

    @pl.when(i >= nu_ref[0])
    def _():
        o_ref[...] = jnp.zeros_like(o_ref)


def _expert_ffn(blk_expert, blk_next, blk_slot, n_used, xs, w1, w3, w2, layer):
    p, dh = xs.shape
    d = 2 * dh
    f = w1.shape[3]
    tb = MOE_TB
    nb = p // tb
    hbm = pl.BlockSpec(memory_space=pl.ANY)
    return pl.pallas_call(
        functools.partial(_ffn_kernel, layer=layer),
        out_shape=jax.ShapeDtypeStruct((p, dh), U32),
        grid_spec=pltpu.PrefetchScalarGridSpec(
            num_scalar_prefetch=4,
            grid=(nb,),
            in_specs=[pl.BlockSpec((tb, dh), lambda i, be, nx, sl, nu: (jnp.minimum(i, nu[0] - 1), 0)),
                      hbm, hbm, hbm],
            out_specs=pl.BlockSpec((tb, dh), lambda i, be, nx, sl, nu: (i, 0)),
            scratch_shapes=[pltpu.VMEM((2, d, f), F32), pltpu.VMEM((2, d, f), F32),
                            pltpu.VMEM((2, f, d), F32),
                            pltpu.VMEM((d, f), BF16), pltpu.VMEM((d, f), BF16),
                            pltpu.VMEM((f, d), BF16),
                            pltpu.SemaphoreType.DMA((2, 3))]),
        compiler_params=_cparams(("arbitrary",)),
        name="moe_expert_ffn",
    )(blk_expert, blk_next, blk_slot, n_used, xs, w1, w3, w2)


def _combine_kernel(dest_ref, x_ref, meta_ref, g_ref, fnw_ref, ys_ref, o_ref, buf, sem, *, final):
    tm = x_ref.shape[0]
    i = pl.program_id(0)
    nsteps = pl.num_programs(0)
    ntok = nsteps * tm

    def copy(tile, t, k):
        half = tile % 2
        return pltpu.make_async_copy(ys_ref.at[pl.ds(dest_ref[k * ntok + tile * tm + t], 1)],
                                     buf.at[half, k, pl.ds(t, 1)], sem.at[half])

    def issue(tile):
        def body(t, carry):
            copy(tile, t, 0).start()
            copy(tile, t, 1).start()
            return carry
        lax.fori_loop(0, tm, body, 0, unroll=DMA_UNROLL)

    def drain(tile):
        def body(t, carry):
            copy(tile, t, 0).wait()
            copy(tile, t, 1).wait()
            return carry
        lax.fori_loop(0, tm, body, 0, unroll=DMA_UNROLL)

    @pl.when(i == 0)
    def _():
        issue(i)

    @pl.when(i + 1 < nsteps)
    def _():
        issue(i + 1)

    drain(i)
    cur = i % 2
    meta = meta_ref[...]
    w1, w2 = meta[:, 2:3], meta[:, 3:4]
    lo1, hi1 = _unpack_halves(buf[cur, 0])
    lo2, hi2 = _unpack_halves(buf[cur, 1])
    y = jnp.concatenate([w1 * lo1 + w2 * lo2, w1 * hi1 + w2 * hi2], axis=1)
    xn = x_ref[...] + g_ref[0] * y
    if final:
        ms = jnp.mean(xn * xn, axis=-1, keepdims=True)
        xn = xn * lax.rsqrt(ms + EPS) * fnw_ref[...]
    o_ref[...] = xn


def _combine(dest, x, meta, g, fnw, ys, *, rows_per_batch, final):
    t, d = x.shape
    tm = 256
    tiles_per_batch = rows_per_batch // tm
    return pl.pallas_call(
        functools.partial(_combine_kernel, final=final),
        out_shape=jax.ShapeDtypeStruct((t, d), F32),
        grid_spec=pltpu.PrefetchScalarGridSpec(
            num_scalar_prefetch=1,
            grid=(t // tm,),
            in_specs=[pl.BlockSpec((tm, d), lambda i, dest: (i, 0)),
                      pl.BlockSpec((tm, LANES), lambda i, dest: (i, 0)),
                      pl.BlockSpec((1, 1, d), lambda i, dest: (i // tiles_per_batch, 0, 0)),
                      pl.BlockSpec((1, d), lambda i, dest: (0, 0)),
                      pl.BlockSpec(memory_space=pl.ANY)],
            out_specs=pl.BlockSpec((tm, d), lambda i, dest: (i, 0)),
            scratch_shapes=[pltpu.VMEM((2, 2, tm, d // 2), U32), pltpu.SemaphoreType.DMA((2,))]),
        compiler_params=_cparams(("arbitrary",)),
        name="moe_combine",
    )(dest, x, meta, g, fnw, ys)


def _hier_moe(x, mod_sc, mod_sh, mod_g, nw, w_group, w_expert, w1, w3, w2, layer, fnw, *,
              rows_per_batch, final):
    t, d = x.shape
    a = 2 * t
    tb = MOE_TB
    wr = jnp.concatenate([w_group, w_expert], axis=1)
    wr = jnp.pad(wr, ((0, 0), (0, LANES - wr.shape[1])))
    h, meta, meta_t, cnt = _router(x, nw, mod_sc, mod_sh, wr, rows_per_batch=rows_per_batch)
    ne = N_EXPERTS
    counts = cnt[0, MOE_GROUPS:MOE_GROUPS + ne].astype(I32)
    padded = ((counts + tb - 1) // tb) * tb
    pend = jnp.cumsum(padded)
    pstart = pend - padded
    eid = meta_t[0:2].astype(I32)
    start_of = jnp.zeros_like(eid)
    for e in range(ne):
        start_of = jnp.where(eid == e, pstart[e], start_of)
    dest = (start_of + meta_t[4:6].astype(I32)).reshape(a)
    nb = (a + ne * (tb - 1) + tb - 1) // tb
    n_used = (pend[-1] // tb).astype(I32)
    blk = jnp.arange(nb, dtype=I32)
    be = jnp.minimum(jnp.sum((pend[None, :] <= (blk * tb)[:, None]).astype(I32), axis=1), ne - 1)
    nonempty = counts > 0
    seg = jnp.cumsum(nonempty.astype(I32)) - 1
    later = lax.cummin(jnp.where(nonempty, jnp.arange(ne, dtype=I32), ne), axis=0, reverse=True)
    nxt_e = jnp.concatenate([later[1:], jnp.full((1,), ne, I32)])
    nxt_e = jnp.where(nxt_e == ne, -1, nxt_e)
    xs = _dispatch(dest, h, nb * tb)
    ys = _expert_ffn(be, nxt_e[be], seg[be] % 2, n_used.reshape(1), xs, w1, w3, w2, layer)
    return _combine(dest, x, meta, mod_g, fnw, ys, rows_per_batch=rows_per_batch, final=final)


def kernel(x, c, ada_w, ada_b, norm_mix, norm_ffn, ssd_w_in, ssd_conv_w, ssd_conv_b, ssd_a_log,
           ssd_dt_bias, ssd_d, ssd_norm_w, ssd_w_out, na_w_qkv, na_rpb, na_w_o,
           moe_w_group, moe_w_expert, moe_w1, moe_w3, moe_w2, final_norm):
    batch, seq, d = x.shape
    depth = ada_w.shape[0]
    xt = x.reshape(batch * seq, d)
    c_pad = jnp.pad(c, ((0, 8 - batch), (0, 0)))
    mod = _ada(c_pad, ada_w, ada_b)[:, :batch]
    fnw = final_norm.reshape(1, d)
    for i in range(depth):
        sh1, sc1, g1, sh2, sc2, g2 = [mod[i, :, k * d:(k + 1) * d].reshape(batch, 1, d)
                                      for k in range(6)]
        j = i // 2
        nw = norm_mix[i].reshape(1, d)
        if i % 2 == 0:
            xt = _ssd_mixer(xt, sc1, sh1, g1, nw, ssd_w_in[j], ssd_conv_w[j], ssd_conv_b[j],
                            ssd_a_log[j], ssd_dt_bias[j], ssd_d[j], ssd_norm_w[j], ssd_w_out[j],
                            batch=batch, seq=seq)
        else:
            xt = _na_mixer(xt, sc1, sh1, g1, nw, na_w_qkv[j], na_rpb[j], na_w_o[j],
                           batch=batch, seq=seq)
        xt = _hier_moe(xt, sc2, sh2, g2, norm_ffn[i].reshape(1, d), moe_w_group[i], moe_w_expert[i],
                       moe_w1, moe_w3, moe_w2, i, fnw, rows_per_batch=seq,
                       final=(i == depth - 1))
    return xt.reshape(batch, seq, d)
```

```python
import functools

import jax
import jax.numpy as jnp
from jax import lax
from jax.experimental import pallas as pl
from jax.experimental.pallas import tpu as pltpu

F32 = jnp.float32
BF16 = jnp.bfloat16
I32 = jnp.int32

EPS = 1e-6
NEG = -1e30
LOG2_E = 1.4426950408889634

D_MODEL = 2048
GRID_W = 64
SSD_HEAD_DIM = 64
SSD_GROUPS = 8
SSD_HEADS_PER_GROUP = 8
SSD_STATE = 128
SSD_CONV = 5
D_INNER = 2 * D_MODEL
GROUP_W = SSD_HEADS_PER_GROUP * SSD_HEAD_DIM
SSD_Q = 128
CONV_HALO = 16
NA_HEAD_DIM = 64
NA_HEADS = D_MODEL // NA_HEAD_DIM
NA_PAIRS = NA_HEADS // 2
WIN_H = 8
WIN_W = 16
NA_ROWS = 4
MOE_GROUPS = 4
MOE_EPG = 8
N_EXPERTS = MOE_GROUPS * MOE_EPG
MOE_D_FF = D_MODEL // 4
MOE_TB = 256
DMA_UNROLL = 8

VMEM_LIMIT = 56 * 1024 * 1024
LANES = 128


def _cparams(sem):
    return pltpu.CompilerParams(dimension_semantics=sem, vmem_limit_bytes=VMEM_LIMIT)


def _silu(v):
    return v * pl.reciprocal(1.0 + jnp.exp(-v), approx=True)


def _softplus(v):
    return jnp.maximum(v, 0.0) + jnp.log1p(jnp.exp(-jnp.abs(v)))


def _split3(v):
    hi = v.astype(BF16)
    r1 = v - hi.astype(F32)
    mid = r1.astype(BF16)
    lo = (r1 - mid.astype(F32)).astype(BF16)
    return hi, mid, lo


def _dot(a, b):
    return jnp.dot(a, b, preferred_element_type=F32)


def _dot_nt(a, b):
    return lax.dot_general(a, b, (((1,), (1,)), ((), ())), preferred_element_type=F32)


def _dot3_left(v, sel):
    hi, mid, lo = _split3(v)
    return _dot(hi, sel) + _dot(mid, sel) + _dot(lo, sel)


def _normmod(x, nw, sc, sh):
    ms = jnp.mean(x * x, axis=-1, keepdims=True)
    return (x * lax.rsqrt(ms + EPS) * nw) * (1.0 + sc) + sh


def _split2(v):
    hi = v.astype(BF16)
    return hi, (v - hi.astype(F32)).astype(BF16)


def _dot_split(a, b_hi, b_lo):
    a_hi, a_lo = _split2(a)
    return _dot(a_hi, b_hi) + (_dot(a_lo, b_hi) + _dot(a_hi, b_lo))


def _ada_kernel(c_ref, w_ref, b_ref, o_ref):
    c = c_ref[...]
    o_ref[0] = _dot_split(c / (1.0 + jnp.exp(-c)), *_split2(w_ref[0])) + b_ref[0]


def _ada(c_pad, ada_w, ada_b):
    depth, d, n = ada_w.shape
    tn = 1024
    return pl.pallas_call(
        _ada_kernel,
        out_shape=jax.ShapeDtypeStruct((depth, 8, n), F32),
        grid=(depth, n // tn),
        in_specs=[pl.BlockSpec((8, d), lambda i, j: (0, 0)),
                  pl.BlockSpec((1, d, tn), lambda i, j: (i, 0, j)),
                  pl.BlockSpec((1, 1, tn), lambda i, j: (i, 0, j))],
        out_specs=pl.BlockSpec((1, 8, tn), lambda i, j: (i, 0, j)),
        compiler_params=_cparams(("arbitrary", "arbitrary")),
        name="ada_mod",
    )(c_pad, ada_w, ada_b.reshape(depth, 1, n))


def _nm_mm_kernel(x_ref, nw_ref, sc_ref, sh_ref, w_ref, o_ref, h_ref, *, pair_major):
    @pl.when(pl.program_id(1) == 0)
    def _():
        h_ref[...] = _normmod(x_ref[...], nw_ref[...], sc_ref[0], sh_ref[0]).astype(BF16)

    r = _dot(h_ref[...], w_ref[...].astype(BF16))
    if pair_major:
        for c in range(o_ref.shape[0]):
            o_ref[c] = r[:, c * LANES:(c + 1) * LANES].astype(o_ref.dtype)
    else:
        o_ref[...] = r.astype(o_ref.dtype)


def _nm_matmul(x, nw, sc, sh, w, *, col0, ncols, tn, out_dtype, rows_per_batch, pair_major=False):
    t, d = x.shape
    tm = 1024
    tiles_per_batch = rows_per_batch // tm
    nj = ncols // tn
    jb = col0 // tn
    if pair_major:
        out_shape = jax.ShapeDtypeStruct((ncols // LANES, t, LANES), out_dtype)
        out_spec = pl.BlockSpec((tn // LANES, tm, LANES), lambda i, j: (j, i, 0))
    else:
        out_shape = jax.ShapeDtypeStruct((t, ncols), out_dtype)
        out_spec = pl.BlockSpec((tm, tn), lambda i, j: (i, j))
    return pl.pallas_call(
        functools.partial(_nm_mm_kernel, pair_major=pair_major),
        out_shape=out_shape,
        grid=(t // tm, nj),
        in_specs=[pl.BlockSpec((tm, d), lambda i, j: (i, 0)),
                  pl.BlockSpec((1, d), lambda i, j: (0, 0)),
                  pl.BlockSpec((1, 1, d), lambda i, j: (i // tiles_per_batch, 0, 0)),
                  pl.BlockSpec((1, 1, d), lambda i, j: (i // tiles_per_batch, 0, 0)),
                  pl.BlockSpec((d, tn), lambda i, j: (0, jb + j))],
        out_specs=out_spec,
        scratch_shapes=[pltpu.VMEM((tm, d), BF16)],
        compiler_params=_cparams(("arbitrary", "arbitrary")),
        name="norm_mod_matmul",
    )(x, nw, sc, sh, w)


def _mm_resid_kernel(a_ref, w_ref, x_ref, g_ref, o_ref, *, pair_major):
    if pair_major:
        a = jnp.concatenate([a_ref[c] for c in range(a_ref.shape[0])], axis=1)
    else:
        a = a_ref[...]
    o_ref[...] = x_ref[...] + g_ref[0] * _dot(a, w_ref[...].astype(BF16))


def _mm_resid(a, w, x, g, *, rows_per_batch, pair_major=False):
    t, n = x.shape
    k = w.shape[0]
    tm, tn = 1024, 512
    tiles_per_batch = rows_per_batch // tm
    if pair_major:
        a_spec = pl.BlockSpec((k // LANES, tm, LANES), lambda i, j: (0, i, 0))
    else:
        a_spec = pl.BlockSpec((tm, k), lambda i, j: (i, 0))
    return pl.pallas_call(
        functools.partial(_mm_resid_kernel, pair_major=pair_major),
        out_shape=jax.ShapeDtypeStruct((t, n), F32),
        grid=(t // tm, n // tn),
        in_specs=[a_spec,
                  pl.BlockSpec((k, tn), lambda i, j: (0, j)),
                  pl.BlockSpec((tm, tn), lambda i, j: (i, j)),
                  pl.BlockSpec((1, 1, tn), lambda i, j: (i // tiles_per_batch, 0, j))],
        out_specs=pl.BlockSpec((tm, tn), lambda i, j: (i, j)),
        compiler_params=_cparams(("arbitrary", "arbitrary")),
        name="matmul_resid",
    )(a, w, x, g)


def _ssd_kernel(z_ref, x_ref, b_ref, c_ref, dtr_ref,
                cwx_ref, cwb_ref, cwc_ref, cbx_ref, cbb_ref, cbc_ref,
                alr_ref, dbr_ref, dsk_ref, nw_ref,
                o_ref,
                xc_s, bc_s, cc_s, yacc_s, st_s, cv_s):
    seq = x_ref.shape[0]
    q = SSD_Q
    nc = seq // q
    halo = CONV_HALO
    nrow = 2 * SSD_HEADS_PER_GROUP

    def conv_piece(j, base, src_ref, w_ref, bias_ref, dst_ref, lo, stage):
        cols = slice(lo, lo + LANES)
        pstart = pl.multiple_of(jnp.maximum(base - halo, 0), halo)
        nstart = pl.multiple_of(jnp.minimum(base + q, seq - halo), halo)
        stage[0:halo, :] = jnp.where(j > 0, src_ref[pl.ds(pstart, halo), cols].astype(F32), 0.0)
        stage[halo:halo + q, :] = src_ref[pl.ds(base, q), cols].astype(F32)
        stage[halo + q:, :] = jnp.where(j < nc - 1, src_ref[pl.ds(nstart, halo), cols].astype(F32), 0.0)
        acc = jnp.broadcast_to(bias_ref[:, cols], (q, LANES))
        for k in range(SSD_CONV):
            first = halo - SSD_CONV // 2 + k
            acc = acc + w_ref[k:k + 1, cols] * stage[first:first + q, :]
        dst_ref[pl.ds(base, q), cols] = _silu(acc).astype(BF16)

    def conv_chunk(j, carry):
        base = pl.multiple_of(j * q, q)
        npx = GROUP_W // LANES
        for i in range(npx):
            conv_piece(j, base, x_ref, cwx_ref, cbx_ref, xc_s, i * LANES, cv_s.at[i])
        conv_piece(j, base, b_ref, cwb_ref, cbb_ref, bc_s, 0, cv_s.at[npx])
        conv_piece(j, base, c_ref, cwc_ref, cbc_ref, cc_s, 0, cv_s.at[npx + 1])
        return carry

    lax.fori_loop(0, nc, conv_chunk, 0)

    row_i = lax.broadcasted_iota(I32, (q, q), 0)
    col_i = lax.broadcasted_iota(I32, (q, q), 1)
    lower = row_i >= col_i
    upper = row_i <= col_i
    lower_b = lower.astype(BF16)
    upper_b = upper.astype(BF16)
    lane_w = lax.broadcasted_iota(I32, (q, GROUP_W), 1)
    even_head = (lane_w & (LANES - 1)) < SSD_HEAD_DIM
    left = lax.broadcasted_iota(I32, (q, LANES), 1) < SSD_HEAD_DIM
    a_row = -jnp.exp(alr_ref[...])
    pad_rows = jnp.zeros((LANES - nrow, q), F32)

    def scan_pass(direction):
        hoff = direction * SSD_HEADS_PER_GROUP
        mask = lower if direction == 0 else upper
        tri = upper_b if direction == 0 else lower_b
        edge = q - 1 if direction == 0 else 0
        st_s[...] = jnp.zeros_like(st_s)

        def chunk(t, carry):
            c = t if direction == 0 else nc - 1 - t
            base = pl.multiple_of(c * q, q)
            rows = pl.ds(base, q)
            dt_r = _softplus(dtr_ref[c] + dbr_ref[...])
            cum_r = _dot3_left(dt_r * a_row, tri) * LOG2_E
            cum_c = jnp.concatenate([cum_r, pad_rows], axis=0).T
            decdt_r = jnp.exp2(cum_r[:, edge:edge + 1] - cum_r) * dt_r
            src_r = cum_r - jnp.log2(dt_r)

            xcb = xc_s[rows, :]
            zero_b = jnp.zeros_like(xcb)
            x_even = jnp.where(even_head, xcb, zero_b)
            x_odd = jnp.where(even_head, zero_b, xcb)
            bm = bc_s[rows, :]
            cm = cc_s[rows, :]
            cb = _dot_nt(cm, bm)
            bm_t = bm.astype(F32).T
            y_off = _dot(cm, st_s[...].astype(BF16))

            ssq = jnp.zeros((q, 1), F32)
            for pp in range(SSD_HEADS_PER_GROUP // 2):
                cols = slice(pp * LANES, (pp + 1) * LANES)
                lhs_y, lhs_s, scales = [], [], []
                for par in range(2):
                    j = hoff + 2 * pp + par
                    cum_b = jnp.broadcast_to(cum_c[:, j:j + 1], (q, q))
                    lmat = jnp.exp2(jnp.where(mask, cum_b - src_r[j:j + 1, :], NEG))
                    lhs_y.append((cb * lmat).astype(BF16))
                    lhs_s.append((bm_t * decdt_r[j:j + 1, :]).astype(BF16))
                    scales.append(jnp.exp2(cum_b))
                x_rhs = jnp.concatenate([x_even[:, cols], x_odd[:, cols]], axis=0)
                sc_tile = jnp.where(left, scales[0], scales[1])
                y = _dot(jnp.concatenate(lhs_y, axis=1), x_rhs) + y_off[:, cols] * sc_tile
                st_s[:, cols] = (st_s[:, cols] * sc_tile[edge:edge + 1, :]
                                 + _dot(jnp.concatenate(lhs_s, axis=1), x_rhs))
                if direction == 0:
                    yacc_s[rows, cols] = y
                else:
                    total = yacc_s[rows, cols] + y + xcb[:, cols].astype(F32) * dsk_ref[:, cols]
                    gated = total * _silu(z_ref[rows, cols].astype(F32))
                    ssq = ssq + jnp.sum(gated * gated, axis=-1, keepdims=True)
                    yacc_s[rows, cols] = gated
            if direction == 1:
                inv = lax.rsqrt(ssq * (1.0 / GROUP_W) + EPS)
                o_ref[rows, :] = (yacc_s[rows, :] * inv * nw_ref[...]).astype(BF16)
            return carry

        lax.fori_loop(0, nc, chunk, 0, unroll=4)

    scan_pass(0)
    scan_pass(1)


def _ssd_core(zx, dt_row, conv_w, conv_b, al_row, db_row, dskip, norm_w, *, batch, seq):
    g = SSD_GROUPS
    nc = seq // SSD_Q
    xb = D_INNER // GROUP_W
    bb = (2 * D_INNER) // SSD_STATE
    cb = bb + g
    cwb = D_INNER // SSD_STATE
    cwc = cwb + g
    return pl.pallas_call(
        _ssd_kernel,
        out_shape=jax.ShapeDtypeStruct((batch * seq, D_INNER), BF16),
        grid=(batch, g),
        in_specs=[
            pl.BlockSpec((seq, GROUP_W), lambda b, i: (b, i)),
            pl.BlockSpec((seq, GROUP_W), lambda b, i: (b, xb + i)),
            pl.BlockSpec((seq, SSD_STATE), lambda b, i: (b, bb + i)),
            pl.BlockSpec((seq, SSD_STATE), lambda b, i: (b, cb + i)),
            pl.BlockSpec((None, None, nc, 2 * SSD_HEADS_PER_GROUP, SSD_Q), lambda b, i: (b, i, 0, 0, 0)),
            pl.BlockSpec((SSD_CONV, GROUP_W), lambda b, i: (0, i)),
            pl.BlockSpec((SSD_CONV, SSD_STATE), lambda b, i: (0, cwb + i)),
            pl.BlockSpec((SSD_CONV, SSD_STATE), lambda b, i: (0, cwc + i)),
            pl.BlockSpec((1, GROUP_W), lambda b, i: (0, i)),
            pl.BlockSpec((1, SSD_STATE), lambda b, i: (0, cwb + i)),
            pl.BlockSpec((1, SSD_STATE), lambda b, i: (0, cwc + i)),
            pl.BlockSpec((None, 2 * SSD_HEADS_PER_GROUP, 1), lambda b, i: (i, 0, 0)),
            pl.BlockSpec((None, 2 * SSD_HEADS_PER_GROUP, 1), lambda b, i: (i, 0, 0)),
            pl.BlockSpec((1, GROUP_W), lambda b, i: (0, i)),
            pl.BlockSpec((1, GROUP_W), lambda b, i: (0, i)),
        ],
        out_specs=pl.BlockSpec((seq, GROUP_W), lambda b, i: (b, i)),
        scratch_shapes=[pltpu.VMEM((seq, GROUP_W), BF16),
                        pltpu.VMEM((seq, SSD_STATE), BF16),
                        pltpu.VMEM((seq, SSD_STATE), BF16),
                        pltpu.VMEM((seq, GROUP_W), F32),
                        pltpu.VMEM((SSD_STATE, GROUP_W), F32),
                        pltpu.VMEM((GROUP_W // LANES + 2, SSD_Q + 2 * CONV_HALO, LANES), F32)],
        compiler_params=_cparams(("arbitrary", "arbitrary")),
        name="ssd_core",
    )(zx, zx, zx, zx, dt_row, conv_w, conv_w, conv_w, conv_b, conv_b, conv_b,
      al_row, db_row, dskip, norm_w)


def _ssd_mixer(x, mod_sc, mod_sh, mod_g, nw, w_in, conv_w, conv_b, a_log, dt_bias, d_skip, norm_w,
               w_out, *, batch, seq):
    g, r = SSD_GROUPS, SSD_HEADS_PER_GROUP
    conv_dim = conv_w.shape[1]
    zx = _nm_matmul(x, nw, mod_sc, mod_sh, w_in, col0=0, ncols=D_INNER + conv_dim, tn=1024,
                    out_dtype=BF16, rows_per_batch=seq)
    dt_raw = _nm_matmul(x, nw, mod_sc, mod_sh, w_in, col0=D_INNER + conv_dim, ncols=2 * g * r,
                        tn=2 * g * r, out_dtype=F32, rows_per_batch=seq)
    nc = seq // SSD_Q
    dt_row = dt_raw.reshape(batch, nc, SSD_Q, 2, g, r).transpose(0, 4, 1, 3, 5, 2)
    dt_row = dt_row.reshape(batch, g, nc, 2 * r, SSD_Q)

    def row_form(p):
        return p.reshape(2, g, r).transpose(1, 0, 2).reshape(g, 2 * r, 1)

    yn = _ssd_core(zx, dt_row, conv_w, conv_b.reshape(1, conv_dim), row_form(a_log), row_form(dt_bias),
                   jnp.repeat(d_skip, SSD_HEAD_DIM).reshape(1, D_INNER), norm_w.reshape(1, D_INNER),
                   batch=batch, seq=seq)
    return _mm_resid(yn, w_out, x, mod_g, rows_per_batch=seq)


def _bias_table_kernel(rpb_ref, o_ref):
    lane = lax.broadcasted_iota(I32, (GRID_W, LANES), 1)
    j = lax.broadcasted_iota(I32, (GRID_W, LANES), 0)
    c = lane & (GRID_W - 1)
    c0 = jnp.clip(j - WIN_W // 2, 0, GRID_W - WIN_W)
    win = (c >= c0) & (c < c0 + WIN_W)
    left = lane < GRID_W

    def one_offset(dy0, carry):
        for par in range(2):
            for m in range(WIN_H // 2):
                tiles = []
                for sub in range(2):
                    row = rpb_ref[par, pl.ds(dy0 + 2 * m + sub, 1), :]
                    shift = (sub * GRID_W - (WIN_W - 1)) % LANES
                    tiles.append(pltpu.roll(jnp.broadcast_to(row, (GRID_W, LANES)), shift, 1,
                                            stride=1, stride_axis=0))
                tile = jnp.where(win, jnp.where(left, tiles[0], tiles[1]), NEG)
                o_ref[dy0, 0, par * GRID_W:(par + 1) * GRID_W, m * LANES:(m + 1) * LANES] = tile
        return carry

    lax.fori_loop(0, o_ref.shape[0], one_offset, 0)


def _na_bias_table(rpb):
    h, ndy, ndx = rpb.shape
    rpb_p = jnp.pad(rpb, ((0, 0), (0, 2 * WIN_H - ndy), (0, LANES - ndx)))
    return pl.pallas_call(
        _bias_table_kernel,
        out_shape=jax.ShapeDtypeStruct((WIN_H, h // 2, 2 * GRID_W, WIN_H * GRID_W), F32),
        grid=(h // 2,),
        in_specs=[pl.BlockSpec((2, 2 * WIN_H, LANES), lambda p: (p, 0, 0))],
        out_specs=pl.BlockSpec((WIN_H, 1, 2 * GRID_W, WIN_H * GRID_W), lambda p: (0, p, 0, 0)),
        compiler_params=_cparams(("arbitrary",)),
        name="na_bias_table",
    )(rpb_p)


def _na_kernel(q_ref, k0_ref, k1_ref, k2_ref, v0_ref, v1_ref, v2_ref, bias_ref, o_ref, kc_s, vc_s,
               s_s, p_s, r_s, *, n_row_blocks):
    rb = pl.program_id(2)
    npairs = q_ref.shape[0]
    blk = NA_ROWS * GRID_W
    nkeys = WIN_H * GRID_W
    for i, (kr, vr) in enumerate(((k0_ref, v0_ref), (k1_ref, v1_ref), (k2_ref, v2_ref))):
        kc_s[:, i * blk:(i + 1) * blk, :] = kr[...]
        vc_s[:, i * blk:(i + 1) * blk, :] = vr[...]
    first = rb == 0
    last = rb == n_row_blocks - 1
    edge = first | last
    lane = lax.broadcasted_iota(I32, (GRID_W, LANES), 1)
    left = lane < NA_HEAD_DIM
    scale = jnp.asarray(NA_HEAD_DIM ** -0.5, BF16)

    def window(qi):
        off = jnp.where(first, 0, jnp.where(last, blk, qi * GRID_W))
        li = jnp.where(edge, NA_ROWS - 1 - qi, NA_ROWS - 1)
        return pl.multiple_of(off, GRID_W), li

    def pair_body(pp, carry):
        for qi in range(NA_ROWS):
            off, li = window(qi)
            q2 = q_ref[pp, qi * GRID_W:(qi + 1) * GRID_W, :] * scale
            zero = jnp.zeros_like(q2)
            qs = jnp.concatenate([jnp.where(left, q2, zero), jnp.where(left, zero, q2)], axis=0)
            kw = kc_s[pp, pl.ds(off, nkeys), :]
            s_s[qi] = _dot_nt(qs, kw) + bias_ref[li, pp]
        for qi in range(NA_ROWS):
            s = s_s[qi]
            p = jnp.exp(s - jnp.max(s, axis=-1, keepdims=True))
            r_s[qi] = 1.0 / jnp.sum(p, axis=-1, keepdims=True)
            p_s[qi] = p.astype(BF16)
        for qi in range(NA_ROWS):
            off, _ = window(qi)
            pv = _dot(p_s[qi], vc_s[pp, pl.ds(off, nkeys), :]) * r_s[qi]
            o = jnp.where(left, pv[0:GRID_W], pv[GRID_W:2 * GRID_W])
            o_ref[pp, qi * GRID_W:(qi + 1) * GRID_W, :] = o.astype(BF16)
        return carry

    lax.fori_loop(0, npairs, pair_body, 0, unroll=4)


def _na_attention(qkv_t, bias_tab, *, batch, seq):
    t = batch * seq
    blk = NA_ROWS * GRID_W
    nrb = seq // blk
    hp = NA_PAIRS // 2
    nsec = NA_PAIRS // hp

    def kv_spec(sec, i):
        def imap(hh, b, r):
            return (sec * nsec + hh, b * nrb + jnp.clip(r - 1, 0, nrb - 3) + i, 0)
        return pl.BlockSpec((hp, blk, LANES), imap)

    return pl.pallas_call(
        functools.partial(_na_kernel, n_row_blocks=nrb),
        out_shape=jax.ShapeDtypeStruct((NA_PAIRS, t, LANES), BF16),
        grid=(nsec, batch, nrb),
        in_specs=[pl.BlockSpec((hp, blk, LANES), lambda hh, b, r: (hh, b * nrb + r, 0)),
                  kv_spec(1, 0), kv_spec(1, 1), kv_spec(1, 2),
                  kv_spec(2, 0), kv_spec(2, 1), kv_spec(2, 2),
                  pl.BlockSpec((NA_ROWS, hp, 2 * GRID_W, WIN_H * GRID_W),
                               lambda hh, b, r: (jnp.where(r == 0, 1, 0), hh, 0, 0))],
        out_specs=pl.BlockSpec((hp, blk, LANES), lambda hh, b, r: (hh, b * nrb + r, 0)),
        scratch_shapes=[pltpu.VMEM((hp, 3 * blk, LANES), BF16),
                        pltpu.VMEM((hp, 3 * blk, LANES), BF16),
                        pltpu.VMEM((NA_ROWS, 2 * GRID_W, WIN_H * GRID_W), F32),
                        pltpu.VMEM((NA_ROWS, 2 * GRID_W, WIN_H * GRID_W), BF16),
                        pltpu.VMEM((NA_ROWS, 2 * GRID_W, 1), F32)],
        compiler_params=_cparams(("arbitrary", "arbitrary", "arbitrary")),
        name="na_attention",
    )(qkv_t, qkv_t, qkv_t, qkv_t, qkv_t, qkv_t, qkv_t, bias_tab)


def _na_mixer(x, mod_sc, mod_sh, mod_g, nw, w_qkv, rpb, w_o, *, batch, seq):
    qkv_t = _nm_matmul(x, nw, mod_sc, mod_sh, w_qkv, col0=0, ncols=3 * D_MODEL, tn=1024,
                       out_dtype=BF16, rows_per_batch=seq, pair_major=True)
    o_t = _na_attention(qkv_t, _na_bias_table(rpb), batch=batch, seq=seq)
    return _mm_resid(o_t, w_o, x, mod_g, rows_per_batch=seq, pair_major=True)


U32 = jnp.uint32


def _pack_halves(vb):
    n = vb.shape[1] // 2
    bits = pltpu.bitcast(vb.astype(F32), U32)
    return (bits[:, :n] >> 16) | bits[:, n:]


def _unpack_halves(w):
    return pltpu.bitcast(w << 16, F32), pltpu.bitcast(w & U32(0xFFFF0000), F32)

def _router_kernel(x_ref, nw_ref, sc_ref, sh_ref, wr_ref, h_ref, meta_ref, meta_t_ref, cnt_ref,
                   carry_s, whi_s, wlo_s):
    @pl.when(pl.program_id(0) == 0)
    def _():
        carry_s[...] = jnp.zeros_like(carry_s)
        whi_s[...], wlo_s[...] = _split2(wr_ref[...])

    h = _normmod(x_ref[...], nw_ref[...], sc_ref[0], sh_ref[0])
    h_ref[...] = _pack_halves(h.astype(BF16))
    logits = _dot_split(h, whi_s[...], wlo_s[...])
    tm = logits.shape[0]
    lane_i = lax.broadcasted_iota(I32, logits.shape, 1)
    lane = lane_i.astype(F32)
    big = 1e9
    gl = jnp.where(lane_i < MOE_GROUPS, logits, NEG)
    gmax = jnp.max(gl, axis=1, keepdims=True)
    gsel = jnp.min(jnp.where(gl == gmax, lane, big), axis=1, keepdims=True)
    gw = 1.0 / jnp.sum(jnp.exp(gl - gmax), axis=1, keepdims=True)
    el = lane - MOE_GROUPS
    lo = gsel * MOE_EPG
    emask = (el >= lo) & (el < lo + MOE_EPG)
    e1 = jnp.where(emask, logits, NEG)
    m1 = jnp.max(e1, axis=1, keepdims=True)
    i1 = jnp.min(jnp.where(e1 == m1, el, big), axis=1, keepdims=True)
    e2 = jnp.where(emask & (el != i1), logits, NEG)
    m2 = jnp.max(e2, axis=1, keepdims=True)
    i2 = jnp.min(jnp.where(e2 == m2, el, big), axis=1, keepdims=True)
    tt = jnp.exp(m2 - m1)
    p1 = 1.0 / (1.0 + tt)
    w1 = gw * p1
    w2 = gw * (tt * p1)
    oh1 = el == i1
    oh2 = el == i2
    cnt = (oh1 | oh2).astype(F32)
    r_i = lax.broadcasted_iota(I32, (tm, tm), 0)
    c_i = lax.broadcasted_iota(I32, (tm, tm), 1)
    before = _dot((r_i > c_i).astype(BF16), cnt.astype(BF16)) + carry_s[...]
    rank1 = jnp.sum(jnp.where(oh1, before, 0.0), axis=1, keepdims=True)
    rank2 = jnp.sum(jnp.where(oh2, before, 0.0), axis=1, keepdims=True)
    carry_s[...] = carry_s[...] + jnp.sum(cnt, axis=0, keepdims=True)
    meta = jnp.zeros_like(logits)
    for pos, val in enumerate((i1, i2, w1, w2, rank1, rank2)):
        meta = jnp.where(lane_i == pos, val, meta)
    meta_ref[...] = meta
    meta_t_ref[...] = meta.T[0:meta_t_ref.shape[0], :]
    cnt_ref[...] = jnp.broadcast_to(carry_s[...], cnt_ref.shape)


def _router(x, nw, sc, sh, wr, *, rows_per_batch):
    t, d = x.shape
    tm = 256
    tiles_per_batch = rows_per_batch // tm
    return pl.pallas_call(
        _router_kernel,
        out_shape=(jax.ShapeDtypeStruct((t, d // 2), U32),
                   jax.ShapeDtypeStruct((t, LANES), F32),
                   jax.ShapeDtypeStruct((8, t), F32),
                   jax.ShapeDtypeStruct((8, LANES), F32)),
        grid=(t // tm,),
        in_specs=[pl.BlockSpec((tm, d), lambda i: (i, 0)),
                  pl.BlockSpec((1, d), lambda i: (0, 0)),
                  pl.BlockSpec((1, 1, d), lambda i: (i // tiles_per_batch, 0, 0)),
                  pl.BlockSpec((1, 1, d), lambda i: (i // tiles_per_batch, 0, 0)),
                  pl.BlockSpec((d, LANES), lambda i: (0, 0))],
        out_specs=(pl.BlockSpec((tm, d // 2), lambda i: (i, 0)),
                   pl.BlockSpec((tm, LANES), lambda i: (i, 0)),
                   pl.BlockSpec((8, tm), lambda i: (0, i)),
                   pl.BlockSpec((8, LANES), lambda i: (0, 0))),
        scratch_shapes=[pltpu.VMEM((1, LANES), F32), pltpu.VMEM((d, LANES), BF16),
                        pltpu.VMEM((d, LANES), BF16)],
        compiler_params=_cparams(("arbitrary",)),
        name="moe_router",
    )(x, nw, sc, sh, wr)


def _slot_kernel(pstart_ref, mt_ref, o_ref):
    eid = mt_ref[0:2, :]
    start = jnp.zeros(eid.shape, I32)
    for e in range(N_EXPERTS):
        start = jnp.where(eid == float(e), pstart_ref[e], start)
    o_ref[...] = start + mt_ref[4:6, :].astype(I32)


def _slots(pstart, meta_t):
    t = meta_t.shape[1]
    return pl.pallas_call(
        _slot_kernel,
        out_shape=jax.ShapeDtypeStruct((2, t), I32),
        grid_spec=pltpu.PrefetchScalarGridSpec(
            num_scalar_prefetch=1,
            grid=(1,),
            in_specs=[pl.BlockSpec(meta_t.shape, lambda i, ps: (0, 0))],
            out_specs=pl.BlockSpec((2, t), lambda i, ps: (0, 0))),
        compiler_params=_cparams(("arbitrary",)),
        name="moe_slots",
    )(pstart, meta_t)


def _dispatch_kernel(dest_ref, zflag_ref, h_ref, xs_ref, zbuf, sem, zsem):
    tm = h_ref.shape[0]
    base = pl.program_id(0) * tm
    ntok = pl.num_programs(0) * tm
    tb = zbuf.shape[0]

    @pl.when(pl.program_id(0) == 0)
    def _():
        zbuf[...] = jnp.zeros_like(zbuf)

        def zcopy(b):
            return pltpu.make_async_copy(zbuf, xs_ref.at[pl.ds(b * tb, tb)], zsem)

        def zstart(b, carry):
            @pl.when(zflag_ref[b] == 1)
            def _():
                zcopy(b).start()
            return carry

        def zwait(b, carry):
            @pl.when(zflag_ref[b] == 1)
            def _():
                zcopy(b).wait()
            return carry

        nblk = xs_ref.shape[0] // tb
        lax.fori_loop(0, nblk, zstart, 0)
        lax.fori_loop(0, nblk, zwait, 0)

    def copy(t, k):
        return pltpu.make_async_copy(h_ref.at[pl.ds(t, 1)],
                                     xs_ref.at[pl.ds(dest_ref[k * ntok + base + t], 1)], sem)

    def issue(t, carry):
        copy(t, 0).start()
        copy(t, 1).start()
        return carry

    def drain(t, carry):
        copy(t, 0).wait()
        copy(t, 1).wait()
        return carry

    lax.fori_loop(0, tm, issue, 0, unroll=DMA_UNROLL)
    lax.fori_loop(0, tm, drain, 0, unroll=DMA_UNROLL)


def _dispatch(dest, zflag, h, n_slots):
    t, d = h.shape
    tm = 256
    return pl.pallas_call(
        _dispatch_kernel,
        out_shape=jax.ShapeDtypeStruct((n_slots, d), U32),
        grid_spec=pltpu.PrefetchScalarGridSpec(
            num_scalar_prefetch=2,
            grid=(t // tm,),
            in_specs=[pl.BlockSpec((tm, d), lambda i, dest, zf: (i, 0))],
            out_specs=pl.BlockSpec(memory_space=pl.ANY),
            scratch_shapes=[pltpu.VMEM((MOE_TB, d), U32), pltpu.SemaphoreType.DMA(()),
                            pltpu.SemaphoreType.DMA(())]),
        compiler_params=_cparams(("arbitrary",)),
        name="moe_dispatch",
    )(dest, zflag, h)


def _ffn_kernel(be_ref, nxt_ref, slot_ref, nu_ref, xs_ref, w1_hbm, w3_hbm, w2_hbm, o_ref,
                wb1, wb3, wb2, w1_s, w3_s, w2_s, sem, *, layer):
    i = pl.program_id(0)

    def fetch(e, s):
        return (pltpu.make_async_copy(w1_hbm.at[layer, e], wb1.at[s], sem.at[s, 0]),
                pltpu.make_async_copy(w3_hbm.at[layer, e], wb3.at[s], sem.at[s, 1]),
                pltpu.make_async_copy(w2_hbm.at[layer, e], wb2.at[s], sem.at[s, 2]))

    @pl.when(i < nu_ref[0])
    def _():
        e = be_ref[i]
        s = slot_ref[i]

        @pl.when((i == 0) | (e != be_ref[jnp.maximum(i - 1, 0)]))
        def _():
            @pl.when(i == 0)
            def _():
                for cp in fetch(e, s):
                    cp.start()

            for cp in fetch(e, s):
                cp.wait()

            @pl.when(nxt_ref[i] >= 0)
            def _():
                for cp in fetch(nxt_ref[i], 1 - s):
                    cp.start()

            w1_s[...] = wb1[s].astype(BF16)
            w3_s[...] = wb3[s].astype(BF16)
            w2_s[...] = wb2[s].astype(BF16)

        lo, hi = _unpack_halves(xs_ref[...])
        xl, xh = lo.astype(BF16), hi.astype(BF16)
        half = xl.shape[1]
        a = _dot(xl, w1_s[0:half, :]) + _dot(xh, w1_s[half:2 * half, :])
        b = _dot(xl, w3_s[0:half, :]) + _dot(xh, w3_s[half:2 * half, :])
        hmid = _silu(a) * b
        o_ref[...] = _pack_halves(_dot(hmid.astype(BF16), w2_s[...]).astype(BF16))

    @pl.when(i >= nu_ref[0])
    def _():
        o_ref[...] = jnp.zeros_like(o_ref)


def _expert_ffn(blk_expert, blk_next, blk_slot, n_used, xs, w1, w3, w2, layer):
    p, dh = xs.shape
    d = 2 * dh
    f = w1.shape[3]
    tb = MOE_TB
    nb = p // tb
    hbm = pl.BlockSpec(memory_space=pl.ANY)
    return pl.pallas_call(
        functools.partial(_ffn_kernel, layer=layer),
        out_shape=jax.ShapeDtypeStruct((p, dh), U32),
        grid_spec=pltpu.PrefetchScalarGridSpec(
            num_scalar_prefetch=4,
            grid=(nb,),
            in_specs=[pl.BlockSpec((tb, dh), lambda i, be, nx, sl, nu: (jnp.minimum(i, nu[0] - 1), 0)),
                      hbm, hbm, hbm],
            out_specs=pl.BlockSpec((tb, dh), lambda i, be, nx, sl, nu: (i, 0)),
            scratch_shapes=[pltpu.VMEM((2, d, f), F32), pltpu.VMEM((2, d, f), F32),
                            pltpu.VMEM((2, f, d), F32),
                            pltpu.VMEM((d, f), BF16), pltpu.VMEM((d, f), BF16),
                            pltpu.VMEM((f, d), BF16),
                            pltpu.SemaphoreType.DMA((2, 3))]),
        compiler_params=_cparams(("arbitrary",)),
        name="moe_expert_ffn",
    )(blk_expert, blk_next, blk_slot, n_used, xs, w1, w3, w2)


def _combine_kernel(dest_ref, x_ref, meta_ref, g_ref, fnw_ref, ys_ref, o_ref, buf, sem, *, final):
    tm = x_ref.shape[0]
    i = pl.program_id(0)
    nsteps = pl.num_programs(0)
    ntok = nsteps * tm

    def copy(tile, t, k):
        half = tile % 2
        return pltpu.make_async_copy(ys_ref.at[pl.ds(dest_ref[k * ntok + tile * tm + t], 1)],
                                     buf.at[half, k, pl.ds(t, 1)], sem.at[half])

    def issue(tile):
        def body(t, carry):
            copy(tile, t, 0).start()
            copy(tile, t, 1).start()
            return carry
        lax.fori_loop(0, tm, body, 0, unroll=DMA_UNROLL)

    def drain(tile):
        def body(t, carry):
            copy(tile, t, 0).wait()
            copy(tile, t, 1).wait()
            return carry
        lax.fori_loop(0, tm, body, 0, unroll=DMA_UNROLL)

    @pl.when(i == 0)
    def _():
        issue(i)

    @pl.when(i + 1 < nsteps)
    def _():
        issue(i + 1)

    drain(i)
    cur = i % 2
    meta = meta_ref[...]
    w1, w2 = meta[:, 2:3], meta[:, 3:4]
    lo1, hi1 = _unpack_halves(buf[cur, 0])
    lo2, hi2 = _unpack_halves(buf[cur, 1])
    y = jnp.concatenate([w1 * lo1 + w2 * lo2, w1 * hi1 + w2 * hi2], axis=1)
    xn = x_ref[...] + g_ref[0] * y
    if final:
        ms = jnp.mean(xn * xn, axis=-1, keepdims=True)
        xn = xn * lax.rsqrt(ms + EPS) * fnw_ref[...]
    o_ref[...] = xn


def _combine(dest, x, meta, g, fnw, ys, *, rows_per_batch, final):
    t, d = x.shape
    tm = 256
    tiles_per_batch = rows_per_batch // tm
    return pl.pallas_call(
        functools.partial(_combine_kernel, final=final),
        out_shape=jax.ShapeDtypeStruct((t, d), F32),
        grid_spec=pltpu.PrefetchScalarGridSpec(
            num_scalar_prefetch=1,
            grid=(t // tm,),
            in_specs=[pl.BlockSpec((tm, d), lambda i, dest: (i, 0)),
                      pl.BlockSpec((tm, LANES), lambda i, dest: (i, 0)),
                      pl.BlockSpec((1, 1, d), lambda i, dest: (i // tiles_per_batch, 0, 0)),
                      pl.BlockSpec((1, d), lambda i, dest: (0, 0)),
                      pl.BlockSpec(memory_space=pl.ANY)],
            out_specs=pl.BlockSpec((tm, d), lambda i, dest: (i, 0)),
            scratch_shapes=[pltpu.VMEM((2, 2, tm, d // 2), U32), pltpu.SemaphoreType.DMA((2,))]),
        compiler_params=_cparams(("arbitrary",)),
        name="moe_combine",
    )(dest, x, meta, g, fnw, ys)


def _hier_moe(x, mod_sc, mod_sh, mod_g, nw, w_group, w_expert, w1, w3, w2, layer, fnw, *,
              rows_per_batch, final):
    t, d = x.shape
    a = 2 * t
    tb = MOE_TB
    wr = jnp.concatenate([w_group, w_expert], axis=1)
    wr = jnp.pad(wr, ((0, 0), (0, LANES - wr.shape[1])))
    h, meta, meta_t, cnt = _router(x, nw, mod_sc, mod_sh, wr, rows_per_batch=rows_per_batch)
    ne = N_EXPERTS
    counts = cnt[0, MOE_GROUPS:MOE_GROUPS + ne].astype(I32)
    padded = ((counts + tb - 1) // tb) * tb
    pend = jnp.cumsum(padded)
    pstart = pend - padded
    dest = _slots(pstart, meta_t).reshape(a)
    nb = (a + ne * (tb - 1) + tb - 1) // tb
    n_used = (pend[-1] // tb).astype(I32)
    blk = jnp.arange(nb, dtype=I32)
    be = jnp.minimum(jnp.sum((pend[None, :] <= (blk * tb)[:, None]).astype(I32), axis=1), ne - 1)
    seg_last = jnp.any((pend[None, :] == ((blk + 1) * tb)[:, None]) & (padded[None, :] > 0), axis=1)
    zflag = (seg_last | (blk >= n_used)).astype(I32)
    nonempty = counts > 0
    seg = jnp.cumsum(nonempty.astype(I32)) - 1
    later = lax.cummin(jnp.where(nonempty, jnp.arange(ne, dtype=I32), ne), axis=0, reverse=True)
    nxt_e = jnp.concatenate([later[1:], jnp.full((1,), ne, I32)])
    nxt_e = jnp.where(nxt_e == ne, -1, nxt_e)
    xs = _dispatch(dest, zflag, h, nb * tb)
    ys = _expert_ffn(be, nxt_e[be], seg[be] % 2, n_used.reshape(1), xs, w1, w3, w2, layer)
    return _combine(dest, x, meta, mod_g, fnw, ys, rows_per_batch=rows_per_batch, final=final)


def kernel(x, c, ada_w, ada_b, norm_mix, norm_ffn, ssd_w_in, ssd_conv_w, ssd_conv_b, ssd_a_log,
           ssd_dt_bias, ssd_d, ssd_norm_w, ssd_w_out, na_w_qkv, na_rpb, na_w_o,
           moe_w_group, moe_w_expert, moe_w1, moe_w3, moe_w2, final_norm):
    batch, seq, d = x.shape
    depth = ada_w.shape[0]
    xt = x.reshape(batch * seq, d)
    c_pad = jnp.pad(c, ((0, 8 - batch), (0, 0)))
    mod = _ada(c_pad, ada_w, ada_b)[:, :batch]
    fnw = final_norm.reshape(1, d)
    for i in range(depth):
        sh1, sc1, g1, sh2, sc2, g2 = [mod[i, :, k * d:(k + 1) * d].reshape(batch, 1, d)
                                      for k in range(6)]
        j = i // 2
        nw = norm_mix[i].reshape(1, d)
        if i % 2 == 0:
            xt = _ssd_mixer(xt, sc1, sh1, g1, nw, ssd_w_in[j], ssd_conv_w[j], ssd_conv_b[j],
                            ssd_a_log[j], ssd_dt_bias[j], ssd_d[j], ssd_norm_w[j], ssd_w_out[j],
                            batch=batch, seq=seq)
        else:
            xt = _na_mixer(xt, sc1, sh1, g1, nw, na_w_qkv[j], na_rpb[j], na_w_o[j],
                           batch=batch, seq=seq)
        xt = _hier_moe(xt, sc2, sh2, g2, norm_ffn[i].reshape(1, d), moe_w_group[i], moe_w_expert[i],
                       moe_w1, moe_w3, moe_w2, i, fnw, rows_per_batch=seq,
                       final=(i == depth - 1))
    return xt.reshape(batch, seq, d)
```

```python
import functools

import jax
import jax.numpy as jnp
from jax import lax
from jax.experimental import pallas as pl
from jax.experimental.pallas import tpu as pltpu

F32 = jnp.float32
BF16 = jnp.bfloat16
I32 = jnp.int32

EPS = 1e-6
NEG = -1e30
LOG2_E = 1.4426950408889634

D_MODEL = 2048
GRID_W = 64
SSD_HEAD_DIM = 64
SSD_GROUPS = 8
SSD_HEADS_PER_GROUP = 8
SSD_STATE = 128
SSD_CONV = 5
D_INNER = 2 * D_MODEL
GROUP_W = SSD_HEADS_PER_GROUP * SSD_HEAD_DIM
SSD_Q = 128
CONV_HALO = 16
NA_HEAD_DIM = 64
NA_HEADS = D_MODEL // NA_HEAD_DIM
NA_PAIRS = NA_HEADS // 2
WIN_H = 8
WIN_W = 16
NA_ROWS = 4
MOE_GROUPS = 4
MOE_EPG = 8
N_EXPERTS = MOE_GROUPS * MOE_EPG
MOE_D_FF = D_MODEL // 4
MOE_TB = 256
DMA_UNROLL = 8

VMEM_LIMIT = 56 * 1024 * 1024
LANES = 128


def _cparams(sem):
    return pltpu.CompilerParams(dimension_semantics=sem, vmem_limit_bytes=VMEM_LIMIT)


def _silu(v):
    return v * pl.reciprocal(1.0 + jnp.exp(-v), approx=True)


def _softplus(v):
    return jnp.maximum(v, 0.0) + jnp.log1p(jnp.exp(-jnp.abs(v)))


def _split3(v):
    hi = v.astype(BF16)
    r1 = v - hi.astype(F32)
    mid = r1.astype(BF16)
    lo = (r1 - mid.astype(F32)).astype(BF16)
    return hi, mid, lo


def _dot(a, b):
    return jnp.dot(a, b, preferred_element_type=F32)


def _dot_nt(a, b):
    return lax.dot_general(a, b, (((1,), (1,)), ((), ())), preferred_element_type=F32)


def _dot3_left(v, sel):
    hi, mid, lo = _split3(v)
    return _dot(hi, sel) + _dot(mid, sel) + _dot(lo, sel)


def _normmod(x, nw, sc, sh):
    ms = jnp.mean(x * x, axis=-1, keepdims=True)
    return (x * lax.rsqrt(ms + EPS) * nw) * (1.0 + sc) + sh


def _split2(v):
    hi = v.astype(BF16)
    return hi, (v - hi.astype(F32)).astype(BF16)


def _dot_split(a, b_hi, b_lo):
    a_hi, a_lo = _split2(a)
    return _dot(a_hi, b_hi) + (_dot(a_lo, b_hi) + _dot(a_hi, b_lo))


def _ada_kernel(c_ref, w_ref, b_ref, o_ref):
    c = c_ref[...]
    o_ref[0] = _dot_split(c / (1.0 + jnp.exp(-c)), *_split2(w_ref[0])) + b_ref[0]


def _ada(c_pad, ada_w, ada_b):
    depth, d, n = ada_w.shape
    tn = 1024
    return pl.pallas_call(
        _ada_kernel,
        out_shape=jax.ShapeDtypeStruct((depth, 8, n), F32),
        grid=(depth, n // tn),
        in_specs=[pl.BlockSpec((8, d), lambda i, j: (0, 0)),
                  pl.BlockSpec((1, d, tn), lambda i, j: (i, 0, j)),
                  pl.BlockSpec((1, 1, tn), lambda i, j: (i, 0, j))],
        out_specs=pl.BlockSpec((1, 8, tn), lambda i, j: (i, 0, j)),
        compiler_params=_cparams(("arbitrary", "arbitrary")),
        name="ada_mod",
    )(c_pad, ada_w, ada_b.reshape(depth, 1, n))


def _nm_mm_kernel(x_ref, nw_ref, sc_ref, sh_ref, w_ref, o_ref, h_ref, *, pair_major):
    @pl.when(pl.program_id(1) == 0)
    def _():
        h_ref[...] = _normmod(x_ref[...], nw_ref[...], sc_ref[0], sh_ref[0]).astype(BF16)

    r = _dot(h_ref[...], w_ref[...].astype(BF16))
    if pair_major:
        for c in range(o_ref.shape[0]):
            o_ref[c] = r[:, c * LANES:(c + 1) * LANES].astype(o_ref.dtype)
    else:
        o_ref[...] = r.astype(o_ref.dtype)


def _nm_matmul(x, nw, sc, sh, w, *, col0, ncols, tn, out_dtype, rows_per_batch, pair_major=False):
    t, d = x.shape
    tm = 1024
    tiles_per_batch = rows_per_batch // tm
    nj = ncols // tn
    jb = col0 // tn
    if pair_major:
        out_shape = jax.ShapeDtypeStruct((ncols // LANES, t, LANES), out_dtype)
        out_spec = pl.BlockSpec((tn // LANES, tm, LANES), lambda i, j: (j, i, 0))
    else:
        out_shape = jax.ShapeDtypeStruct((t, ncols), out_dtype)
        out_spec = pl.BlockSpec((tm, tn), lambda i, j: (i, j))
    return pl.pallas_call(
        functools.partial(_nm_mm_kernel, pair_major=pair_major),
        out_shape=out_shape,
        grid=(t // tm, nj),
        in_specs=[pl.BlockSpec((tm, d), lambda i, j: (i, 0)),
                  pl.BlockSpec((1, d), lambda i, j: (0, 0)),
                  pl.BlockSpec((1, 1, d), lambda i, j: (i // tiles_per_batch, 0, 0)),
                  pl.BlockSpec((1, 1, d), lambda i, j: (i // tiles_per_batch, 0, 0)),
                  pl.BlockSpec((d, tn), lambda i, j: (0, jb + j))],
        out_specs=out_spec,
        scratch_shapes=[pltpu.VMEM((tm, d), BF16)],
        compiler_params=_cparams(("arbitrary", "arbitrary")),
        name="norm_mod_matmul",
    )(x, nw, sc, sh, w)


def _mm_resid_kernel(a_ref, w_ref, x_ref, g_ref, o_ref, *, pair_major):
    if pair_major:
        a = jnp.concatenate([a_ref[c] for c in range(a_ref.shape[0])], axis=1)
    else:
        a = a_ref[...]
    o_ref[...] = x_ref[...] + g_ref[0] * _dot(a, w_ref[...].astype(BF16))


def _mm_resid(a, w, x, g, *, rows_per_batch, pair_major=False):
    t, n = x.shape
    k = w.shape[0]
    tm, tn = 1024, 512
    tiles_per_batch = rows_per_batch // tm
    if pair_major:
        a_spec = pl.BlockSpec((k // LANES, tm, LANES), lambda i, j: (0, i, 0))
    else:
        a_spec = pl.BlockSpec((tm, k), lambda i, j: (i, 0))
    return pl.pallas_call(
        functools.partial(_mm_resid_kernel, pair_major=pair_major),
        out_shape=jax.ShapeDtypeStruct((t, n), F32),
        grid=(t // tm, n // tn),
        in_specs=[a_spec,
                  pl.BlockSpec((k, tn), lambda i, j: (0, j)),
                  pl.BlockSpec((tm, tn), lambda i, j: (i, j)),
                  pl.BlockSpec((1, 1, tn), lambda i, j: (i // tiles_per_batch, 0, j))],
        out_specs=pl.BlockSpec((tm, tn), lambda i, j: (i, j)),
        compiler_params=_cparams(("arbitrary", "arbitrary")),
        name="matmul_resid",
    )(a, w, x, g)


def _ssd_kernel(z_ref, x_ref, b_ref, c_ref, dtr_ref,
                cwx_ref, cwb_ref, cwc_ref, cbx_ref, cbb_ref, cbc_ref,
                alr_ref, dbr_ref, dsk_ref, nw_ref,
                o_ref,
                xc_s, bc_s, cc_s, yacc_s, st_s, cv_s):
    seq = x_ref.shape[0]
    q = SSD_Q
    nc = seq // q
    halo = CONV_HALO
    nrow = 2 * SSD_HEADS_PER_GROUP

    def conv_piece(j, base, src_ref, w_ref, bias_ref, dst_ref, lo, stage):
        cols = slice(lo, lo + LANES)
        pstart = pl.multiple_of(jnp.maximum(base - halo, 0), halo)
        nstart = pl.multiple_of(jnp.minimum(base + q, seq - halo), halo)
        stage[0:halo, :] = jnp.where(j > 0, src_ref[pl.ds(pstart, halo), cols].astype(F32), 0.0)
        stage[halo:halo + q, :] = src_ref[pl.ds(base, q), cols].astype(F32)
        stage[halo + q:, :] = jnp.where(j < nc - 1, src_ref[pl.ds(nstart, halo), cols].astype(F32), 0.0)
        acc = jnp.broadcast_to(bias_ref[:, cols], (q, LANES))
        for k in range(SSD_CONV):
            first = halo - SSD_CONV // 2 + k
            acc = acc + w_ref[k:k + 1, cols] * stage[first:first + q, :]
        dst_ref[pl.ds(base, q), cols] = _silu(acc).astype(BF16)

    def conv_chunk(j, carry):
        base = pl.multiple_of(j * q, q)
        npx = GROUP_W // LANES
        for i in range(npx):
            conv_piece(j, base, x_ref, cwx_ref, cbx_ref, xc_s, i * LANES, cv_s.at[i])
        conv_piece(j, base, b_ref, cwb_ref, cbb_ref, bc_s, 0, cv_s.at[npx])
        conv_piece(j, base, c_ref, cwc_ref, cbc_ref, cc_s, 0, cv_s.at[npx + 1])
        return carry

    lax.fori_loop(0, nc, conv_chunk, 0)

    row_i = lax.broadcasted_iota(I32, (q, q), 0)
    col_i = lax.broadcasted_iota(I32, (q, q), 1)
    lower = row_i >= col_i
    upper = row_i <= col_i
    lower_b = lower.astype(BF16)
    upper_b = upper.astype(BF16)
    lane_w = lax.broadcasted_iota(I32, (q, GROUP_W), 1)
    even_head = (lane_w & (LANES - 1)) < SSD_HEAD_DIM
    left = lax.broadcasted_iota(I32, (q, LANES), 1) < SSD_HEAD_DIM
    a_row = -jnp.exp(alr_ref[...])
    pad_rows = jnp.zeros((LANES - nrow, q), F32)

    def scan_pass(direction):
        hoff = direction * SSD_HEADS_PER_GROUP
        mask = lower if direction == 0 else upper
        tri = upper_b if direction == 0 else lower_b
        edge = q - 1 if direction == 0 else 0
        st_s[...] = jnp.zeros_like(st_s)

        def chunk(t, carry):
            c = t if direction == 0 else nc - 1 - t
            base = pl.multiple_of(c * q, q)
            rows = pl.ds(base, q)
            dt_r = _softplus(dtr_ref[c] + dbr_ref[...])
            cum_r = _dot3_left(dt_r * a_row, tri) * LOG2_E
            cum_c = jnp.concatenate([cum_r, pad_rows], axis=0).T
            decdt_r = jnp.exp2(cum_r[:, edge:edge + 1] - cum_r) * dt_r
            src_r = cum_r - jnp.log2(dt_r)

            xcb = xc_s[rows, :]
            zero_b = jnp.zeros_like(xcb)
            x_even = jnp.where(even_head, xcb, zero_b)
            x_odd = jnp.where(even_head, zero_b, xcb)
            bm = bc_s[rows, :]
            cm = cc_s[rows, :]
            cb = _dot_nt(cm, bm)
            bm_t = bm.astype(F32).T
            y_off = _dot(cm, st_s[...].astype(BF16))

            ssq = jnp.zeros((q, 1), F32)
            for pp in range(SSD_HEADS_PER_GROUP // 2):
                cols = slice(pp * LANES, (pp + 1) * LANES)
                lhs_y, lhs_s, scales = [], [], []
                for par in range(2):
                    j = hoff + 2 * pp + par
                    cum_b = jnp.broadcast_to(cum_c[:, j:j + 1], (q, q))
                    lmat = jnp.exp2(jnp.where(mask, cum_b - src_r[j:j + 1, :], NEG))
                    lhs_y.append((cb * lmat).astype(BF16))
                    lhs_s.append((bm_t * decdt_r[j:j + 1, :]).astype(BF16))
                    scales.append(jnp.exp2(cum_b))
                x_rhs = jnp.concatenate([x_even[:, cols], x_odd[:, cols]], axis=0)
                sc_tile = jnp.where(left, scales[0], scales[1])
                y = _dot(jnp.concatenate(lhs_y, axis=1), x_rhs) + y_off[:, cols] * sc_tile
                st_s[:, cols] = (st_s[:, cols] * sc_tile[edge:edge + 1, :]
                                 + _dot(jnp.concatenate(lhs_s, axis=1), x_rhs))
                if direction == 0:
                    yacc_s[rows, cols] = y
                else:
                    total = yacc_s[rows, cols] + y + xcb[:, cols].astype(F32) * dsk_ref[:, cols]
                    gated = total * _silu(z_ref[rows, cols].astype(F32))
                    ssq = ssq + jnp.sum(gated * gated, axis=-1, keepdims=True)
                    yacc_s[rows, cols] = gated
            if direction == 1:
                inv = lax.rsqrt(ssq * (1.0 / GROUP_W) + EPS)
                o_ref[rows, :] = (yacc_s[rows, :] * inv * nw_ref[...]).astype(BF16)
            return carry

        lax.fori_loop(0, nc, chunk, 0, unroll=4)

    scan_pass(0)
    scan_pass(1)


def _ssd_core(zx, dt_row, conv_w, conv_b, al_row, db_row, dskip, norm_w, *, batch, seq):
    g = SSD_GROUPS
    nc = seq // SSD_Q
    xb = D_INNER // GROUP_W
    bb = (2 * D_INNER) // SSD_STATE
    cb = bb + g
    cwb = D_INNER // SSD_STATE
    cwc = cwb + g
    return pl.pallas_call(
        _ssd_kernel,
        out_shape=jax.ShapeDtypeStruct((batch * seq, D_INNER), BF16),
        grid=(batch, g),
        in_specs=[
            pl.BlockSpec((seq, GROUP_W), lambda b, i: (b, i)),
            pl.BlockSpec((seq, GROUP_W), lambda b, i: (b, xb + i)),
            pl.BlockSpec((seq, SSD_STATE), lambda b, i: (b, bb + i)),
            pl.BlockSpec((seq, SSD_STATE), lambda b, i: (b, cb + i)),
            pl.BlockSpec((None, None, nc, 2 * SSD_HEADS_PER_GROUP, SSD_Q), lambda b, i: (b, i, 0, 0, 0)),
            pl.BlockSpec((SSD_CONV, GROUP_W), lambda b, i: (0, i)),
            pl.BlockSpec((SSD_CONV, SSD_STATE), lambda b, i: (0, cwb + i)),
            pl.BlockSpec((SSD_CONV, SSD_STATE), lambda b, i: (0, cwc + i)),
            pl.BlockSpec((1, GROUP_W), lambda b, i: (0, i)),
            pl.BlockSpec((1, SSD_STATE), lambda b, i: (0, cwb + i)),
            pl.BlockSpec((1, SSD_STATE), lambda b, i: (0, cwc + i)),
            pl.BlockSpec((None, 2 * SSD_HEADS_PER_GROUP, 1), lambda b, i: (i, 0, 0)),
            pl.BlockSpec((None, 2 * SSD_HEADS_PER_GROUP, 1), lambda b, i: (i, 0, 0)),
            pl.BlockSpec((1, GROUP_W), lambda b, i: (0, i)),
            pl.BlockSpec((1, GROUP_W), lambda b, i: (0, i)),
        ],
        out_specs=pl.BlockSpec((seq, GROUP_W), lambda b, i: (b, i)),
        scratch_shapes=[pltpu.VMEM((seq, GROUP_W), BF16),
                        pltpu.VMEM((seq, SSD_STATE), BF16),
                        pltpu.VMEM((seq, SSD_STATE), BF16),
                        pltpu.VMEM((seq, GROUP_W), F32),
                        pltpu.VMEM((SSD_STATE, GROUP_W), F32),
                        pltpu.VMEM((GROUP_W // LANES + 2, SSD_Q + 2 * CONV_HALO, LANES), F32)],
        compiler_params=_cparams(("arbitrary", "arbitrary")),
        name="ssd_core",
    )(zx, zx, zx, zx, dt_row, conv_w, conv_w, conv_w, conv_b, conv_b, conv_b,
      al_row, db_row, dskip, norm_w)


def _ssd_mixer(x, mod_sc, mod_sh, mod_g, nw, w_in, conv_w, conv_b, a_log, dt_bias, d_skip, norm_w,
               w_out, *, batch, seq):
    g, r = SSD_GROUPS, SSD_HEADS_PER_GROUP
    conv_dim = conv_w.shape[1]
    zx = _nm_matmul(x, nw, mod_sc, mod_sh, w_in, col0=0, ncols=D_INNER + conv_dim, tn=1024,
                    out_dtype=BF16, rows_per_batch=seq)
    dt_raw = _nm_matmul(x, nw, mod_sc, mod_sh, w_in, col0=D_INNER + conv_dim, ncols=2 * g * r,
                        tn=2 * g * r, out_dtype=F32, rows_per_batch=seq)
    nc = seq // SSD_Q
    dt_row = dt_raw.reshape(batch, nc, SSD_Q, 2, g, r).transpose(0, 4, 1, 3, 5, 2)
    dt_row = dt_row.reshape(batch, g, nc, 2 * r, SSD_Q)

    def row_form(p):
        return p.reshape(2, g, r).transpose(1, 0, 2).reshape(g, 2 * r, 1)

    yn = _ssd_core(zx, dt_row, conv_w, conv_b.reshape(1, conv_dim), row_form(a_log), row_form(dt_bias),
                   jnp.repeat(d_skip, SSD_HEAD_DIM).reshape(1, D_INNER), norm_w.reshape(1, D_INNER),
                   batch=batch, seq=seq)
    return _mm_resid(yn, w_out, x, mod_g, rows_per_batch=seq)


def _bias_table_kernel(rpb_ref, o_ref):
    lane = lax.broadcasted_iota(I32, (GRID_W, LANES), 1)
    j = lax.broadcasted_iota(I32, (GRID_W, LANES), 0)
    c = lane & (GRID_W - 1)
    c0 = jnp.clip(j - WIN_W // 2, 0, GRID_W - WIN_W)
    win = (c >= c0) & (c < c0 + WIN_W)
    left = lane < GRID_W

    def one_offset(dy0, carry):
        for par in range(2):
            for m in range(WIN_H // 2):
                tiles = []
                for sub in range(2):
                    row = rpb_ref[par, pl.ds(dy0 + 2 * m + sub, 1), :]
                    shift = (sub * GRID_W - (WIN_W - 1)) % LANES
                    tiles.append(pltpu.roll(jnp.broadcast_to(row, (GRID_W, LANES)), shift, 1,
                                            stride=1, stride_axis=0))
                tile = jnp.where(win, jnp.where(left, tiles[0], tiles[1]), NEG)
                o_ref[dy0, 0, par * GRID_W:(par + 1) * GRID_W, m * LANES:(m + 1) * LANES] = tile
        return carry

    lax.fori_loop(0, o_ref.shape[0], one_offset, 0)


def _na_bias_table(rpb):
    h, ndy, ndx = rpb.shape
    rpb_p = jnp.pad(rpb, ((0, 0), (0, 2 * WIN_H - ndy), (0, LANES - ndx)))
    return pl.pallas_call(
        _bias_table_kernel,
        out_shape=jax.ShapeDtypeStruct((WIN_H, h // 2, 2 * GRID_W, WIN_H * GRID_W), F32),
        grid=(h // 2,),
        in_specs=[pl.BlockSpec((2, 2 * WIN_H, LANES), lambda p: (p, 0, 0))],
        out_specs=pl.BlockSpec((WIN_H, 1, 2 * GRID_W, WIN_H * GRID_W), lambda p: (0, p, 0, 0)),
        compiler_params=_cparams(("arbitrary",)),
        name="na_bias_table",
    )(rpb_p)


def _na_kernel(q_ref, k0_ref, k1_ref, k2_ref, v0_ref, v1_ref, v2_ref, bias_ref, o_ref, kc_s, vc_s,
               s_s, p_s, r_s, *, n_row_blocks):
    rb = pl.program_id(2)
    npairs = q_ref.shape[0]
    blk = NA_ROWS * GRID_W
    nkeys = WIN_H * GRID_W
    for i, (kr, vr) in enumerate(((k0_ref, v0_ref), (k1_ref, v1_ref), (k2_ref, v2_ref))):
        kc_s[:, i * blk:(i + 1) * blk, :] = kr[...]
        vc_s[:, i * blk:(i + 1) * blk, :] = vr[...]
    first = rb == 0
    last = rb == n_row_blocks - 1
    edge = first | last
    lane = lax.broadcasted_iota(I32, (GRID_W, LANES), 1)
    left = lane < NA_HEAD_DIM
    scale = jnp.asarray(NA_HEAD_DIM ** -0.5, BF16)

    def window(qi):
        off = jnp.where(first, 0, jnp.where(last, blk, qi * GRID_W))
        li = jnp.where(edge, NA_ROWS - 1 - qi, NA_ROWS - 1)
        return pl.multiple_of(off, GRID_W), li

    def pair_body(pp, carry):
        for qi in range(NA_ROWS):
            off, li = window(qi)
            q2 = q_ref[pp, qi * GRID_W:(qi + 1) * GRID_W, :] * scale
            zero = jnp.zeros_like(q2)
            qs = jnp.concatenate([jnp.where(left, q2, zero), jnp.where(left, zero, q2)], axis=0)
            kw = kc_s[pp, pl.ds(off, nkeys), :]
            s_s[qi] = _dot_nt(qs, kw) + bias_ref[li, pp]
        for qi in range(NA_ROWS):
            s = s_s[qi]
            p = jnp.exp(s - jnp.max(s, axis=-1, keepdims=True))
            r_s[qi] = 1.0 / jnp.sum(p, axis=-1, keepdims=True)
            p_s[qi] = p.astype(BF16)
        for qi in range(NA_ROWS):
            off, _ = window(qi)
            pv = _dot(p_s[qi], vc_s[pp, pl.ds(off, nkeys), :]) * r_s[qi]
            o = jnp.where(left, pv[0:GRID_W], pv[GRID_W:2 * GRID_W])
            o_ref[pp, qi * GRID_W:(qi + 1) * GRID_W, :] = o.astype(BF16)
        return carry

    lax.fori_loop(0, npairs, pair_body, 0, unroll=4)


def _na_attention(qkv_t, bias_tab, *, batch, seq):
    t = batch * seq
    blk = NA_ROWS * GRID_W
    nrb = seq // blk
    hp = NA_PAIRS // 2
    nsec = NA_PAIRS // hp

    def kv_spec(sec, i):
        def imap(hh, b, r):
            return (sec * nsec + hh, b * nrb + jnp.clip(r - 1, 0, nrb - 3) + i, 0)
        return pl.BlockSpec((hp, blk, LANES), imap)

    return pl.pallas_call(
        functools.partial(_na_kernel, n_row_blocks=nrb),
        out_shape=jax.ShapeDtypeStruct((NA_PAIRS, t, LANES), BF16),
        grid=(nsec, batch, nrb),
        in_specs=[pl.BlockSpec((hp, blk, LANES), lambda hh, b, r: (hh, b * nrb + r, 0)),
                  kv_spec(1, 0), kv_spec(1, 1), kv_spec(1, 2),
                  kv_spec(2, 0), kv_spec(2, 1), kv_spec(2, 2),
                  pl.BlockSpec((NA_ROWS, hp, 2 * GRID_W, WIN_H * GRID_W),
                               lambda hh, b, r: (jnp.where(r == 0, 1, 0), hh, 0, 0))],
        out_specs=pl.BlockSpec((hp, blk, LANES), lambda hh, b, r: (hh, b * nrb + r, 0)),
        scratch_shapes=[pltpu.VMEM((hp, 3 * blk, LANES), BF16),
                        pltpu.VMEM((hp, 3 * blk, LANES), BF16),
                        pltpu.VMEM((NA_ROWS, 2 * GRID_W, WIN_H * GRID_W), F32),
                        pltpu.VMEM((NA_ROWS, 2 * GRID_W, WIN_H * GRID_W), BF16),
                        pltpu.VMEM((NA_ROWS, 2 * GRID_W, 1), F32)],
        compiler_params=_cparams(("arbitrary", "arbitrary", "arbitrary")),
        name="na_attention",
    )(qkv_t, qkv_t, qkv_t, qkv_t, qkv_t, qkv_t, qkv_t, bias_tab)


def _na_mixer(x, mod_sc, mod_sh, mod_g, nw, w_qkv, rpb, w_o, *, batch, seq):
    qkv_t = _nm_matmul(x, nw, mod_sc, mod_sh, w_qkv, col0=0, ncols=3 * D_MODEL, tn=1024,
                       out_dtype=BF16, rows_per_batch=seq, pair_major=True)
    o_t = _na_attention(qkv_t, _na_bias_table(rpb), batch=batch, seq=seq)
    return _mm_resid(o_t, w_o, x, mod_g, rows_per_batch=seq, pair_major=True)


U32 = jnp.uint32


def _pack_halves(vb):
    n = vb.shape[1] // 2
    bits = pltpu.bitcast(vb.astype(F32), U32)
    return (bits[:, :n] >> 16) | bits[:, n:]


def _unpack_halves(w):
    return pltpu.bitcast(w << 16, F32), pltpu.bitcast(w & U32(0xFFFF0000), F32)


TOK_SUB = (D_MODEL // 2) // LANES


def _store_token_tiles(ref, packed):
    rows = packed.shape[0]
    for s in range(TOK_SUB):
        ref[pl.ds(s, rows, stride=TOK_SUB), :] = packed[:, s * LANES:(s + 1) * LANES]


def _load_token_tiles(ref):
    rows = ref.shape[0] // TOK_SUB
    return jnp.concatenate([ref[pl.ds(s, rows, stride=TOK_SUB), :] for s in range(TOK_SUB)], axis=1)


def _router_kernel(x_ref, nw_ref, sc_ref, sh_ref, wr_ref, h_ref, meta_ref, meta_t_ref, cnt_ref,
                   carry_s, whi_s, wlo_s):
    @pl.when(pl.program_id(0) == 0)
    def _():
        carry_s[...] = jnp.zeros_like(carry_s)
        whi_s[...], wlo_s[...] = _split2(wr_ref[...])

    h = _normmod(x_ref[...], nw_ref[...], sc_ref[0], sh_ref[0])
    _store_token_tiles(h_ref, _pack_halves(h.astype(BF16)))
    logits = _dot_split(h, whi_s[...], wlo_s[...])
    tm = logits.shape[0]
    lane_i = lax.broadcasted_iota(I32, logits.shape, 1)
    lane = lane_i.astype(F32)
    big = 1e9
    gl = jnp.where(lane_i < MOE_GROUPS, logits, NEG)
    gmax = jnp.max(gl, axis=1, keepdims=True)
    gsel = jnp.min(jnp.where(gl == gmax, lane, big), axis=1, keepdims=True)
    gw = 1.0 / jnp.sum(jnp.exp(gl - gmax), axis=1, keepdims=True)
    el = lane - MOE_GROUPS
    lo = gsel * MOE_EPG
    emask = (el >= lo) & (el < lo + MOE_EPG)
    e1 = jnp.where(emask, logits, NEG)
    m1 = jnp.max(e1, axis=1, keepdims=True)
    i1 = jnp.min(jnp.where(e1 == m1, el, big), axis=1, keepdims=True)
    e2 = jnp.where(emask & (el != i1), logits, NEG)
    m2 = jnp.max(e2, axis=1, keepdims=True)
    i2 = jnp.min(jnp.where(e2 == m2, el, big), axis=1, keepdims=True)
    tt = jnp.exp(m2 - m1)
    p1 = 1.0 / (1.0 + tt)
    w1 = gw * p1
    w2 = gw * (tt * p1)
    oh1 = el == i1
    oh2 = el == i2
    cnt = (oh1 | oh2).astype(F32)
    r_i = lax.broadcasted_iota(I32, (tm, tm), 0)
    c_i = lax.broadcasted_iota(I32, (tm, tm), 1)
    before = _dot((r_i > c_i).astype(BF16), cnt.astype(BF16)) + carry_s[...]
    rank1 = jnp.sum(jnp.where(oh1, before, 0.0), axis=1, keepdims=True)
    rank2 = jnp.sum(jnp.where(oh2, before, 0.0), axis=1, keepdims=True)
    carry_s[...] = carry_s[...] + jnp.sum(cnt, axis=0, keepdims=True)
    meta = jnp.zeros_like(logits)
    for pos, val in enumerate((i1, i2, w1, w2, rank1, rank2)):
        meta = jnp.where(lane_i == pos, val, meta)
    meta_ref[...] = meta
    meta_t_ref[...] = meta.T[0:meta_t_ref.shape[0], :]
    cnt_ref[...] = jnp.broadcast_to(carry_s[...], cnt_ref.shape)


def _router(x, nw, sc, sh, wr, *, rows_per_batch):
    t, d = x.shape
    tm = 256
    tiles_per_batch = rows_per_batch // tm
    return pl.pallas_call(
        _router_kernel,
        out_shape=(jax.ShapeDtypeStruct((t * TOK_SUB, LANES), U32),
                   jax.ShapeDtypeStruct((t, LANES), F32),
                   jax.ShapeDtypeStruct((8, t), F32),
                   jax.ShapeDtypeStruct((8, LANES), F32)),
        grid=(t // tm,),
        in_specs=[pl.BlockSpec((tm, d), lambda i: (i, 0)),
                  pl.BlockSpec((1, d), lambda i: (0, 0)),
                  pl.BlockSpec((1, 1, d), lambda i: (i // tiles_per_batch, 0, 0)),
                  pl.BlockSpec((1, 1, d), lambda i: (i // tiles_per_batch, 0, 0)),
                  pl.BlockSpec((d, LANES), lambda i: (0, 0))],
        out_specs=(pl.BlockSpec((tm * TOK_SUB, LANES), lambda i: (i, 0)),
                   pl.BlockSpec((tm, LANES), lambda i: (i, 0)),
                   pl.BlockSpec((8, tm), lambda i: (0, i)),
                   pl.BlockSpec((8, LANES), lambda i: (0, 0))),
        scratch_shapes=[pltpu.VMEM((1, LANES), F32), pltpu.VMEM((d, LANES), BF16),
                        pltpu.VMEM((d, LANES), BF16)],
        compiler_params=_cparams(("arbitrary",)),
        name="moe_router",
    )(x, nw, sc, sh, wr)


def _slot_kernel(pstart_ref, mt_ref, o_ref):
    eid = mt_ref[0:2, :]
    start = jnp.zeros(eid.shape, I32)
    for e in range(N_EXPERTS):
        start = jnp.where(eid == float(e), pstart_ref[e], start)
    o_ref[...] = start + mt_ref[4:6, :].astype(I32)


def _slots(pstart, meta_t):
    t = meta_t.shape[1]
    return pl.pallas_call(
        _slot_kernel,
        out_shape=jax.ShapeDtypeStruct((2, t), I32),
        grid_spec=pltpu.PrefetchScalarGridSpec(
            num_scalar_prefetch=1,
            grid=(1,),
            in_specs=[pl.BlockSpec(meta_t.shape, lambda i, ps: (0, 0))],
            out_specs=pl.BlockSpec((2, t), lambda i, ps: (0, 0))),
        compiler_params=_cparams(("arbitrary",)),
        name="moe_slots",
    )(pstart, meta_t)


def _token_tile(ref, r):
    return ref.at[pl.ds(pl.multiple_of(r * TOK_SUB, TOK_SUB), TOK_SUB)]


def _dispatch_kernel(dest_ref, zflag_ref, h_ref, xs_ref, zbuf, sem, zsem):
    tm = h_ref.shape[0] // TOK_SUB
    base = pl.program_id(0) * tm
    ntok = pl.num_programs(0) * tm
    tb = zbuf.shape[0]

    @pl.when(pl.program_id(0) == 0)
    def _():
        zbuf[...] = jnp.zeros_like(zbuf)

        def zcopy(b):
            return pltpu.make_async_copy(zbuf, xs_ref.at[pl.ds(pl.multiple_of(b * tb, tb), tb)], zsem)

        def zstart(b, carry):
            @pl.when(zflag_ref[b] == 1)
            def _():
                zcopy(b).start()
            return carry

        def zwait(b, carry):
            @pl.when(zflag_ref[b] == 1)
            def _():
                zcopy(b).wait()
            return carry

        nblk = xs_ref.shape[0] // tb
        lax.fori_loop(0, nblk, zstart, 0)
        lax.fori_loop(0, nblk, zwait, 0)

    def copy(t, k):
        return pltpu.make_async_copy(_token_tile(h_ref, t),
                                     _token_tile(xs_ref, dest_ref[k * ntok + base + t]), sem)

    def issue(t, carry):
        copy(t, 0).start()
        copy(t, 1).start()
        return carry

    def drain(t, carry):
        copy(t, 0).wait()
        copy(t, 1).wait()
        return carry

    lax.fori_loop(0, tm, issue, 0, unroll=DMA_UNROLL)
    lax.fori_loop(0, tm, drain, 0, unroll=DMA_UNROLL)


def _dispatch(dest, zflag, h, n_slots):
    t = h.shape[0] // TOK_SUB
    tm = 256
    return pl.pallas_call(
        _dispatch_kernel,
        out_shape=jax.ShapeDtypeStruct((n_slots * TOK_SUB, LANES), U32),
        grid_spec=pltpu.PrefetchScalarGridSpec(
            num_scalar_prefetch=2,
            grid=(t // tm,),
            in_specs=[pl.BlockSpec((tm * TOK_SUB, LANES), lambda i, dest, zf: (i, 0))],
            out_specs=pl.BlockSpec(memory_space=pl.ANY),
            scratch_shapes=[pltpu.VMEM((MOE_TB * TOK_SUB, LANES), U32), pltpu.SemaphoreType.DMA(()),
                            pltpu.SemaphoreType.DMA(())]),
        compiler_params=_cparams(("arbitrary",)),
        name="moe_dispatch",
    )(dest, zflag, h)


def _ffn_kernel(be_ref, nxt_ref, slot_ref, nu_ref, xs_ref, w1_hbm, w3_hbm, w2_hbm, o_ref,
                wb1, wb3, wb2, w1_s, w3_s, w2_s, sem, *, layer):
    i = pl.program_id(0)

    def fetch(e, s):
        return (pltpu.make_async_copy(w1_hbm.at[layer, e], wb1.at[s], sem.at[s, 0]),
                pltpu.make_async_copy(w3_hbm.at[layer, e], wb3.at[s], sem.at[s, 1]),
                pltpu.make_async_copy(w2_hbm.at[layer, e], wb2.at[s], sem.at[s, 2]))

    @pl.when(i < nu_ref[0])
    def _():
        e = be_ref[i]
        s = slot_ref[i]

        @pl.when((i == 0) | (e != be_ref[jnp.maximum(i - 1, 0)]))
        def _():
            @pl.when(i == 0)
            def _():
                for cp in fetch(e, s):
                    cp.start()

            for cp in fetch(e, s):
                cp.wait()

            @pl.when(nxt_ref[i] >= 0)
            def _():
                for cp in fetch(nxt_ref[i], 1 - s):
                    cp.start()

            w1_s[...] = wb1[s].astype(BF16)
            w3_s[...] = wb3[s].astype(BF16)
            w2_s[...] = wb2[s].astype(BF16)

        lo, hi = _unpack_halves(_load_token_tiles(xs_ref))
        xl, xh = lo.astype(BF16), hi.astype(BF16)
        half = xl.shape[1]
        a = _dot(xl, w1_s[0:half, :]) + _dot(xh, w1_s[half:2 * half, :])
        b = _dot(xl, w3_s[0:half, :]) + _dot(xh, w3_s[half:2 * half, :])
        hmid = _silu(a) * b
        _store_token_tiles(o_ref, _pack_halves(_dot(hmid.astype(BF16), w2_s[...]).astype(BF16)))

    @pl.when(i >= nu_ref[0])
    def _():
        o_ref[...] = jnp.zeros_like(o_ref)


def _expert_ffn(blk_expert, blk_next, blk_slot, n_used, xs, w1, w3, w2, layer):
    d, f = w1.shape[2], w1.shape[3]
    rows = MOE_TB * TOK_SUB
    nb = xs.shape[0] // rows
    hbm = pl.BlockSpec(memory_space=pl.ANY)
    return pl.pallas_call(
        functools.partial(_ffn_kernel, layer=layer),
        out_shape=jax.ShapeDtypeStruct(xs.shape, U32),
        grid_spec=pltpu.PrefetchScalarGridSpec(
            num_scalar_prefetch=4,
            grid=(nb,),
            in_specs=[pl.BlockSpec((rows, LANES),
                                   lambda i, be, nx, sl, nu: (jnp.minimum(i, nu[0] - 1), 0)),
                      hbm, hbm, hbm],
            out_specs=pl.BlockSpec((rows, LANES), lambda i, be, nx, sl, nu: (i, 0)),
            scratch_shapes=[pltpu.VMEM((2, d, f), F32), pltpu.VMEM((2, d, f), F32),
                            pltpu.VMEM((2, f, d), F32),
                            pltpu.VMEM((d, f), BF16), pltpu.VMEM((d, f), BF16),
                            pltpu.VMEM((f, d), BF16),
                            pltpu.SemaphoreType.DMA((2, 3))]),
        compiler_params=_cparams(("arbitrary",)),
        name="moe_expert_ffn",
    )(blk_expert, blk_next, blk_slot, n_used, xs, w1, w3, w2)


def _combine_kernel(dest_ref, x_ref, meta_ref, g_ref, fnw_ref, ys_ref, o_ref, buf, sem, *, final):
    tm = x_ref.shape[0]
    i = pl.program_id(0)
    nsteps = pl.num_programs(0)
    ntok = nsteps * tm

    def copy(tile, t, k):
        half = tile % 2
        return pltpu.make_async_copy(_token_tile(ys_ref, dest_ref[k * ntok + tile * tm + t]),
                                     _token_tile(buf.at[half, k], t), sem.at[half])

    def issue(tile):
        def body(t, carry):
            copy(tile, t, 0).start()
            copy(tile, t, 1).start()
            return carry
        lax.fori_loop(0, tm, body, 0, unroll=DMA_UNROLL)

    def drain(tile):
        def body(t, carry):
            copy(tile, t, 0).wait()
            copy(tile, t, 1).wait()
            return carry
        lax.fori_loop(0, tm, body, 0, unroll=DMA_UNROLL)

    @pl.when(i == 0)
    def _():
        issue(i)

    @pl.when(i + 1 < nsteps)
    def _():
        issue(i + 1)

    drain(i)
    cur = i % 2
    meta = meta_ref[...]
    w1, w2 = meta[:, 2:3], meta[:, 3:4]
    lo1, hi1 = _unpack_halves(_load_token_tiles(buf.at[cur, 0]))
    lo2, hi2 = _unpack_halves(_load_token_tiles(buf.at[cur, 1]))
    y = jnp.concatenate([w1 * lo1 + w2 * lo2, w1 * hi1 + w2 * hi2], axis=1)
    xn = x_ref[...] + g_ref[0] * y
    if final:
        ms = jnp.mean(xn * xn, axis=-1, keepdims=True)
        xn = xn * lax.rsqrt(ms + EPS) * fnw_ref[...]
    o_ref[...] = xn


def _combine(dest, x, meta, g, fnw, ys, *, rows_per_batch, final):
    t, d = x.shape
    tm = 256
    tiles_per_batch = rows_per_batch // tm
    return pl.pallas_call(
        functools.partial(_combine_kernel, final=final),
        out_shape=jax.ShapeDtypeStruct((t, d), F32),
        grid_spec=pltpu.PrefetchScalarGridSpec(
            num_scalar_prefetch=1,
            grid=(t // tm,),
            in_specs=[pl.BlockSpec((tm, d), lambda i, dest: (i, 0)),
                      pl.BlockSpec((tm, LANES), lambda i, dest: (i, 0)),
                      pl.BlockSpec((1, 1, d), lambda i, dest: (i // tiles_per_batch, 0, 0)),
                      pl.BlockSpec((1, d), lambda i, dest: (0, 0)),
                      pl.BlockSpec(memory_space=pl.ANY)],
            out_specs=pl.BlockSpec((tm, d), lambda i, dest: (i, 0)),
            scratch_shapes=[pltpu.VMEM((2, 2, tm * TOK_SUB, LANES), U32),
                            pltpu.SemaphoreType.DMA((2,))]),
        compiler_params=_cparams(("arbitrary",)),
        name="moe_combine",
    )(dest, x, meta, g, fnw, ys)


def _hier_moe(x, mod_sc, mod_sh, mod_g, nw, w_group, w_expert, w1, w3, w2, layer, fnw, *,
              rows_per_batch, final):
    t, d = x.shape
    a = 2 * t
    tb = MOE_TB
    wr = jnp.concatenate([w_group, w_expert], axis=1)
    wr = jnp.pad(wr, ((0, 0), (0, LANES - wr.shape[1])))
    h, meta, meta_t, cnt = _router(x, nw, mod_sc, mod_sh, wr, rows_per_batch=rows_per_batch)
    ne = N_EXPERTS
    counts = cnt[0, MOE_GROUPS:MOE_GROUPS + ne].astype(I32)
    padded = ((counts + tb - 1) // tb) * tb
    pend = jnp.cumsum(padded)
    pstart = pend - padded
    dest = _slots(pstart, meta_t).reshape(a)
    nb = (a + ne * (tb - 1) + tb - 1) // tb
    n_used = (pend[-1] // tb).astype(I32)
    blk = jnp.arange(nb, dtype=I32)
    be = jnp.minimum(jnp.sum((pend[None, :] <= (blk * tb)[:, None]).astype(I32), axis=1), ne - 1)
    seg_last = jnp.any((pend[None, :] == ((blk + 1) * tb)[:, None]) & (padded[None, :] > 0), axis=1)
    zflag = (seg_last | (blk >= n_used)).astype(I32)
    nonempty = counts > 0
    seg = jnp.cumsum(nonempty.astype(I32)) - 1
    later = lax.cummin(jnp.where(nonempty, jnp.arange(ne, dtype=I32), ne), axis=0, reverse=True)
    nxt_e = jnp.concatenate([later[1:], jnp.full((1,), ne, I32)])
    nxt_e = jnp.where(nxt_e == ne, -1, nxt_e)
    xs = _dispatch(dest, zflag, h, nb * tb)
    ys = _expert_ffn(be, nxt_e[be], seg[be] % 2, n_used.reshape(1), xs, w1, w3, w2, layer)
    return _combine(dest, x, meta, mod_g, fnw, ys, rows_per_batch=rows_per_batch, final=final)


def kernel(x, c, ada_w, ada_b, norm_mix, norm_ffn, ssd_w_in, ssd_conv_w, ssd_conv_b, ssd_a_log,
           ssd_dt_bias, ssd_d, ssd_norm_w, ssd_w_out, na_w_qkv, na_rpb, na_w_o,
           moe_w_group, moe_w_expert, moe_w1, moe_w3, moe_w2, final_norm):
    batch, seq, d = x.shape
    depth = ada_w.shape[0]
    xt = x.reshape(batch * seq, d)
    c_pad = jnp.pad(c, ((0, 8 - batch), (0, 0)))
    mod = _ada(c_pad, ada_w, ada_b)[:, :batch]
    fnw = final_norm.reshape(1, d)
    for i in range(depth):
        sh1, sc1, g1, sh2, sc2, g2 = [mod[i, :, k * d:(k + 1) * d].reshape(batch, 1, d)
                                      for k in range(6)]
        j = i // 2
        nw = norm_mix[i].reshape(1, d)
        if i % 2 == 0:
            xt = _ssd_mixer(xt, sc1, sh1, g1, nw, ssd_w_in[j], ssd_conv_w[j], ssd_conv_b[j],
                            ssd_a_log[j], ssd_dt_bias[j], ssd_d[j], ssd_norm_w[j], ssd_w_out[j],
                            batch=batch, seq=seq)
        else:
            xt = _na_mixer(xt, sc1, sh1, g1, nw, na_w_qkv[j], na_rpb[j], na_w_o[j],
                           batch=batch, seq=seq)
        xt = _hier_moe(xt, sc2, sh2, g2, norm_ffn[i].reshape(1, d), moe_w_group[i], moe_w_expert[i],
                       moe_w1, moe_w3, moe_w2, i, fnw, rows_per_batch=seq,
                       final=(i == depth - 1))
    return xt.reshape(batch, seq, d)
```

```python
import functools

import jax
import jax.numpy as jnp
from jax import lax
from jax.experimental import pallas as pl
from jax.experimental.pallas import tpu as pltpu

F32 = jnp.float32
BF16 = jnp.bfloat16
I32 = jnp.int32

EPS = 1e-6
NEG = -1e30
LOG2_E = 1.4426950408889634

D_MODEL = 2048
GRID_W = 64
SSD_HEAD_DIM = 64
SSD_GROUPS = 8
SSD_HEADS_PER_GROUP = 8
SSD_STATE = 128
SSD_CONV = 5
D_INNER = 2 * D_MODEL
GROUP_W = SSD_HEADS_PER_GROUP * SSD_HEAD_DIM
SSD_Q = 128
CONV_HALO = 16
NA_HEAD_DIM = 64
NA_HEADS = D_MODEL // NA_HEAD_DIM
NA_PAIRS = NA_HEADS // 2
WIN_H = 8
WIN_W = 16
NA_ROWS = 4
MOE_GROUPS = 4
MOE_EPG = 8
N_EXPERTS = MOE_GROUPS * MOE_EPG
MOE_D_FF = D_MODEL // 4
MOE_TB = 256
DMA_UNROLL = 8

VMEM_LIMIT = 56 * 1024 * 1024
LANES = 128


def _cparams(sem):
    return pltpu.CompilerParams(dimension_semantics=sem, vmem_limit_bytes=VMEM_LIMIT)


def _silu(v):
    return v * pl.reciprocal(1.0 + jnp.exp(-v), approx=True)


def _softplus(v):
    return jnp.maximum(v, 0.0) + jnp.log1p(jnp.exp(-jnp.abs(v)))


def _split3(v):
    hi = v.astype(BF16)
    r1 = v - hi.astype(F32)
    mid = r1.astype(BF16)
    lo = (r1 - mid.astype(F32)).astype(BF16)
    return hi, mid, lo


def _dot(a, b):
    return jnp.dot(a, b, preferred_element_type=F32)


def _dot_nt(a, b):
    return lax.dot_general(a, b, (((1,), (1,)), ((), ())), preferred_element_type=F32)


def _dot3_left(v, sel):
    hi, mid, lo = _split3(v)
    return _dot(hi, sel) + _dot(mid, sel) + _dot(lo, sel)


def _normmod(x, nw, sc, sh):
    ms = jnp.mean(x * x, axis=-1, keepdims=True)
    return (x * lax.rsqrt(ms + EPS) * nw) * (1.0 + sc) + sh


def _split2(v):
    hi = v.astype(BF16)
    return hi, (v - hi.astype(F32)).astype(BF16)


def _dot_split(a, b_hi, b_lo):
    a_hi, a_lo = _split2(a)
    return _dot(a_hi, b_hi) + (_dot(a_lo, b_hi) + _dot(a_hi, b_lo))


def _ada_kernel(c_ref, w_ref, b_ref, o_ref):
    c = c_ref[...]
    o_ref[0] = _dot_split(c / (1.0 + jnp.exp(-c)), *_split2(w_ref[0])) + b_ref[0]


def _ada(c_pad, ada_w, ada_b):
    depth, d, n = ada_w.shape
    tn = 1024
    return pl.pallas_call(
        _ada_kernel,
        out_shape=jax.ShapeDtypeStruct((depth, 8, n), F32),
        grid=(depth, n // tn),
        in_specs=[pl.BlockSpec((8, d), lambda i, j: (0, 0)),
                  pl.BlockSpec((1, d, tn), lambda i, j: (i, 0, j)),
                  pl.BlockSpec((1, 1, tn), lambda i, j: (i, 0, j))],
        out_specs=pl.BlockSpec((1, 8, tn), lambda i, j: (i, 0, j)),
        compiler_params=_cparams(("arbitrary", "arbitrary")),
        name="ada_mod",
    )(c_pad, ada_w, ada_b.reshape(depth, 1, n))


def _nm_mm_kernel(x_ref, nw_ref, sc_ref, sh_ref, w_ref, *rest, pair_major, tail):
    if tail:
        wt_ref, o_ref, ot_ref, h_ref = rest
    else:
        o_ref, h_ref = rest

    @pl.when(pl.program_id(1) == 0)
    def _():
        h_ref[...] = _normmod(x_ref[...], nw_ref[...], sc_ref[0], sh_ref[0]).astype(BF16)

    r = _dot(h_ref[...], w_ref[...].astype(BF16))
    if pair_major:
        for c in range(o_ref.shape[0]):
            o_ref[c] = r[:, c * LANES:(c + 1) * LANES].astype(o_ref.dtype)
    else:
        o_ref[...] = r.astype(o_ref.dtype)

    if tail:
        @pl.when(pl.program_id(1) == pl.num_programs(1) - 1)
        def _():
            ot_ref[...] = _dot(h_ref[...], wt_ref[...].astype(BF16))


def _nm_matmul(x, nw, sc, sh, w, *, ncols, tn, out_dtype, rows_per_batch, pair_major=False,
               tail_cols=0):
    t, d = x.shape
    tm = 1024
    tiles_per_batch = rows_per_batch // tm
    if pair_major:
        out_shape = jax.ShapeDtypeStruct((ncols // LANES, t, LANES), out_dtype)
        out_spec = pl.BlockSpec((tn // LANES, tm, LANES), lambda i, j: (j, i, 0))
    else:
        out_shape = jax.ShapeDtypeStruct((t, ncols), out_dtype)
        out_spec = pl.BlockSpec((tm, tn), lambda i, j: (i, j))
    in_specs = [pl.BlockSpec((tm, d), lambda i, j: (i, 0)),
                pl.BlockSpec((1, d), lambda i, j: (0, 0)),
                pl.BlockSpec((1, 1, d), lambda i, j: (i // tiles_per_batch, 0, 0)),
                pl.BlockSpec((1, 1, d), lambda i, j: (i // tiles_per_batch, 0, 0)),
                pl.BlockSpec((d, tn), lambda i, j: (0, j))]
    operands = [x, nw, sc, sh, w]
    if tail_cols:
        tail_blk = ncols // tail_cols
        in_specs.append(pl.BlockSpec((d, tail_cols), lambda i, j: (0, tail_blk)))
        operands.append(w)
        out_shape = (out_shape, jax.ShapeDtypeStruct((t, tail_cols), F32))
        out_spec = (out_spec, pl.BlockSpec((tm, tail_cols), lambda i, j: (i, 0)))
    return pl.pallas_call(
        functools.partial(_nm_mm_kernel, pair_major=pair_major, tail=bool(tail_cols)),
        out_shape=out_shape,
        grid=(t // tm, ncols // tn),
        in_specs=in_specs,
        out_specs=out_spec,
        scratch_shapes=[pltpu.VMEM((tm, d), BF16)],
        compiler_params=_cparams(("arbitrary", "arbitrary")),
        name="norm_mod_matmul",
    )(*operands)


def _mm_resid_kernel(a_ref, w_ref, x_ref, g_ref, o_ref, *, pair_major):
    if pair_major:
        a = jnp.concatenate([a_ref[c] for c in range(a_ref.shape[0])], axis=1)
    else:
        a = a_ref[...]
    o_ref[...] = x_ref[...] + g_ref[0] * _dot(a, w_ref[...].astype(BF16))


def _mm_resid(a, w, x, g, *, rows_per_batch, pair_major=False):
    t, n = x.shape
    k = w.shape[0]
    tm, tn = 1024, 512
    tiles_per_batch = rows_per_batch // tm
    if pair_major:
        a_spec = pl.BlockSpec((k // LANES, tm, LANES), lambda i, j: (0, i, 0))
    else:
        a_spec = pl.BlockSpec((tm, k), lambda i, j: (i, 0))
    return pl.pallas_call(
        functools.partial(_mm_resid_kernel, pair_major=pair_major),
        out_shape=jax.ShapeDtypeStruct((t, n), F32),
        grid=(t // tm, n // tn),
        in_specs=[a_spec,
                  pl.BlockSpec((k, tn), lambda i, j: (0, j)),
                  pl.BlockSpec((tm, tn), lambda i, j: (i, j)),
                  pl.BlockSpec((1, 1, tn), lambda i, j: (i // tiles_per_batch, 0, j))],
        out_specs=pl.BlockSpec((tm, tn), lambda i, j: (i, j)),
        compiler_params=_cparams(("arbitrary", "arbitrary")),
        name="matmul_resid",
    )(a, w, x, g)


def _ssd_kernel(z_ref, x_ref, b_ref, c_ref, dtr_ref,
                cwx_ref, cwb_ref, cwc_ref, cbx_ref, cbb_ref, cbc_ref,
                alr_ref, dbr_ref, dsk_ref, nw_ref,
                o_ref,
                xc_s, bc_s, cc_s, yacc_s, st_s, cv_s):
    seq = x_ref.shape[0]
    q = SSD_Q
    nc = seq // q
    halo = CONV_HALO
    nrow = 2 * SSD_HEADS_PER_GROUP

    def conv_piece(j, base, src_ref, w_ref, bias_ref, dst_ref, lo, stage):
        cols = slice(lo, lo + LANES)
        pstart = pl.multiple_of(jnp.maximum(base - halo, 0), halo)
        nstart = pl.multiple_of(jnp.minimum(base + q, seq - halo), halo)
        stage[0:halo, :] = jnp.where(j > 0, src_ref[pl.ds(pstart, halo), cols].astype(F32), 0.0)
        stage[halo:halo + q, :] = src_ref[pl.ds(base, q), cols].astype(F32)
        stage[halo + q:, :] = jnp.where(j < nc - 1, src_ref[pl.ds(nstart, halo), cols].astype(F32), 0.0)
        acc = jnp.broadcast_to(bias_ref[:, cols], (q, LANES))
        for k in range(SSD_CONV):
            first = halo - SSD_CONV // 2 + k
            acc = acc + w_ref[k:k + 1, cols] * stage[first:first + q, :]
        dst_ref[pl.ds(base, q), cols] = _silu(acc).astype(BF16)

    def conv_chunk(j, carry):
        base = pl.multiple_of(j * q, q)
        npx = GROUP_W // LANES
        for i in range(npx):
            conv_piece(j, base, x_ref, cwx_ref, cbx_ref, xc_s, i * LANES, cv_s.at[i])
        conv_piece(j, base, b_ref, cwb_ref, cbb_ref, bc_s, 0, cv_s.at[npx])
        conv_piece(j, base, c_ref, cwc_ref, cbc_ref, cc_s, 0, cv_s.at[npx + 1])
        return carry

    lax.fori_loop(0, nc, conv_chunk, 0)

    row_i = lax.broadcasted_iota(I32, (q, q), 0)
    col_i = lax.broadcasted_iota(I32, (q, q), 1)
    lower = row_i >= col_i
    upper = row_i <= col_i
    lower_b = lower.astype(BF16)
    upper_b = upper.astype(BF16)
    lane_w = lax.broadcasted_iota(I32, (q, GROUP_W), 1)
    even_head = (lane_w & (LANES - 1)) < SSD_HEAD_DIM
    left = lax.broadcasted_iota(I32, (q, LANES), 1) < SSD_HEAD_DIM
    a_row = -jnp.exp(alr_ref[...])
    pad_rows = jnp.zeros((LANES - nrow, q), F32)

    def scan_pass(direction):
        hoff = direction * SSD_HEADS_PER_GROUP
        mask = lower if direction == 0 else upper
        tri = upper_b if direction == 0 else lower_b
        edge = q - 1 if direction == 0 else 0
        st_s[...] = jnp.zeros_like(st_s)

        def chunk(t, carry):
            c = t if direction == 0 else nc - 1 - t
            base = pl.multiple_of(c * q, q)
            rows = pl.ds(base, q)
            dt_r = _softplus(dtr_ref[c] + dbr_ref[...])
            cum_r = _dot3_left(dt_r * a_row, tri) * LOG2_E
            cum_c = jnp.concatenate([cum_r, pad_rows], axis=0).T
            decdt_r = jnp.exp2(cum_r[:, edge:edge + 1] - cum_r) * dt_r
            src_r = cum_r - jnp.log2(dt_r)

            xcb = xc_s[rows, :]
            zero_b = jnp.zeros_like(xcb)
            x_even = jnp.where(even_head, xcb, zero_b)
            x_odd = jnp.where(even_head, zero_b, xcb)
            bm = bc_s[rows, :]
            cm = cc_s[rows, :]
            cb = _dot_nt(cm, bm)
            bm_t = bm.astype(F32).T
            y_off = _dot(cm, st_s[...].astype(BF16))

            ssq = jnp.zeros((q, 1), F32)
            for pp in range(SSD_HEADS_PER_GROUP // 2):
                cols = slice(pp * LANES, (pp + 1) * LANES)
                lhs_y, lhs_s, scales = [], [], []
                for par in range(2):
                    j = hoff + 2 * pp + par
                    cum_b = jnp.broadcast_to(cum_c[:, j:j + 1], (q, q))
                    lmat = jnp.exp2(jnp.where(mask, cum_b - src_r[j:j + 1, :], NEG))
                    lhs_y.append((cb * lmat).astype(BF16))
                    lhs_s.append((bm_t * decdt_r[j:j + 1, :]).astype(BF16))
                    scales.append(jnp.exp2(cum_b))
                x_rhs = jnp.concatenate([x_even[:, cols], x_odd[:, cols]], axis=0)
                sc_tile = jnp.where(left, scales[0], scales[1])
                y = _dot(jnp.concatenate(lhs_y, axis=1), x_rhs) + y_off[:, cols] * sc_tile
                st_s[:, cols] = (st_s[:, cols] * sc_tile[edge:edge + 1, :]
                                 + _dot(jnp.concatenate(lhs_s, axis=1), x_rhs))
                if direction == 0:
                    yacc_s[rows, cols] = y
                else:
                    total = yacc_s[rows, cols] + y + xcb[:, cols].astype(F32) * dsk_ref[:, cols]
                    gated = total * _silu(z_ref[rows, cols].astype(F32))
                    ssq = ssq + jnp.sum(gated * gated, axis=-1, keepdims=True)
                    yacc_s[rows, cols] = gated
            if direction == 1:
                inv = lax.rsqrt(ssq * (1.0 / GROUP_W) + EPS)
                o_ref[rows, :] = (yacc_s[rows, :] * inv * nw_ref[...]).astype(BF16)
            return carry

        lax.fori_loop(0, nc, chunk, 0, unroll=4)

    scan_pass(0)
    scan_pass(1)


def _ssd_core(zx, dt_row, conv_w, conv_b, al_row, db_row, dskip, norm_w, *, batch, seq):
    g = SSD_GROUPS
    nc = seq // SSD_Q
    xb = D_INNER // GROUP_W
    bb = (2 * D_INNER) // SSD_STATE
    cb = bb + g
    cwb = D_INNER // SSD_STATE
    cwc = cwb + g
    return pl.pallas_call(
        _ssd_kernel,
        out_shape=jax.ShapeDtypeStruct((batch * seq, D_INNER), BF16),
        grid=(batch, g),
        in_specs=[
            pl.BlockSpec((seq, GROUP_W), lambda b, i: (b, i)),
            pl.BlockSpec((seq, GROUP_W), lambda b, i: (b, xb + i)),
            pl.BlockSpec((seq, SSD_STATE), lambda b, i: (b, bb + i)),
            pl.BlockSpec((seq, SSD_STATE), lambda b, i: (b, cb + i)),
            pl.BlockSpec((None, None, nc, 2 * SSD_HEADS_PER_GROUP, SSD_Q), lambda b, i: (b, i, 0, 0, 0)),
            pl.BlockSpec((SSD_CONV, GROUP_W), lambda b, i: (0, i)),
            pl.BlockSpec((SSD_CONV, SSD_STATE), lambda b, i: (0, cwb + i)),
            pl.BlockSpec((SSD_CONV, SSD_STATE), lambda b, i: (0, cwc + i)),
            pl.BlockSpec((1, GROUP_W), lambda b, i: (0, i)),
            pl.BlockSpec((1, SSD_STATE), lambda b, i: (0, cwb + i)),
            pl.BlockSpec((1, SSD_STATE), lambda b, i: (0, cwc + i)),
            pl.BlockSpec((None, 2 * SSD_HEADS_PER_GROUP, 1), lambda b, i: (i, 0, 0)),
            pl.BlockSpec((None, 2 * SSD_HEADS_PER_GROUP, 1), lambda b, i: (i, 0, 0)),
            pl.BlockSpec((1, GROUP_W), lambda b, i: (0, i)),
            pl.BlockSpec((1, GROUP_W), lambda b, i: (0, i)),
        ],
        out_specs=pl.BlockSpec((seq, GROUP_W), lambda b, i: (b, i)),
        scratch_shapes=[pltpu.VMEM((seq, GROUP_W), BF16),
                        pltpu.VMEM((seq, SSD_STATE), BF16),
                        pltpu.VMEM((seq, SSD_STATE), BF16),
                        pltpu.VMEM((seq, GROUP_W), F32),
                        pltpu.VMEM((SSD_STATE, GROUP_W), F32),
                        pltpu.VMEM((GROUP_W // LANES + 2, SSD_Q + 2 * CONV_HALO, LANES), F32)],
        compiler_params=_cparams(("arbitrary", "arbitrary")),
        name="ssd_core",
    )(zx, zx, zx, zx, dt_row, conv_w, conv_w, conv_w, conv_b, conv_b, conv_b,
      al_row, db_row, dskip, norm_w)


def _ssd_mixer(x, mod_sc, mod_sh, mod_g, nw, w_in, conv_w, conv_b, a_log, dt_bias, d_skip, norm_w,
               w_out, *, batch, seq):
    g, r = SSD_GROUPS, SSD_HEADS_PER_GROUP
    conv_dim = conv_w.shape[1]
    zx, dt_raw = _nm_matmul(x, nw, mod_sc, mod_sh, w_in, ncols=D_INNER + conv_dim, tn=1024,
                            out_dtype=BF16, rows_per_batch=seq, tail_cols=2 * g * r)
    nc = seq // SSD_Q
    dt_row = dt_raw.reshape(batch, nc, SSD_Q, 2, g, r).transpose(0, 4, 1, 3, 5, 2)
    dt_row = dt_row.reshape(batch, g, nc, 2 * r, SSD_Q)

    def row_form(p):
        return p.reshape(2, g, r).transpose(1, 0, 2).reshape(g, 2 * r, 1)

    yn = _ssd_core(zx, dt_row, conv_w, conv_b.reshape(1, conv_dim), row_form(a_log), row_form(dt_bias),
                   jnp.repeat(d_skip, SSD_HEAD_DIM).reshape(1, D_INNER), norm_w.reshape(1, D_INNER),
                   batch=batch, seq=seq)
    return _mm_resid(yn, w_out, x, mod_g, rows_per_batch=seq)


def _bias_table_kernel(rpb_ref, o_ref):
    lane = lax.broadcasted_iota(I32, (GRID_W, LANES), 1)
    j = lax.broadcasted_iota(I32, (GRID_W, LANES), 0)
    c = lane & (GRID_W - 1)
    c0 = jnp.clip(j - WIN_W // 2, 0, GRID_W - WIN_W)
    win = (c >= c0) & (c < c0 + WIN_W)
    left = lane < GRID_W

    def one_offset(dy0, carry):
        for par in range(2):
            for m in range(WIN_H // 2):
                tiles = []
                for sub in range(2):
                    row = rpb_ref[par, pl.ds(dy0 + 2 * m + sub, 1), :]
                    shift = (sub * GRID_W - (WIN_W - 1)) % LANES
                    tiles.append(pltpu.roll(jnp.broadcast_to(row, (GRID_W, LANES)), shift, 1,
                                            stride=1, stride_axis=0))
                tile = jnp.where(win, jnp.where(left, tiles[0], tiles[1]), NEG)
                o_ref[dy0, 0, par * GRID_W:(par + 1) * GRID_W, m * LANES:(m + 1) * LANES] = tile
        return carry

    lax.fori_loop(0, o_ref.shape[0], one_offset, 0)


def _na_bias_table(rpb):
    h, ndy, ndx = rpb.shape
    rpb_p = jnp.pad(rpb, ((0, 0), (0, 2 * WIN_H - ndy), (0, LANES - ndx)))
    return pl.pallas_call(
        _bias_table_kernel,
        out_shape=jax.ShapeDtypeStruct((WIN_H, h // 2, 2 * GRID_W, WIN_H * GRID_W), F32),
        grid=(h // 2,),
        in_specs=[pl.BlockSpec((2, 2 * WIN_H, LANES), lambda p: (p, 0, 0))],
        out_specs=pl.BlockSpec((WIN_H, 1, 2 * GRID_W, WIN_H * GRID_W), lambda p: (0, p, 0, 0)),
        compiler_params=_cparams(("arbitrary",)),
        name="na_bias_table",
    )(rpb_p)


def _na_kernel(q_ref, k0_ref, k1_ref, k2_ref, v0_ref, v1_ref, v2_ref, bias_ref, o_ref, kc_s, vc_s,
               s_s, p_s, r_s, *, n_row_blocks):
    rb = pl.program_id(2)
    npairs = q_ref.shape[0]
    blk = NA_ROWS * GRID_W
    nkeys = WIN_H * GRID_W
    for i, (kr, vr) in enumerate(((k0_ref, v0_ref), (k1_ref, v1_ref), (k2_ref, v2_ref))):
        kc_s[:, i * blk:(i + 1) * blk, :] = kr[...]
        vc_s[:, i * blk:(i + 1) * blk, :] = vr[...]
    first = rb == 0
    last = rb == n_row_blocks - 1
    edge = first | last
    lane = lax.broadcasted_iota(I32, (GRID_W, LANES), 1)
    left = lane < NA_HEAD_DIM
    scale = jnp.asarray(NA_HEAD_DIM ** -0.5, BF16)

    def window(qi):
        off = jnp.where(first, 0, jnp.where(last, blk, qi * GRID_W))
        li = jnp.where(edge, NA_ROWS - 1 - qi, NA_ROWS - 1)
        return pl.multiple_of(off, GRID_W), li

    def pair_body(pp, carry):
        for qi in range(NA_ROWS):
            off, li = window(qi)
            q2 = q_ref[pp, qi * GRID_W:(qi + 1) * GRID_W, :] * scale
            zero = jnp.zeros_like(q2)
            qs = jnp.concatenate([jnp.where(left, q2, zero), jnp.where(left, zero, q2)], axis=0)
            kw = kc_s[pp, pl.ds(off, nkeys), :]
            s_s[qi] = _dot_nt(qs, kw) + bias_ref[li, pp]
        for qi in range(NA_ROWS):
            s = s_s[qi]
            p = jnp.exp(s - jnp.max(s, axis=-1, keepdims=True))
            r_s[qi] = 1.0 / jnp.sum(p, axis=-1, keepdims=True)
            p_s[qi] = p.astype(BF16)
        for qi in range(NA_ROWS):
            off, _ = window(qi)
            pv = _dot(p_s[qi], vc_s[pp, pl.ds(off, nkeys), :]) * r_s[qi]
            o = jnp.where(left, pv[0:GRID_W], pv[GRID_W:2 * GRID_W])
            o_ref[pp, qi * GRID_W:(qi + 1) * GRID_W, :] = o.astype(BF16)
        return carry

    lax.fori_loop(0, npairs, pair_body, 0, unroll=4)


def _na_attention(qkv_t, bias_tab, *, batch, seq):
    t = batch * seq
    blk = NA_ROWS * GRID_W
    nrb = seq // blk
    hp = NA_PAIRS // 2
    nsec = NA_PAIRS // hp

    def kv_spec(sec, i):
        def imap(hh, b, r):
            return (sec * nsec + hh, b * nrb + jnp.clip(r - 1, 0, nrb - 3) + i, 0)
        return pl.BlockSpec((hp, blk, LANES), imap)

    return pl.pallas_call(
        functools.partial(_na_kernel, n_row_blocks=nrb),
        out_shape=jax.ShapeDtypeStruct((NA_PAIRS, t, LANES), BF16),
        grid=(nsec, batch, nrb),
        in_specs=[pl.BlockSpec((hp, blk, LANES), lambda hh, b, r: (hh, b * nrb + r, 0)),
                  kv_spec(1, 0), kv_spec(1, 1), kv_spec(1, 2),
                  kv_spec(2, 0), kv_spec(2, 1), kv_spec(2, 2),
                  pl.BlockSpec((NA_ROWS, hp, 2 * GRID_W, WIN_H * GRID_W),
                               lambda hh, b, r: (jnp.where(r == 0, 1, 0), hh, 0, 0))],
        out_specs=pl.BlockSpec((hp, blk, LANES), lambda hh, b, r: (hh, b * nrb + r, 0)),
        scratch_shapes=[pltpu.VMEM((hp, 3 * blk, LANES), BF16),
                        pltpu.VMEM((hp, 3 * blk, LANES), BF16),
                        pltpu.VMEM((NA_ROWS, 2 * GRID_W, WIN_H * GRID_W), F32),
                        pltpu.VMEM((NA_ROWS, 2 * GRID_W, WIN_H * GRID_W), BF16),
                        pltpu.VMEM((NA_ROWS, 2 * GRID_W, 1), F32)],
        compiler_params=_cparams(("arbitrary", "arbitrary", "arbitrary")),
        name="na_attention",
    )(qkv_t, qkv_t, qkv_t, qkv_t, qkv_t, qkv_t, qkv_t, bias_tab)


def _na_mixer(x, mod_sc, mod_sh, mod_g, nw, w_qkv, rpb, w_o, *, batch, seq):
    qkv_t = _nm_matmul(x, nw, mod_sc, mod_sh, w_qkv, ncols=3 * D_MODEL, tn=1024,
                       out_dtype=BF16, rows_per_batch=seq, pair_major=True)
    o_t = _na_attention(qkv_t, _na_bias_table(rpb), batch=batch, seq=seq)
    return _mm_resid(o_t, w_o, x, mod_g, rows_per_batch=seq, pair_major=True)


U32 = jnp.uint32


def _pack_halves(vb):
    n = vb.shape[1] // 2
    bits = pltpu.bitcast(vb.astype(F32), U32)
    return (bits[:, :n] >> 16) | bits[:, n:]


def _unpack_halves(w):
    return pltpu.bitcast(w << 16, F32), pltpu.bitcast(w & U32(0xFFFF0000), F32)


TOK_SUB = (D_MODEL // 2) // LANES


def _store_token_tiles(ref, packed):
    rows = packed.shape[0]
    for s in range(TOK_SUB):
        ref[pl.ds(s, rows, stride=TOK_SUB), :] = packed[:, s * LANES:(s + 1) * LANES]


def _load_token_tiles(ref):
    rows = ref.shape[0] // TOK_SUB
    return jnp.concatenate([ref[pl.ds(s, rows, stride=TOK_SUB), :] for s in range(TOK_SUB)], axis=1)


def _router_kernel(x_ref, nw_ref, sc_ref, sh_ref, wr_ref, h_ref, meta_ref, meta_t_ref, cnt_ref,
                   carry_s, whi_s, wlo_s):
    @pl.when(pl.program_id(0) == 0)
    def _():
        carry_s[...] = jnp.zeros_like(carry_s)
        whi_s[...], wlo_s[...] = _split2(wr_ref[...])

    h = _normmod(x_ref[...], nw_ref[...], sc_ref[0], sh_ref[0])
    _store_token_tiles(h_ref, _pack_halves(h.astype(BF16)))
    logits = _dot_split(h, whi_s[...], wlo_s[...])
    tm = logits.shape[0]
    lane_i = lax.broadcasted_iota(I32, logits.shape, 1)
    lane = lane_i.astype(F32)
    big = 1e9
    gl = jnp.where(lane_i < MOE_GROUPS, logits, NEG)
    gmax = jnp.max(gl, axis=1, keepdims=True)
    gsel = jnp.min(jnp.where(gl == gmax, lane, big), axis=1, keepdims=True)
    gw = 1.0 / jnp.sum(jnp.exp(gl - gmax), axis=1, keepdims=True)
    el = lane - MOE_GROUPS
    lo = gsel * MOE_EPG
    emask = (el >= lo) & (el < lo + MOE_EPG)
    e1 = jnp.where(emask, logits, NEG)
    m1 = jnp.max(e1, axis=1, keepdims=True)
    i1 = jnp.min(jnp.where(e1 == m1, el, big), axis=1, keepdims=True)
    e2 = jnp.where(emask & (el != i1), logits, NEG)
    m2 = jnp.max(e2, axis=1, keepdims=True)
    i2 = jnp.min(jnp.where(e2 == m2, el, big), axis=1, keepdims=True)
    tt = jnp.exp(m2 - m1)
    p1 = 1.0 / (1.0 + tt)
    w1 = gw * p1
    w2 = gw * (tt * p1)
    oh1 = el == i1
    oh2 = el == i2
    cnt = (oh1 | oh2).astype(F32)
    r_i = lax.broadcasted_iota(I32, (tm, tm), 0)
    c_i = lax.broadcasted_iota(I32, (tm, tm), 1)
    before = _dot((r_i > c_i).astype(BF16), cnt.astype(BF16)) + carry_s[...]
    rank1 = jnp.sum(jnp.where(oh1, before, 0.0), axis=1, keepdims=True)
    rank2 = jnp.sum(jnp.where(oh2, before, 0.0), axis=1, keepdims=True)
    carry_s[...] = carry_s[...] + jnp.sum(cnt, axis=0, keepdims=True)
    meta = jnp.zeros_like(logits)
    for pos, val in enumerate((i1, i2, w1, w2, rank1, rank2)):
        meta = jnp.where(lane_i == pos, val, meta)
    meta_ref[...] = meta
    meta_t_ref[...] = meta.T[0:meta_t_ref.shape[0], :]
    cnt_ref[...] = jnp.broadcast_to(carry_s[...], cnt_ref.shape)


def _router(x, nw, sc, sh, wr, *, rows_per_batch):
    t, d = x.shape
    tm = 256
    tiles_per_batch = rows_per_batch // tm
    return pl.pallas_call(
        _router_kernel,
        out_shape=(jax.ShapeDtypeStruct((t * TOK_SUB, LANES), U32),
                   jax.ShapeDtypeStruct((t, LANES), F32),
                   jax.ShapeDtypeStruct((8, t), F32),
                   jax.ShapeDtypeStruct((8, LANES), F32)),
        grid=(t // tm,),
        in_specs=[pl.BlockSpec((tm, d), lambda i: (i, 0)),
                  pl.BlockSpec((1, d), lambda i: (0, 0)),
                  pl.BlockSpec((1, 1, d), lambda i: (i // tiles_per_batch, 0, 0)),
                  pl.BlockSpec((1, 1, d), lambda i: (i // tiles_per_batch, 0, 0)),
                  pl.BlockSpec((d, LANES), lambda i: (0, 0))],
        out_specs=(pl.BlockSpec((tm * TOK_SUB, LANES), lambda i: (i, 0)),
                   pl.BlockSpec((tm, LANES), lambda i: (i, 0)),
                   pl.BlockSpec((8, tm), lambda i: (0, i)),
                   pl.BlockSpec((8, LANES), lambda i: (0, 0))),
        scratch_shapes=[pltpu.VMEM((1, LANES), F32), pltpu.VMEM((d, LANES), BF16),
                        pltpu.VMEM((d, LANES), BF16)],
        compiler_params=_cparams(("arbitrary",)),
        name="moe_router",
    )(x, nw, sc, sh, wr)


def _slot_kernel(pstart_ref, mt_ref, o_ref):
    eid = mt_ref[0:2, :]
    start = jnp.zeros(eid.shape, I32)
    for e in range(N_EXPERTS):
        start = jnp.where(eid == float(e), pstart_ref[e], start)
    o_ref[...] = start + mt_ref[4:6, :].astype(I32)


def _slots(pstart, meta_t):
    t = meta_t.shape[1]
    return pl.pallas_call(
        _slot_kernel,
        out_shape=jax.ShapeDtypeStruct((2, t), I32),
        grid_spec=pltpu.PrefetchScalarGridSpec(
            num_scalar_prefetch=1,
            grid=(1,),
            in_specs=[pl.BlockSpec(meta_t.shape, lambda i, ps: (0, 0))],
            out_specs=pl.BlockSpec((2, t), lambda i, ps: (0, 0))),
        compiler_params=_cparams(("arbitrary",)),
        name="moe_slots",
    )(pstart, meta_t)


def _token_tile(ref, r):
    return ref.at[pl.ds(pl.multiple_of(r * TOK_SUB, TOK_SUB), TOK_SUB)]


def _dispatch_kernel(dest_ref, zflag_ref, h_ref, xs_ref, zbuf, sem, zsem):
    tm = h_ref.shape[0] // TOK_SUB
    base = pl.program_id(0) * tm
    ntok = pl.num_programs(0) * tm
    tb = zbuf.shape[0]

    @pl.when(pl.program_id(0) == 0)
    def _():
        zbuf[...] = jnp.zeros_like(zbuf)

        def zcopy(b):
            return pltpu.make_async_copy(zbuf, xs_ref.at[pl.ds(pl.multiple_of(b * tb, tb), tb)], zsem)

        def zstart(b, carry):
            @pl.when(zflag_ref[b] == 1)
            def _():
                zcopy(b).start()
            return carry

        def zwait(b, carry):
            @pl.when(zflag_ref[b] == 1)
            def _():
                zcopy(b).wait()
            return carry

        nblk = xs_ref.shape[0] // tb
        lax.fori_loop(0, nblk, zstart, 0)
        lax.fori_loop(0, nblk, zwait, 0)

    def copy(t, k):
        return pltpu.make_async_copy(_token_tile(h_ref, t),
                                     _token_tile(xs_ref, dest_ref[k * ntok + base + t]), sem)

    def issue(t, carry):
        copy(t, 0).start(priority=0)
        copy(t, 1).start(priority=1)
        return carry

    def drain(t, carry):
        copy(t, 0).wait()
        copy(t, 1).wait()
        return carry

    lax.fori_loop(0, tm, issue, 0, unroll=DMA_UNROLL)
    lax.fori_loop(0, tm, drain, 0, unroll=DMA_UNROLL)


def _dispatch(dest, zflag, h, n_slots):
    t = h.shape[0] // TOK_SUB
    tm = 256
    return pl.pallas_call(
        _dispatch_kernel,
        out_shape=jax.ShapeDtypeStruct((n_slots * TOK_SUB, LANES), U32),
        grid_spec=pltpu.PrefetchScalarGridSpec(
            num_scalar_prefetch=2,
            grid=(t // tm,),
            in_specs=[pl.BlockSpec((tm * TOK_SUB, LANES), lambda i, dest, zf: (i, 0))],
            out_specs=pl.BlockSpec(memory_space=pl.ANY),
            scratch_shapes=[pltpu.VMEM((MOE_TB * TOK_SUB, LANES), U32), pltpu.SemaphoreType.DMA(()),
                            pltpu.SemaphoreType.DMA(())]),
        compiler_params=_cparams(("arbitrary",)),
        name="moe_dispatch",
    )(dest, zflag, h)


def _ffn_kernel(be_ref, nxt_ref, slot_ref, nu_ref, xs_ref, w1_hbm, w3_hbm, w2_hbm, o_ref,
                wb1, wb3, wb2, w1_s, w3_s, w2_s, sem, *, layer):
    i = pl.program_id(0)

    def fetch(e, s):
        return (pltpu.make_async_copy(w1_hbm.at[layer, e], wb1.at[s], sem.at[s, 0]),
                pltpu.make_async_copy(w3_hbm.at[layer, e], wb3.at[s], sem.at[s, 1]),
                pltpu.make_async_copy(w2_hbm.at[layer, e], wb2.at[s], sem.at[s, 2]))

    @pl.when(i < nu_ref[0])
    def _():
        e = be_ref[i]
        s = slot_ref[i]

        @pl.when((i == 0) | (e != be_ref[jnp.maximum(i - 1, 0)]))
        def _():
            @pl.when(i == 0)
            def _():
                for cp in fetch(e, s):
                    cp.start()

            for cp in fetch(e, s):
                cp.wait()

            @pl.when(nxt_ref[i] >= 0)
            def _():
                for cp in fetch(nxt_ref[i], 1 - s):
                    cp.start()

            w1_s[...] = wb1[s].astype(BF16)
            w3_s[...] = wb3[s].astype(BF16)
            w2_s[...] = wb2[s].astype(BF16)

        lo, hi = _unpack_halves(_load_token_tiles(xs_ref))
        xl, xh = lo.astype(BF16), hi.astype(BF16)
        half = xl.shape[1]
        a = _dot(xl, w1_s[0:half, :]) + _dot(xh, w1_s[half:2 * half, :])
        b = _dot(xl, w3_s[0:half, :]) + _dot(xh, w3_s[half:2 * half, :])
        hmid = _silu(a) * b
        _store_token_tiles(o_ref, _pack_halves(_dot(hmid.astype(BF16), w2_s[...]).astype(BF16)))

    @pl.when(i >= nu_ref[0])
    def _():
        o_ref[...] = jnp.zeros_like(o_ref)


def _expert_ffn(blk_expert, blk_next, blk_slot, n_used, xs, w1, w3, w2, layer):
    d, f = w1.shape[2], w1.shape[3]
    rows = MOE_TB * TOK_SUB
    nb = xs.shape[0] // rows
    hbm = pl.BlockSpec(memory_space=pl.ANY)
    return pl.pallas_call(
        functools.partial(_ffn_kernel, layer=layer),
        out_shape=jax.ShapeDtypeStruct(xs.shape, U32),
        grid_spec=pltpu.PrefetchScalarGridSpec(
            num_scalar_prefetch=4,
            grid=(nb,),
            in_specs=[pl.BlockSpec((rows, LANES),
                                   lambda i, be, nx, sl, nu: (jnp.minimum(i, nu[0] - 1), 0)),
                      hbm, hbm, hbm],
            out_specs=pl.BlockSpec((rows, LANES), lambda i, be, nx, sl, nu: (i, 0)),
            scratch_shapes=[pltpu.VMEM((2, d, f), F32), pltpu.VMEM((2, d, f), F32),
                            pltpu.VMEM((2, f, d), F32),
                            pltpu.VMEM((d, f), BF16), pltpu.VMEM((d, f), BF16),
                            pltpu.VMEM((f, d), BF16),
                            pltpu.SemaphoreType.DMA((2, 3))]),
        compiler_params=_cparams(("arbitrary",)),
        name="moe_expert_ffn",
    )(blk_expert, blk_next, blk_slot, n_used, xs, w1, w3, w2)


def _combine_kernel(dest_ref, x_ref, meta_ref, g_ref, fnw_ref, ys_ref, o_ref, buf, sem, *, final):
    tm = x_ref.shape[0]
    i = pl.program_id(0)
    nsteps = pl.num_programs(0)
    ntok = nsteps * tm

    def copy(tile, t, k):
        half = tile % 2
        return pltpu.make_async_copy(_token_tile(ys_ref, dest_ref[k * ntok + tile * tm + t]),
                                     _token_tile(buf.at[half, k], t), sem.at[half])

    def issue(tile):
        def body(t, carry):
            copy(tile, t, 0).start(priority=0)
            copy(tile, t, 1).start(priority=1)
            return carry
        lax.fori_loop(0, tm, body, 0, unroll=DMA_UNROLL)

    def drain(tile):
        def body(t, carry):
            copy(tile, t, 0).wait()
            copy(tile, t, 1).wait()
            return carry
        lax.fori_loop(0, tm, body, 0, unroll=DMA_UNROLL)

    @pl.when(i == 0)
    def _():
        issue(i)

    @pl.when(i + 1 < nsteps)
    def _():
        issue(i + 1)

    drain(i)
    cur = i % 2
    meta = meta_ref[...]
    w1, w2 = meta[:, 2:3], meta[:, 3:4]
    lo1, hi1 = _unpack_halves(_load_token_tiles(buf.at[cur, 0]))
    lo2, hi2 = _unpack_halves(_load_token_tiles(buf.at[cur, 1]))
    y = jnp.concatenate([w1 * lo1 + w2 * lo2, w1 * hi1 + w2 * hi2], axis=1)
    xn = x_ref[...] + g_ref[0] * y
    if final:
        ms = jnp.mean(xn * xn, axis=-1, keepdims=True)
        xn = xn * lax.rsqrt(ms + EPS) * fnw_ref[...]
    o_ref[...] = xn


def _combine(dest, x, meta, g, fnw, ys, *, rows_per_batch, final):
    t, d = x.shape
    tm = 256
    tiles_per_batch = rows_per_batch // tm
    return pl.pallas_call(
        functools.partial(_combine_kernel, final=final),
        out_shape=jax.ShapeDtypeStruct((t, d), F32),
        grid_spec=pltpu.PrefetchScalarGridSpec(
            num_scalar_prefetch=1,
            grid=(t // tm,),
            in_specs=[pl.BlockSpec((tm, d), lambda i, dest: (i, 0)),
                      pl.BlockSpec((tm, LANES), lambda i, dest: (i, 0)),
                      pl.BlockSpec((1, 1, d), lambda i, dest: (i // tiles_per_batch, 0, 0)),
                      pl.BlockSpec((1, d), lambda i, dest: (0, 0)),
                      pl.BlockSpec(memory_space=pl.ANY)],
            out_specs=pl.BlockSpec((tm, d), lambda i, dest: (i, 0)),
            scratch_shapes=[pltpu.VMEM((2, 2, tm * TOK_SUB, LANES), U32),
                            pltpu.SemaphoreType.DMA((2,))]),
        compiler_params=_cparams(("arbitrary",)),
        name="moe_combine",
    )(dest, x, meta, g, fnw, ys)


def _hier_moe(x, mod_sc, mod_sh, mod_g, nw, w_group, w_expert, w1, w3, w2, layer, fnw, *,
              rows_per_batch, final):
    t, d = x.shape
    a = 2 * t
    tb = MOE_TB
    wr = jnp.concatenate([w_group, w_expert], axis=1)
    wr = jnp.pad(wr, ((0, 0), (0, LANES - wr.shape[1])))
    h, meta, meta_t, cnt = _router(x, nw, mod_sc, mod_sh, wr, rows_per_batch=rows_per_batch)
    ne = N_EXPERTS
    counts = cnt[0, MOE_GROUPS:MOE_GROUPS + ne].astype(I32)
    padded = ((counts + tb - 1) // tb) * tb
    pend = jnp.cumsum(padded)
    pstart = pend - padded
    dest = _slots(pstart, meta_t).reshape(a)
    nb = (a + ne * (tb - 1) + tb - 1) // tb
    n_used = (pend[-1] // tb).astype(I32)
    blk = jnp.arange(nb, dtype=I32)
    be = jnp.minimum(jnp.sum((pend[None, :] <= (blk * tb)[:, None]).astype(I32), axis=1), ne - 1)
    seg_last = jnp.any((pend[None, :] == ((blk + 1) * tb)[:, None]) & (padded[None, :] > 0), axis=1)
    zflag = (seg_last | (blk >= n_used)).astype(I32)
    nonempty = counts > 0
    seg = jnp.cumsum(nonempty.astype(I32)) - 1
    later = lax.cummin(jnp.where(nonempty, jnp.arange(ne, dtype=I32), ne), axis=0, reverse=True)
    nxt_e = jnp.concatenate([later[1:], jnp.full((1,), ne, I32)])
    nxt_e = jnp.where(nxt_e == ne, -1, nxt_e)
    xs = _dispatch(dest, zflag, h, nb * tb)
    ys = _expert_ffn(be, nxt_e[be], seg[be] % 2, n_used.reshape(1), xs, w1, w3, w2, layer)
    return _combine(dest, x, meta, mod_g, fnw, ys, rows_per_batch=rows_per_batch, final=final)


def kernel(x, c, ada_w, ada_b, norm_mix, norm_ffn, ssd_w_in, ssd_conv_w, ssd_conv_b, ssd_a_log,
           ssd_dt_bias, ssd_d, ssd_norm_w, ssd_w_out, na_w_qkv, na_rpb, na_w_o,
           moe_w_group, moe_w_expert, moe_w1, moe_w3, moe_w2, final_norm):
    batch, seq, d = x.shape
    depth = ada_w.shape[0]
    xt = x.reshape(batch * seq, d)
    c_pad = jnp.pad(c, ((0, 8 - batch), (0, 0)))
    mod = _ada(c_pad, ada_w, ada_b)[:, :batch]
    fnw = final_norm.reshape(1, d)
    for i in range(depth):
        sh1, sc1, g1, sh2, sc2, g2 = [mod[i, :, k * d:(k + 1) * d].reshape(batch, 1, d)
                                      for k in range(6)]
        j = i // 2
        nw = norm_mix[i].reshape(1, d)
        if i % 2 == 0:
            xt = _ssd_mixer(xt, sc1, sh1, g1, nw, ssd_w_in[j], ssd_conv_w[j], ssd_conv_b[j],
                            ssd_a_log[j], ssd_dt_bias[j], ssd_d[j], ssd_norm_w[j], ssd_w_out[j],
                            batch=batch, seq=seq)
        else:
            xt = _na_mixer(xt, sc1, sh1, g1, nw, na_w_qkv[j], na_rpb[j], na_w_o[j],
                           batch=batch, seq=seq)
        xt = _hier_moe(xt, sc2, sh2, g2, norm_ffn[i].reshape(1, d), moe_w_group[i], moe_w_expert[i],
                       moe_w1, moe_w3, moe_w2, i, fnw, rows_per_batch=seq,
                       final=(i == depth - 1))
    return xt.reshape(batch, seq, d)
```

```python
import functools

import jax
import jax.numpy as jnp
from jax import lax
from jax.experimental import pallas as pl
from jax.experimental.pallas import tpu as pltpu

F32 = jnp.float32
BF16 = jnp.bfloat16
I32 = jnp.int32

EPS = 1e-6
NEG = -1e30
LOG2_E = 1.4426950408889634

D_MODEL = 2048
GRID_W = 64
SSD_HEAD_DIM = 64
SSD_GROUPS = 8
SSD_HEADS_PER_GROUP = 8
SSD_STATE = 128
SSD_CONV = 5
D_INNER = 2 * D_MODEL
GROUP_W = SSD_HEADS_PER_GROUP * SSD_HEAD_DIM
SSD_Q = 128
CONV_HALO = 16
NA_HEAD_DIM = 64
NA_HEADS = D_MODEL // NA_HEAD_DIM
NA_PAIRS = NA_HEADS // 2
WIN_H = 8
WIN_W = 16
NA_ROWS = 4
MOE_GROUPS = 4
MOE_EPG = 8
N_EXPERTS = MOE_GROUPS * MOE_EPG
MOE_D_FF = D_MODEL // 4
MOE_TB = 256
DMA_UNROLL = 8

VMEM_LIMIT = 56 * 1024 * 1024
LANES = 128


def _cparams(sem):
    return pltpu.CompilerParams(dimension_semantics=sem, vmem_limit_bytes=VMEM_LIMIT)


def _silu(v):
    return v * pl.reciprocal(1.0 + jnp.exp(-v), approx=True)


def _softplus(v):
    return jnp.maximum(v, 0.0) + jnp.log1p(jnp.exp(-jnp.abs(v)))


def _split3(v):
    hi = v.astype(BF16)
    r1 = v - hi.astype(F32)
    mid = r1.astype(BF16)
    lo = (r1 - mid.astype(F32)).astype(BF16)
    return hi, mid, lo


def _dot(a, b):
    return jnp.dot(a, b, preferred_element_type=F32)


def _dot_nt(a, b):
    return lax.dot_general(a, b, (((1,), (1,)), ((), ())), preferred_element_type=F32)


def _dot3_left(v, sel):
    hi, mid, lo = _split3(v)
    return _dot(hi, sel) + _dot(mid, sel) + _dot(lo, sel)


def _normmod(x, nw, sc, sh):
    ms = jnp.mean(x * x, axis=-1, keepdims=True)
    return (x * lax.rsqrt(ms + EPS) * nw) * (1.0 + sc) + sh


def _split2(v):
    hi = v.astype(BF16)
    return hi, (v - hi.astype(F32)).astype(BF16)


def _dot_split(a, b_hi, b_lo):
    a_hi, a_lo = _split2(a)
    return _dot(a_hi, b_hi) + (_dot(a_lo, b_hi) + _dot(a_hi, b_lo))


def _ada_kernel(c_ref, w_ref, b_ref, o_ref):
    c = c_ref[...]
    o_ref[0] = _dot_split(c / (1.0 + jnp.exp(-c)), *_split2(w_ref[0])) + b_ref[0]


def _ada(c_pad, ada_w, ada_b):
    depth, d, n = ada_w.shape
    tn = 1024
    return pl.pallas_call(
        _ada_kernel,
        out_shape=jax.ShapeDtypeStruct((depth, 8, n), F32),
        grid=(depth, n // tn),
        in_specs=[pl.BlockSpec((8, d), lambda i, j: (0, 0)),
                  pl.BlockSpec((1, d, tn), lambda i, j: (i, 0, j)),
                  pl.BlockSpec((1, 1, tn), lambda i, j: (i, 0, j))],
        out_specs=pl.BlockSpec((1, 8, tn), lambda i, j: (i, 0, j)),
        compiler_params=_cparams(("arbitrary", "arbitrary")),
        name="ada_mod",
    )(c_pad, ada_w, ada_b.reshape(depth, 1, n))


def _nm_mm_kernel(x_ref, nw_ref, sc_ref, sh_ref, w_ref, *rest, pair_major, tail):
    if tail:
        wt_ref, o_ref, ot_ref, h_ref = rest
    else:
        o_ref, h_ref = rest

    @pl.when(pl.program_id(1) == 0)
    def _():
        h_ref[...] = _normmod(x_ref[...], nw_ref[...], sc_ref[0], sh_ref[0]).astype(BF16)

    r = _dot(h_ref[...], w_ref[...].astype(BF16))
    if pair_major:
        for c in range(o_ref.shape[0]):
            o_ref[c] = r[:, c * LANES:(c + 1) * LANES].astype(o_ref.dtype)
    else:
        o_ref[...] = r.astype(o_ref.dtype)

    if tail:
        @pl.when(pl.program_id(1) == pl.num_programs(1) - 1)
        def _():
            ot_ref[...] = _dot(h_ref[...], wt_ref[...].astype(BF16))


def _nm_matmul(x, nw, sc, sh, w, *, ncols, tn, out_dtype, rows_per_batch, pair_major=False,
               tail_cols=0):
    t, d = x.shape
    tm = 1024
    tiles_per_batch = rows_per_batch // tm
    if pair_major:
        out_shape = jax.ShapeDtypeStruct((ncols // LANES, t, LANES), out_dtype)
        out_spec = pl.BlockSpec((tn // LANES, tm, LANES), lambda i, j: (j, i, 0))
    else:
        out_shape = jax.ShapeDtypeStruct((t, ncols), out_dtype)
        out_spec = pl.BlockSpec((tm, tn), lambda i, j: (i, j))
    in_specs = [pl.BlockSpec((tm, d), lambda i, j: (i, 0)),
                pl.BlockSpec((1, d), lambda i, j: (0, 0)),
                pl.BlockSpec((1, 1, d), lambda i, j: (i // tiles_per_batch, 0, 0)),
                pl.BlockSpec((1, 1, d), lambda i, j: (i // tiles_per_batch, 0, 0)),
                pl.BlockSpec((d, tn), lambda i, j: (0, j))]
    operands = [x, nw, sc, sh, w]
    if tail_cols:
        tail_blk = ncols // tail_cols
        in_specs.append(pl.BlockSpec((d, tail_cols), lambda i, j: (0, tail_blk)))
        operands.append(w)
        out_shape = (out_shape, jax.ShapeDtypeStruct((t, tail_cols), F32))
        out_spec = (out_spec, pl.BlockSpec((tm, tail_cols), lambda i, j: (i, 0)))
    return pl.pallas_call(
        functools.partial(_nm_mm_kernel, pair_major=pair_major, tail=bool(tail_cols)),
        out_shape=out_shape,
        grid=(t // tm, ncols // tn),
        in_specs=in_specs,
        out_specs=out_spec,
        scratch_shapes=[pltpu.VMEM((tm, d), BF16)],
        compiler_params=_cparams(("arbitrary", "arbitrary")),
        name="norm_mod_matmul",
    )(*operands)


def _mm_resid_kernel(a_ref, w_ref, x_ref, g_ref, o_ref, *, pair_major):
    if pair_major:
        a = jnp.concatenate([a_ref[c] for c in range(a_ref.shape[0])], axis=1)
    else:
        a = a_ref[...]
    o_ref[...] = x_ref[...] + g_ref[0] * _dot(a, w_ref[...].astype(BF16))


def _mm_resid(a, w, x, g, *, rows_per_batch, pair_major=False):
    t, n = x.shape
    k = w.shape[0]
    tn = 512
    tm = (4 * 1024 * 1024) // k
    tiles_per_batch = rows_per_batch // tm
    if pair_major:
        a_spec = pl.BlockSpec((k // LANES, tm, LANES), lambda i, j: (0, i, 0))
    else:
        a_spec = pl.BlockSpec((tm, k), lambda i, j: (i, 0))
    return pl.pallas_call(
        functools.partial(_mm_resid_kernel, pair_major=pair_major),
        out_shape=jax.ShapeDtypeStruct((t, n), F32),
        grid=(t // tm, n // tn),
        in_specs=[a_spec,
                  pl.BlockSpec((k, tn), lambda i, j: (0, j)),
                  pl.BlockSpec((tm, tn), lambda i, j: (i, j)),
                  pl.BlockSpec((1, 1, tn), lambda i, j: (i // tiles_per_batch, 0, j))],
        out_specs=pl.BlockSpec((tm, tn), lambda i, j: (i, j)),
        compiler_params=_cparams(("arbitrary", "arbitrary")),
        name="matmul_resid",
    )(a, w, x, g)


def _ssd_kernel(z_ref, x_ref, b_ref, c_ref, dtr_ref,
                cwx_ref, cwb_ref, cwc_ref, cbx_ref, cbb_ref, cbc_ref,
                alr_ref, dbr_ref, dsk_ref, nw_ref,
                o_ref,
                xc_s, bc_s, cc_s, yacc_s, st_s, cv_s):
    seq = x_ref.shape[0]
    q = SSD_Q
    nc = seq // q
    halo = CONV_HALO
    nrow = 2 * SSD_HEADS_PER_GROUP

    def conv_piece(j, base, src_ref, w_ref, bias_ref, dst_ref, lo, stage):
        cols = slice(lo, lo + LANES)
        pstart = pl.multiple_of(jnp.maximum(base - halo, 0), halo)
        nstart = pl.multiple_of(jnp.minimum(base + q, seq - halo), halo)
        stage[0:halo, :] = jnp.where(j > 0, src_ref[pl.ds(pstart, halo), cols].astype(F32), 0.0)
        stage[halo:halo + q, :] = src_ref[pl.ds(base, q), cols].astype(F32)
        stage[halo + q:, :] = jnp.where(j < nc - 1, src_ref[pl.ds(nstart, halo), cols].astype(F32), 0.0)
        acc = jnp.broadcast_to(bias_ref[:, cols], (q, LANES))
        for k in range(SSD_CONV):
            first = halo - SSD_CONV // 2 + k
            acc = acc + w_ref[k:k + 1, cols] * stage[first:first + q, :]
        dst_ref[pl.ds(base, q), cols] = _silu(acc).astype(BF16)

    def conv_chunk(j, carry):
        base = pl.multiple_of(j * q, q)
        npx = GROUP_W // LANES
        for i in range(npx):
            conv_piece(j, base, x_ref, cwx_ref, cbx_ref, xc_s, i * LANES, cv_s.at[i])
        conv_piece(j, base, b_ref, cwb_ref, cbb_ref, bc_s, 0, cv_s.at[npx])
        conv_piece(j, base, c_ref, cwc_ref, cbc_ref, cc_s, 0, cv_s.at[npx + 1])
        return carry

    lax.fori_loop(0, nc, conv_chunk, 0)

    row_i = lax.broadcasted_iota(I32, (q, q), 0)
    col_i = lax.broadcasted_iota(I32, (q, q), 1)
    lower = row_i >= col_i
    upper = row_i <= col_i
    lower_b = lower.astype(BF16)
    upper_b = upper.astype(BF16)
    left =lax.broadcasted_iota(I32, (q, LANES), 1) < SSD_HEAD_DIM
    a_row = -jnp.exp(alr_ref[...])
    pad_rows = jnp.zeros((LANES - nrow, q), F32)

    def scan_pass(direction):
        hoff = direction * SSD_HEADS_PER_GROUP
        mask = lower if direction == 0 else upper
        tri = upper_b if direction == 0 else lower_b
        edge = q - 1 if direction == 0 else 0
        st_s[...] = jnp.zeros_like(st_s)

        def chunk(t, carry):
            c = t if direction == 0 else nc - 1 - t
            base = pl.multiple_of(c * q, q)
            rows = pl.ds(base, q)
            dt_r = _softplus(dtr_ref[c] + dbr_ref[...])
            cum_r = _dot3_left(dt_r * a_row, tri) * LOG2_E
            cum_c = jnp.concatenate([cum_r, pad_rows], axis=0).T
            decdt_r = jnp.exp2(cum_r[:, edge:edge + 1] - cum_r) * dt_r
            src_r = cum_r - jnp.log2(dt_r)

            bm = bc_s[rows, :]
            cm = cc_s[rows, :]
            cb = _dot_nt(cm, bm)
            bm_t = bm.astype(F32).T
            y_off_all = _dot(cm, st_s[...].astype(BF16)) if direction == 1 else None

            ssq = jnp.zeros((q, 1), F32)
            for pp in range(SSD_HEADS_PER_GROUP // 2):
                cols = slice(pp * LANES, (pp + 1) * LANES)
                xcb = xc_s[rows, cols]
                zero_b = jnp.zeros_like(xcb)
                st_in = st_s[:, cols]
                lhs_y, lhs_s, scales = [], [], []
                for par in range(2):
                    j = hoff + 2 * pp + par
                    cum_b = jnp.broadcast_to(cum_c[:, j:j + 1], (q, q))
                    lmat = jnp.exp2(jnp.where(mask, cum_b - src_r[j:j + 1, :], NEG))
                    lhs_y.append((cb * lmat).astype(BF16))
                    lhs_s.append((bm_t * decdt_r[j:j + 1, :]).astype(BF16))
                    scales.append(jnp.exp2(cum_b))
                x_rhs = jnp.concatenate([jnp.where(left, xcb, zero_b), jnp.where(left, zero_b, xcb)],
                                        axis=0)
                sc_tile = jnp.where(left, scales[0], scales[1])
                y_off = _dot(cm, st_in.astype(BF16)) if y_off_all is None else y_off_all[:, cols]
                y = _dot(jnp.concatenate(lhs_y, axis=1), x_rhs) + y_off * sc_tile
                st_s[:, cols] = (st_in * sc_tile[edge:edge + 1, :]
                                 + _dot(jnp.concatenate(lhs_s, axis=1), x_rhs))
                if direction == 0:
                    yacc_s[rows, cols] = y
                else:
                    total = yacc_s[rows, cols] + y + xcb.astype(F32) * dsk_ref[:, cols]
                    gated = total * _silu(z_ref[rows, cols].astype(F32))
                    ssq = ssq + jnp.sum(gated * gated, axis=-1, keepdims=True)
                    yacc_s[rows, cols] = gated
            if direction == 1:
                inv = lax.rsqrt(ssq * (1.0 / GROUP_W) + EPS)
                o_ref[rows, :] = (yacc_s[rows, :] * inv * nw_ref[...]).astype(BF16)
            return carry

        lax.fori_loop(0, nc, chunk, 0, unroll=8 if direction == 0 else 4)

    scan_pass(0)
    scan_pass(1)


def _ssd_core(zx, dt_row, conv_w, conv_b, al_row, db_row, dskip, norm_w, *, batch, seq):
    g = SSD_GROUPS
    nc = seq // SSD_Q
    xb = D_INNER // GROUP_W
    bb = (2 * D_INNER) // SSD_STATE
    cb = bb + g
    cwb = D_INNER // SSD_STATE
    cwc = cwb + g
    return pl.pallas_call(
        _ssd_kernel,
        out_shape=jax.ShapeDtypeStruct((batch * seq, D_INNER), BF16),
        grid=(batch, g),
        in_specs=[
            pl.BlockSpec((seq, GROUP_W), lambda b, i: (b, i)),
            pl.BlockSpec((seq, GROUP_W), lambda b, i: (b, xb + i)),
            pl.BlockSpec((seq, SSD_STATE), lambda b, i: (b, bb + i)),
            pl.BlockSpec((seq, SSD_STATE), lambda b, i: (b, cb + i)),
            pl.BlockSpec((None, None, nc, 2 * SSD_HEADS_PER_GROUP, SSD_Q), lambda b, i: (b, i, 0, 0, 0)),
            pl.BlockSpec((SSD_CONV, GROUP_W), lambda b, i: (0, i)),
            pl.BlockSpec((SSD_CONV, SSD_STATE), lambda b, i: (0, cwb + i)),
            pl.BlockSpec((SSD_CONV, SSD_STATE), lambda b, i: (0, cwc + i)),
            pl.BlockSpec((1, GROUP_W), lambda b, i: (0, i)),
            pl.BlockSpec((1, SSD_STATE), lambda b, i: (0, cwb + i)),
            pl.BlockSpec((1, SSD_STATE), lambda b, i: (0, cwc + i)),
            pl.BlockSpec((None, 2 * SSD_HEADS_PER_GROUP, 1), lambda b, i: (i, 0, 0)),
            pl.BlockSpec((None, 2 * SSD_HEADS_PER_GROUP, 1), lambda b, i: (i, 0, 0)),
            pl.BlockSpec((1, GROUP_W), lambda b, i: (0, i)),
            pl.BlockSpec((1, GROUP_W), lambda b, i: (0, i)),
        ],
        out_specs=pl.BlockSpec((seq, GROUP_W), lambda b, i: (b, i)),
        scratch_shapes=[pltpu.VMEM((seq, GROUP_W), BF16),
                        pltpu.VMEM((seq, SSD_STATE), BF16),
                        pltpu.VMEM((seq, SSD_STATE), BF16),
                        pltpu.VMEM((seq, GROUP_W), F32),
                        pltpu.VMEM((SSD_STATE, GROUP_W), F32),
                        pltpu.VMEM((GROUP_W // LANES + 2, SSD_Q + 2 * CONV_HALO, LANES), F32)],
        compiler_params=_cparams(("arbitrary", "arbitrary")),
        name="ssd_core",
    )(zx, zx, zx, zx, dt_row, conv_w, conv_w, conv_w, conv_b, conv_b, conv_b,
      al_row, db_row, dskip, norm_w)


def _ssd_mixer(x, mod_sc, mod_sh, mod_g, nw, w_in, conv_w, conv_b, a_log, dt_bias, d_skip, norm_w,
               w_out, *, batch, seq):
    g, r = SSD_GROUPS, SSD_HEADS_PER_GROUP
    conv_dim = conv_w.shape[1]
    zx, dt_raw = _nm_matmul(x, nw, mod_sc, mod_sh, w_in, ncols=D_INNER + conv_dim, tn=1024,
                            out_dtype=BF16, rows_per_batch=seq, tail_cols=2 * g * r)
    nc = seq // SSD_Q
    dt_row = dt_raw.reshape(batch, nc, SSD_Q, 2, g, r).transpose(0, 4, 1, 3, 5, 2)
    dt_row = dt_row.reshape(batch, g, nc, 2 * r, SSD_Q)

    def row_form(p):
        return p.reshape(2, g, r).transpose(1, 0, 2).reshape(g, 2 * r, 1)

    yn = _ssd_core(zx, dt_row, conv_w, conv_b.reshape(1, conv_dim), row_form(a_log), row_form(dt_bias),
                   jnp.repeat(d_skip, SSD_HEAD_DIM).reshape(1, D_INNER), norm_w.reshape(1, D_INNER),
                   batch=batch, seq=seq)
    return _mm_resid(yn, w_out, x, mod_g, rows_per_batch=seq)


def _bias_table_kernel(rpb_ref, o_ref):
    lane = lax.broadcasted_iota(I32, (GRID_W, LANES), 1)
    j = lax.broadcasted_iota(I32, (GRID_W, LANES), 0)
    c = lane & (GRID_W - 1)
    c0 = jnp.clip(j - WIN_W // 2, 0, GRID_W - WIN_W)
    win = (c >= c0) & (c < c0 + WIN_W)
    left = lane < GRID_W

    def one_offset(dy0, carry):
        for par in range(2):
            for m in range(WIN_H // 2):
                tiles = []
                for sub in range(2):
                    row = rpb_ref[par, pl.ds(dy0 + 2 * m + sub, 1), :]
                    shift = (sub * GRID_W - (WIN_W - 1)) % LANES
                    tiles.append(pltpu.roll(jnp.broadcast_to(row, (GRID_W, LANES)), shift, 1,
                                            stride=1, stride_axis=0))
                tile = jnp.where(win, jnp.where(left, tiles[0], tiles[1]), NEG)
                o_ref[dy0, 0, par * GRID_W:(par + 1) * GRID_W, m * LANES:(m + 1) * LANES] = tile
        return carry

    lax.fori_loop(0, o_ref.shape[0], one_offset, 0)


def _na_bias_table(rpb):
    h, ndy, ndx = rpb.shape
    rpb_p = jnp.pad(rpb, ((0, 0), (0, 2 * WIN_H - ndy), (0, LANES - ndx)))
    return pl.pallas_call(
        _bias_table_kernel,
        out_shape=jax.ShapeDtypeStruct((WIN_H, h // 2, 2 * GRID_W, WIN_H * GRID_W), F32),
        grid=(h // 2,),
        in_specs=[pl.BlockSpec((2, 2 * WIN_H, LANES), lambda p: (p, 0, 0))],
        out_specs=pl.BlockSpec((WIN_H, 1, 2 * GRID_W, WIN_H * GRID_W), lambda p: (0, p, 0, 0)),
        compiler_params=_cparams(("arbitrary",)),
        name="na_bias_table",
    )(rpb_p)


def _na_kernel(q_ref, k0_ref, k1_ref, k2_ref, v0_ref, v1_ref, v2_ref, bias_ref, o_ref, kc_s, vc_s,
               s_s, p_s, r_s, *, n_row_blocks):
    rb = pl.program_id(2)
    npairs = q_ref.shape[0]
    blk = NA_ROWS * GRID_W
    nkeys = WIN_H * GRID_W
    for i, (kr, vr) in enumerate(((k0_ref, v0_ref), (k1_ref, v1_ref), (k2_ref, v2_ref))):
        kc_s[:, i * blk:(i + 1) * blk, :] = kr[...]
        vc_s[:, i * blk:(i + 1) * blk, :] = vr[...]
    first = rb == 0
    last = rb == n_row_blocks - 1
    edge = first | last
    lane = lax.broadcasted_iota(I32, (GRID_W, LANES), 1)
    left = lane < NA_HEAD_DIM
    scale = jnp.asarray(NA_HEAD_DIM ** -0.5, BF16)

    def window(qi):
        off = jnp.where(first, 0, jnp.where(last, blk, qi * GRID_W))
        li = jnp.where(edge, NA_ROWS - 1 - qi, NA_ROWS - 1)
        return pl.multiple_of(off, GRID_W), li

    def pair_body(pp, carry):
        for qi in range(NA_ROWS):
            off, li = window(qi)
            q2 = q_ref[pp, qi * GRID_W:(qi + 1) * GRID_W, :] * scale
            zero = jnp.zeros_like(q2)
            qs = jnp.concatenate([jnp.where(left, q2, zero), jnp.where(left, zero, q2)], axis=0)
            kw = kc_s[pp, pl.ds(off, nkeys), :]
            s_s[qi] = _dot_nt(qs, kw) + bias_ref[li, pp]
        for qi in range(NA_ROWS):
            s = s_s[qi]
            p = jnp.exp(s - jnp.max(s, axis=-1, keepdims=True))
            r_s[qi] = 1.0 / jnp.sum(p, axis=-1, keepdims=True)
            p_s[qi] = p.astype(BF16)
        for qi in range(NA_ROWS):
            off, _ = window(qi)
            pv = _dot(p_s[qi], vc_s[pp, pl.ds(off, nkeys), :]) * r_s[qi]
            o = jnp.where(left, pv[0:GRID_W], pv[GRID_W:2 * GRID_W])
            o_ref[pp, qi * GRID_W:(qi + 1) * GRID_W, :] = o.astype(BF16)
        return carry

    lax.fori_loop(0, npairs, pair_body, 0, unroll=4)


def _na_attention(qkv_t, bias_tab, *, batch, seq):
    t = batch * seq
    blk = NA_ROWS * GRID_W
    nrb = seq // blk
    hp = NA_PAIRS // 2
    nsec = NA_PAIRS // hp

    def kv_spec(sec, i):
        def imap(hh, b, r):
            return (sec * nsec + hh, b * nrb + jnp.clip(r - 1, 0, nrb - 3) + i, 0)
        return pl.BlockSpec((hp, blk, LANES), imap)

    return pl.pallas_call(
        functools.partial(_na_kernel, n_row_blocks=nrb),
        out_shape=jax.ShapeDtypeStruct((NA_PAIRS, t, LANES), BF16),
        grid=(nsec, batch, nrb),
        in_specs=[pl.BlockSpec((hp, blk, LANES), lambda hh, b, r: (hh, b * nrb + r, 0)),
                  kv_spec(1, 0), kv_spec(1, 1), kv_spec(1, 2),
                  kv_spec(2, 0), kv_spec(2, 1), kv_spec(2, 2),
                  pl.BlockSpec((NA_ROWS, hp, 2 * GRID_W, WIN_H * GRID_W),
                               lambda hh, b, r: (jnp.where(r == 0, 1, 0), hh, 0, 0))],
        out_specs=pl.BlockSpec((hp, blk, LANES), lambda hh, b, r: (hh, b * nrb + r, 0)),
        scratch_shapes=[pltpu.VMEM((hp, 3 * blk, LANES), BF16),
                        pltpu.VMEM((hp, 3 * blk, LANES), BF16),
                        pltpu.VMEM((NA_ROWS, 2 * GRID_W, WIN_H * GRID_W), F32),
                        pltpu.VMEM((NA_ROWS, 2 * GRID_W, WIN_H * GRID_W), BF16),
                        pltpu.VMEM((NA_ROWS, 2 * GRID_W, 1), F32)],
        compiler_params=_cparams(("arbitrary", "arbitrary", "arbitrary")),
        name="na_attention",
    )(qkv_t, qkv_t, qkv_t, qkv_t, qkv_t, qkv_t, qkv_t, bias_tab)


def _na_mixer(x, mod_sc, mod_sh, mod_g, nw, w_qkv, rpb, w_o, *, batch, seq):
    qkv_t = _nm_matmul(x, nw, mod_sc, mod_sh, w_qkv, ncols=3 * D_MODEL, tn=1024,
                       out_dtype=BF16, rows_per_batch=seq, pair_major=True)
    o_t = _na_attention(qkv_t, _na_bias_table(rpb), batch=batch, seq=seq)
    return _mm_resid(o_t, w_o, x, mod_g, rows_per_batch=seq, pair_major=True)


U32 = jnp.uint32


def _pack_halves(vb):
    n = vb.shape[1] // 2
    bits = pltpu.bitcast(vb.astype(F32), U32)
    return (bits[:, :n] >> 16) | bits[:, n:]


def _unpack_halves(w):
    return pltpu.bitcast(w << 16, F32), pltpu.bitcast(w & U32(0xFFFF0000), F32)


TOK_SUB = (D_MODEL // 2) // LANES


def _store_token_tiles(ref, packed):
    rows = packed.shape[0]
    for s in range(TOK_SUB):
        ref[pl.ds(s, rows, stride=TOK_SUB), :] = packed[:, s * LANES:(s + 1) * LANES]


def _load_token_tiles(ref):
    rows = ref.shape[0] // TOK_SUB
    return jnp.concatenate([ref[pl.ds(s, rows, stride=TOK_SUB), :] for s in range(TOK_SUB)], axis=1)


def _router_kernel(x_ref, nw_ref, sc_ref, sh_ref, wr_ref, h_ref, meta_ref, meta_t_ref, cnt_ref,
                   carry_s, whi_s, wlo_s):
    @pl.when(pl.program_id(0) == 0)
    def _():
        carry_s[...] = jnp.zeros_like(carry_s)
        whi_s[...], wlo_s[...] = _split2(wr_ref[...])

    h = _normmod(x_ref[...], nw_ref[...], sc_ref[0], sh_ref[0])
    _store_token_tiles(h_ref, _pack_halves(h.astype(BF16)))
    logits = _dot_split(h, whi_s[...], wlo_s[...])
    tm = logits.shape[0]
    lane_i = lax.broadcasted_iota(I32, logits.shape, 1)
    lane = lane_i.astype(F32)
    big = 1e9
    gl = jnp.where(lane_i < MOE_GROUPS, logits, NEG)
    gmax = jnp.max(gl, axis=1, keepdims=True)
    gsel = jnp.min(jnp.where(gl == gmax, lane, big), axis=1, keepdims=True)
    gw = 1.0 / jnp.sum(jnp.exp(gl - gmax), axis=1, keepdims=True)
    el = lane - MOE_GROUPS
    lo = gsel * MOE_EPG
    emask = (el >= lo) & (el < lo + MOE_EPG)
    e1 = jnp.where(emask, logits, NEG)
    m1 = jnp.max(e1, axis=1, keepdims=True)
    i1 = jnp.min(jnp.where(e1 == m1, el, big), axis=1, keepdims=True)
    e2 = jnp.where(emask & (el != i1), logits, NEG)
    m2 = jnp.max(e2, axis=1, keepdims=True)
    i2 = jnp.min(jnp.where(e2 == m2, el, big), axis=1, keepdims=True)
    tt = jnp.exp(m2 - m1)
    p1 = 1.0 / (1.0 + tt)
    w1 = gw * p1
    w2 = gw * (tt * p1)
    oh1 = el == i1
    oh2 = el == i2
    cnt = (oh1 | oh2).astype(F32)
    r_i = lax.broadcasted_iota(I32, (tm, tm), 0)
    c_i = lax.broadcasted_iota(I32, (tm, tm), 1)
    before = _dot((r_i > c_i).astype(BF16), cnt.astype(BF16)) + carry_s[...]
    rank1 = jnp.sum(jnp.where(oh1, before, 0.0), axis=1, keepdims=True)
    rank2 = jnp.sum(jnp.where(oh2, before, 0.0), axis=1, keepdims=True)
    carry_s[...] = carry_s[...] + jnp.sum(cnt, axis=0, keepdims=True)
    meta = jnp.zeros_like(logits)
    for pos, val in enumerate((i1, i2, w1, w2, rank1, rank2)):
        meta = jnp.where(lane_i == pos, val, meta)
    meta_ref[...] = meta
    meta_t_ref[...] = meta.T[0:meta_t_ref.shape[0], :]
    cnt_ref[...] = jnp.broadcast_to(carry_s[...], cnt_ref.shape)


def _router(x, nw, sc, sh, wr, *, rows_per_batch):
    t, d = x.shape
    tm = 256
    tiles_per_batch = rows_per_batch // tm
    return pl.pallas_call(
        _router_kernel,
        out_shape=(jax.ShapeDtypeStruct((t * TOK_SUB, LANES), U32),
                   jax.ShapeDtypeStruct((t, LANES), F32),
                   jax.ShapeDtypeStruct((8, t), F32),
                   jax.ShapeDtypeStruct((8, LANES), F32)),
        grid=(t // tm,),
        in_specs=[pl.BlockSpec((tm, d), lambda i: (i, 0)),
                  pl.BlockSpec((1, d), lambda i: (0, 0)),
                  pl.BlockSpec((1, 1, d), lambda i: (i // tiles_per_batch, 0, 0)),
                  pl.BlockSpec((1, 1, d), lambda i: (i // tiles_per_batch, 0, 0)),
                  pl.BlockSpec((d, LANES), lambda i: (0, 0))],
        out_specs=(pl.BlockSpec((tm * TOK_SUB, LANES), lambda i: (i, 0)),
                   pl.BlockSpec((tm, LANES), lambda i: (i, 0)),
                   pl.BlockSpec((8, tm), lambda i: (0, i)),
                   pl.BlockSpec((8, LANES), lambda i: (0, 0))),
        scratch_shapes=[pltpu.VMEM((1, LANES), F32), pltpu.VMEM((d, LANES), BF16),
                        pltpu.VMEM((d, LANES), BF16)],
        compiler_params=_cparams(("arbitrary",)),
        name="moe_router",
    )(x, nw, sc, sh, wr)


def _slot_kernel(pstart_ref, mt_ref, o_ref):
    eid = mt_ref[0:2, :]
    start = jnp.zeros(eid.shape, I32)
    for e in range(N_EXPERTS):
        start = jnp.where(eid == float(e), pstart_ref[e], start)
    o_ref[...] = start + mt_ref[4:6, :].astype(I32)


def _slots(pstart, meta_t):
    t = meta_t.shape[1]
    return pl.pallas_call(
        _slot_kernel,
        out_shape=jax.ShapeDtypeStruct((2, t), I32),
        grid_spec=pltpu.PrefetchScalarGridSpec(
            num_scalar_prefetch=1,
            grid=(1,),
            in_specs=[pl.BlockSpec(meta_t.shape, lambda i, ps: (0, 0))],
            out_specs=pl.BlockSpec((2, t), lambda i, ps: (0, 0))),
        compiler_params=_cparams(("arbitrary",)),
        name="moe_slots",
    )(pstart, meta_t)


def _token_tile(ref, r):
    return ref.at[pl.ds(pl.multiple_of(r * TOK_SUB, TOK_SUB), TOK_SUB)]


def _dispatch_kernel(dest_ref, zflag_ref, h_ref, xs_ref, zbuf, sem, zsem):
    tm = h_ref.shape[0] // TOK_SUB
    base = pl.program_id(0) * tm
    ntok = pl.num_programs(0) * tm
    tb = zbuf.shape[0]

    @pl.when(pl.program_id(0) == 0)
    def _():
        zbuf[...] = jnp.zeros_like(zbuf)

        def zcopy(b):
            return pltpu.make_async_copy(zbuf, xs_ref.at[pl.ds(pl.multiple_of(b * tb, tb), tb)], zsem)

        def zstart(b, carry):
            @pl.when(zflag_ref[b] == 1)
            def _():
                zcopy(b).start()
            return carry

        def zwait(b, carry):
            @pl.when(zflag_ref[b] == 1)
            def _():
                zcopy(b).wait()
            return carry

        nblk = xs_ref.shape[0] // tb
        lax.fori_loop(0, nblk, zstart, 0)
        lax.fori_loop(0, nblk, zwait, 0)

    def copy(t, k):
        return pltpu.make_async_copy(_token_tile(h_ref, t),
                                     _token_tile(xs_ref, dest_ref[k * ntok + base + t]), sem)

    def issue(t, carry):
        copy(t, 0).start(priority=0)
        copy(t, 1).start(priority=1)
        return carry

    def drain(t, carry):
        copy(t, 0).wait()
        copy(t, 1).wait()
        return carry

    lax.fori_loop(0, tm, issue, 0, unroll=DMA_UNROLL)
    lax.fori_loop(0, tm, drain, 0, unroll=DMA_UNROLL)


def _dispatch(dest, zflag, h, n_slots):
    t = h.shape[0] // TOK_SUB
    tm = 256
    return pl.pallas_call(
        _dispatch_kernel,
        out_shape=jax.ShapeDtypeStruct((n_slots * TOK_SUB, LANES), U32),
        grid_spec=pltpu.PrefetchScalarGridSpec(
            num_scalar_prefetch=2,
            grid=(t // tm,),
            in_specs=[pl.BlockSpec((tm * TOK_SUB, LANES), lambda i, dest, zf: (i, 0))],
            out_specs=pl.BlockSpec(memory_space=pl.ANY),
            scratch_shapes=[pltpu.VMEM((MOE_TB * TOK_SUB, LANES), U32), pltpu.SemaphoreType.DMA(()),
                            pltpu.SemaphoreType.DMA(())]),
        compiler_params=_cparams(("arbitrary",)),
        name="moe_dispatch",
    )(dest, zflag, h)


def _ffn_kernel(be_ref, nxt_ref, slot_ref, nu_ref, xs_ref, w1_hbm, w3_hbm, w2_hbm, o_ref,
                wb1, wb3, wb2, w1_s, w3_s, w2_s, sem, *, layer):
    i = pl.program_id(0)

    def fetch(e, s):
        return (pltpu.make_async_copy(w1_hbm.at[layer, e], wb1.at[s], sem.at[s, 0]),
                pltpu.make_async_copy(w3_hbm.at[layer, e], wb3.at[s], sem.at[s, 1]),
                pltpu.make_async_copy(w2_hbm.at[layer, e], wb2.at[s], sem.at[s, 2]))

    @pl.when(i < nu_ref[0])
    def _():
        e = be_ref[i]
        s = slot_ref[i]

        @pl.when((i == 0) | (e != be_ref[jnp.maximum(i - 1, 0)]))
        def _():
            @pl.when(i == 0)
            def _():
                for cp in fetch(e, s):
                    cp.start()

            for cp in fetch(e, s):
                cp.wait()

            @pl.when(nxt_ref[i] >= 0)
            def _():
                for cp in fetch(nxt_ref[i], 1 - s):
                    cp.start()

            w1_s[...] = wb1[s].astype(BF16)
            w3_s[...] = wb3[s].astype(BF16)
            w2_s[...] = wb2[s].astype(BF16)

        lo, hi = _unpack_halves(_load_token_tiles(xs_ref))
        xl, xh = lo.astype(BF16), hi.astype(BF16)
        half = xl.shape[1]
        a = _dot(xl, w1_s[0:half, :]) + _dot(xh, w1_s[half:2 * half, :])
        b = _dot(xl, w3_s[0:half, :]) + _dot(xh, w3_s[half:2 * half, :])
        hmid = _silu(a) * b
        _store_token_tiles(o_ref, _pack_halves(_dot(hmid.astype(BF16), w2_s[...]).astype(BF16)))

    @pl.when(i >= nu_ref[0])
    def _():
        o_ref[...] = jnp.zeros_like(o_ref)


def _expert_ffn(blk_expert, blk_next, blk_slot, n_used, xs, w1, w3, w2, layer):
    d, f = w1.shape[2], w1.shape[3]
    rows = MOE_TB * TOK_SUB
    nb = xs.shape[0] // rows
    hbm = pl.BlockSpec(memory_space=pl.ANY)
    return pl.pallas_call(
        functools.partial(_ffn_kernel, layer=layer),
        out_shape=jax.ShapeDtypeStruct(xs.shape, U32),
        grid_spec=pltpu.PrefetchScalarGridSpec(
            num_scalar_prefetch=4,
            grid=(nb,),
            in_specs=[pl.BlockSpec((rows, LANES),
                                   lambda i, be, nx, sl, nu: (jnp.minimum(i, nu[0] - 1), 0)),
                      hbm, hbm, hbm],
            out_specs=pl.BlockSpec((rows, LANES), lambda i, be, nx, sl, nu: (i, 0)),
            scratch_shapes=[pltpu.VMEM((2, d, f), F32), pltpu.VMEM((2, d, f), F32),
                            pltpu.VMEM((2, f, d), F32),
                            pltpu.VMEM((d, f), BF16), pltpu.VMEM((d, f), BF16),
                            pltpu.VMEM((f, d), BF16),
                            pltpu.SemaphoreType.DMA((2, 3))]),
        compiler_params=_cparams(("arbitrary",)),
        name="moe_expert_ffn",
    )(blk_expert, blk_next, blk_slot, n_used, xs, w1, w3, w2)


def _combine_kernel(dest_ref, x_ref, meta_ref, g_ref, fnw_ref, ys_ref, o_ref, buf, sem, *, final):
    tm = x_ref.shape[0]
    i = pl.program_id(0)
    nsteps = pl.num_programs(0)
    ntok = nsteps * tm

    def copy(tile, t, k):
        half = tile % 2
        return pltpu.make_async_copy(_token_tile(ys_ref, dest_ref[k * ntok + tile * tm + t]),
                                     _token_tile(buf.at[half, k], t), sem.at[half])

    def issue(tile):
        def body(t, carry):
            copy(tile, t, 0).start(priority=0)
            copy(tile, t, 1).start(priority=1)
            return carry
        lax.fori_loop(0, tm, body, 0, unroll=DMA_UNROLL)

    def drain(tile):
        def body(t, carry):
            copy(tile, t, 0).wait()
            copy(tile, t, 1).wait()
            return carry
        lax.fori_loop(0, tm, body, 0, unroll=DMA_UNROLL)

    @pl.when(i == 0)
    def _():
        issue(i)

    @pl.when(i + 1 < nsteps)
    def _():
        issue(i + 1)

    drain(i)
    cur = i % 2
    meta = meta_ref[...]
    w1, w2 = meta[:, 2:3], meta[:, 3:4]
    lo1, hi1 = _unpack_halves(_load_token_tiles(buf.at[cur, 0]))
    lo2, hi2 = _unpack_halves(_load_token_tiles(buf.at[cur, 1]))
    y = jnp.concatenate([w1 * lo1 + w2 * lo2, w1 * hi1 + w2 * hi2], axis=1)
    xn = x_ref[...] + g_ref[0] * y
    if final:
        ms = jnp.mean(xn * xn, axis=-1, keepdims=True)
        xn = xn * lax.rsqrt(ms + EPS) * fnw_ref[...]
    o_ref[...] = xn


def _combine(dest, x, meta, g, fnw, ys, *, rows_per_batch, final):
    t, d = x.shape
    tm = 256
    tiles_per_batch = rows_per_batch // tm
    return pl.pallas_call(
        functools.partial(_combine_kernel, final=final),
        out_shape=jax.ShapeDtypeStruct((t, d), F32),
        grid_spec=pltpu.PrefetchScalarGridSpec(
            num_scalar_prefetch=1,
            grid=(t // tm,),
            in_specs=[pl.BlockSpec((tm, d), lambda i, dest: (i, 0)),
                      pl.BlockSpec((tm, LANES), lambda i, dest: (i, 0)),
                      pl.BlockSpec((1, 1, d), lambda i, dest: (i // tiles_per_batch, 0, 0)),
                      pl.BlockSpec((1, d), lambda i, dest: (0, 0)),
                      pl.BlockSpec(memory_space=pl.ANY)],
            out_specs=pl.BlockSpec((tm, d), lambda i, dest: (i, 0)),
            scratch_shapes=[pltpu.VMEM((2, 2, tm * TOK_SUB, LANES), U32),
                            pltpu.SemaphoreType.DMA((2,))]),
        compiler_params=_cparams(("arbitrary",)),
        name="moe_combine",
    )(dest, x, meta, g, fnw, ys)


def _hier_moe(x, mod_sc, mod_sh, mod_g, nw, w_group, w_expert, w1, w3, w2, layer, fnw, *,
              rows_per_batch, final):
    t, d = x.shape
    a = 2 * t
    tb = MOE_TB
    wr = jnp.concatenate([w_group, w_expert], axis=1)
    wr = jnp.pad(wr, ((0, 0), (0, LANES - wr.shape[1])))
    h, meta, meta_t, cnt = _router(x, nw, mod_sc, mod_sh, wr, rows_per_batch=rows_per_batch)
    ne = N_EXPERTS
    counts = cnt[0, MOE_GROUPS:MOE_GROUPS + ne].astype(I32)
    padded = ((counts + tb - 1) // tb) * tb
    pend = jnp.cumsum(padded)
    pstart = pend - padded
    dest = _slots(pstart, meta_t).reshape(a)
    nb = (a + ne * (tb - 1) + tb - 1) // tb
    n_used = (pend[-1] // tb).astype(I32)
    blk = jnp.arange(nb, dtype=I32)
    be = jnp.minimum(jnp.sum((pend[None, :] <= (blk * tb)[:, None]).astype(I32), axis=1), ne - 1)
    seg_last = jnp.any((pend[None, :] == ((blk + 1) * tb)[:, None]) & (padded[None, :] > 0), axis=1)
    zflag = (seg_last | (blk >= n_used)).astype(I32)
    nonempty = counts > 0
    seg = jnp.cumsum(nonempty.astype(I32)) - 1
    later = lax.cummin(jnp.where(nonempty, jnp.arange(ne, dtype=I32), ne), axis=0, reverse=True)
    nxt_e = jnp.concatenate([later[1:], jnp.full((1,), ne, I32)])
    nxt_e = jnp.where(nxt_e == ne, -1, nxt_e)
    xs = _dispatch(dest, zflag, h, nb * tb)
    ys = _expert_ffn(be, nxt_e[be], seg[be] % 2, n_used.reshape(1), xs, w1, w3, w2, layer)
    return _combine(dest, x, meta, mod_g, fnw, ys, rows_per_batch=rows_per_batch, final=final)


def kernel(x, c, ada_w, ada_b, norm_mix, norm_ffn, ssd_w_in, ssd_conv_w, ssd_conv_b, ssd_a_log,
           ssd_dt_bias, ssd_d, ssd_norm_w, ssd_w_out, na_w_qkv, na_rpb, na_w_o,
           moe_w_group, moe_w_expert, moe_w1, moe_w3, moe_w2, final_norm):
    batch, seq, d = x.shape
    depth = ada_w.shape[0]
    xt = x.reshape(batch * seq, d)
    c_pad = jnp.pad(c, ((0, 8 - batch), (0, 0)))
    mod = _ada(c_pad, ada_w, ada_b)[:, :batch]
    fnw = final_norm.reshape(1, d)
    for i in range(depth):
        sh1, sc1, g1, sh2, sc2, g2 = [mod[i, :, k * d:(k + 1) * d].reshape(batch, 1, d)
                                      for k in range(6)]
        j = i // 2
        nw = norm_mix[i].reshape(1, d)
        if i % 2 == 0:
            xt = _ssd_mixer(xt, sc1, sh1, g1, nw, ssd_w_in[j], ssd_conv_w[j], ssd_conv_b[j],
                            ssd_a_log[j], ssd_dt_bias[j], ssd_d[j], ssd_norm_w[j], ssd_w_out[j],
                            batch=batch, seq=seq)
        else:
            xt = _na_mixer(xt, sc1, sh1, g1, nw, na_w_qkv[j], na_rpb[j], na_w_o[j],
                           batch=batch, seq=seq)
        xt = _hier_moe(xt, sc2, sh2, g2, norm_ffn[i].reshape(1, d), moe_w_group[i], moe_w_expert[i],
                       moe_w1, moe_w3, moe_w2, i, fnw, rows_per_batch=seq,
                       final=(i == depth - 1))
    return xt.reshape(batch, seq, d)
```

```python
import functools

import jax
import jax.numpy as jnp
from jax import lax
from jax.experimental import pallas as pl
from jax.experimental.pallas import tpu as pltpu

F32 = jnp.float32
BF16 = jnp.bfloat16
I32 = jnp.int32

EPS = 1e-6
NEG = -1e30
LOG2_E = 1.4426950408889634

D_MODEL = 2048
GRID_W = 64
SSD_HEAD_DIM = 64
SSD_GROUPS = 8
SSD_HEADS_PER_GROUP = 8
SSD_STATE = 128
SSD_CONV = 5
D_INNER = 2 * D_MODEL
GROUP_W = SSD_HEADS_PER_GROUP * SSD_HEAD_DIM
SSD_Q = 128
CONV_HALO = 16
NA_HEAD_DIM = 64
NA_HEADS = D_MODEL // NA_HEAD_DIM
NA_PAIRS = NA_HEADS // 2
WIN_H = 8
WIN_W = 16
NA_ROWS = 4
MOE_GROUPS = 4
MOE_EPG = 8
N_EXPERTS = MOE_GROUPS * MOE_EPG
MOE_D_FF = D_MODEL // 4
MOE_TB = 256
DMA_UNROLL = 8

VMEM_LIMIT = 56 * 1024 * 1024
LANES = 128


def _cparams(sem):
    return pltpu.CompilerParams(dimension_semantics=sem, vmem_limit_bytes=VMEM_LIMIT)


def _silu(v):
    return v * pl.reciprocal(1.0 + jnp.exp(-v), approx=True)


def _softplus(v):
    return jnp.maximum(v, 0.0) + jnp.log1p(jnp.exp(-jnp.abs(v)))


def _split3(v):
    hi = v.astype(BF16)
    r1 = v - hi.astype(F32)
    mid = r1.astype(BF16)
    lo = (r1 - mid.astype(F32)).astype(BF16)
    return hi, mid, lo


def _dot(a, b):
    return jnp.dot(a, b, preferred_element_type=F32)


def _dot_nt(a, b):
    return lax.dot_general(a, b, (((1,), (1,)), ((), ())), preferred_element_type=F32)


def _dot3_left(v, sel):
    hi, mid, lo = _split3(v)
    return _dot(hi, sel) + _dot(mid, sel) + _dot(lo, sel)


def _normmod(x, nw, sc, sh):
    ms = jnp.mean(x * x, axis=-1, keepdims=True)
    return (x * lax.rsqrt(ms + EPS) * nw) * (1.0 + sc) + sh


def _split2(v):
    hi = v.astype(BF16)
    return hi, (v - hi.astype(F32)).astype(BF16)


def _dot_split(a, b_hi, b_lo):
    a_hi, a_lo = _split2(a)
    return _dot(a_hi, b_hi) + (_dot(a_lo, b_hi) + _dot(a_hi, b_lo))


def _ada_kernel(c_ref, w_ref, b_ref, o_ref):
    c = c_ref[...]
    o_ref[0] = _dot_split(c / (1.0 + jnp.exp(-c)), *_split2(w_ref[0])) + b_ref[0]


def _ada(c_pad, ada_w, ada_b):
    depth, d, n = ada_w.shape
    tn = 1024
    return pl.pallas_call(
        _ada_kernel,
        out_shape=jax.ShapeDtypeStruct((depth, 8, n), F32),
        grid=(depth, n // tn),
        in_specs=[pl.BlockSpec((8, d), lambda i, j: (0, 0)),
                  pl.BlockSpec((1, d, tn), lambda i, j: (i, 0, j)),
                  pl.BlockSpec((1, 1, tn), lambda i, j: (i, 0, j))],
        out_specs=pl.BlockSpec((1, 8, tn), lambda i, j: (i, 0, j)),
        compiler_params=_cparams(("arbitrary", "arbitrary")),
        name="ada_mod",
    )(c_pad, ada_w, ada_b.reshape(depth, 1, n))


def _nm_mm_kernel(x_ref, nw_ref, sc_ref, sh_ref, w_ref, *rest, pair_major, tail, lead_tiles,
                  lead_scale):
    if tail:
        wt_ref, o_ref, ot_ref, h_ref = rest
    else:
        o_ref, h_ref = rest

    @pl.when(pl.program_id(1) == 0)
    def _():
        h_ref[...] = _normmod(x_ref[...], nw_ref[...], sc_ref[0], sh_ref[0]).astype(BF16)

    r = _dot(h_ref[...], w_ref[...].astype(BF16))
    if lead_tiles:
        r = r * jnp.where(pl.program_id(1) < lead_tiles, lead_scale, 1.0)
    if pair_major:
        for c in range(o_ref.shape[0]):
            o_ref[c] = r[:, c * LANES:(c + 1) * LANES].astype(o_ref.dtype)
    else:
        o_ref[...] = r.astype(o_ref.dtype)

    if tail:
        @pl.when(pl.program_id(1) == pl.num_programs(1) - 1)
        def _():
            ot_ref[...] = _dot(h_ref[...], wt_ref[...].astype(BF16))


def _nm_matmul(x, nw, sc, sh, w, *, ncols, tn, out_dtype, rows_per_batch, pair_major=False,
               tail_cols=0, lead_cols=0, lead_scale=1.0):
    t, d = x.shape
    tm = 1024
    tiles_per_batch = rows_per_batch // tm
    if pair_major:
        out_shape = jax.ShapeDtypeStruct((ncols // LANES, t, LANES), out_dtype)
        out_spec = pl.BlockSpec((tn // LANES, tm, LANES), lambda i, j: (j, i, 0))
    else:
        out_shape = jax.ShapeDtypeStruct((t, ncols), out_dtype)
        out_spec = pl.BlockSpec((tm, tn), lambda i, j: (i, j))
    in_specs = [pl.BlockSpec((tm, d), lambda i, j: (i, 0)),
                pl.BlockSpec((1, d), lambda i, j: (0, 0)),
                pl.BlockSpec((1, 1, d), lambda i, j: (i // tiles_per_batch, 0, 0)),
                pl.BlockSpec((1, 1, d), lambda i, j: (i // tiles_per_batch, 0, 0)),
                pl.BlockSpec((d, tn), lambda i, j: (0, j))]
    operands = [x, nw, sc, sh, w]
    if tail_cols:
        tail_blk = ncols // tail_cols
        in_specs.append(pl.BlockSpec((d, tail_cols), lambda i, j: (0, tail_blk)))
        operands.append(w)
        out_shape = (out_shape, jax.ShapeDtypeStruct((t, tail_cols), F32))
        out_spec = (out_spec, pl.BlockSpec((tm, tail_cols), lambda i, j: (i, 0)))
    return pl.pallas_call(
        functools.partial(_nm_mm_kernel, pair_major=pair_major, tail=bool(tail_cols),
                          lead_tiles=lead_cols // tn, lead_scale=lead_scale),
        out_shape=out_shape,
        grid=(t // tm, ncols // tn),
        in_specs=in_specs,
        out_specs=out_spec,
        scratch_shapes=[pltpu.VMEM((tm, d), BF16)],
        compiler_params=_cparams(("arbitrary", "arbitrary")),
        name="norm_mod_matmul",
    )(*operands)


def _mm_resid_kernel(a_ref, w_ref, x_ref, g_ref, o_ref, *, pair_major):
    if pair_major:
        a = jnp.concatenate([a_ref[c] for c in range(a_ref.shape[0])], axis=1)
    else:
        a = a_ref[...]
    o_ref[...] = x_ref[...] + g_ref[0] * _dot(a, w_ref[...].astype(BF16))


def _mm_resid(a, w, x, g, *, rows_per_batch, pair_major=False):
    t, n = x.shape
    k = w.shape[0]
    tn = 512
    tm = (4 * 1024 * 1024) // k
    tiles_per_batch = rows_per_batch // tm
    if pair_major:
        a_spec = pl.BlockSpec((k // LANES, tm, LANES), lambda i, j: (0, i, 0))
    else:
        a_spec = pl.BlockSpec((tm, k), lambda i, j: (i, 0))
    return pl.pallas_call(
        functools.partial(_mm_resid_kernel, pair_major=pair_major),
        out_shape=jax.ShapeDtypeStruct((t, n), F32),
        grid=(t // tm, n // tn),
        in_specs=[a_spec,
                  pl.BlockSpec((k, tn), lambda i, j: (0, j)),
                  pl.BlockSpec((tm, tn), lambda i, j: (i, j)),
                  pl.BlockSpec((1, 1, tn), lambda i, j: (i // tiles_per_batch, 0, j))],
        out_specs=pl.BlockSpec((tm, tn), lambda i, j: (i, j)),
        compiler_params=_cparams(("arbitrary", "arbitrary")),
        name="matmul_resid",
    )(a, w, x, g)


def _ssd_kernel(z_ref, x_ref, b_ref, c_ref, dtr_ref,
                cwx_ref, cwb_ref, cwc_ref, cbx_ref, cbb_ref, cbc_ref,
                alr_ref, dbr_ref, dsk_ref, nw_ref,
                o_ref,
                xc_s, bc_s, cc_s, yacc_s, st_s, cv_s):
    seq = x_ref.shape[0]
    q = SSD_Q
    nc = seq // q
    halo = CONV_HALO
    nrow = 2 * SSD_HEADS_PER_GROUP

    def conv_piece(j, base, src_ref, w_ref, bias_ref, dst_ref, lo, stage):
        cols = slice(lo, lo + LANES)
        pstart = pl.multiple_of(jnp.maximum(base - halo, 0), halo)
        nstart = pl.multiple_of(jnp.minimum(base + q, seq - halo), halo)
        stage[0:halo, :] = jnp.where(j > 0, src_ref[pl.ds(pstart, halo), cols].astype(F32), 0.0)
        stage[halo:halo + q, :] = src_ref[pl.ds(base, q), cols].astype(F32)
        stage[halo + q:, :] = jnp.where(j < nc - 1, src_ref[pl.ds(nstart, halo), cols].astype(F32), 0.0)
        acc = jnp.broadcast_to(bias_ref[:, cols], (q, LANES))
        for k in range(SSD_CONV):
            first = halo - SSD_CONV // 2 + k
            acc = acc + w_ref[k:k + 1, cols] * stage[first:first + q, :]
        dst_ref[pl.ds(base, q), cols] = _silu(acc).astype(BF16)

    def conv_chunk(j, carry):
        base = pl.multiple_of(j * q, q)
        npx = GROUP_W // LANES
        for i in range(npx):
            conv_piece(j, base, x_ref, cwx_ref, cbx_ref, xc_s, i * LANES, cv_s.at[i])
        conv_piece(j, base, b_ref, cwb_ref, cbb_ref, bc_s, 0, cv_s.at[npx])
        conv_piece(j, base, c_ref, cwc_ref, cbc_ref, cc_s, 0, cv_s.at[npx + 1])
        return carry

    lax.fori_loop(0, nc, conv_chunk, 0)

    row_i = lax.broadcasted_iota(I32, (q, q), 0)
    col_i = lax.broadcasted_iota(I32, (q, q), 1)
    lower = row_i >= col_i
    upper = row_i <= col_i
    lower_b = lower.astype(BF16)
    upper_b = upper.astype(BF16)
    left =lax.broadcasted_iota(I32, (q, LANES), 1) < SSD_HEAD_DIM
    a_row = -jnp.exp(alr_ref[...])
    pad_rows = jnp.zeros((LANES - nrow, q), F32)

    def scan_pass(direction):
        hoff = direction * SSD_HEADS_PER_GROUP
        mask = lower if direction == 0 else upper
        tri = upper_b if direction == 0 else lower_b
        edge = q - 1 if direction == 0 else 0
        st_s[...] = jnp.zeros_like(st_s)

        def chunk(t, carry):
            c = t if direction == 0 else nc - 1 - t
            base = pl.multiple_of(c * q, q)
            rows = pl.ds(base, q)
            dt_r = _softplus(dtr_ref[c] + dbr_ref[...])
            cum_r = _dot3_left(dt_r * a_row, tri) * LOG2_E
            cum_c = jnp.concatenate([cum_r, pad_rows], axis=0).T
            decdt_r = jnp.exp2(cum_r[:, edge:edge + 1] - cum_r) * dt_r
            src_r = cum_r - jnp.log2(dt_r)

            bm = bc_s[rows, :]
            cm = cc_s[rows, :]
            cb = _dot_nt(cm, bm)
            bm_t = bm.astype(F32).T
            y_off_all = _dot(cm, st_s[...].astype(BF16)) if direction == 1 else None

            ssq = jnp.zeros((q, 1), F32)
            for pp in range(SSD_HEADS_PER_GROUP // 2):
                cols = slice(pp * LANES, (pp + 1) * LANES)
                xcb = xc_s[rows, cols]
                zero_b = jnp.zeros_like(xcb)
                st_in = st_s[:, cols]
                lhs_y, lhs_s, scales = [], [], []
                for par in range(2):
                    j = hoff + 2 * pp + par
                    cum_b = jnp.broadcast_to(cum_c[:, j:j + 1], (q, q))
                    lmat = jnp.exp2(jnp.where(mask, cum_b - src_r[j:j + 1, :], NEG))
                    lhs_y.append((cb * lmat).astype(BF16))
                    lhs_s.append((bm_t * decdt_r[j:j + 1, :]).astype(BF16))
                    scales.append(jnp.exp2(cum_b))
                x_rhs = jnp.concatenate([jnp.where(left, xcb, zero_b), jnp.where(left, zero_b, xcb)],
                                        axis=0)
                sc_tile = jnp.where(left, scales[0], scales[1])
                y_off = _dot(cm, st_in.astype(BF16)) if y_off_all is None else y_off_all[:, cols]
                y = _dot(jnp.concatenate(lhs_y, axis=1), x_rhs) + y_off * sc_tile
                st_s[:, cols] = (st_in * sc_tile[edge:edge + 1, :]
                                 + _dot(jnp.concatenate(lhs_s, axis=1), x_rhs))
                if direction == 0:
                    yacc_s[rows, cols] = y
                else:
                    total = yacc_s[rows, cols] + y + xcb.astype(F32) * dsk_ref[:, cols]
                    gated = total * _silu(z_ref[rows, cols].astype(F32))
                    ssq = ssq + jnp.sum(gated * gated, axis=-1, keepdims=True)
                    yacc_s[rows, cols] = gated
            if direction == 1:
                inv = lax.rsqrt(ssq * (1.0 / GROUP_W) + EPS)
                o_ref[rows, :] = (yacc_s[rows, :] * inv * nw_ref[...]).astype(BF16)
            return carry

        lax.fori_loop(0, nc, chunk, 0, unroll=8)

    scan_pass(0)
    scan_pass(1)


def _ssd_core(zx, dt_row, conv_w, conv_b, al_row, db_row, dskip, norm_w, *, batch, seq):
    g = SSD_GROUPS
    nc = seq // SSD_Q
    xb = D_INNER // GROUP_W
    bb = (2 * D_INNER) // SSD_STATE
    cb = bb + g
    cwb = D_INNER // SSD_STATE
    cwc = cwb + g
    return pl.pallas_call(
        _ssd_kernel,
        out_shape=jax.ShapeDtypeStruct((batch * seq, D_INNER), BF16),
        grid=(batch, g),
        in_specs=[
            pl.BlockSpec((seq, GROUP_W), lambda b, i: (b, i)),
            pl.BlockSpec((seq, GROUP_W), lambda b, i: (b, xb + i)),
            pl.BlockSpec((seq, SSD_STATE), lambda b, i: (b, bb + i)),
            pl.BlockSpec((seq, SSD_STATE), lambda b, i: (b, cb + i)),
            pl.BlockSpec((None, None, nc, 2 * SSD_HEADS_PER_GROUP, SSD_Q), lambda b, i: (b, i, 0, 0, 0)),
            pl.BlockSpec((SSD_CONV, GROUP_W), lambda b, i: (0, i)),
            pl.BlockSpec((SSD_CONV, SSD_STATE), lambda b, i: (0, cwb + i)),
            pl.BlockSpec((SSD_CONV, SSD_STATE), lambda b, i: (0, cwc + i)),
            pl.BlockSpec((1, GROUP_W), lambda b, i: (0, i)),
            pl.BlockSpec((1, SSD_STATE), lambda b, i: (0, cwb + i)),
            pl.BlockSpec((1, SSD_STATE), lambda b, i: (0, cwc + i)),
            pl.BlockSpec((None, 2 * SSD_HEADS_PER_GROUP, 1), lambda b, i: (i, 0, 0)),
            pl.BlockSpec((None, 2 * SSD_HEADS_PER_GROUP, 1), lambda b, i: (i, 0, 0)),
            pl.BlockSpec((1, GROUP_W), lambda b, i: (0, i)),
            pl.BlockSpec((1, GROUP_W), lambda b, i: (0, i)),
        ],
        out_specs=pl.BlockSpec((seq, GROUP_W), lambda b, i: (b, i)),
        scratch_shapes=[pltpu.VMEM((seq, GROUP_W), BF16),
                        pltpu.VMEM((seq, SSD_STATE), BF16),
                        pltpu.VMEM((seq, SSD_STATE), BF16),
                        pltpu.VMEM((seq, GROUP_W), F32),
                        pltpu.VMEM((SSD_STATE, GROUP_W), F32),
                        pltpu.VMEM((GROUP_W // LANES + 2, SSD_Q + 2 * CONV_HALO, LANES), F32)],
        compiler_params=_cparams(("arbitrary", "arbitrary")),
        name="ssd_core",
    )(zx, zx, zx, zx, dt_row, conv_w, conv_w, conv_w, conv_b, conv_b, conv_b,
      al_row, db_row, dskip, norm_w)


def _ssd_mixer(x, mod_sc, mod_sh, mod_g, nw, w_in, conv_w, conv_b, a_log, dt_bias, d_skip, norm_w,
               w_out, *, batch, seq):
    g, r = SSD_GROUPS, SSD_HEADS_PER_GROUP
    conv_dim = conv_w.shape[1]
    zx, dt_raw = _nm_matmul(x, nw, mod_sc, mod_sh, w_in, ncols=D_INNER + conv_dim, tn=1024,
                            out_dtype=BF16, rows_per_batch=seq, tail_cols=2 * g * r)
    nc = seq // SSD_Q
    dt_row = dt_raw.reshape(batch, nc, SSD_Q, 2, g, r).transpose(0, 4, 1, 3, 5, 2)
    dt_row = dt_row.reshape(batch, g, nc, 2 * r, SSD_Q)

    def row_form(p):
        return p.reshape(2, g, r).transpose(1, 0, 2).reshape(g, 2 * r, 1)

    yn = _ssd_core(zx, dt_row, conv_w, conv_b.reshape(1, conv_dim), row_form(a_log), row_form(dt_bias),
                   jnp.repeat(d_skip, SSD_HEAD_DIM).reshape(1, D_INNER), norm_w.reshape(1, D_INNER),
                   batch=batch, seq=seq)
    return _mm_resid(yn, w_out, x, mod_g, rows_per_batch=seq)


def _bias_table_kernel(rpb_ref, o_ref):
    lane = lax.broadcasted_iota(I32, (GRID_W, LANES), 1)
    j = lax.broadcasted_iota(I32, (GRID_W, LANES), 0)
    c = lane & (GRID_W - 1)
    c0 = jnp.clip(j - WIN_W // 2, 0, GRID_W - WIN_W)
    win = (c >= c0) & (c < c0 + WIN_W)
    left = lane < GRID_W

    def one_offset(dy0, carry):
        for par in range(2):
            for m in range(WIN_H // 2):
                tiles = []
                for sub in range(2):
                    row = rpb_ref[par, pl.ds(dy0 + 2 * m + sub, 1), :]
                    shift = (sub * GRID_W - (WIN_W - 1)) % LANES
                    tiles.append(pltpu.roll(jnp.broadcast_to(row, (GRID_W, LANES)), shift, 1,
                                            stride=1, stride_axis=0))
                tile = jnp.where(win, jnp.where(left, tiles[0], tiles[1]) * LOG2_E, NEG)
                o_ref[dy0, 0, par * GRID_W:(par + 1) * GRID_W, m * LANES:(m + 1) * LANES] = tile
        return carry

    lax.fori_loop(0, o_ref.shape[0], one_offset, 0)


def _na_bias_table(rpb):
    h, ndy, ndx = rpb.shape
    rpb_p = jnp.pad(rpb, ((0, 0), (0, 2 * WIN_H - ndy), (0, LANES - ndx)))
    return pl.pallas_call(
        _bias_table_kernel,
        out_shape=jax.ShapeDtypeStruct((WIN_H, h // 2, 2 * GRID_W, WIN_H * GRID_W), F32),
        grid=(h // 2,),
        in_specs=[pl.BlockSpec((2, 2 * WIN_H, LANES), lambda p: (p, 0, 0))],
        out_specs=pl.BlockSpec((WIN_H, 1, 2 * GRID_W, WIN_H * GRID_W), lambda p: (0, p, 0, 0)),
        compiler_params=_cparams(("arbitrary",)),
        name="na_bias_table",
    )(rpb_p)


def _na_kernel(q_ref, k0_ref, k1_ref, k2_ref, v0_ref, v1_ref, v2_ref, bias_ref, o_ref, kc_s, vc_s,
               s_s, p_s, r_s, *, n_row_blocks):
    rb = pl.program_id(2)
    npairs = q_ref.shape[0]
    blk = NA_ROWS * GRID_W
    nkeys = WIN_H * GRID_W
    for i, (kr, vr) in enumerate(((k0_ref, v0_ref), (k1_ref, v1_ref), (k2_ref, v2_ref))):
        kc_s[:, i * blk:(i + 1) * blk, :] = kr[...]
        vc_s[:, i * blk:(i + 1) * blk, :] = vr[...]
    first = rb == 0
    last = rb == n_row_blocks - 1
    edge = first | last
    lane = lax.broadcasted_iota(I32, (GRID_W, LANES), 1)
    left = lane < NA_HEAD_DIM

    def window(qi):
        off = jnp.where(first, 0, jnp.where(last, blk, qi * GRID_W))
        li = jnp.where(edge, NA_ROWS - 1 - qi, NA_ROWS - 1)
        return pl.multiple_of(off, GRID_W), li

    def pair_body(pp, carry):
        for qi in range(NA_ROWS):
            off, li = window(qi)
            q2 = q_ref[pp, qi * GRID_W:(qi + 1) * GRID_W, :]
            zero = jnp.zeros_like(q2)
            qs = jnp.concatenate([jnp.where(left, q2, zero), jnp.where(left, zero, q2)], axis=0)
            kw = kc_s[pp, pl.ds(off, nkeys), :]
            s_s[qi] = _dot_nt(qs, kw) + bias_ref[li, pp]
        for qi in range(NA_ROWS):
            s = s_s[qi]
            p = jnp.exp2(s - jnp.max(s, axis=-1, keepdims=True))
            r_s[qi] = 1.0 / jnp.sum(p, axis=-1, keepdims=True)
            p_s[qi] = p.astype(BF16)
        for qi in range(NA_ROWS):
            off, _ = window(qi)
            pv = _dot(p_s[qi], vc_s[pp, pl.ds(off, nkeys), :]) * r_s[qi]
            o = jnp.where(left, pv[0:GRID_W], pv[GRID_W:2 * GRID_W])
            o_ref[pp, qi * GRID_W:(qi + 1) * GRID_W, :] = o.astype(BF16)
        return carry

    lax.fori_loop(0, npairs, pair_body, 0, unroll=4)


def _na_attention(qkv_t, bias_tab, *, batch, seq):
    t = batch * seq
    blk = NA_ROWS * GRID_W
    nrb = seq // blk
    hp = NA_PAIRS // 2
    nsec = NA_PAIRS // hp

    def kv_spec(sec, i):
        def imap(hh, b, r):
            return (sec * nsec + hh, b * nrb + jnp.clip(r - 1, 0, nrb - 3) + i, 0)
        return pl.BlockSpec((hp, blk, LANES), imap)

    return pl.pallas_call(
        functools.partial(_na_kernel, n_row_blocks=nrb),
        out_shape=jax.ShapeDtypeStruct((NA_PAIRS, t, LANES), BF16),
        grid=(nsec, batch, nrb),
        in_specs=[pl.BlockSpec((hp, blk, LANES), lambda hh, b, r: (hh, b * nrb + r, 0)),
                  kv_spec(1, 0), kv_spec(1, 1), kv_spec(1, 2),
                  kv_spec(2, 0), kv_spec(2, 1), kv_spec(2, 2),
                  pl.BlockSpec((NA_ROWS, hp, 2 * GRID_W, WIN_H * GRID_W),
                               lambda hh, b, r: (jnp.where(r == 0, 1, 0), hh, 0, 0))],
        out_specs=pl.BlockSpec((hp, blk, LANES), lambda hh, b, r: (hh, b * nrb + r, 0)),
        scratch_shapes=[pltpu.VMEM((hp, 3 * blk, LANES), BF16),
                        pltpu.VMEM((hp, 3 * blk, LANES), BF16),
                        pltpu.VMEM((NA_ROWS, 2 * GRID_W, WIN_H * GRID_W), F32),
                        pltpu.VMEM((NA_ROWS, 2 * GRID_W, WIN_H * GRID_W), BF16),
                        pltpu.VMEM((NA_ROWS, 2 * GRID_W, 1), F32)],
        compiler_params=_cparams(("arbitrary", "arbitrary", "arbitrary")),
        name="na_attention",
    )(qkv_t, qkv_t, qkv_t, qkv_t, qkv_t, qkv_t, qkv_t, bias_tab)


def _na_mixer(x, mod_sc, mod_sh, mod_g, nw, w_qkv, rpb, w_o, *, batch, seq):
    qkv_t = _nm_matmul(x, nw, mod_sc, mod_sh, w_qkv, ncols=3 * D_MODEL, tn=1024,
                       out_dtype=BF16, rows_per_batch=seq, pair_major=True,
                       lead_cols=D_MODEL, lead_scale=NA_HEAD_DIM ** -0.5 * LOG2_E)
    o_t = _na_attention(qkv_t, _na_bias_table(rpb), batch=batch, seq=seq)
    return _mm_resid(o_t, w_o, x, mod_g, rows_per_batch=seq, pair_major=True)


U32 = jnp.uint32


def _pack_halves(vb):
    n = vb.shape[1] // 2
    bits = pltpu.bitcast(vb.astype(F32), U32)
    return (bits[:, :n] >> 16) | bits[:, n:]


def _unpack_halves(w):
    return pltpu.bitcast(w << 16, F32), pltpu.bitcast(w & U32(0xFFFF0000), F32)


TOK_SUB = (D_MODEL // 2) // LANES


def _store_token_tiles(ref, packed):
    rows = packed.shape[0]
    for s in range(TOK_SUB):
        ref[pl.ds(s, rows, stride=TOK_SUB), :] = packed[:, s * LANES:(s + 1) * LANES]


def _load_token_tiles(ref):
    rows = ref.shape[0] // TOK_SUB
    return jnp.concatenate([ref[pl.ds(s, rows, stride=TOK_SUB), :] for s in range(TOK_SUB)], axis=1)


def _router_kernel(x_ref, nw_ref, sc_ref, sh_ref, wr_ref, h_ref, meta_ref, meta_t_ref, cnt_ref,
                   carry_s, whi_s, wlo_s):
    @pl.when(pl.program_id(0) == 0)
    def _():
        carry_s[...] = jnp.zeros_like(carry_s)
        whi_s[...], wlo_s[...] = _split2(wr_ref[...])

    h = _normmod(x_ref[...], nw_ref[...], sc_ref[0], sh_ref[0])
    _store_token_tiles(h_ref, _pack_halves(h.astype(BF16)))
    logits = _dot_split(h, whi_s[...], wlo_s[...])
    tm = logits.shape[0]
    lane_i = lax.broadcasted_iota(I32, logits.shape, 1)
    lane = lane_i.astype(F32)
    big = 1e9
    gl = jnp.where(lane_i < MOE_GROUPS, logits, NEG)
    gmax = jnp.max(gl, axis=1, keepdims=True)
    gsel = jnp.min(jnp.where(gl == gmax, lane, big), axis=1, keepdims=True)
    gw = 1.0 / jnp.sum(jnp.exp(gl - gmax), axis=1, keepdims=True)
    el = lane - MOE_GROUPS
    lo = gsel * MOE_EPG
    emask = (el >= lo) & (el < lo + MOE_EPG)
    e1 = jnp.where(emask, logits, NEG)
    m1 = jnp.max(e1, axis=1, keepdims=True)
    i1 = jnp.min(jnp.where(e1 == m1, el, big), axis=1, keepdims=True)
    e2 = jnp.where(emask & (el != i1), logits, NEG)
    m2 = jnp.max(e2, axis=1, keepdims=True)
    i2 = jnp.min(jnp.where(e2 == m2, el, big), axis=1, keepdims=True)
    tt = jnp.exp(m2 - m1)
    p1 = 1.0 / (1.0 + tt)
    w1 = gw * p1
    w2 = gw * (tt * p1)
    oh1 = el == i1
    oh2 = el == i2
    cnt = (oh1 | oh2).astype(F32)
    r_i = lax.broadcasted_iota(I32, (tm, tm), 0)
    c_i = lax.broadcasted_iota(I32, (tm, tm), 1)
    before = _dot((r_i > c_i).astype(BF16), cnt.astype(BF16)) + carry_s[...]
    rank1 = jnp.sum(jnp.where(oh1, before, 0.0), axis=1, keepdims=True)
    rank2 = jnp.sum(jnp.where(oh2, before, 0.0), axis=1, keepdims=True)
    carry_s[...] = carry_s[...] + jnp.sum(cnt, axis=0, keepdims=True)
    meta = jnp.zeros_like(logits)
    for pos, val in enumerate((i1, i2, w1, w2, rank1, rank2)):
        meta = jnp.where(lane_i == pos, val, meta)
    meta_ref[...] = meta
    meta_t_ref[...] = meta.T[0:meta_t_ref.shape[0], :]
    cnt_ref[...] = jnp.broadcast_to(carry_s[...], cnt_ref.shape)


def _router(x, nw, sc, sh, wr, *, rows_per_batch):
    t, d = x.shape
    tm = 256
    tiles_per_batch = rows_per_batch // tm
    return pl.pallas_call(
        _router_kernel,
        out_shape=(jax.ShapeDtypeStruct((t * TOK_SUB, LANES), U32),
                   jax.ShapeDtypeStruct((t, LANES), F32),
                   jax.ShapeDtypeStruct((8, t), F32),
                   jax.ShapeDtypeStruct((8, LANES), F32)),
        grid=(t // tm,),
        in_specs=[pl.BlockSpec((tm, d), lambda i: (i, 0)),
                  pl.BlockSpec((1, d), lambda i: (0, 0)),
                  pl.BlockSpec((1, 1, d), lambda i: (i // tiles_per_batch, 0, 0)),
                  pl.BlockSpec((1, 1, d), lambda i: (i // tiles_per_batch, 0, 0)),
                  pl.BlockSpec((d, LANES), lambda i: (0, 0))],
        out_specs=(pl.BlockSpec((tm * TOK_SUB, LANES), lambda i: (i, 0)),
                   pl.BlockSpec((tm, LANES), lambda i: (i, 0)),
                   pl.BlockSpec((8, tm), lambda i: (0, i)),
                   pl.BlockSpec((8, LANES), lambda i: (0, 0))),
        scratch_shapes=[pltpu.VMEM((1, LANES), F32), pltpu.VMEM((d, LANES), BF16),
                        pltpu.VMEM((d, LANES), BF16)],
        compiler_params=_cparams(("arbitrary",)),
        name="moe_router",
    )(x, nw, sc, sh, wr)


def _slot_kernel(pstart_ref, mt_ref, o_ref):
    eid = mt_ref[0:2, :]
    start = jnp.zeros(eid.shape, I32)
    for e in range(N_EXPERTS):
        start = jnp.where(eid == float(e), pstart_ref[e], start)
    o_ref[...] = start + mt_ref[4:6, :].astype(I32)


def _slots(pstart, meta_t):
    t = meta_t.shape[1]
    return pl.pallas_call(
        _slot_kernel,
        out_shape=jax.ShapeDtypeStruct((2, t), I32),
        grid_spec=pltpu.PrefetchScalarGridSpec(
            num_scalar_prefetch=1,
            grid=(1,),
            in_specs=[pl.BlockSpec(meta_t.shape, lambda i, ps: (0, 0))],
            out_specs=pl.BlockSpec((2, t), lambda i, ps: (0, 0))),
        compiler_params=_cparams(("arbitrary",)),
        name="moe_slots",
    )(pstart, meta_t)


def _token_tile(ref, r):
    return ref.at[pl.ds(pl.multiple_of(r * TOK_SUB, TOK_SUB), TOK_SUB)]


def _dispatch_kernel(dest_ref, zflag_ref, h_ref, xs_ref, zbuf, sem, zsem):
    tm = h_ref.shape[0] // TOK_SUB
    base = pl.program_id(0) * tm
    ntok = pl.num_programs(0) * tm
    tb = zbuf.shape[0]

    @pl.when(pl.program_id(0) == 0)
    def _():
        zbuf[...] = jnp.zeros_like(zbuf)

        def zcopy(b):
            return pltpu.make_async_copy(zbuf, xs_ref.at[pl.ds(pl.multiple_of(b * tb, tb), tb)], zsem)

        def zstart(b, carry):
            @pl.when(zflag_ref[b] == 1)
            def _():
                zcopy(b).start()
            return carry

        def zwait(b, carry):
            @pl.when(zflag_ref[b] == 1)
            def _():
                zcopy(b).wait()
            return carry

        nblk = xs_ref.shape[0] // tb
        lax.fori_loop(0, nblk, zstart, 0)
        lax.fori_loop(0, nblk, zwait, 0)

    def copy(t, k):
        return pltpu.make_async_copy(_token_tile(h_ref, t),
                                     _token_tile(xs_ref, dest_ref[k * ntok + base + t]), sem)

    def issue(t, carry):
        copy(t, 0).start(priority=0)
        copy(t, 1).start(priority=1)
        return carry

    def drain(t, carry):
        copy(t, 0).wait()
        copy(t, 1).wait()
        return carry

    lax.fori_loop(0, tm, issue, 0, unroll=DMA_UNROLL)
    lax.fori_loop(0, tm, drain, 0, unroll=DMA_UNROLL)


def _dispatch(dest, zflag, h, n_slots):
    t = h.shape[0] // TOK_SUB
    tm = 256
    return pl.pallas_call(
        _dispatch_kernel,
        out_shape=jax.ShapeDtypeStruct((n_slots * TOK_SUB, LANES), U32),
        grid_spec=pltpu.PrefetchScalarGridSpec(
            num_scalar_prefetch=2,
            grid=(t // tm,),
            in_specs=[pl.BlockSpec((tm * TOK_SUB, LANES), lambda i, dest, zf: (i, 0))],
            out_specs=pl.BlockSpec(memory_space=pl.ANY),
            scratch_shapes=[pltpu.VMEM((MOE_TB * TOK_SUB, LANES), U32), pltpu.SemaphoreType.DMA(()),
                            pltpu.SemaphoreType.DMA(())]),
        compiler_params=_cparams(("arbitrary",)),
        name="moe_dispatch",
    )(dest, zflag, h)


def _ffn_kernel(be_ref, nxt_ref, slot_ref, nu_ref, xs_ref, w1_hbm, w3_hbm, w2_hbm, o_ref,
                wb1, wb3, wb2, w1_s, w3_s, w2_s, sem, *, layer):
    i = pl.program_id(0)

    def fetch(e, s):
        return (pltpu.make_async_copy(w1_hbm.at[layer, e], wb1.at[s], sem.at[s, 0]),
                pltpu.make_async_copy(w3_hbm.at[layer, e], wb3.at[s], sem.at[s, 1]),
                pltpu.make_async_copy(w2_hbm.at[layer, e], wb2.at[s], sem.at[s, 2]))

    @pl.when(i < nu_ref[0])
    def _():
        e = be_ref[i]
        s = slot_ref[i]

        @pl.when((i == 0) | (e != be_ref[jnp.maximum(i - 1, 0)]))
        def _():
            @pl.when(i == 0)
            def _():
                for cp in fetch(e, s):
                    cp.start()

            for cp in fetch(e, s):
                cp.wait()

            @pl.when(nxt_ref[i] >= 0)
            def _():
                for cp in fetch(nxt_ref[i], 1 - s):
                    cp.start()

            w1_s[...] = wb1[s].astype(BF16)
            w3_s[...] = wb3[s].astype(BF16)
            w2_s[...] = wb2[s].astype(BF16)

        lo, hi = _unpack_halves(_load_token_tiles(xs_ref))
        xl, xh = lo.astype(BF16), hi.astype(BF16)
        half = xl.shape[1]
        a = _dot(xl, w1_s[0:half, :]) + _dot(xh, w1_s[half:2 * half, :])
        b = _dot(xl, w3_s[0:half, :]) + _dot(xh, w3_s[half:2 * half, :])
        hmid = _silu(a) * b
        _store_token_tiles(o_ref, _pack_halves(_dot(hmid.astype(BF16), w2_s[...]).astype(BF16)))

    @pl.when(i >= nu_ref[0])
    def _():
        o_ref[...] = jnp.zeros_like(o_ref)


def _expert_ffn(blk_expert, blk_next, blk_slot, n_used, xs, w1, w3, w2, layer):
    d, f = w1.shape[2], w1.shape[3]
    rows = MOE_TB * TOK_SUB
    nb = xs.shape[0] // rows
    hbm = pl.BlockSpec(memory_space=pl.ANY)
    return pl.pallas_call(
        functools.partial(_ffn_kernel, layer=layer),
        out_shape=jax.ShapeDtypeStruct(xs.shape, U32),
        grid_spec=pltpu.PrefetchScalarGridSpec(
            num_scalar_prefetch=4,
            grid=(nb,),
            in_specs=[pl.BlockSpec((rows, LANES),
                                   lambda i, be, nx, sl, nu: (jnp.minimum(i, nu[0] - 1), 0)),
                      hbm, hbm, hbm],
            out_specs=pl.BlockSpec((rows, LANES), lambda i, be, nx, sl, nu: (i, 0)),
            scratch_shapes=[pltpu.VMEM((2, d, f), F32), pltpu.VMEM((2, d, f), F32),
                            pltpu.VMEM((2, f, d), F32),
                            pltpu.VMEM((d, f), BF16), pltpu.VMEM((d, f), BF16),
                            pltpu.VMEM((f, d), BF16),
                            pltpu.SemaphoreType.DMA((2, 3))]),
        compiler_params=_cparams(("arbitrary",)),
        name="moe_expert_ffn",
    )(blk_expert, blk_next, blk_slot, n_used, xs, w1, w3, w2)


def _combine_kernel(dest_ref, x_ref, meta_ref, g_ref, fnw_ref, ys_ref, o_ref, buf, sem, *, final):
    tm = x_ref.shape[0]
    i = pl.program_id(0)
    nsteps = pl.num_programs(0)
    ntok = nsteps * tm

    def copy(tile, t, k):
        half = tile % 2
        return pltpu.make_async_copy(_token_tile(ys_ref, dest_ref[k * ntok + tile * tm + t]),
                                     _token_tile(buf.at[half, k], t), sem.at[half])

    def issue(tile):
        def body(t, carry):
            copy(tile, t, 0).start(priority=0)
            copy(tile, t, 1).start(priority=1)
            return carry
        lax.fori_loop(0, tm, body, 0, unroll=DMA_UNROLL)

    def drain(tile):
        def body(t, carry):
            copy(tile, t, 0).wait()
            copy(tile, t, 1).wait()
            return carry
        lax.fori_loop(0, tm, body, 0, unroll=DMA_UNROLL)

    @pl.when(i == 0)
    def _():
        issue(i)

    @pl.when(i + 1 < nsteps)
    def _():
        issue(i + 1)

    drain(i)
    cur = i % 2
    meta = meta_ref[...]
    w1, w2 = meta[:, 2:3], meta[:, 3:4]
    lo1, hi1 = _unpack_halves(_load_token_tiles(buf.at[cur, 0]))
    lo2, hi2 = _unpack_halves(_load_token_tiles(buf.at[cur, 1]))
    y = jnp.concatenate([w1 * lo1 + w2 * lo2, w1 * hi1 + w2 * hi2], axis=1)
    xn = x_ref[...] + g_ref[0] * y
    if final:
        ms = jnp.mean(xn * xn, axis=-1, keepdims=True)
        xn = xn * lax.rsqrt(ms + EPS) * fnw_ref[...]
    o_ref[...] = xn


def _combine(dest, x, meta, g, fnw, ys, *, rows_per_batch, final):
    t, d = x.shape
    tm = 256
    tiles_per_batch = rows_per_batch // tm
    return pl.pallas_call(
        functools.partial(_combine_kernel, final=final),
        out_shape=jax.ShapeDtypeStruct((t, d), F32),
        grid_spec=pltpu.PrefetchScalarGridSpec(
            num_scalar_prefetch=1,
            grid=(t // tm,),
            in_specs=[pl.BlockSpec((tm, d), lambda i, dest: (i, 0)),
                      pl.BlockSpec((tm, LANES), lambda i, dest: (i, 0)),
                      pl.BlockSpec((1, 1, d), lambda i, dest: (i // tiles_per_batch, 0, 0)),
                      pl.BlockSpec((1, d), lambda i, dest: (0, 0)),
                      pl.BlockSpec(memory_space=pl.ANY)],
            out_specs=pl.BlockSpec((tm, d), lambda i, dest: (i, 0)),
            scratch_shapes=[pltpu.VMEM((2, 2, tm * TOK_SUB, LANES), U32),
                            pltpu.SemaphoreType.DMA((2,))]),
        compiler_params=_cparams(("arbitrary",)),
        name="moe_combine",
    )(dest, x, meta, g, fnw, ys)


def _hier_moe(x, mod_sc, mod_sh, mod_g, nw, w_group, w_expert, w1, w3, w2, layer, fnw, *,
              rows_per_batch, final):
    t, d = x.shape
    a = 2 * t
    tb = MOE_TB
    wr = jnp.concatenate([w_group, w_expert], axis=1)
    wr = jnp.pad(wr, ((0, 0), (0, LANES - wr.shape[1])))
    h, meta, meta_t, cnt = _router(x, nw, mod_sc, mod_sh, wr, rows_per_batch=rows_per_batch)
    ne = N_EXPERTS
    counts = cnt[0, MOE_GROUPS:MOE_GROUPS + ne].astype(I32)
    padded = ((counts + tb - 1) // tb) * tb
    pend = jnp.cumsum(padded)
    pstart = pend - padded
    dest = _slots(pstart, meta_t).reshape(a)
    nb = (a + ne * (tb - 1) + tb - 1) // tb
    n_used = (pend[-1] // tb).astype(I32)
    blk = jnp.arange(nb, dtype=I32)
    be = jnp.minimum(jnp.sum((pend[None, :] <= (blk * tb)[:, None]).astype(I32), axis=1), ne - 1)
    seg_last = jnp.any((pend[None, :] == ((blk + 1) * tb)[:, None]) & (padded[None, :] > 0), axis=1)
    zflag = (seg_last | (blk >= n_used)).astype(I32)
    nonempty = counts > 0
    seg = jnp.cumsum(nonempty.astype(I32)) - 1
    later = lax.cummin(jnp.where(nonempty, jnp.arange(ne, dtype=I32), ne), axis=0, reverse=True)
    nxt_e = jnp.concatenate([later[1:], jnp.full((1,), ne, I32)])
    nxt_e = jnp.where(nxt_e == ne, -1, nxt_e)
    xs = _dispatch(dest, zflag, h, nb * tb)
    ys = _expert_ffn(be, nxt_e[be], seg[be] % 2, n_used.reshape(1), xs, w1, w3, w2, layer)
    return _combine(dest, x, meta, mod_g, fnw, ys, rows_per_batch=rows_per_batch, final=final)


def kernel(x, c, ada_w, ada_b, norm_mix, norm_ffn, ssd_w_in, ssd_conv_w, ssd_conv_b, ssd_a_log,
           ssd_dt_bias, ssd_d, ssd_norm_w, ssd_w_out, na_w_qkv, na_rpb, na_w_o,
           moe_w_group, moe_w_expert, moe_w1, moe_w3, moe_w2, final_norm):
    batch, seq, d = x.shape
    depth = ada_w.shape[0]
    xt = x.reshape(batch * seq, d)
    c_pad = jnp.pad(c, ((0, 8 - batch), (0, 0)))
    mod = _ada(c_pad, ada_w, ada_b)[:, :batch]
    fnw = final_norm.reshape(1, d)
    for i in range(depth):
        sh1, sc1, g1, sh2, sc2, g2 = [mod[i, :, k * d:(k + 1) * d].reshape(batch, 1, d)
                                      for k in range(6)]
        j = i // 2
        nw = norm_mix[i].reshape(1, d)
        if i % 2 == 0:
            xt = _ssd_mixer(xt, sc1, sh1, g1, nw, ssd_w_in[j], ssd_conv_w[j], ssd_conv_b[j],
                            ssd_a_log[j], ssd_dt_bias[j], ssd_d[j], ssd_norm_w[j], ssd_w_out[j],
                            batch=batch, seq=seq)
        else:
            xt = _na_mixer(xt, sc1, sh1, g1, nw, na_w_qkv[j], na_rpb[j], na_w_o[j],
                           batch=batch, seq=seq)
        xt = _hier_moe(xt, sc2, sh2, g2, norm_ffn[i].reshape(1, d), moe_w_group[i], moe_w_expert[i],
                       moe_w1, moe_w3, moe_w2, i, fnw, rows_per_batch=seq,
                       final=(i == depth - 1))
    return xt.reshape(batch, seq, d)
```

```python
import functools

import jax
import jax.numpy as jnp
from jax import lax
from jax.experimental import pallas as pl
from jax.experimental.pallas import tpu as pltpu

F32 = jnp.float32
BF16 = jnp.bfloat16
I32 = jnp.int32

EPS = 1e-6
NEG = -1e30
LOG2_E = 1.4426950408889634

D_MODEL = 2048
GRID_W = 64
SSD_HEAD_DIM = 64
SSD_GROUPS = 8
SSD_HEADS_PER_GROUP = 8
SSD_STATE = 128
SSD_CONV = 5
D_INNER = 2 * D_MODEL
GROUP_W = SSD_HEADS_PER_GROUP * SSD_HEAD_DIM
SSD_Q = 128
CONV_HALO = 16
NA_HEAD_DIM = 64
NA_HEADS = D_MODEL // NA_HEAD_DIM
NA_PAIRS = NA_HEADS // 2
WIN_H = 8
WIN_W = 16
NA_ROWS = 4
MOE_GROUPS = 4
MOE_EPG = 8
N_EXPERTS = MOE_GROUPS * MOE_EPG
MOE_D_FF = D_MODEL // 4
MOE_TB = 256
DMA_UNROLL = 8
ROW_GROUP = 8

VMEM_LIMIT = 56 * 1024 * 1024
LANES = 128


def _cparams(sem):
    return pltpu.CompilerParams(dimension_semantics=sem, vmem_limit_bytes=VMEM_LIMIT)


def _silu(v):
    return v * pl.reciprocal(1.0 + jnp.exp(-v), approx=True)


def _softplus(v):
    return jnp.maximum(v, 0.0) + jnp.log1p(jnp.exp(-jnp.abs(v)))


def _split3(v):
    hi = v.astype(BF16)
    r1 = v - hi.astype(F32)
    mid = r1.astype(BF16)
    lo = (r1 - mid.astype(F32)).astype(BF16)
    return hi, mid, lo


def _dot(a, b):
    return jnp.dot(a, b, preferred_element_type=F32)


def _dot_nt(a, b):
    return lax.dot_general(a, b, (((1,), (1,)), ((), ())), preferred_element_type=F32)


def _dot3_left(v, sel):
    hi, mid, lo = _split3(v)
    return _dot(hi, sel) + _dot(mid, sel) + _dot(lo, sel)


def _normmod(x, nw, sc, sh):
    ms = jnp.mean(x * x, axis=-1, keepdims=True)
    return (x * lax.rsqrt(ms + EPS) * nw) * (1.0 + sc) + sh


def _split2(v):
    hi = v.astype(BF16)
    return hi, (v - hi.astype(F32)).astype(BF16)


def _dot_split(a, b_hi, b_lo):
    a_hi, a_lo = _split2(a)
    return _dot(a_hi, b_hi) + (_dot(a_lo, b_hi) + _dot(a_hi, b_lo))


def _ada_kernel(c_ref, w_ref, b_ref, o_ref):
    c = c_ref[...]
    o_ref[0] = _dot_split(c / (1.0 + jnp.exp(-c)), *_split2(w_ref[0])) + b_ref[0]


def _ada(c_pad, ada_w, ada_b):
    depth, d, n = ada_w.shape
    tn = 2048
    return pl.pallas_call(
        _ada_kernel,
        out_shape=jax.ShapeDtypeStruct((depth, 8, n), F32),
        grid=(depth, n // tn),
        in_specs=[pl.BlockSpec((8, d), lambda i, j: (0, 0)),
                  pl.BlockSpec((1, d, tn), lambda i, j: (i, 0, j)),
                  pl.BlockSpec((1, 1, tn), lambda i, j: (i, 0, j))],
        out_specs=pl.BlockSpec((1, 8, tn), lambda i, j: (i, 0, j)),
        compiler_params=_cparams(("arbitrary", "arbitrary")),
        name="ada_mod",
    )(c_pad, ada_w, ada_b.reshape(depth, 1, n))


def _nm_mm_kernel(x_ref, nw_ref, sc_ref, sh_ref, w_ref, *rest, pair_major, tail, lead_tiles,
                  lead_scale):
    if tail:
        wt_ref, o_ref, ot_ref, h_ref = rest
    else:
        o_ref, h_ref = rest

    @pl.when(pl.program_id(1) == 0)
    def _():
        h_ref[...] = _normmod(x_ref[...], nw_ref[...], sc_ref[0], sh_ref[0]).astype(BF16)

    r = _dot(h_ref[...], w_ref[...].astype(BF16))
    if lead_tiles:
        r = r * jnp.where(pl.program_id(1) < lead_tiles, lead_scale, 1.0)
    if pair_major:
        for c in range(o_ref.shape[0]):
            o_ref[c] = r[:, c * LANES:(c + 1) * LANES].astype(o_ref.dtype)
    else:
        o_ref[...] = r.astype(o_ref.dtype)

    if tail:
        @pl.when(pl.program_id(1) == pl.num_programs(1) - 1)
        def _():
            ot_ref[...] = _dot(h_ref[...], wt_ref[...].astype(BF16))


def _nm_matmul(x, nw, sc, sh, w, *, ncols, tn, out_dtype, rows_per_batch, pair_major=False,
               tail_cols=0, lead_cols=0, lead_scale=1.0):
    t, d = x.shape
    tm = 1024
    tiles_per_batch = rows_per_batch // tm
    if pair_major:
        out_shape = jax.ShapeDtypeStruct((ncols // LANES, t, LANES), out_dtype)
        out_spec = pl.BlockSpec((tn // LANES, tm, LANES), lambda i, j: (j, i, 0))
    else:
        out_shape = jax.ShapeDtypeStruct((t, ncols), out_dtype)
        out_spec = pl.BlockSpec((tm, tn), lambda i, j: (i, j))
    in_specs = [pl.BlockSpec((tm, d), lambda i, j: (i, 0)),
                pl.BlockSpec((1, d), lambda i, j: (0, 0)),
                pl.BlockSpec((1, 1, d), lambda i, j: (i // tiles_per_batch, 0, 0)),
                pl.BlockSpec((1, 1, d), lambda i, j: (i // tiles_per_batch, 0, 0)),
                pl.BlockSpec((d, tn), lambda i, j: (0, j))]
    operands = [x, nw, sc, sh, w]
    if tail_cols:
        tail_blk = ncols // tail_cols
        in_specs.append(pl.BlockSpec((d, tail_cols), lambda i, j: (0, tail_blk)))
        operands.append(w)
        out_shape = (out_shape, jax.ShapeDtypeStruct((t, tail_cols), F32))
        out_spec = (out_spec, pl.BlockSpec((tm, tail_cols), lambda i, j: (i, 0)))
    return pl.pallas_call(
        functools.partial(_nm_mm_kernel, pair_major=pair_major, tail=bool(tail_cols),
                          lead_tiles=lead_cols // tn, lead_scale=lead_scale),
        out_shape=out_shape,
        grid=(t // tm, ncols // tn),
        in_specs=in_specs,
        out_specs=out_spec,
        scratch_shapes=[pltpu.VMEM((tm, d), BF16)],
        compiler_params=_cparams(("arbitrary", "arbitrary")),
        name="norm_mod_matmul",
    )(*operands)


def _mm_resid_kernel(a_ref, w_ref, x_ref, g_ref, o_ref, *, pair_major):
    if pair_major:
        a = jnp.concatenate([a_ref[c] for c in range(a_ref.shape[0])], axis=1)
    else:
        a = a_ref[...]
    o_ref[...] = x_ref[...] + g_ref[0] * _dot(a, w_ref[...].astype(BF16))


def _mm_resid(a, w, x, g, *, rows_per_batch, pair_major=False):
    t, n = x.shape
    k = w.shape[0]
    tn = 512
    tm = (4 * 1024 * 1024) // k
    tiles_per_batch = rows_per_batch // tm
    if pair_major:
        a_spec = pl.BlockSpec((k // LANES, tm, LANES), lambda i, j: (0, i, 0))
    else:
        a_spec = pl.BlockSpec((tm, k), lambda i, j: (i, 0))
    return pl.pallas_call(
        functools.partial(_mm_resid_kernel, pair_major=pair_major),
        out_shape=jax.ShapeDtypeStruct((t, n), F32),
        grid=(t // tm, n // tn),
        in_specs=[a_spec,
                  pl.BlockSpec((k, tn), lambda i, j: (0, j)),
                  pl.BlockSpec((tm, tn), lambda i, j: (i, j)),
                  pl.BlockSpec((1, 1, tn), lambda i, j: (i // tiles_per_batch, 0, j))],
        out_specs=pl.BlockSpec((tm, tn), lambda i, j: (i, j)),
        compiler_params=_cparams(("arbitrary", "arbitrary")),
        name="matmul_resid",
    )(a, w, x, g)


def _ssd_kernel(z_ref, x_ref, b_ref, c_ref, dtr_ref,
                cwx_ref, cwb_ref, cwc_ref, cbx_ref, cbb_ref, cbc_ref,
                alr_ref, dbr_ref, dsk_ref, nw_ref,
                o_ref,
                xc_s, bc_s, cc_s, yacc_s, st_s, cv_s):
    seq = x_ref.shape[0]
    q = SSD_Q
    nc = seq // q
    halo = CONV_HALO
    nrow = 2 * SSD_HEADS_PER_GROUP

    def conv_piece(j, base, src_ref, w_ref, bias_ref, dst_ref, lo, stage):
        cols = slice(lo, lo + LANES)
        pstart = pl.multiple_of(jnp.maximum(base - halo, 0), halo)
        nstart = pl.multiple_of(jnp.minimum(base + q, seq - halo), halo)
        stage[0:halo, :] = jnp.where(j > 0, src_ref[pl.ds(pstart, halo), cols].astype(F32), 0.0)
        stage[halo:halo + q, :] = src_ref[pl.ds(base, q), cols].astype(F32)
        stage[halo + q:, :] = jnp.where(j < nc - 1, src_ref[pl.ds(nstart, halo), cols].astype(F32), 0.0)
        acc = jnp.broadcast_to(bias_ref[:, cols], (q, LANES))
        for k in range(SSD_CONV):
            first = halo - SSD_CONV // 2 + k
            acc = acc + w_ref[k:k + 1, cols] * stage[first:first + q, :]
        dst_ref[pl.ds(base, q), cols] = _silu(acc).astype(BF16)

    def conv_chunk(j, carry):
        base = pl.multiple_of(j * q, q)
        npx = GROUP_W // LANES
        for i in range(npx):
            conv_piece(j, base, x_ref, cwx_ref, cbx_ref, xc_s, i * LANES, cv_s.at[i])
        conv_piece(j, base, b_ref, cwb_ref, cbb_ref, bc_s, 0, cv_s.at[npx])
        conv_piece(j, base, c_ref, cwc_ref, cbc_ref, cc_s, 0, cv_s.at[npx + 1])
        return carry

    lax.fori_loop(0, nc, conv_chunk, 0)

    row_i = lax.broadcasted_iota(I32, (q, q), 0)
    col_i = lax.broadcasted_iota(I32, (q, q), 1)
    lower = row_i >= col_i
    upper = row_i <= col_i
    lower_b = lower.astype(BF16)
    upper_b = upper.astype(BF16)
    left =lax.broadcasted_iota(I32, (q, LANES), 1) < SSD_HEAD_DIM
    a_row = -jnp.exp(alr_ref[...])
    pad_rows = jnp.zeros((LANES - nrow, q), F32)

    def scan_pass(direction):
        hoff = direction * SSD_HEADS_PER_GROUP
        mask = lower if direction == 0 else upper
        tri = upper_b if direction == 0 else lower_b
        edge = q - 1 if direction == 0 else 0
        st_s[...] = jnp.zeros_like(st_s)

        def chunk(t, carry):
            c = t if direction == 0 else nc - 1 - t
            base = pl.multiple_of(c * q, q)
            rows = pl.ds(base, q)
            dt_r = _softplus(dtr_ref[c] + dbr_ref[...])
            cum_r = _dot3_left(dt_r * a_row, tri) * LOG2_E
            cum_c = jnp.concatenate([cum_r, pad_rows], axis=0).T
            decdt_r = jnp.exp2(cum_r[:, edge:edge + 1] - cum_r) * dt_r
            src_r = cum_r - jnp.log2(dt_r)

            bm = bc_s[rows, :]
            cm = cc_s[rows, :]
            cb = _dot_nt(cm, bm)
            bm_t = bm.astype(F32).T
            y_off_all = _dot(cm, st_s[...].astype(BF16)) if direction == 1 else None

            ssq = jnp.zeros((q, 1), F32)
            for pp in range(SSD_HEADS_PER_GROUP // 2):
                cols = slice(pp * LANES, (pp + 1) * LANES)
                xcb = xc_s[rows, cols]
                zero_b = jnp.zeros_like(xcb)
                st_in = st_s[:, cols]
                lhs_y, lhs_s, scales = [], [], []
                for par in range(2):
                    j = hoff + 2 * pp + par
                    cum_b = jnp.broadcast_to(cum_c[:, j:j + 1], (q, q))
                    lmat = jnp.exp2(jnp.where(mask, cum_b - src_r[j:j + 1, :], NEG))
                    lhs_y.append((cb * lmat).astype(BF16))
                    lhs_s.append((bm_t * decdt_r[j:j + 1, :]).astype(BF16))
                    scales.append(jnp.exp2(cum_b))
                x_rhs = jnp.concatenate([jnp.where(left, xcb, zero_b), jnp.where(left, zero_b, xcb)],
                                        axis=0)
                sc_tile = jnp.where(left, scales[0], scales[1])
                y_off = _dot(cm, st_in.astype(BF16)) if y_off_all is None else y_off_all[:, cols]
                y = _dot(jnp.concatenate(lhs_y, axis=1), x_rhs) + y_off * sc_tile
                st_s[:, cols] = (st_in * sc_tile[edge:edge + 1, :]
                                 + _dot(jnp.concatenate(lhs_s, axis=1), x_rhs))
                if direction == 0:
                    yacc_s[rows, cols] = y
                else:
                    total = yacc_s[rows, cols] + y + xcb.astype(F32) * dsk_ref[:, cols]
                    gated = total * _silu(z_ref[rows, cols].astype(F32))
                    ssq = ssq + jnp.sum(gated * gated, axis=-1, keepdims=True)
                    yacc_s[rows, cols] = gated
            if direction == 1:
                inv = lax.rsqrt(ssq * (1.0 / GROUP_W) + EPS)
                o_ref[rows, :] = (yacc_s[rows, :] * inv * nw_ref[...]).astype(BF16)
            return carry

        lax.fori_loop(0, nc, chunk, 0, unroll=8)

    scan_pass(0)
    scan_pass(1)


def _ssd_core(zx, dt_row, conv_w, conv_b, al_row, db_row, dskip, norm_w, *, batch, seq):
    g = SSD_GROUPS
    nc = seq // SSD_Q
    xb = D_INNER // GROUP_W
    bb = (2 * D_INNER) // SSD_STATE
    cb = bb + g
    cwb = D_INNER // SSD_STATE
    cwc = cwb + g
    return pl.pallas_call(
        _ssd_kernel,
        out_shape=jax.ShapeDtypeStruct((batch * seq, D_INNER), BF16),
        grid=(batch, g),
        in_specs=[
            pl.BlockSpec((seq, GROUP_W), lambda b, i: (b, i)),
            pl.BlockSpec((seq, GROUP_W), lambda b, i: (b, xb + i)),
            pl.BlockSpec((seq, SSD_STATE), lambda b, i: (b, bb + i)),
            pl.BlockSpec((seq, SSD_STATE), lambda b, i: (b, cb + i)),
            pl.BlockSpec((None, None, nc, 2 * SSD_HEADS_PER_GROUP, SSD_Q), lambda b, i: (b, i, 0, 0, 0)),
            pl.BlockSpec((SSD_CONV, GROUP_W), lambda b, i: (0, i)),
            pl.BlockSpec((SSD_CONV, SSD_STATE), lambda b, i: (0, cwb + i)),
            pl.BlockSpec((SSD_CONV, SSD_STATE), lambda b, i: (0, cwc + i)),
            pl.BlockSpec((1, GROUP_W), lambda b, i: (0, i)),
            pl.BlockSpec((1, SSD_STATE), lambda b, i: (0, cwb + i)),
            pl.BlockSpec((1, SSD_STATE), lambda b, i: (0, cwc + i)),
            pl.BlockSpec((None, 2 * SSD_HEADS_PER_GROUP, 1), lambda b, i: (i, 0, 0)),
            pl.BlockSpec((None, 2 * SSD_HEADS_PER_GROUP, 1), lambda b, i: (i, 0, 0)),
            pl.BlockSpec((1, GROUP_W), lambda b, i: (0, i)),
            pl.BlockSpec((1, GROUP_W), lambda b, i: (0, i)),
        ],
        out_specs=pl.BlockSpec((seq, GROUP_W), lambda b, i: (b, i)),
        scratch_shapes=[pltpu.VMEM((seq, GROUP_W), BF16),
                        pltpu.VMEM((seq, SSD_STATE), BF16),
                        pltpu.VMEM((seq, SSD_STATE), BF16),
                        pltpu.VMEM((seq, GROUP_W), F32),
                        pltpu.VMEM((SSD_STATE, GROUP_W), F32),
                        pltpu.VMEM((GROUP_W // LANES + 2, SSD_Q + 2 * CONV_HALO, LANES), F32)],
        compiler_params=_cparams(("arbitrary", "arbitrary")),
        name="ssd_core",
    )(zx, zx, zx, zx, dt_row, conv_w, conv_w, conv_w, conv_b, conv_b, conv_b,
      al_row, db_row, dskip, norm_w)


def _ssd_mixer(x, mod_sc, mod_sh, mod_g, nw, w_in, conv_w, conv_b, a_log, dt_bias, d_skip, norm_w,
               w_out, *, batch, seq):
    g, r = SSD_GROUPS, SSD_HEADS_PER_GROUP
    conv_dim = conv_w.shape[1]
    zx, dt_raw = _nm_matmul(x, nw, mod_sc, mod_sh, w_in, ncols=D_INNER + conv_dim, tn=1024,
                            out_dtype=BF16, rows_per_batch=seq, tail_cols=2 * g * r)
    nc = seq // SSD_Q
    dt_row = dt_raw.reshape(batch, nc, SSD_Q, 2, g, r).transpose(0, 4, 1, 3, 5, 2)
    dt_row = dt_row.reshape(batch, g, nc, 2 * r, SSD_Q)

    def row_form(p):
        return p.reshape(2, g, r).transpose(1, 0, 2).reshape(g, 2 * r, 1)

    yn = _ssd_core(zx, dt_row, conv_w, conv_b.reshape(1, conv_dim), row_form(a_log), row_form(dt_bias),
                   jnp.repeat(d_skip, SSD_HEAD_DIM).reshape(1, D_INNER), norm_w.reshape(1, D_INNER),
                   batch=batch, seq=seq)
    return _mm_resid(yn, w_out, x, mod_g, rows_per_batch=seq)


def _bias_table_kernel(rpb_ref, o_ref):
    lane = lax.broadcasted_iota(I32, (GRID_W, LANES), 1)
    j = lax.broadcasted_iota(I32, (GRID_W, LANES), 0)
    c = lane & (GRID_W - 1)
    c0 = jnp.clip(j - WIN_W // 2, 0, GRID_W - WIN_W)
    win = (c >= c0) & (c < c0 + WIN_W)
    left = lane < GRID_W

    def one_offset(dy0, carry):
        for par in range(2):
            for m in range(WIN_H // 2):
                tiles = []
                for sub in range(2):
                    row = rpb_ref[par, pl.ds(dy0 + 2 * m + sub, 1), :]
                    shift = (sub * GRID_W - (WIN_W - 1)) % LANES
                    tiles.append(pltpu.roll(jnp.broadcast_to(row, (GRID_W, LANES)), shift, 1,
                                            stride=1, stride_axis=0))
                tile = jnp.where(win, jnp.where(left, tiles[0], tiles[1]) * LOG2_E, NEG)
                o_ref[dy0, 0, par * GRID_W:(par + 1) * GRID_W, m * LANES:(m + 1) * LANES] = tile
        return carry

    lax.fori_loop(0, o_ref.shape[0], one_offset, 0)


def _na_bias_table(rpb):
    h, ndy, ndx = rpb.shape
    rpb_p = jnp.pad(rpb, ((0, 0), (0, 2 * WIN_H - ndy), (0, LANES - ndx)))
    return pl.pallas_call(
        _bias_table_kernel,
        out_shape=jax.ShapeDtypeStruct((WIN_H, h // 2, 2 * GRID_W, WIN_H * GRID_W), F32),
        grid=(h // 2,),
        in_specs=[pl.BlockSpec((2, 2 * WIN_H, LANES), lambda p: (p, 0, 0))],
        out_specs=pl.BlockSpec((WIN_H, 1, 2 * GRID_W, WIN_H * GRID_W), lambda p: (0, p, 0, 0)),
        compiler_params=_cparams(("arbitrary",)),
        name="na_bias_table",
    )(rpb_p)


def _na_kernel(q_ref, kv_hbm, bias_ref, o_ref, kc_s, vc_s, s_s, p_s, r_s, sem, *, n_row_blocks):
    rb = pl.program_id(2)
    npairs = q_ref.shape[0]
    blk = NA_ROWS * GRID_W
    nkeys = WIN_H * GRID_W
    n_sec, n_batch = pl.num_programs(0), pl.num_programs(1)
    step = (pl.program_id(0) * n_batch + pl.program_id(1)) * n_row_blocks + rb
    nsteps = n_sec * n_batch * n_row_blocks

    def fetch(st, half):
        r = st % n_row_blocks
        hb = st // n_row_blocks
        tok0 = ((hb % n_batch) * n_row_blocks + jnp.clip(r - 1, 0, n_row_blocks - 3)) * blk
        tok0 = pl.multiple_of(tok0, blk)

        def one(sec, dst, j):
            first_pair = (sec * n_sec + hb // n_batch) * npairs
            return pltpu.make_async_copy(
                kv_hbm.at[pl.ds(first_pair, npairs), pl.ds(tok0, 3 * blk), :], dst.at[half],
                sem.at[half, j])
        return one(1, kc_s, 0), one(2, vc_s, 1)

    cur = step % 2

    @pl.when(step == 0)
    def _():
        for cp in fetch(step, cur):
            cp.start()

    @pl.when(step + 1 < nsteps)
    def _():
        for cp in fetch(step + 1, 1 - cur):
            cp.start()

    for cp in fetch(step, cur):
        cp.wait()
    first = rb == 0
    last = rb == n_row_blocks - 1
    edge = first | last
    lane = lax.broadcasted_iota(I32, (GRID_W, LANES), 1)
    left = lane < NA_HEAD_DIM

    def window(qi):
        off = jnp.where(first, 0, jnp.where(last, blk, qi * GRID_W))
        li = jnp.where(edge, NA_ROWS - 1 - qi, NA_ROWS - 1)
        return pl.multiple_of(off, GRID_W), li

    def pair_body(pp, carry):
        for qi in range(NA_ROWS):
            off, li = window(qi)
            q2 = q_ref[pp, qi * GRID_W:(qi + 1) * GRID_W, :]
            zero = jnp.zeros_like(q2)
            qs = jnp.concatenate([jnp.where(left, q2, zero), jnp.where(left, zero, q2)], axis=0)
            kw = kc_s[cur, pp, pl.ds(off, nkeys), :]
            s_s[qi] = _dot_nt(qs, kw) + bias_ref[li, pp]
        for qi in range(NA_ROWS):
            s = s_s[qi]
            p = jnp.exp2(s - jnp.max(s, axis=-1, keepdims=True))
            r_s[qi] = 1.0 / jnp.sum(p, axis=-1, keepdims=True)
            p_s[qi] = p.astype(BF16)
        for qi in range(NA_ROWS):
            off, _ = window(qi)
            pv = _dot(p_s[qi], vc_s[cur, pp, pl.ds(off, nkeys), :]) * r_s[qi]
            o = jnp.where(left, pv[0:GRID_W], pv[GRID_W:2 * GRID_W])
            o_ref[pp, qi * GRID_W:(qi + 1) * GRID_W, :] = o.astype(BF16)
        return carry

    lax.fori_loop(0, npairs, pair_body, 0, unroll=4)


def _na_attention(qkv_t, bias_tab, *, batch, seq):
    t = batch * seq
    blk = NA_ROWS * GRID_W
    nrb = seq // blk
    hp = NA_PAIRS // 2
    nsec = NA_PAIRS // hp

    return pl.pallas_call(
        functools.partial(_na_kernel, n_row_blocks=nrb),
        out_shape=jax.ShapeDtypeStruct((NA_PAIRS, t, LANES), BF16),
        grid=(nsec, batch, nrb),
        in_specs=[pl.BlockSpec((hp, blk, LANES), lambda hh, b, r: (hh, b * nrb + r, 0)),
                  pl.BlockSpec(memory_space=pl.ANY),
                  pl.BlockSpec((NA_ROWS, hp, 2 * GRID_W, WIN_H * GRID_W),
                               lambda hh, b, r: (jnp.where(r == 0, 1, 0), hh, 0, 0))],
        out_specs=pl.BlockSpec((hp, blk, LANES), lambda hh, b, r: (hh, b * nrb + r, 0)),
        scratch_shapes=[pltpu.VMEM((2, hp, 3 * blk, LANES), BF16),
                        pltpu.VMEM((2, hp, 3 * blk, LANES), BF16),
                        pltpu.VMEM((NA_ROWS, 2 * GRID_W, WIN_H * GRID_W), F32),
                        pltpu.VMEM((NA_ROWS, 2 * GRID_W, WIN_H * GRID_W), BF16),
                        pltpu.VMEM((NA_ROWS, 2 * GRID_W, 1), F32),
                        pltpu.SemaphoreType.DMA((2, 2))],
        compiler_params=_cparams(("arbitrary", "arbitrary", "arbitrary")),
        name="na_attention",
    )(qkv_t, qkv_t, bias_tab)


def _na_mixer(x, mod_sc, mod_sh, mod_g, nw, w_qkv, rpb, w_o, *, batch, seq):
    qkv_t = _nm_matmul(x, nw, mod_sc, mod_sh, w_qkv, ncols=3 * D_MODEL, tn=1024,
                       out_dtype=BF16, rows_per_batch=seq, pair_major=True,
                       lead_cols=D_MODEL, lead_scale=NA_HEAD_DIM ** -0.5 * LOG2_E)
    o_t = _na_attention(qkv_t, _na_bias_table(rpb), batch=batch, seq=seq)
    return _mm_resid(o_t, w_o, x, mod_g, rows_per_batch=seq, pair_major=True)


U32 = jnp.uint32


def _pack_halves(vb):
    n = vb.shape[1] // 2
    bits = pltpu.bitcast(vb.astype(F32), U32)
    return (bits[:, :n] >> 16) | bits[:, n:]


def _unpack_halves(w):
    return pltpu.bitcast(w << 16, F32), pltpu.bitcast(w & U32(0xFFFF0000), F32)


TOK_SUB = (D_MODEL // 2) // LANES


def _store_token_tiles(ref, packed):
    rows = packed.shape[0]
    for s in range(TOK_SUB):
        ref[pl.ds(s, rows, stride=TOK_SUB), :] = packed[:, s * LANES:(s + 1) * LANES]


def _load_token_tiles(ref):
    rows = ref.shape[0] // TOK_SUB
    return jnp.concatenate([ref[pl.ds(s, rows, stride=TOK_SUB), :] for s in range(TOK_SUB)], axis=1)


def _router_kernel(x_ref, nw_ref, sc_ref, sh_ref, wr_ref, h_ref, meta_ref, meta_t_ref, cnt_ref,
                   carry_s, whi_s, wlo_s):
    @pl.when(pl.program_id(0) == 0)
    def _():
        carry_s[...] = jnp.zeros_like(carry_s)
        whi_s[...], wlo_s[...] = _split2(wr_ref[...])

    h = _normmod(x_ref[...], nw_ref[...], sc_ref[0], sh_ref[0])
    _store_token_tiles(h_ref, _pack_halves(h.astype(BF16)))
    logits = _dot_split(h, whi_s[...], wlo_s[...])
    tm = logits.shape[0]
    lane_i = lax.broadcasted_iota(I32, logits.shape, 1)
    lane = lane_i.astype(F32)
    big = 1e9
    gl = jnp.where(lane_i < MOE_GROUPS, logits, NEG)
    gmax = jnp.max(gl, axis=1, keepdims=True)
    gsel = jnp.min(jnp.where(gl == gmax, lane, big), axis=1, keepdims=True)
    gw = 1.0 / jnp.sum(jnp.exp(gl - gmax), axis=1, keepdims=True)
    el = lane - MOE_GROUPS
    lo = gsel * MOE_EPG
    emask = (el >= lo) & (el < lo + MOE_EPG)
    e1 = jnp.where(emask, logits, NEG)
    m1 = jnp.max(e1, axis=1, keepdims=True)
    i1 = jnp.min(jnp.where(e1 == m1, el, big), axis=1, keepdims=True)
    e2 = jnp.where(emask & (el != i1), logits, NEG)
    m2 = jnp.max(e2, axis=1, keepdims=True)
    i2 = jnp.min(jnp.where(e2 == m2, el, big), axis=1, keepdims=True)
    tt = jnp.exp(m2 - m1)
    p1 = 1.0 / (1.0 + tt)
    w1 = gw * p1
    w2 = gw * (tt * p1)
    oh1 = el == i1
    oh2 = el == i2
    cnt = (oh1 | oh2).astype(F32)
    r_i = lax.broadcasted_iota(I32, (tm, tm), 0)
    c_i = lax.broadcasted_iota(I32, (tm, tm), 1)
    before = _dot((r_i > c_i).astype(BF16), cnt.astype(BF16)) + carry_s[...]
    rank1 = jnp.sum(jnp.where(oh1, before, 0.0), axis=1, keepdims=True)
    rank2 = jnp.sum(jnp.where(oh2, before, 0.0), axis=1, keepdims=True)
    carry_s[...] = carry_s[...] + jnp.sum(cnt, axis=0, keepdims=True)
    meta = jnp.zeros_like(logits)
    for pos, val in enumerate((i1, i2, w1, w2, rank1, rank2)):
        meta = jnp.where(lane_i == pos, val, meta)
    meta_ref[...] = meta
    meta_t_ref[...] = meta.T[0:meta_t_ref.shape[0], :]
    cnt_ref[...] = jnp.broadcast_to(carry_s[...], cnt_ref.shape)


def _router(x, nw, sc, sh, wr, *, rows_per_batch):
    t, d = x.shape
    tm = 256
    tiles_per_batch = rows_per_batch // tm
    return pl.pallas_call(
        _router_kernel,
        out_shape=(jax.ShapeDtypeStruct((t * TOK_SUB, LANES), U32),
                   jax.ShapeDtypeStruct((t, LANES), F32),
                   jax.ShapeDtypeStruct((8, t), F32),
                   jax.ShapeDtypeStruct((8, LANES), F32)),
        grid=(t // tm,),
        in_specs=[pl.BlockSpec((tm, d), lambda i: (i, 0)),
                  pl.BlockSpec((1, d), lambda i: (0, 0)),
                  pl.BlockSpec((1, 1, d), lambda i: (i // tiles_per_batch, 0, 0)),
                  pl.BlockSpec((1, 1, d), lambda i: (i // tiles_per_batch, 0, 0)),
                  pl.BlockSpec((d, LANES), lambda i: (0, 0))],
        out_specs=(pl.BlockSpec((tm * TOK_SUB, LANES), lambda i: (i, 0)),
                   pl.BlockSpec((tm, LANES), lambda i: (i, 0)),
                   pl.BlockSpec((8, tm), lambda i: (0, i)),
                   pl.BlockSpec((8, LANES), lambda i: (0, 0))),
        scratch_shapes=[pltpu.VMEM((1, LANES), F32), pltpu.VMEM((d, LANES), BF16),
                        pltpu.VMEM((d, LANES), BF16)],
        compiler_params=_cparams(("arbitrary",)),
        name="moe_router",
    )(x, nw, sc, sh, wr)


def _slot_kernel(pstart_ref, mt_ref, o_ref):
    eid = mt_ref[0:2, :]
    start = jnp.zeros(eid.shape, I32)
    for e in range(N_EXPERTS):
        start = jnp.where(eid == float(e), pstart_ref[e], start)
    o_ref[...] = start + mt_ref[4:6, :].astype(I32)


def _slots(pstart, meta_t):
    t = meta_t.shape[1]
    return pl.pallas_call(
        _slot_kernel,
        out_shape=jax.ShapeDtypeStruct((2, t), I32),
        grid_spec=pltpu.PrefetchScalarGridSpec(
            num_scalar_prefetch=1,
            grid=(1,),
            in_specs=[pl.BlockSpec(meta_t.shape, lambda i, ps: (0, 0))],
            out_specs=pl.BlockSpec((2, t), lambda i, ps: (0, 0))),
        compiler_params=_cparams(("arbitrary",)),
        name="moe_slots",
    )(pstart, meta_t)


def _token_tile(ref, r):
    return ref.at[pl.ds(pl.multiple_of(r * TOK_SUB, TOK_SUB), TOK_SUB)]


def _dispatch_kernel(dest_ref, zflag_ref, h_ref, xs_ref, zbuf, sem, zsem):
    tm = h_ref.shape[0] // TOK_SUB
    base = pl.program_id(0) * tm
    ntok = pl.num_programs(0) * tm
    tb = zbuf.shape[0]

    @pl.when(pl.program_id(0) == 0)
    def _():
        zbuf[...] = jnp.zeros_like(zbuf)

        def zcopy(b):
            return pltpu.make_async_copy(zbuf, xs_ref.at[pl.ds(pl.multiple_of(b * tb, tb), tb)], zsem)

        def zstart(b, carry):
            @pl.when(zflag_ref[b] == 1)
            def _():
                zcopy(b).start()
            return carry

        def zwait(b, carry):
            @pl.when(zflag_ref[b] == 1)
            def _():
                zcopy(b).wait()
            return carry

        nblk = xs_ref.shape[0] // tb
        lax.fori_loop(0, nblk, zstart, 0)
        lax.fori_loop(0, nblk, zwait, 0)

    def copy(t, k):
        return pltpu.make_async_copy(_token_tile(h_ref, t),
                                     _token_tile(xs_ref, dest_ref[k * ntok + base + t]), sem)

    def issue(t, carry):
        copy(t, 0).start(priority=0)
        copy(t, 1).start(priority=1)
        return carry

    def drain(t, carry):
        copy(t, 0).wait()
        copy(t, 1).wait()
        return carry

    lax.fori_loop(0, tm, issue, 0, unroll=DMA_UNROLL)
    lax.fori_loop(0, tm, drain, 0, unroll=DMA_UNROLL)


def _dispatch(dest, zflag, h, n_slots):
    t = h.shape[0] // TOK_SUB
    tm = 256
    return pl.pallas_call(
        _dispatch_kernel,
        out_shape=jax.ShapeDtypeStruct((n_slots * TOK_SUB, LANES), U32),
        grid_spec=pltpu.PrefetchScalarGridSpec(
            num_scalar_prefetch=2,
            grid=(t // tm,),
            in_specs=[pl.BlockSpec((tm * TOK_SUB, LANES), lambda i, dest, zf: (i, 0))],
            out_specs=pl.BlockSpec(memory_space=pl.ANY),
            scratch_shapes=[pltpu.VMEM((MOE_TB * TOK_SUB, LANES), U32), pltpu.SemaphoreType.DMA(()),
                            pltpu.SemaphoreType.DMA(())]),
        compiler_params=_cparams(("arbitrary",)),
        name="moe_dispatch",
    )(dest, zflag, h)


def _ffn_kernel(be_ref, nxt_ref, slot_ref, nu_ref, xs_ref, w1_hbm, w3_hbm, w2_hbm, o_ref,
                wb1, wb3, wb2, w1_s, w3_s, w2_s, sem, *, layer):
    i = pl.program_id(0)

    def fetch(e, s):
        return (pltpu.make_async_copy(w1_hbm.at[layer, e], wb1.at[s], sem.at[s, 0]),
                pltpu.make_async_copy(w3_hbm.at[layer, e], wb3.at[s], sem.at[s, 1]),
                pltpu.make_async_copy(w2_hbm.at[layer, e], wb2.at[s], sem.at[s, 2]))

    @pl.when(i < nu_ref[0])
    def _():
        e = be_ref[i]
        s = slot_ref[i]

        @pl.when((i == 0) | (e != be_ref[jnp.maximum(i - 1, 0)]))
        def _():
            @pl.when(i == 0)
            def _():
                for cp in fetch(e, s):
                    cp.start()

            for cp in fetch(e, s):
                cp.wait()

            @pl.when(nxt_ref[i] >= 0)
            def _():
                for cp in fetch(nxt_ref[i], 1 - s):
                    cp.start()

            w1_s[...] = wb1[s].astype(BF16)
            w3_s[...] = wb3[s].astype(BF16)
            w2_s[...] = wb2[s].astype(BF16)

        lo, hi = _unpack_halves(_load_token_tiles(xs_ref))
        xl, xh = lo.astype(BF16), hi.astype(BF16)
        half = xl.shape[1]
        a = _dot(xl, w1_s[0:half, :]) + _dot(xh, w1_s[half:2 * half, :])
        b = _dot(xl, w3_s[0:half, :]) + _dot(xh, w3_s[half:2 * half, :])
        hmid = _silu(a) * b
        _store_token_tiles(o_ref, _pack_halves(_dot(hmid.astype(BF16), w2_s[...]).astype(BF16)))

    @pl.when(i >= nu_ref[0])
    def _():
        o_ref[...] = jnp.zeros_like(o_ref)


def _expert_ffn(blk_expert, blk_next, blk_slot, n_used, xs, w1, w3, w2, layer):
    d, f = w1.shape[2], w1.shape[3]
    rows = MOE_TB * TOK_SUB
    nb = xs.shape[0] // rows
    hbm = pl.BlockSpec(memory_space=pl.ANY)
    return pl.pallas_call(
        functools.partial(_ffn_kernel, layer=layer),
        out_shape=jax.ShapeDtypeStruct(xs.shape, U32),
        grid_spec=pltpu.PrefetchScalarGridSpec(
            num_scalar_prefetch=4,
            grid=(nb,),
            in_specs=[pl.BlockSpec((rows, LANES),
                                   lambda i, be, nx, sl, nu: (jnp.minimum(i, nu[0] - 1), 0)),
                      hbm, hbm, hbm],
            out_specs=pl.BlockSpec((rows, LANES), lambda i, be, nx, sl, nu: (i, 0)),
            scratch_shapes=[pltpu.VMEM((2, d, f), F32), pltpu.VMEM((2, d, f), F32),
                            pltpu.VMEM((2, f, d), F32),
                            pltpu.VMEM((d, f), BF16), pltpu.VMEM((d, f), BF16),
                            pltpu.VMEM((f, d), BF16),
                            pltpu.SemaphoreType.DMA((2, 3))]),
        compiler_params=_cparams(("arbitrary",)),
        name="moe_expert_ffn",
    )(blk_expert, blk_next, blk_slot, n_used, xs, w1, w3, w2)


def _combine_kernel(dest_ref, x_ref, meta_ref, g_ref, fnw_ref, ys_ref, o_ref, buf, sem, *, final):
    tm = x_ref.shape[0]
    i = pl.program_id(0)
    nsteps = pl.num_programs(0)
    ntok = nsteps * tm

    def copy(tile, t, k):
        half = tile % 2
        return pltpu.make_async_copy(_token_tile(ys_ref, dest_ref[k * ntok + tile * tm + t]),
                                     _token_tile(buf.at[half, k], t), sem.at[half])

    def issue_rows(tile, t0, n):
        for t in range(n):
            copy(tile, t0 + t, 0).start(priority=0)
            copy(tile, t0 + t, 1).start(priority=1)

    def drain(tile):
        def body(t, carry):
            copy(tile, t, 0).wait()
            copy(tile, t, 1).wait()
            return carry
        lax.fori_loop(0, tm, body, 0, unroll=DMA_UNROLL)

    @pl.when(i == 0)
    def _():
        def first(gi, carry):
            issue_rows(i, gi * ROW_GROUP, ROW_GROUP)
            return carry
        lax.fori_loop(0, tm // ROW_GROUP, first, 0)

    drain(i)
    cur = i % 2
    nxt = (i + 1) % nsteps

    def group(gi, carry):
        r0 = pl.multiple_of(gi * ROW_GROUP, ROW_GROUP)
        issue_rows(nxt, r0, ROW_GROUP)
        rows = pl.ds(r0, ROW_GROUP)
        meta = meta_ref[rows, :]
        w1, w2 = meta[:, 2:3], meta[:, 3:4]
        tiles = pl.ds(pl.multiple_of(r0 * TOK_SUB, ROW_GROUP * TOK_SUB), ROW_GROUP * TOK_SUB)
        lo1, hi1 = _unpack_halves(_load_token_tiles(buf.at[cur, 0, tiles]))
        lo2, hi2 = _unpack_halves(_load_token_tiles(buf.at[cur, 1, tiles]))
        y = jnp.concatenate([w1 * lo1 + w2 * lo2, w1 * hi1 + w2 * hi2], axis=1)
        xn = x_ref[rows, :] + g_ref[0] * y
        if final:
            ms = jnp.mean(xn * xn, axis=-1, keepdims=True)
            xn = xn * lax.rsqrt(ms + EPS) * fnw_ref[...]
        o_ref[rows, :] = xn
        return carry

    lax.fori_loop(0, tm // ROW_GROUP, group, 0, unroll=2)

    @pl.when(i == nsteps - 1)
    def _():
        drain(nxt)


def _combine(dest, x, meta, g, fnw, ys, *, rows_per_batch, final):
    t, d = x.shape
    tm = 256
    tiles_per_batch = rows_per_batch // tm
    return pl.pallas_call(
        functools.partial(_combine_kernel, final=final),
        out_shape=jax.ShapeDtypeStruct((t, d), F32),
        grid_spec=pltpu.PrefetchScalarGridSpec(
            num_scalar_prefetch=1,
            grid=(t // tm,),
            in_specs=[pl.BlockSpec((tm, d), lambda i, dest: (i, 0)),
                      pl.BlockSpec((tm, LANES), lambda i, dest: (i, 0)),
                      pl.BlockSpec((1, 1, d), lambda i, dest: (i // tiles_per_batch, 0, 0)),
                      pl.BlockSpec((1, d), lambda i, dest: (0, 0)),
                      pl.BlockSpec(memory_space=pl.ANY)],
            out_specs=pl.BlockSpec((tm, d), lambda i, dest: (i, 0)),
            scratch_shapes=[pltpu.VMEM((2, 2, tm * TOK_SUB, LANES), U32),
                            pltpu.SemaphoreType.DMA((2,))]),
        compiler_params=_cparams(("arbitrary",)),
        name="moe_combine",
    )(dest, x, meta, g, fnw, ys)


def _hier_moe(x, mod_sc, mod_sh, mod_g, nw, w_group, w_expert, w1, w3, w2, layer, fnw, *,
              rows_per_batch, final):
    t, d = x.shape
    a = 2 * t
    tb = MOE_TB
    wr = jnp.concatenate([w_group, w_expert], axis=1)
    wr = jnp.pad(wr, ((0, 0), (0, LANES - wr.shape[1])))
    h, meta, meta_t, cnt = _router(x, nw, mod_sc, mod_sh, wr, rows_per_batch=rows_per_batch)
    ne = N_EXPERTS
    counts = cnt[0, MOE_GROUPS:MOE_GROUPS + ne].astype(I32)
    padded = ((counts + tb - 1) // tb) * tb
    pend = jnp.cumsum(padded)
    pstart = pend - padded
    dest = _slots(pstart, meta_t).reshape(a)
    nb = (a + ne * (tb - 1) + tb - 1) // tb
    n_used = (pend[-1] // tb).astype(I32)
    blk = jnp.arange(nb, dtype=I32)
    be = jnp.minimum(jnp.sum((pend[None, :] <= (blk * tb)[:, None]).astype(I32), axis=1), ne - 1)
    seg_last = jnp.any((pend[None, :] == ((blk + 1) * tb)[:, None]) & (padded[None, :] > 0), axis=1)
    zflag = (seg_last | (blk >= n_used)).astype(I32)
    nonempty = counts > 0
    seg = jnp.cumsum(nonempty.astype(I32)) - 1
    later = lax.cummin(jnp.where(nonempty, jnp.arange(ne, dtype=I32), ne), axis=0, reverse=True)
    nxt_e = jnp.concatenate([later[1:], jnp.full((1,), ne, I32)])
    nxt_e = jnp.where(nxt_e == ne, -1, nxt_e)
    xs = _dispatch(dest, zflag, h, nb * tb)
    ys = _expert_ffn(be, nxt_e[be], seg[be] % 2, n_used.reshape(1), xs, w1, w3, w2, layer)
    return _combine(dest, x, meta, mod_g, fnw, ys, rows_per_batch=rows_per_batch, final=final)


def kernel(x, c, ada_w, ada_b, norm_mix, norm_ffn, ssd_w_in, ssd_conv_w, ssd_conv_b, ssd_a_log,
           ssd_dt_bias, ssd_d, ssd_norm_w, ssd_w_out, na_w_qkv, na_rpb, na_w_o,
           moe_w_group, moe_w_expert, moe_w1, moe_w3, moe_w2, final_norm):
    batch, seq, d = x.shape
    depth = ada_w.shape[0]
    xt = x.reshape(batch * seq, d)
    c_pad = jnp.pad(c, ((0, 8 - batch), (0, 0)))
    mod = _ada(c_pad, ada_w, ada_b)[:, :batch]
    fnw = final_norm.reshape(1, d)
    for i in range(depth):
        sh1, sc1, g1, sh2, sc2, g2 = [mod[i, :, k * d:(k + 1) * d].reshape(batch, 1, d)
                                      for k in range(6)]
        j = i // 2
        nw = norm_mix[i].reshape(1, d)
        if i % 2 == 0:
            xt = _ssd_mixer(xt, sc1, sh1, g1, nw, ssd_w_in[j], ssd_conv_w[j], ssd_conv_b[j],
                            ssd_a_log[j], ssd_dt_bias[j], ssd_d[j], ssd_norm_w[j], ssd_w_out[j],
                            batch=batch, seq=seq)
        else:
            xt = _na_mixer(xt, sc1, sh1, g1, nw, na_w_qkv[j], na_rpb[j], na_w_o[j],
                           batch=batch, seq=seq)
        xt = _hier_moe(xt, sc2, sh2, g2, norm_ffn[i].reshape(1, d), moe_w_group[i], moe_w_expert[i],
                       moe_w1, moe_w3, moe_w2, i, fnw, rows_per_batch=seq,
                       final=(i == depth - 1))
    return xt.reshape(batch, seq, d)
```

```python
import functools

import jax
import jax.numpy as jnp
from jax import lax
from jax.experimental import pallas as pl
from jax.experimental.pallas import tpu as pltpu

F32 = jnp.float32
BF16 = jnp.bfloat16
I32 = jnp.int32

EPS = 1e-6
NEG = -1e30
LOG2_E = 1.4426950408889634

D_MODEL = 2048
GRID_W = 64
SSD_HEAD_DIM = 64
SSD_GROUPS = 8
SSD_HEADS_PER_GROUP = 8
SSD_STATE = 128
SSD_CONV = 5
D_INNER = 2 * D_MODEL
GROUP_W = SSD_HEADS_PER_GROUP * SSD_HEAD_DIM
SSD_Q = 128
CONV_HALO = 16
NA_HEAD_DIM = 64
NA_HEADS = D_MODEL // NA_HEAD_DIM
NA_PAIRS = NA_HEADS // 2
WIN_H = 8
WIN_W = 16
NA_ROWS = 4
MOE_GROUPS = 4
MOE_EPG = 8
N_EXPERTS = MOE_GROUPS * MOE_EPG
MOE_D_FF = D_MODEL // 4
MOE_TB = 256
DMA_UNROLL = 8
VMEM_LIMIT = 56 * 1024 * 1024
LANES = 128


def _cparams(sem):
    return pltpu.CompilerParams(dimension_semantics=sem, vmem_limit_bytes=VMEM_LIMIT)


def _silu(v):
    return v * pl.reciprocal(1.0 + jnp.exp(-v), approx=True)


def _softplus(v):
    return jnp.maximum(v, 0.0) + jnp.log1p(jnp.exp(-jnp.abs(v)))


def _split3(v):
    hi = v.astype(BF16)
    r1 = v - hi.astype(F32)
    mid = r1.astype(BF16)
    lo = (r1 - mid.astype(F32)).astype(BF16)
    return hi, mid, lo


def _dot(a, b):
    return jnp.dot(a, b, preferred_element_type=F32)


def _dot_nt(a, b):
    return lax.dot_general(a, b, (((1,), (1,)), ((), ())), preferred_element_type=F32)


def _dot3_left(v, sel):
    hi, mid, lo = _split3(v)
    return _dot(hi, sel) + _dot(mid, sel) + _dot(lo, sel)


def _normmod(x, nw, sc, sh):
    ms = jnp.mean(x * x, axis=-1, keepdims=True)
    return (x * lax.rsqrt(ms + EPS) * nw) * (1.0 + sc) + sh


def _split2(v):
    hi = v.astype(BF16)
    return hi, (v - hi.astype(F32)).astype(BF16)


def _dot_split(a, b_hi, b_lo):
    a_hi, a_lo = _split2(a)
    return _dot(a_hi, b_hi) + (_dot(a_lo, b_hi) + _dot(a_hi, b_lo))


def _ada_kernel(c_ref, w_ref, b_ref, o_ref):
    c = c_ref[...]
    o_ref[0] = _dot_split(c / (1.0 + jnp.exp(-c)), *_split2(w_ref[0])) + b_ref[0]


def _ada(c_pad, ada_w, ada_b):
    depth, d, n = ada_w.shape
    tn = 2048
    return pl.pallas_call(
        _ada_kernel,
        out_shape=jax.ShapeDtypeStruct((depth, 8, n), F32),
        grid=(depth, n // tn),
        in_specs=[pl.BlockSpec((8, d), lambda i, j: (0, 0)),
                  pl.BlockSpec((1, d, tn), lambda i, j: (i, 0, j)),
                  pl.BlockSpec((1, 1, tn), lambda i, j: (i, 0, j))],
        out_specs=pl.BlockSpec((1, 8, tn), lambda i, j: (i, 0, j)),
        compiler_params=_cparams(("arbitrary", "arbitrary")),
        name="ada_mod",
    )(c_pad, ada_w, ada_b.reshape(depth, 1, n))


def _nm_mm_kernel(x_ref, nw_ref, sc_ref, sh_ref, w_ref, *rest, pair_major, tail, lead_tiles,
                  lead_scale):
    if tail:
        wt_ref, o_ref, ot_ref, h_ref = rest
    else:
        o_ref, h_ref = rest

    @pl.when(pl.program_id(1) == 0)
    def _():
        h_ref[...] = _normmod(x_ref[...], nw_ref[...], sc_ref[0], sh_ref[0]).astype(BF16)

    r = _dot(h_ref[...], w_ref[...].astype(BF16))
    if lead_tiles:
        r = r * jnp.where(pl.program_id(1) < lead_tiles, lead_scale, 1.0)
    if pair_major:
        for c in range(o_ref.shape[0]):
            o_ref[c] = r[:, c * LANES:(c + 1) * LANES].astype(o_ref.dtype)
    else:
        o_ref[...] = r.astype(o_ref.dtype)

    if tail:
        @pl.when(pl.program_id(1) == pl.num_programs(1) - 1)
        def _():
            ot_ref[...] = _dot(h_ref[...], wt_ref[...].astype(BF16))


def _nm_matmul(x, nw, sc, sh, w, *, ncols, tn, out_dtype, rows_per_batch, pair_major=False,
               tail_cols=0, lead_cols=0, lead_scale=1.0):
    t, d = x.shape
    tm = 1024
    tiles_per_batch = rows_per_batch // tm
    if pair_major:
        out_shape = jax.ShapeDtypeStruct((ncols // LANES, t, LANES), out_dtype)
        out_spec = pl.BlockSpec((tn // LANES, tm, LANES), lambda i, j: (j, i, 0))
    else:
        out_shape = jax.ShapeDtypeStruct((t, ncols), out_dtype)
        out_spec = pl.BlockSpec((tm, tn), lambda i, j: (i, j))
    in_specs = [pl.BlockSpec((tm, d), lambda i, j: (i, 0)),
                pl.BlockSpec((1, d), lambda i, j: (0, 0)),
                pl.BlockSpec((1, 1, d), lambda i, j: (i // tiles_per_batch, 0, 0)),
                pl.BlockSpec((1, 1, d), lambda i, j: (i // tiles_per_batch, 0, 0)),
                pl.BlockSpec((d, tn), lambda i, j: (0, j))]
    operands = [x, nw, sc, sh, w]
    if tail_cols:
        tail_blk = ncols // tail_cols
        in_specs.append(pl.BlockSpec((d, tail_cols), lambda i, j: (0, tail_blk)))
        operands.append(w)
        out_shape = (out_shape, jax.ShapeDtypeStruct((t, tail_cols), F32))
        out_spec = (out_spec, pl.BlockSpec((tm, tail_cols), lambda i, j: (i, 0)))
    return pl.pallas_call(
        functools.partial(_nm_mm_kernel, pair_major=pair_major, tail=bool(tail_cols),
                          lead_tiles=lead_cols // tn, lead_scale=lead_scale),
        out_shape=out_shape,
        grid=(t // tm, ncols // tn),
        in_specs=in_specs,
        out_specs=out_spec,
        scratch_shapes=[pltpu.VMEM((tm, d), BF16)],
        compiler_params=_cparams(("arbitrary", "arbitrary")),
        name="norm_mod_matmul",
    )(*operands)


def _mm_resid_kernel(a_ref, w_ref, x_ref, g_ref, o_ref, *, pair_major):
    if pair_major:
        a = jnp.concatenate([a_ref[c] for c in range(a_ref.shape[0])], axis=1)
    else:
        a = a_ref[...]
    o_ref[...] = x_ref[...] + g_ref[0] * _dot(a, w_ref[...].astype(BF16))


def _mm_resid(a, w, x, g, *, rows_per_batch, pair_major=False):
    t, n = x.shape
    k = w.shape[0]
    tn = 512
    tm = (4 * 1024 * 1024) // k
    tiles_per_batch = rows_per_batch // tm
    if pair_major:
        a_spec = pl.BlockSpec((k // LANES, tm, LANES), lambda i, j: (0, i, 0))
    else:
        a_spec = pl.BlockSpec((tm, k), lambda i, j: (i, 0))
    return pl.pallas_call(
        functools.partial(_mm_resid_kernel, pair_major=pair_major),
        out_shape=jax.ShapeDtypeStruct((t, n), F32),
        grid=(t // tm, n // tn),
        in_specs=[a_spec,
                  pl.BlockSpec((k, tn), lambda i, j: (0, j)),
                  pl.BlockSpec((tm, tn), lambda i, j: (i, j)),
                  pl.BlockSpec((1, 1, tn), lambda i, j: (i // tiles_per_batch, 0, j))],
        out_specs=pl.BlockSpec((tm, tn), lambda i, j: (i, j)),
        compiler_params=_cparams(("arbitrary", "arbitrary")),
        name="matmul_resid",
    )(a, w, x, g)


def _ssd_kernel(z_ref, x_ref, b_ref, c_ref, dtr_ref,
                cwx_ref, cwb_ref, cwc_ref, cbx_ref, cbb_ref, cbc_ref,
                alr_ref, dbr_ref, dsk_ref, nw_ref,
                o_ref,
                xc_s, bc_s, cc_s, yacc_s, st_s, cv_s):
    seq = x_ref.shape[0]
    q = SSD_Q
    nc = seq // q
    halo = CONV_HALO
    nrow = 2 * SSD_HEADS_PER_GROUP

    def conv_piece(j, base, src_ref, w_ref, bias_ref, dst_ref, lo, stage):
        cols = slice(lo, lo + LANES)
        pstart = pl.multiple_of(jnp.maximum(base - halo, 0), halo)
        nstart = pl.multiple_of(jnp.minimum(base + q, seq - halo), halo)
        stage[0:halo, :] = jnp.where(j > 0, src_ref[pl.ds(pstart, halo), cols].astype(F32), 0.0)
        stage[halo:halo + q, :] = src_ref[pl.ds(base, q), cols].astype(F32)
        stage[halo + q:, :] = jnp.where(j < nc - 1, src_ref[pl.ds(nstart, halo), cols].astype(F32), 0.0)
        acc = jnp.broadcast_to(bias_ref[:, cols], (q, LANES))
        for k in range(SSD_CONV):
            first = halo - SSD_CONV // 2 + k
            acc = acc + w_ref[k:k + 1, cols] * stage[first:first + q, :]
        dst_ref[pl.ds(base, q), cols] = _silu(acc).astype(BF16)

    def conv_chunk(j, carry):
        base = pl.multiple_of(j * q, q)
        npx = GROUP_W // LANES
        for i in range(npx):
            conv_piece(j, base, x_ref, cwx_ref, cbx_ref, xc_s, i * LANES, cv_s.at[i])
        conv_piece(j, base, b_ref, cwb_ref, cbb_ref, bc_s, 0, cv_s.at[npx])
        conv_piece(j, base, c_ref, cwc_ref, cbc_ref, cc_s, 0, cv_s.at[npx + 1])
        return carry

    lax.fori_loop(0, nc, conv_chunk, 0)

    row_i = lax.broadcasted_iota(I32, (q, q), 0)
    col_i = lax.broadcasted_iota(I32, (q, q), 1)
    lower = row_i >= col_i
    upper = row_i <= col_i
    lower_b = lower.astype(BF16)
    upper_b = upper.astype(BF16)
    left =lax.broadcasted_iota(I32, (q, LANES), 1) < SSD_HEAD_DIM
    a_row = -jnp.exp(alr_ref[...])
    pad_rows = jnp.zeros((LANES - nrow, q), F32)

    def scan_pass(direction):
        hoff = direction * SSD_HEADS_PER_GROUP
        mask = lower if direction == 0 else upper
        tri = upper_b if direction == 0 else lower_b
        edge = q - 1 if direction == 0 else 0
        st_s[...] = jnp.zeros_like(st_s)

        def chunk(t, carry):
            c = t if direction == 0 else nc - 1 - t
            base = pl.multiple_of(c * q, q)
            rows = pl.ds(base, q)
            dt_r = _softplus(dtr_ref[c] + dbr_ref[...])
            cum_r = _dot3_left(dt_r * a_row, tri) * LOG2_E
            cum_c = jnp.concatenate([cum_r, pad_rows], axis=0).T
            decdt_r = jnp.exp2(cum_r[:, edge:edge + 1] - cum_r) * dt_r
            src_r = cum_r - jnp.log2(dt_r)

            bm = bc_s[rows, :]
            cm = cc_s[rows, :]
            cb = _dot_nt(cm, bm)
            bm_t = bm.astype(F32).T
            y_off_all = _dot(cm, st_s[...].astype(BF16)) if direction == 1 else None

            ssq = jnp.zeros((q, 1), F32)
            for pp in range(SSD_HEADS_PER_GROUP // 2):
                cols = slice(pp * LANES, (pp + 1) * LANES)
                xcb = xc_s[rows, cols]
                zero_b = jnp.zeros_like(xcb)
                st_in = st_s[:, cols]
                lhs_y, lhs_s, scales = [], [], []
                for par in range(2):
                    j = hoff + 2 * pp + par
                    cum_b = jnp.broadcast_to(cum_c[:, j:j + 1], (q, q))
                    lmat = jnp.exp2(jnp.where(mask, cum_b - src_r[j:j + 1, :], NEG))
                    lhs_y.append((cb * lmat).astype(BF16))
                    lhs_s.append((bm_t * decdt_r[j:j + 1, :]).astype(BF16))
                    scales.append(jnp.exp2(cum_b))
                x_rhs = jnp.concatenate([jnp.where(left, xcb, zero_b), jnp.where(left, zero_b, xcb)],
                                        axis=0)
                sc_tile = jnp.where(left, scales[0], scales[1])
                y_off = _dot(cm, st_in.astype(BF16)) if y_off_all is None else y_off_all[:, cols]
                y = _dot(jnp.concatenate(lhs_y, axis=1), x_rhs) + y_off * sc_tile
                st_s[:, cols] = (st_in * sc_tile[edge:edge + 1, :]
                                 + _dot(jnp.concatenate(lhs_s, axis=1), x_rhs))
                if direction == 0:
                    yacc_s[rows, cols] = y
                else:
                    total = yacc_s[rows, cols] + y + xcb.astype(F32) * dsk_ref[:, cols]
                    gated = total * _silu(z_ref[rows, cols].astype(F32))
                    ssq = ssq + jnp.sum(gated * gated, axis=-1, keepdims=True)
                    yacc_s[rows, cols] = gated
            if direction == 1:
                inv = lax.rsqrt(ssq * (1.0 / GROUP_W) + EPS)
                o_ref[rows, :] = (yacc_s[rows, :] * inv * nw_ref[...]).astype(BF16)
            return carry

        lax.fori_loop(0, nc, chunk, 0, unroll=8)

    scan_pass(0)
    scan_pass(1)


def _ssd_core(zx, dt_row, conv_w, conv_b, al_row, db_row, dskip, norm_w, *, batch, seq):
    g = SSD_GROUPS
    nc = seq // SSD_Q
    xb = D_INNER // GROUP_W
    bb = (2 * D_INNER) // SSD_STATE
    cb = bb + g
    cwb = D_INNER // SSD_STATE
    cwc = cwb + g
    return pl.pallas_call(
        _ssd_kernel,
        out_shape=jax.ShapeDtypeStruct((batch * seq, D_INNER), BF16),
        grid=(batch, g),
        in_specs=[
            pl.BlockSpec((seq, GROUP_W), lambda b, i: (b, i)),
            pl.BlockSpec((seq, GROUP_W), lambda b, i: (b, xb + i)),
            pl.BlockSpec((seq, SSD_STATE), lambda b, i: (b, bb + i)),
            pl.BlockSpec((seq, SSD_STATE), lambda b, i: (b, cb + i)),
            pl.BlockSpec((None, None, nc, 2 * SSD_HEADS_PER_GROUP, SSD_Q), lambda b, i: (b, i, 0, 0, 0)),
            pl.BlockSpec((SSD_CONV, GROUP_W), lambda b, i: (0, i)),
            pl.BlockSpec((SSD_CONV, SSD_STATE), lambda b, i: (0, cwb + i)),
            pl.BlockSpec((SSD_CONV, SSD_STATE), lambda b, i: (0, cwc + i)),
            pl.BlockSpec((1, GROUP_W), lambda b, i: (0, i)),
            pl.BlockSpec((1, SSD_STATE), lambda b, i: (0, cwb + i)),
            pl.BlockSpec((1, SSD_STATE), lambda b, i: (0, cwc + i)),
            pl.BlockSpec((None, 2 * SSD_HEADS_PER_GROUP, 1), lambda b, i: (i, 0, 0)),
            pl.BlockSpec((None, 2 * SSD_HEADS_PER_GROUP, 1), lambda b, i: (i, 0, 0)),
            pl.BlockSpec((1, GROUP_W), lambda b, i: (0, i)),
            pl.BlockSpec((1, GROUP_W), lambda b, i: (0, i)),
        ],
        out_specs=pl.BlockSpec((seq, GROUP_W), lambda b, i: (b, i)),
        scratch_shapes=[pltpu.VMEM((seq, GROUP_W), BF16),
                        pltpu.VMEM((seq, SSD_STATE), BF16),
                        pltpu.VMEM((seq, SSD_STATE), BF16),
                        pltpu.VMEM((seq, GROUP_W), F32),
                        pltpu.VMEM((SSD_STATE, GROUP_W), F32),
                        pltpu.VMEM((GROUP_W // LANES + 2, SSD_Q + 2 * CONV_HALO, LANES), F32)],
        compiler_params=_cparams(("arbitrary", "arbitrary")),
        name="ssd_core",
    )(zx, zx, zx, zx, dt_row, conv_w, conv_w, conv_w, conv_b, conv_b, conv_b,
      al_row, db_row, dskip, norm_w)


def _ssd_mixer(x, mod_sc, mod_sh, mod_g, nw, w_in, conv_w, conv_b, a_log, dt_bias, d_skip, norm_w,
               w_out, *, batch, seq):
    g, r = SSD_GROUPS, SSD_HEADS_PER_GROUP
    conv_dim = conv_w.shape[1]
    zx, dt_raw = _nm_matmul(x, nw, mod_sc, mod_sh, w_in, ncols=D_INNER + conv_dim, tn=1024,
                            out_dtype=BF16, rows_per_batch=seq, tail_cols=2 * g * r)
    nc = seq // SSD_Q
    dt_row = dt_raw.reshape(batch, nc, SSD_Q, 2, g, r).transpose(0, 4, 1, 3, 5, 2)
    dt_row = dt_row.reshape(batch, g, nc, 2 * r, SSD_Q)

    def row_form(p):
        return p.reshape(2, g, r).transpose(1, 0, 2).reshape(g, 2 * r, 1)

    yn = _ssd_core(zx, dt_row, conv_w, conv_b.reshape(1, conv_dim), row_form(a_log), row_form(dt_bias),
                   jnp.repeat(d_skip, SSD_HEAD_DIM).reshape(1, D_INNER), norm_w.reshape(1, D_INNER),
                   batch=batch, seq=seq)
    return _mm_resid(yn, w_out, x, mod_g, rows_per_batch=seq)


def _bias_table_kernel(rpb_ref, o_ref):
    lane = lax.broadcasted_iota(I32, (GRID_W, LANES), 1)
    j = lax.broadcasted_iota(I32, (GRID_W, LANES), 0)
    c = lane & (GRID_W - 1)
    c0 = jnp.clip(j - WIN_W // 2, 0, GRID_W - WIN_W)
    win = (c >= c0) & (c < c0 + WIN_W)
    left = lane < GRID_W

    def one_offset(dy0, carry):
        for par in range(2):
            for m in range(WIN_H // 2):
                tiles = []
                for sub in range(2):
                    row = rpb_ref[par, pl.ds(dy0 + 2 * m + sub, 1), :]
                    shift = (sub * GRID_W - (WIN_W - 1)) % LANES
                    tiles.append(pltpu.roll(jnp.broadcast_to(row, (GRID_W, LANES)), shift, 1,
                                            stride=1, stride_axis=0))
                tile = jnp.where(win, jnp.where(left, tiles[0], tiles[1]) * LOG2_E, NEG)
                o_ref[dy0, 0, par * GRID_W:(par + 1) * GRID_W, m * LANES:(m + 1) * LANES] = tile
        return carry

    lax.fori_loop(0, o_ref.shape[0], one_offset, 0)


def _na_bias_table(rpb):
    h, ndy, ndx = rpb.shape
    rpb_p = jnp.pad(rpb, ((0, 0), (0, 2 * WIN_H - ndy), (0, LANES - ndx)))
    return pl.pallas_call(
        _bias_table_kernel,
        out_shape=jax.ShapeDtypeStruct((WIN_H, h // 2, 2 * GRID_W, WIN_H * GRID_W), F32),
        grid=(h // 2,),
        in_specs=[pl.BlockSpec((2, 2 * WIN_H, LANES), lambda p: (p, 0, 0))],
        out_specs=pl.BlockSpec((WIN_H, 1, 2 * GRID_W, WIN_H * GRID_W), lambda p: (0, p, 0, 0)),
        compiler_params=_cparams(("arbitrary",)),
        name="na_bias_table",
    )(rpb_p)


def _na_kernel(q_ref, kv_hbm, bias_ref, o_ref, kc_s, vc_s, s_s, p_s, r_s, sem, *, n_row_blocks):
    rb = pl.program_id(2)
    npairs = q_ref.shape[0]
    blk = NA_ROWS * GRID_W
    nkeys = WIN_H * GRID_W
    n_sec, n_batch = pl.num_programs(0), pl.num_programs(1)
    step = (pl.program_id(0) * n_batch + pl.program_id(1)) * n_row_blocks + rb
    nsteps = n_sec * n_batch * n_row_blocks

    def fetch(st, half):
        r = st % n_row_blocks
        hb = st // n_row_blocks
        tok0 = ((hb % n_batch) * n_row_blocks + jnp.clip(r - 1, 0, n_row_blocks - 3)) * blk
        tok0 = pl.multiple_of(tok0, blk)

        def one(sec, dst, j):
            first_pair = (sec * n_sec + hb // n_batch) * npairs
            return pltpu.make_async_copy(
                kv_hbm.at[pl.ds(first_pair, npairs), pl.ds(tok0, 3 * blk), :], dst.at[half],
                sem.at[half, j])
        return one(1, kc_s, 0), one(2, vc_s, 1)

    cur = step % 2

    @pl.when(step == 0)
    def _():
        for cp in fetch(step, cur):
            cp.start()

    @pl.when(step + 1 < nsteps)
    def _():
        for cp in fetch(step + 1, 1 - cur):
            cp.start()

    for cp in fetch(step, cur):
        cp.wait()
    first = rb == 0
    last = rb == n_row_blocks - 1
    edge = first | last
    lane = lax.broadcasted_iota(I32, (GRID_W, LANES), 1)
    left = lane < NA_HEAD_DIM

    def window(qi):
        off = jnp.where(first, 0, jnp.where(last, blk, qi * GRID_W))
        li = jnp.where(edge, NA_ROWS - 1 - qi, NA_ROWS - 1)
        return pl.multiple_of(off, GRID_W), li

    def pair_body(pp, carry):
        for qi in range(NA_ROWS):
            off, li = window(qi)
            q2 = q_ref[pp, qi * GRID_W:(qi + 1) * GRID_W, :]
            zero = jnp.zeros_like(q2)
            qs = jnp.concatenate([jnp.where(left, q2, zero), jnp.where(left, zero, q2)], axis=0)
            kw = kc_s[cur, pp, pl.ds(off, nkeys), :]
            s_s[qi] = _dot_nt(qs, kw) + bias_ref[li, pp]
        for qi in range(NA_ROWS):
            s = s_s[qi]
            p = jnp.exp2(s - jnp.max(s, axis=-1, keepdims=True))
            r_s[qi] = 1.0 / jnp.sum(p, axis=-1, keepdims=True)
            p_s[qi] = p.astype(BF16)
        for qi in range(NA_ROWS):
            off, _ = window(qi)
            pv = _dot(p_s[qi], vc_s[cur, pp, pl.ds(off, nkeys), :]) * r_s[qi]
            o = jnp.where(left, pv[0:GRID_W], pv[GRID_W:2 * GRID_W])
            o_ref[pp, qi * GRID_W:(qi + 1) * GRID_W, :] = o.astype(BF16)
        return carry

    lax.fori_loop(0, npairs, pair_body, 0, unroll=4)


def _na_attention(qkv_t, bias_tab, *, batch, seq):
    t = batch * seq
    blk = NA_ROWS * GRID_W
    nrb = seq // blk
    hp = NA_PAIRS // 2
    nsec = NA_PAIRS // hp

    return pl.pallas_call(
        functools.partial(_na_kernel, n_row_blocks=nrb),
        out_shape=jax.ShapeDtypeStruct((NA_PAIRS, t, LANES), BF16),
        grid=(nsec, batch, nrb),
        in_specs=[pl.BlockSpec((hp, blk, LANES), lambda hh, b, r: (hh, b * nrb + r, 0)),
                  pl.BlockSpec(memory_space=pl.ANY),
                  pl.BlockSpec((NA_ROWS, hp, 2 * GRID_W, WIN_H * GRID_W),
                               lambda hh, b, r: (jnp.where(r == 0, 1, 0), hh, 0, 0))],
        out_specs=pl.BlockSpec((hp, blk, LANES), lambda hh, b, r: (hh, b * nrb + r, 0)),
        scratch_shapes=[pltpu.VMEM((2, hp, 3 * blk, LANES), BF16),
                        pltpu.VMEM((2, hp, 3 * blk, LANES), BF16),
                        pltpu.VMEM((NA_ROWS, 2 * GRID_W, WIN_H * GRID_W), F32),
                        pltpu.VMEM((NA_ROWS, 2 * GRID_W, WIN_H * GRID_W), BF16),
                        pltpu.VMEM((NA_ROWS, 2 * GRID_W, 1), F32),
                        pltpu.SemaphoreType.DMA((2, 2))],
        compiler_params=_cparams(("arbitrary", "arbitrary", "arbitrary")),
        name="na_attention",
    )(qkv_t, qkv_t, bias_tab)


def _na_mixer(x, mod_sc, mod_sh, mod_g, nw, w_qkv, rpb, w_o, *, batch, seq):
    qkv_t = _nm_matmul(x, nw, mod_sc, mod_sh, w_qkv, ncols=3 * D_MODEL, tn=1024,
                       out_dtype=BF16, rows_per_batch=seq, pair_major=True,
                       lead_cols=D_MODEL, lead_scale=NA_HEAD_DIM ** -0.5 * LOG2_E)
    o_t = _na_attention(qkv_t, _na_bias_table(rpb), batch=batch, seq=seq)
    return _mm_resid(o_t, w_o, x, mod_g, rows_per_batch=seq, pair_major=True)


U32 = jnp.uint32


def _pack_halves(vb):
    n = vb.shape[1] // 2
    bits = pltpu.bitcast(vb.astype(F32), U32)
    return (bits[:, :n] >> 16) | bits[:, n:]


def _unpack_halves(w):
    return pltpu.bitcast(w << 16, F32), pltpu.bitcast(w & U32(0xFFFF0000), F32)


TOK_SUB = (D_MODEL // 2) // LANES


def _store_token_tiles(ref, packed):
    rows = packed.shape[0]
    for s in range(TOK_SUB):
        ref[pl.ds(s, rows, stride=TOK_SUB), :] = packed[:, s * LANES:(s + 1) * LANES]


def _load_token_tiles(ref):
    rows = ref.shape[0] // TOK_SUB
    return jnp.concatenate([ref[pl.ds(s, rows, stride=TOK_SUB), :] for s in range(TOK_SUB)], axis=1)


def _router_kernel(x_ref, nw_ref, sc_ref, sh_ref, wr_ref, h_ref, meta_ref, meta_t_ref, cnt_ref,
                   carry_s, whi_s, wlo_s):
    @pl.when(pl.program_id(0) == 0)
    def _():
        carry_s[...] = jnp.zeros_like(carry_s)
        whi_s[...], wlo_s[...] = _split2(wr_ref[...])

    h = _normmod(x_ref[...], nw_ref[...], sc_ref[0], sh_ref[0])
    _store_token_tiles(h_ref, _pack_halves(h.astype(BF16)))
    logits = _dot_split(h, whi_s[...], wlo_s[...])
    tm = logits.shape[0]
    lane_i = lax.broadcasted_iota(I32, logits.shape, 1)
    lane = lane_i.astype(F32)
    big = 1e9
    gl = jnp.where(lane_i < MOE_GROUPS, logits, NEG)
    gmax = jnp.max(gl, axis=1, keepdims=True)
    gsel = jnp.min(jnp.where(gl == gmax, lane, big), axis=1, keepdims=True)
    gw = 1.0 / jnp.sum(jnp.exp(gl - gmax), axis=1, keepdims=True)
    el = lane - MOE_GROUPS
    lo = gsel * MOE_EPG
    emask = (el >= lo) & (el < lo + MOE_EPG)
    e1 = jnp.where(emask, logits, NEG)
    m1 = jnp.max(e1, axis=1, keepdims=True)
    i1 = jnp.min(jnp.where(e1 == m1, el, big), axis=1, keepdims=True)
    e2 = jnp.where(emask & (el != i1), logits, NEG)
    m2 = jnp.max(e2, axis=1, keepdims=True)
    i2 = jnp.min(jnp.where(e2 == m2, el, big), axis=1, keepdims=True)
    tt = jnp.exp(m2 - m1)
    p1 = 1.0 / (1.0 + tt)
    w1 = gw * p1
    w2 = gw * (tt * p1)
    oh1 = el == i1
    oh2 = el == i2
    cnt = (oh1 | oh2).astype(F32)
    r_i = lax.broadcasted_iota(I32, (tm, tm), 0)
    c_i = lax.broadcasted_iota(I32, (tm, tm), 1)
    before = _dot((r_i > c_i).astype(BF16), cnt.astype(BF16)) + carry_s[...]
    rank1 = jnp.sum(jnp.where(oh1, before, 0.0), axis=1, keepdims=True)
    rank2 = jnp.sum(jnp.where(oh2, before, 0.0), axis=1, keepdims=True)
    carry_s[...] = carry_s[...] + jnp.sum(cnt, axis=0, keepdims=True)
    meta = jnp.zeros_like(logits)
    for pos, val in enumerate((i1, i2, w1, w2, rank1, rank2)):
        meta = jnp.where(lane_i == pos, val, meta)
    meta_ref[...] = meta
    meta_t_ref[...] = meta.T[0:meta_t_ref.shape[0], :]
    cnt_ref[...] = jnp.broadcast_to(carry_s[...], cnt_ref.shape)


def _router(x, nw, sc, sh, wr, *, rows_per_batch):
    t, d = x.shape
    tm = 256
    tiles_per_batch = rows_per_batch // tm
    return pl.pallas_call(
        _router_kernel,
        out_shape=(jax.ShapeDtypeStruct((t * TOK_SUB, LANES), U32),
                   jax.ShapeDtypeStruct((t, LANES), F32),
                   jax.ShapeDtypeStruct((8, t), F32),
                   jax.ShapeDtypeStruct((8, LANES), F32)),
        grid=(t // tm,),
        in_specs=[pl.BlockSpec((tm, d), lambda i: (i, 0)),
                  pl.BlockSpec((1, d), lambda i: (0, 0)),
                  pl.BlockSpec((1, 1, d), lambda i: (i // tiles_per_batch, 0, 0)),
                  pl.BlockSpec((1, 1, d), lambda i: (i // tiles_per_batch, 0, 0)),
                  pl.BlockSpec((d, LANES), lambda i: (0, 0))],
        out_specs=(pl.BlockSpec((tm * TOK_SUB, LANES), lambda i: (i, 0)),
                   pl.BlockSpec((tm, LANES), lambda i: (i, 0)),
                   pl.BlockSpec((8, tm), lambda i: (0, i)),
                   pl.BlockSpec((8, LANES), lambda i: (0, 0))),
        scratch_shapes=[pltpu.VMEM((1, LANES), F32), pltpu.VMEM((d, LANES), BF16),
                        pltpu.VMEM((d, LANES), BF16)],
        compiler_params=_cparams(("arbitrary",)),
        name="moe_router",
    )(x, nw, sc, sh, wr)


def _slot_kernel(pstart_ref, mt_ref, o_ref):
    eid = mt_ref[0:2, :]
    start = jnp.zeros(eid.shape, I32)
    for e in range(N_EXPERTS):
        start = jnp.where(eid == float(e), pstart_ref[e], start)
    o_ref[...] = start + mt_ref[4:6, :].astype(I32)


def _slots(pstart, meta_t):
    t = meta_t.shape[1]
    return pl.pallas_call(
        _slot_kernel,
        out_shape=jax.ShapeDtypeStruct((2, t), I32),
        grid_spec=pltpu.PrefetchScalarGridSpec(
            num_scalar_prefetch=1,
            grid=(1,),
            in_specs=[pl.BlockSpec(meta_t.shape, lambda i, ps: (0, 0))],
            out_specs=pl.BlockSpec((2, t), lambda i, ps: (0, 0))),
        compiler_params=_cparams(("arbitrary",)),
        name="moe_slots",
    )(pstart, meta_t)


def _token_tile(ref, r):
    return ref.at[pl.ds(pl.multiple_of(r * TOK_SUB, TOK_SUB), TOK_SUB)]


def _dispatch_kernel(dest_ref, zflag_ref, h_ref, xs_ref, zbuf, sem, zsem):
    tm = h_ref.shape[0] // TOK_SUB
    base = pl.program_id(0) * tm
    ntok = pl.num_programs(0) * tm
    tb = zbuf.shape[0]

    @pl.when(pl.program_id(0) == 0)
    def _():
        zbuf[...] = jnp.zeros_like(zbuf)

        def zcopy(b):
            return pltpu.make_async_copy(zbuf, xs_ref.at[pl.ds(pl.multiple_of(b * tb, tb), tb)], zsem)

        def zstart(b, carry):
            @pl.when(zflag_ref[b] == 1)
            def _():
                zcopy(b).start()
            return carry

        def zwait(b, carry):
            @pl.when(zflag_ref[b] == 1)
            def _():
                zcopy(b).wait()
            return carry

        nblk = xs_ref.shape[0] // tb
        lax.fori_loop(0, nblk, zstart, 0)
        lax.fori_loop(0, nblk, zwait, 0)

    def copy(t, k):
        return pltpu.make_async_copy(_token_tile(h_ref, t),
                                     _token_tile(xs_ref, dest_ref[k * ntok + base + t]), sem)

    def issue(t, carry):
        copy(t, 0).start(priority=0)
        copy(t, 1).start(priority=1)
        return carry

    def drain(t, carry):
        copy(t, 0).wait()
        copy(t, 1).wait()
        return carry

    lax.fori_loop(0, tm, issue, 0, unroll=DMA_UNROLL)
    lax.fori_loop(0, tm, drain, 0, unroll=DMA_UNROLL)


def _dispatch(dest, zflag, h, n_slots):
    t = h.shape[0] // TOK_SUB
    tm = 256
    return pl.pallas_call(
        _dispatch_kernel,
        out_shape=jax.ShapeDtypeStruct((n_slots * TOK_SUB, LANES), U32),
        grid_spec=pltpu.PrefetchScalarGridSpec(
            num_scalar_prefetch=2,
            grid=(t // tm,),
            in_specs=[pl.BlockSpec((tm * TOK_SUB, LANES), lambda i, dest, zf: (i, 0))],
            out_specs=pl.BlockSpec(memory_space=pl.ANY),
            scratch_shapes=[pltpu.VMEM((MOE_TB * TOK_SUB, LANES), U32), pltpu.SemaphoreType.DMA(()),
                            pltpu.SemaphoreType.DMA(())]),
        compiler_params=_cparams(("arbitrary",)),
        name="moe_dispatch",
    )(dest, zflag, h)


def _ffn_kernel(be_ref, nxt_ref, slot_ref, nu_ref, xs_ref, w1_hbm, w3_hbm, w2_hbm, o_ref,
                wb1, wb3, wb2, w1_s, w3_s, w2_s, sem, *, layer):
    i = pl.program_id(0)

    def fetch(e, s):
        return (pltpu.make_async_copy(w1_hbm.at[layer, e], wb1.at[s], sem.at[s, 0]),
                pltpu.make_async_copy(w3_hbm.at[layer, e], wb3.at[s], sem.at[s, 1]),
                pltpu.make_async_copy(w2_hbm.at[layer, e], wb2.at[s], sem.at[s, 2]))

    @pl.when(i < nu_ref[0])
    def _():
        e = be_ref[i]
        s = slot_ref[i]

        @pl.when((i == 0) | (e != be_ref[jnp.maximum(i - 1, 0)]))
        def _():
            @pl.when(i == 0)
            def _():
                for cp in fetch(e, s):
                    cp.start()

            for cp in fetch(e, s):
                cp.wait()

            @pl.when(nxt_ref[i] >= 0)
            def _():
                for cp in fetch(nxt_ref[i], 1 - s):
                    cp.start()

            w1_s[...] = wb1[s].astype(BF16)
            w3_s[...] = wb3[s].astype(BF16)
            w2_s[...] = wb2[s].astype(BF16)

        lo, hi = _unpack_halves(_load_token_tiles(xs_ref))
        xl, xh = lo.astype(BF16), hi.astype(BF16)
        half = xl.shape[1]
        a = _dot(xl, w1_s[0:half, :]) + _dot(xh, w1_s[half:2 * half, :])
        b = _dot(xl, w3_s[0:half, :]) + _dot(xh, w3_s[half:2 * half, :])
        hmid = _silu(a) * b
        _store_token_tiles(o_ref, _pack_halves(_dot(hmid.astype(BF16), w2_s[...]).astype(BF16)))

    @pl.when(i >= nu_ref[0])
    def _():
        o_ref[...] = jnp.zeros_like(o_ref)


def _expert_ffn(blk_expert, blk_next, blk_slot, n_used, xs, w1, w3, w2, layer):
    d, f = w1.shape[2], w1.shape[3]
    rows = MOE_TB * TOK_SUB
    nb = xs.shape[0] // rows
    hbm = pl.BlockSpec(memory_space=pl.ANY)
    return pl.pallas_call(
        functools.partial(_ffn_kernel, layer=layer),
        out_shape=jax.ShapeDtypeStruct(xs.shape, U32),
        grid_spec=pltpu.PrefetchScalarGridSpec(
            num_scalar_prefetch=4,
            grid=(nb,),
            in_specs=[pl.BlockSpec((rows, LANES),
                                   lambda i, be, nx, sl, nu: (jnp.minimum(i, nu[0] - 1), 0)),
                      hbm, hbm, hbm],
            out_specs=pl.BlockSpec((rows, LANES), lambda i, be, nx, sl, nu: (i, 0)),
            scratch_shapes=[pltpu.VMEM((2, d, f), F32), pltpu.VMEM((2, d, f), F32),
                            pltpu.VMEM((2, f, d), F32),
                            pltpu.VMEM((d, f), BF16), pltpu.VMEM((d, f), BF16),
                            pltpu.VMEM((f, d), BF16),
                            pltpu.SemaphoreType.DMA((2, 3))]),
        compiler_params=_cparams(("arbitrary",)),
        name="moe_expert_ffn",
    )(blk_expert, blk_next, blk_slot, n_used, xs, w1, w3, w2)


def _combine_kernel(dest_ref, x_ref, meta_ref, g_ref, fnw_ref, ys_ref, o_ref, buf, sem, *, final):
    tm = x_ref.shape[0]
    i = pl.program_id(0)
    nsteps = pl.num_programs(0)
    ntok = nsteps * tm

    def copy(tile, t, k):
        half = tile % 2
        return pltpu.make_async_copy(_token_tile(ys_ref, dest_ref[k * ntok + tile * tm + t]),
                                     _token_tile(buf.at[half, k], t), sem.at[half])

    def issue(tile):
        def body(t, carry):
            copy(tile, t, 0).start(priority=0)
            copy(tile, t, 1).start(priority=1)
            return carry
        lax.fori_loop(0, tm, body, 0, unroll=DMA_UNROLL)

    def drain(tile):
        def body(t, carry):
            copy(tile, t, 0).wait()
            copy(tile, t, 1).wait()
            return carry
        lax.fori_loop(0, tm, body, 0, unroll=DMA_UNROLL)

    @pl.when(i == 0)
    def _():
        issue(i)

    @pl.when(i + 1 < nsteps)
    def _():
        issue(i + 1)

    drain(i)
    cur = i % 2
    meta = meta_ref[...]
    w1, w2 = meta[:, 2:3], meta[:, 3:4]
    lo1, hi1 = _unpack_halves(_load_token_tiles(buf.at[cur, 0]))
    lo2, hi2 = _unpack_halves(_load_token_tiles(buf.at[cur, 1]))
    y = jnp.concatenate([w1 * lo1 + w2 * lo2, w1 * hi1 + w2 * hi2], axis=1)
    xn = x_ref[...] + g_ref[0] * y
    if final:
        ms = jnp.mean(xn * xn, axis=-1, keepdims=True)
        xn = xn * lax.rsqrt(ms + EPS) * fnw_ref[...]
    o_ref[...] = xn


def _combine(dest, x, meta, g, fnw, ys, *, rows_per_batch, final):
    t, d = x.shape
    tm = 256
    tiles_per_batch = rows_per_batch // tm
    return pl.pallas_call(
        functools.partial(_combine_kernel, final=final),
        out_shape=jax.ShapeDtypeStruct((t, d), F32),
        grid_spec=pltpu.PrefetchScalarGridSpec(
            num_scalar_prefetch=1,
            grid=(t // tm,),
            in_specs=[pl.BlockSpec((tm, d), lambda i, dest: (i, 0)),
                      pl.BlockSpec((tm, LANES), lambda i, dest: (i, 0)),
                      pl.BlockSpec((1, 1, d), lambda i, dest: (i // tiles_per_batch, 0, 0)),
                      pl.BlockSpec((1, d), lambda i, dest: (0, 0)),
                      pl.BlockSpec(memory_space=pl.ANY)],
            out_specs=pl.BlockSpec((tm, d), lambda i, dest: (i, 0)),
            scratch_shapes=[pltpu.VMEM((2, 2, tm * TOK_SUB, LANES), U32),
                            pltpu.SemaphoreType.DMA((2,))]),
        compiler_params=_cparams(("arbitrary",)),
        name="moe_combine",
    )(dest, x, meta, g, fnw, ys)


def _hier_moe(x, mod_sc, mod_sh, mod_g, nw, w_group, w_expert, w1, w3, w2, layer, fnw, *,
              rows_per_batch, final):
    t, d = x.shape
    a = 2 * t
    tb = MOE_TB
    wr = jnp.concatenate([w_group, w_expert], axis=1)
    wr = jnp.pad(wr, ((0, 0), (0, LANES - wr.shape[1])))
    h, meta, meta_t, cnt = _router(x, nw, mod_sc, mod_sh, wr, rows_per_batch=rows_per_batch)
    ne = N_EXPERTS
    counts = cnt[0, MOE_GROUPS:MOE_GROUPS + ne].astype(I32)
    padded = ((counts + tb - 1) // tb) * tb
    pend = jnp.cumsum(padded)
    pstart = pend - padded
    dest = _slots(pstart, meta_t).reshape(a)
    nb = (a + ne * (tb - 1) + tb - 1) // tb
    n_used = (pend[-1] // tb).astype(I32)
    blk = jnp.arange(nb, dtype=I32)
    be = jnp.minimum(jnp.sum((pend[None, :] <= (blk * tb)[:, None]).astype(I32), axis=1), ne - 1)
    seg_last = jnp.any((pend[None, :] == ((blk + 1) * tb)[:, None]) & (padded[None, :] > 0), axis=1)
    zflag = (seg_last | (blk >= n_used)).astype(I32)
    nonempty = counts > 0
    seg = jnp.cumsum(nonempty.astype(I32)) - 1
    later = lax.cummin(jnp.where(nonempty, jnp.arange(ne, dtype=I32), ne), axis=0, reverse=True)
    nxt_e = jnp.concatenate([later[1:], jnp.full((1,), ne, I32)])
    nxt_e = jnp.where(nxt_e == ne, -1, nxt_e)
    xs = _dispatch(dest, zflag, h, nb * tb)
    ys = _expert_ffn(be, nxt_e[be], seg[be] % 2, n_used.reshape(1), xs, w1, w3, w2, layer)
    return _combine(dest, x, meta, mod_g, fnw, ys, rows_per_batch=rows_per_batch, final=final)


def kernel(x, c, ada_w, ada_b, norm_mix, norm_ffn, ssd_w_in, ssd_conv_w, ssd_conv_b, ssd_a_log,
           ssd_dt_bias, ssd_d, ssd_norm_w, ssd_w_out, na_w_qkv, na_rpb, na_w_o,
           moe_w_group, moe_w_expert, moe_w1, moe_w3, moe_w2, final_norm):
    batch, seq, d = x.shape
    depth = ada_w.shape[0]
    xt = x.reshape(batch * seq, d)
    c_pad = jnp.pad(c, ((0, 8 - batch), (0, 0)))
    mod = _ada(c_pad, ada_w, ada_b)[:, :batch]
    fnw = final_norm.reshape(1, d)
    for i in range(depth):
        sh1, sc1, g1, sh2, sc2, g2 = [mod[i, :, k * d:(k + 1) * d].reshape(batch, 1, d)
                                      for k in range(6)]
        j = i // 2
        nw = norm_mix[i].reshape(1, d)
        if i % 2 == 0:
            xt = _ssd_mixer(xt, sc1, sh1, g1, nw, ssd_w_in[j], ssd_conv_w[j], ssd_conv_b[j],
                            ssd_a_log[j], ssd_dt_bias[j], ssd_d[j], ssd_norm_w[j], ssd_w_out[j],
                            batch=batch, seq=seq)
        else:
            xt = _na_mixer(xt, sc1, sh1, g1, nw, na_w_qkv[j], na_rpb[j], na_w_o[j],
                           batch=batch, seq=seq)
        xt = _hier_moe(xt, sc2, sh2, g2, norm_ffn[i].reshape(1, d), moe_w_group[i], moe_w_expert[i],
                       moe_w1, moe_w3, moe_w2, i, fnw, rows_per_batch=seq,
                       final=(i == depth - 1))
    return xt.reshape(batch, seq, d)
```

```python
import functools

import jax
import jax.numpy as jnp
from jax import lax
from jax.experimental import pallas as pl
from jax.experimental.pallas import tpu as pltpu

F32 = jnp.float32
BF16 = jnp.bfloat16
I32 = jnp.int32

EPS = 1e-6
NEG = -1e30
LOG2_E = 1.4426950408889634

D_MODEL = 2048
GRID_W = 64
SSD_HEAD_DIM = 64
SSD_GROUPS = 8
SSD_HEADS_PER_GROUP = 8
SSD_STATE = 128
SSD_CONV = 5
D_INNER = 2 * D_MODEL
GROUP_W = SSD_HEADS_PER_GROUP * SSD_HEAD_DIM
SSD_Q = 128
CONV_HALO = 16
NA_HEAD_DIM = 64
NA_HEADS = D_MODEL // NA_HEAD_DIM
NA_PAIRS = NA_HEADS // 2
WIN_H = 8
WIN_W = 16
NA_ROWS = 4
MOE_GROUPS = 4
MOE_EPG = 8
N_EXPERTS = MOE_GROUPS * MOE_EPG
MOE_D_FF = D_MODEL // 4
MOE_TB = 256
DMA_UNROLL = 8
VMEM_LIMIT = 56 * 1024 * 1024
LANES = 128

MM_ROWS = 1024
MM_COLS = 1024
RESID_COLS = 512
RESID_LHS_ELEMS = 4 * 1024 * 1024
ADA_COLS = 2048
TOKEN_TILE = 256


def _cparams(sem):
    return pltpu.CompilerParams(dimension_semantics=sem, vmem_limit_bytes=VMEM_LIMIT)


def _silu(v):
    return v * pl.reciprocal(1.0 + jnp.exp(-v), approx=True)


def _softplus(v):
    return jnp.maximum(v, 0.0) + jnp.log1p(jnp.exp(-jnp.abs(v)))


def _split3(v):
    hi = v.astype(BF16)
    r1 = v - hi.astype(F32)
    mid = r1.astype(BF16)
    lo = (r1 - mid.astype(F32)).astype(BF16)
    return hi, mid, lo


def _dot(a, b):
    return jnp.dot(a, b, preferred_element_type=F32)


def _dot_nt(a, b):
    return lax.dot_general(a, b, (((1,), (1,)), ((), ())), preferred_element_type=F32)


def _dot3_left(v, sel):
    hi, mid, lo = _split3(v)
    return _dot(hi, sel) + _dot(mid, sel) + _dot(lo, sel)


def _normmod(x, nw, sc, sh):
    ms = jnp.mean(x * x, axis=-1, keepdims=True)
    return (x * lax.rsqrt(ms + EPS) * nw) * (1.0 + sc) + sh


def _split2(v):
    hi = v.astype(BF16)
    return hi, (v - hi.astype(F32)).astype(BF16)


def _dot_split(a, b_hi, b_lo):
    a_hi, a_lo = _split2(a)
    return _dot(a_hi, b_hi) + (_dot(a_lo, b_hi) + _dot(a_hi, b_lo))


def _ada_kernel(c_ref, w_ref, b_ref, o_ref):
    c = c_ref[...]
    o_ref[0] = _dot_split(c / (1.0 + jnp.exp(-c)), *_split2(w_ref[0])) + b_ref[0]


def _ada(c_pad, ada_w, ada_b):
    depth, d, n = ada_w.shape
    tn = ADA_COLS
    return pl.pallas_call(
        _ada_kernel,
        out_shape=jax.ShapeDtypeStruct((depth, 8, n), F32),
        grid=(depth, n // tn),
        in_specs=[pl.BlockSpec((8, d), lambda i, j: (0, 0)),
                  pl.BlockSpec((1, d, tn), lambda i, j: (i, 0, j)),
                  pl.BlockSpec((1, 1, tn), lambda i, j: (i, 0, j))],
        out_specs=pl.BlockSpec((1, 8, tn), lambda i, j: (i, 0, j)),
        compiler_params=_cparams(("arbitrary", "arbitrary")),
        name="ada_mod",
    )(c_pad, ada_w, ada_b.reshape(depth, 1, n))


def _nm_mm_kernel(x_ref, nw_ref, sc_ref, sh_ref, w_ref, *rest, pair_major, tail, lead_tiles,
                  lead_scale):
    if tail:
        wt_ref, o_ref, ot_ref, h_ref = rest
    else:
        o_ref, h_ref = rest

    @pl.when(pl.program_id(1) == 0)
    def _():
        h_ref[...] = _normmod(x_ref[...], nw_ref[...], sc_ref[0], sh_ref[0]).astype(BF16)

    r = _dot(h_ref[...], w_ref[...].astype(BF16))
    if lead_tiles:
        r = r * jnp.where(pl.program_id(1) < lead_tiles, lead_scale, 1.0)
    if pair_major:
        for c in range(o_ref.shape[0]):
            o_ref[c] = r[:, c * LANES:(c + 1) * LANES].astype(o_ref.dtype)
    else:
        o_ref[...] = r.astype(o_ref.dtype)

    if tail:
        @pl.when(pl.program_id(1) == pl.num_programs(1) - 1)
        def _():
            ot_ref[...] = _dot(h_ref[...], wt_ref[...].astype(BF16))


def _nm_matmul(x, nw, sc, sh, w, *, ncols, tn, out_dtype, rows_per_batch, pair_major=False,
               tail_cols=0, lead_cols=0, lead_scale=1.0):
    t, d = x.shape
    tm = MM_ROWS
    tiles_per_batch = rows_per_batch // tm
    if pair_major:
        out_shape = jax.ShapeDtypeStruct((ncols // LANES, t, LANES), out_dtype)
        out_spec = pl.BlockSpec((tn // LANES, tm, LANES), lambda i, j: (j, i, 0))
    else:
        out_shape = jax.ShapeDtypeStruct((t, ncols), out_dtype)
        out_spec = pl.BlockSpec((tm, tn), lambda i, j: (i, j))
    in_specs = [pl.BlockSpec((tm, d), lambda i, j: (i, 0)),
                pl.BlockSpec((1, d), lambda i, j: (0, 0)),
                pl.BlockSpec((1, 1, d), lambda i, j: (i // tiles_per_batch, 0, 0)),
                pl.BlockSpec((1, 1, d), lambda i, j: (i // tiles_per_batch, 0, 0)),
                pl.BlockSpec((d, tn), lambda i, j: (0, j))]
    operands = [x, nw, sc, sh, w]
    if tail_cols:
        tail_blk = ncols // tail_cols
        in_specs.append(pl.BlockSpec((d, tail_cols), lambda i, j: (0, tail_blk)))
        operands.append(w)
        out_shape = (out_shape, jax.ShapeDtypeStruct((t, tail_cols), F32))
        out_spec = (out_spec, pl.BlockSpec((tm, tail_cols), lambda i, j: (i, 0)))
    return pl.pallas_call(
        functools.partial(_nm_mm_kernel, pair_major=pair_major, tail=bool(tail_cols),
                          lead_tiles=lead_cols // tn, lead_scale=lead_scale),
        out_shape=out_shape,
        grid=(t // tm, ncols // tn),
        in_specs=in_specs,
        out_specs=out_spec,
        scratch_shapes=[pltpu.VMEM((tm, d), BF16)],
        compiler_params=_cparams(("arbitrary", "arbitrary")),
        name="norm_mod_matmul",
    )(*operands)


def _mm_resid_kernel(a_ref, w_ref, x_ref, g_ref, o_ref, *, pair_major):
    if pair_major:
        a = jnp.concatenate([a_ref[c] for c in range(a_ref.shape[0])], axis=1)
    else:
        a = a_ref[...]
    o_ref[...] = x_ref[...] + g_ref[0] * _dot(a, w_ref[...].astype(BF16))


def _mm_resid(a, w, x, g, *, rows_per_batch, pair_major=False):
    t, n = x.shape
    k = w.shape[0]
    tn = RESID_COLS
    tm = RESID_LHS_ELEMS // k
    tiles_per_batch = rows_per_batch // tm
    if pair_major:
        a_spec = pl.BlockSpec((k // LANES, tm, LANES), lambda i, j: (0, i, 0))
    else:
        a_spec = pl.BlockSpec((tm, k), lambda i, j: (i, 0))
    return pl.pallas_call(
        functools.partial(_mm_resid_kernel, pair_major=pair_major),
        out_shape=jax.ShapeDtypeStruct((t, n), F32),
        grid=(t // tm, n // tn),
        in_specs=[a_spec,
                  pl.BlockSpec((k, tn), lambda i, j: (0, j)),
                  pl.BlockSpec((tm, tn), lambda i, j: (i, j)),
                  pl.BlockSpec((1, 1, tn), lambda i, j: (i // tiles_per_batch, 0, j))],
        out_specs=pl.BlockSpec((tm, tn), lambda i, j: (i, j)),
        compiler_params=_cparams(("arbitrary", "arbitrary")),
        name="matmul_resid",
    )(a, w, x, g)


def _ssd_kernel(z_ref, x_ref, b_ref, c_ref, dtr_ref,
                cwx_ref, cwb_ref, cwc_ref, cbx_ref, cbb_ref, cbc_ref,
                alr_ref, dbr_ref, dsk_ref, nw_ref,
                o_ref,
                xc_s, bc_s, cc_s, yacc_s, st_s, cv_s):
    seq = x_ref.shape[0]
    q = SSD_Q
    nc = seq // q
    halo = CONV_HALO
    nrow = 2 * SSD_HEADS_PER_GROUP

    def conv_piece(j, base, src_ref, w_ref, bias_ref, dst_ref, lo, stage):
        cols = slice(lo, lo + LANES)
        pstart = pl.multiple_of(jnp.maximum(base - halo, 0), halo)
        nstart = pl.multiple_of(jnp.minimum(base + q, seq - halo), halo)
        stage[0:halo, :] = jnp.where(j > 0, src_ref[pl.ds(pstart, halo), cols].astype(F32), 0.0)
        stage[halo:halo + q, :] = src_ref[pl.ds(base, q), cols].astype(F32)
        stage[halo + q:, :] = jnp.where(j < nc - 1, src_ref[pl.ds(nstart, halo), cols].astype(F32), 0.0)
        acc = jnp.broadcast_to(bias_ref[:, cols], (q, LANES))
        for k in range(SSD_CONV):
            first = halo - SSD_CONV // 2 + k
            acc = acc + w_ref[k:k + 1, cols] * stage[first:first + q, :]
        dst_ref[pl.ds(base, q), cols] = _silu(acc).astype(BF16)

    def conv_chunk(j, carry):
        base = pl.multiple_of(j * q, q)
        npx = GROUP_W // LANES
        for i in range(npx):
            conv_piece(j, base, x_ref, cwx_ref, cbx_ref, xc_s, i * LANES, cv_s.at[i])
        conv_piece(j, base, b_ref, cwb_ref, cbb_ref, bc_s, 0, cv_s.at[npx])
        conv_piece(j, base, c_ref, cwc_ref, cbc_ref, cc_s, 0, cv_s.at[npx + 1])
        return carry

    lax.fori_loop(0, nc, conv_chunk, 0)

    row_i = lax.broadcasted_iota(I32, (q, q), 0)
    col_i = lax.broadcasted_iota(I32, (q, q), 1)
    lower = row_i >= col_i
    upper = row_i <= col_i
    lower_b = lower.astype(BF16)
    upper_b = upper.astype(BF16)
    left =lax.broadcasted_iota(I32, (q, LANES), 1) < SSD_HEAD_DIM
    a_row = -jnp.exp(alr_ref[...])
    pad_rows = jnp.zeros((LANES - nrow, q), F32)

    def scan_pass(direction):
        hoff = direction * SSD_HEADS_PER_GROUP
        mask = lower if direction == 0 else upper
        tri = upper_b if direction == 0 else lower_b
        edge = q - 1 if direction == 0 else 0
        st_s[...] = jnp.zeros_like(st_s)

        def chunk(t, carry):
            c = t if direction == 0 else nc - 1 - t
            base = pl.multiple_of(c * q, q)
            rows = pl.ds(base, q)
            dt_r = _softplus(dtr_ref[c] + dbr_ref[...])
            cum_r = _dot3_left(dt_r * a_row, tri) * LOG2_E
            cum_c = jnp.concatenate([cum_r, pad_rows], axis=0).T
            decdt_r = jnp.exp2(cum_r[:, edge:edge + 1] - cum_r) * dt_r
            src_r = cum_r - jnp.log2(dt_r)

            bm = bc_s[rows, :]
            cm = cc_s[rows, :]
            cb = _dot_nt(cm, bm)
            bm_t = bm.astype(F32).T
            y_off_all = _dot(cm, st_s[...].astype(BF16)) if direction == 1 else None

            ssq = jnp.zeros((q, 1), F32)
            for pp in range(SSD_HEADS_PER_GROUP // 2):
                cols = slice(pp * LANES, (pp + 1) * LANES)
                xcb = xc_s[rows, cols]
                zero_b = jnp.zeros_like(xcb)
                st_in = st_s[:, cols]
                lhs_y, lhs_s, scales = [], [], []
                for par in range(2):
                    j = hoff + 2 * pp + par
                    cum_b = jnp.broadcast_to(cum_c[:, j:j + 1], (q, q))
                    lmat = jnp.exp2(jnp.where(mask, cum_b - src_r[j:j + 1, :], NEG))
                    lhs_y.append((cb * lmat).astype(BF16))
                    lhs_s.append((bm_t * decdt_r[j:j + 1, :]).astype(BF16))
                    scales.append(jnp.exp2(cum_b))
                x_rhs = jnp.concatenate([jnp.where(left, xcb, zero_b), jnp.where(left, zero_b, xcb)],
                                        axis=0)
                sc_tile = jnp.where(left, scales[0], scales[1])
                y_off = _dot(cm, st_in.astype(BF16)) if y_off_all is None else y_off_all[:, cols]
                y = _dot(jnp.concatenate(lhs_y, axis=1), x_rhs) + y_off * sc_tile
                st_s[:, cols] = (st_in * sc_tile[edge:edge + 1, :]
                                 + _dot(jnp.concatenate(lhs_s, axis=1), x_rhs))
                if direction == 0:
                    yacc_s[rows, cols] = y
                else:
                    total = yacc_s[rows, cols] + y + xcb.astype(F32) * dsk_ref[:, cols]
                    gated = total * _silu(z_ref[rows, cols].astype(F32))
                    ssq = ssq + jnp.sum(gated * gated, axis=-1, keepdims=True)
                    yacc_s[rows, cols] = gated
            if direction == 1:
                inv = lax.rsqrt(ssq * (1.0 / GROUP_W) + EPS)
                o_ref[rows, :] = (yacc_s[rows, :] * inv * nw_ref[...]).astype(BF16)
            return carry

        lax.fori_loop(0, nc, chunk, 0, unroll=8)

    scan_pass(0)
    scan_pass(1)


def _ssd_core(zx, dt_row, conv_w, conv_b, al_row, db_row, dskip, norm_w, *, batch, seq):
    g = SSD_GROUPS
    nc = seq // SSD_Q
    xb = D_INNER // GROUP_W
    bb = (2 * D_INNER) // SSD_STATE
    cb = bb + g
    cwb = D_INNER // SSD_STATE
    cwc = cwb + g
    return pl.pallas_call(
        _ssd_kernel,
        out_shape=jax.ShapeDtypeStruct((batch * seq, D_INNER), BF16),
        grid=(batch, g),
        in_specs=[
            pl.BlockSpec((seq, GROUP_W), lambda b, i: (b, i)),
            pl.BlockSpec((seq, GROUP_W), lambda b, i: (b, xb + i)),
            pl.BlockSpec((seq, SSD_STATE), lambda b, i: (b, bb + i)),
            pl.BlockSpec((seq, SSD_STATE), lambda b, i: (b, cb + i)),
            pl.BlockSpec((None, None, nc, 2 * SSD_HEADS_PER_GROUP, SSD_Q), lambda b, i: (b, i, 0, 0, 0)),
            pl.BlockSpec((SSD_CONV, GROUP_W), lambda b, i: (0, i)),
            pl.BlockSpec((SSD_CONV, SSD_STATE), lambda b, i: (0, cwb + i)),
            pl.BlockSpec((SSD_CONV, SSD_STATE), lambda b, i: (0, cwc + i)),
            pl.BlockSpec((1, GROUP_W), lambda b, i: (0, i)),
            pl.BlockSpec((1, SSD_STATE), lambda b, i: (0, cwb + i)),
            pl.BlockSpec((1, SSD_STATE), lambda b, i: (0, cwc + i)),
            pl.BlockSpec((None, 2 * SSD_HEADS_PER_GROUP, 1), lambda b, i: (i, 0, 0)),
            pl.BlockSpec((None, 2 * SSD_HEADS_PER_GROUP, 1), lambda b, i: (i, 0, 0)),
            pl.BlockSpec((1, GROUP_W), lambda b, i: (0, i)),
            pl.BlockSpec((1, GROUP_W), lambda b, i: (0, i)),
        ],
        out_specs=pl.BlockSpec((seq, GROUP_W), lambda b, i: (b, i)),
        scratch_shapes=[pltpu.VMEM((seq, GROUP_W), BF16),
                        pltpu.VMEM((seq, SSD_STATE), BF16),
                        pltpu.VMEM((seq, SSD_STATE), BF16),
                        pltpu.VMEM((seq, GROUP_W), F32),
                        pltpu.VMEM((SSD_STATE, GROUP_W), F32),
                        pltpu.VMEM((GROUP_W // LANES + 2, SSD_Q + 2 * CONV_HALO, LANES), F32)],
        compiler_params=_cparams(("arbitrary", "arbitrary")),
        name="ssd_core",
    )(zx, zx, zx, zx, dt_row, conv_w, conv_w, conv_w, conv_b, conv_b, conv_b,
      al_row, db_row, dskip, norm_w)


def _ssd_mixer(x, mod_sc, mod_sh, mod_g, nw, w_in, conv_w, conv_b, a_log, dt_bias, d_skip, norm_w,
               w_out, *, batch, seq):
    g, r = SSD_GROUPS, SSD_HEADS_PER_GROUP
    conv_dim = conv_w.shape[1]
    zx, dt_raw = _nm_matmul(x, nw, mod_sc, mod_sh, w_in, ncols=D_INNER + conv_dim, tn=MM_COLS,
                            out_dtype=BF16, rows_per_batch=seq, tail_cols=2 * g * r)
    nc = seq // SSD_Q
    dt_row = dt_raw.reshape(batch, nc, SSD_Q, 2, g, r).transpose(0, 4, 1, 3, 5, 2)
    dt_row = dt_row.reshape(batch, g, nc, 2 * r, SSD_Q)

    def row_form(p):
        return p.reshape(2, g, r).transpose(1, 0, 2).reshape(g, 2 * r, 1)

    yn = _ssd_core(zx, dt_row, conv_w, conv_b.reshape(1, conv_dim), row_form(a_log), row_form(dt_bias),
                   jnp.repeat(d_skip, SSD_HEAD_DIM).reshape(1, D_INNER), norm_w.reshape(1, D_INNER),
                   batch=batch, seq=seq)
    return _mm_resid(yn, w_out, x, mod_g, rows_per_batch=seq)


def _bias_table_kernel(rpb_ref, o_ref):
    lane = lax.broadcasted_iota(I32, (GRID_W, LANES), 1)
    j = lax.broadcasted_iota(I32, (GRID_W, LANES), 0)
    c = lane & (GRID_W - 1)
    c0 = jnp.clip(j - WIN_W // 2, 0, GRID_W - WIN_W)
    win = (c >= c0) & (c < c0 + WIN_W)
    left = lane < GRID_W

    def one_offset(dy0, carry):
        for par in range(2):
            for m in range(WIN_H // 2):
                tiles = []
                for sub in range(2):
                    row = rpb_ref[par, pl.ds(dy0 + 2 * m + sub, 1), :]
                    shift = (sub * GRID_W - (WIN_W - 1)) % LANES
                    tiles.append(pltpu.roll(jnp.broadcast_to(row, (GRID_W, LANES)), shift, 1,
                                            stride=1, stride_axis=0))
                tile = jnp.where(win, jnp.where(left, tiles[0], tiles[1]) * LOG2_E, NEG)
                o_ref[dy0, 0, par * GRID_W:(par + 1) * GRID_W, m * LANES:(m + 1) * LANES] = tile
        return carry

    lax.fori_loop(0, o_ref.shape[0], one_offset, 0)


def _na_bias_table(rpb):
    h, ndy, ndx = rpb.shape
    rpb_p = jnp.pad(rpb, ((0, 0), (0, 2 * WIN_H - ndy), (0, LANES - ndx)))
    return pl.pallas_call(
        _bias_table_kernel,
        out_shape=jax.ShapeDtypeStruct((WIN_H, h // 2, 2 * GRID_W, WIN_H * GRID_W), F32),
        grid=(h // 2,),
        in_specs=[pl.BlockSpec((2, 2 * WIN_H, LANES), lambda p: (p, 0, 0))],
        out_specs=pl.BlockSpec((WIN_H, 1, 2 * GRID_W, WIN_H * GRID_W), lambda p: (0, p, 0, 0)),
        compiler_params=_cparams(("arbitrary",)),
        name="na_bias_table",
    )(rpb_p)


def _na_kernel(q_ref, kv_hbm, bias_ref, o_ref, kc_s, vc_s, s_s, p_s, r_s, sem, *, n_row_blocks):
    rb = pl.program_id(2)
    npairs = q_ref.shape[0]
    blk = NA_ROWS * GRID_W
    nkeys = WIN_H * GRID_W
    n_sec, n_batch = pl.num_programs(0), pl.num_programs(1)
    step = (pl.program_id(0) * n_batch + pl.program_id(1)) * n_row_blocks + rb
    nsteps = n_sec * n_batch * n_row_blocks

    def fetch(st, half):
        r = st % n_row_blocks
        hb = st // n_row_blocks
        tok0 = ((hb % n_batch) * n_row_blocks + jnp.clip(r - 1, 0, n_row_blocks - 3)) * blk
        tok0 = pl.multiple_of(tok0, blk)

        def one(sec, dst, j):
            first_pair = (sec * n_sec + hb // n_batch) * npairs
            return pltpu.make_async_copy(
                kv_hbm.at[pl.ds(first_pair, npairs), pl.ds(tok0, 3 * blk), :], dst.at[half],
                sem.at[half, j])
        return one(1, kc_s, 0), one(2, vc_s, 1)

    cur = step % 2

    @pl.when(step == 0)
    def _():
        for cp in fetch(step, cur):
            cp.start()

    @pl.when(step + 1 < nsteps)
    def _():
        for cp in fetch(step + 1, 1 - cur):
            cp.start()

    for cp in fetch(step, cur):
        cp.wait()
    first = rb == 0
    last = rb == n_row_blocks - 1
    edge = first | last
    lane = lax.broadcasted_iota(I32, (GRID_W, LANES), 1)
    left = lane < NA_HEAD_DIM

    def window(qi):
        off = jnp.where(first, 0, jnp.where(last, blk, qi * GRID_W))
        li = jnp.where(edge, NA_ROWS - 1 - qi, NA_ROWS - 1)
        return pl.multiple_of(off, GRID_W), li

    def pair_body(pp, carry):
        for qi in range(NA_ROWS):
            off, li = window(qi)
            q2 = q_ref[pp, qi * GRID_W:(qi + 1) * GRID_W, :]
            zero = jnp.zeros_like(q2)
            qs = jnp.concatenate([jnp.where(left, q2, zero), jnp.where(left, zero, q2)], axis=0)
            kw = kc_s[cur, pp, pl.ds(off, nkeys), :]
            s_s[qi] = _dot_nt(qs, kw) + bias_ref[li, pp]
        for qi in range(NA_ROWS):
            s = s_s[qi]
            p = jnp.exp2(s - jnp.max(s, axis=-1, keepdims=True))
            r_s[qi] = 1.0 / jnp.sum(p, axis=-1, keepdims=True)
            p_s[qi] = p.astype(BF16)
        for qi in range(NA_ROWS):
            off, _ = window(qi)
            pv = _dot(p_s[qi], vc_s[cur, pp, pl.ds(off, nkeys), :]) * r_s[qi]
            o = jnp.where(left, pv[0:GRID_W], pv[GRID_W:2 * GRID_W])
            o_ref[pp, qi * GRID_W:(qi + 1) * GRID_W, :] = o.astype(BF16)
        return carry

    lax.fori_loop(0, npairs, pair_body, 0, unroll=4)


def _na_attention(qkv_t, bias_tab, *, batch, seq):
    t = batch * seq
    blk = NA_ROWS * GRID_W
    nrb = seq // blk
    hp = NA_PAIRS // 2
    nsec = NA_PAIRS // hp

    return pl.pallas_call(
        functools.partial(_na_kernel, n_row_blocks=nrb),
        out_shape=jax.ShapeDtypeStruct((NA_PAIRS, t, LANES), BF16),
        grid=(nsec, batch, nrb),
        in_specs=[pl.BlockSpec((hp, blk, LANES), lambda hh, b, r: (hh, b * nrb + r, 0)),
                  pl.BlockSpec(memory_space=pl.ANY),
                  pl.BlockSpec((NA_ROWS, hp, 2 * GRID_W, WIN_H * GRID_W),
                               lambda hh, b, r: (jnp.where(r == 0, 1, 0), hh, 0, 0))],
        out_specs=pl.BlockSpec((hp, blk, LANES), lambda hh, b, r: (hh, b * nrb + r, 0)),
        scratch_shapes=[pltpu.VMEM((2, hp, 3 * blk, LANES), BF16),
                        pltpu.VMEM((2, hp, 3 * blk, LANES), BF16),
                        pltpu.VMEM((NA_ROWS, 2 * GRID_W, WIN_H * GRID_W), F32),
                        pltpu.VMEM((NA_ROWS, 2 * GRID_W, WIN_H * GRID_W), BF16),
                        pltpu.VMEM((NA_ROWS, 2 * GRID_W, 1), F32),
                        pltpu.SemaphoreType.DMA((2, 2))],
        compiler_params=_cparams(("arbitrary", "arbitrary", "arbitrary")),
        name="na_attention",
    )(qkv_t, qkv_t, bias_tab)


def _na_mixer(x, mod_sc, mod_sh, mod_g, nw, w_qkv, rpb, w_o, *, batch, seq):
    qkv_t = _nm_matmul(x, nw, mod_sc, mod_sh, w_qkv, ncols=3 * D_MODEL, tn=MM_COLS,
                       out_dtype=BF16, rows_per_batch=seq, pair_major=True,
                       lead_cols=D_MODEL, lead_scale=NA_HEAD_DIM ** -0.5 * LOG2_E)
    o_t = _na_attention(qkv_t, _na_bias_table(rpb), batch=batch, seq=seq)
    return _mm_resid(o_t, w_o, x, mod_g, rows_per_batch=seq, pair_major=True)


U32 = jnp.uint32


def _pack_halves(vb):
    n = vb.shape[1] // 2
    bits = pltpu.bitcast(vb.astype(F32), U32)
    return (bits[:, :n] >> 16) | bits[:, n:]


def _unpack_halves(w):
    return pltpu.bitcast(w << 16, F32), pltpu.bitcast(w & U32(0xFFFF0000), F32)


TOK_SUB = (D_MODEL // 2) // LANES


def _store_token_tiles(ref, packed):
    rows = packed.shape[0]
    for s in range(TOK_SUB):
        ref[pl.ds(s, rows, stride=TOK_SUB), :] = packed[:, s * LANES:(s + 1) * LANES]


def _load_token_tiles(ref):
    rows = ref.shape[0] // TOK_SUB
    return jnp.concatenate([ref[pl.ds(s, rows, stride=TOK_SUB), :] for s in range(TOK_SUB)], axis=1)


def _router_kernel(x_ref, nw_ref, sc_ref, sh_ref, wr_ref, h_ref, meta_ref, meta_t_ref, cnt_ref,
                   carry_s, whi_s, wlo_s):
    @pl.when(pl.program_id(0) == 0)
    def _():
        carry_s[...] = jnp.zeros_like(carry_s)
        whi_s[...], wlo_s[...] = _split2(wr_ref[...])

    h = _normmod(x_ref[...], nw_ref[...], sc_ref[0], sh_ref[0])
    _store_token_tiles(h_ref, _pack_halves(h.astype(BF16)))
    logits = _dot_split(h, whi_s[...], wlo_s[...])
    tm = logits.shape[0]
    lane_i = lax.broadcasted_iota(I32, logits.shape, 1)
    lane = lane_i.astype(F32)
    big = 1e9
    gl = jnp.where(lane_i < MOE_GROUPS, logits, NEG)
    gmax = jnp.max(gl, axis=1, keepdims=True)
    gsel = jnp.min(jnp.where(gl == gmax, lane, big), axis=1, keepdims=True)
    gw = 1.0 / jnp.sum(jnp.exp(gl - gmax), axis=1, keepdims=True)
    el = lane - MOE_GROUPS
    lo = gsel * MOE_EPG
    emask = (el >= lo) & (el < lo + MOE_EPG)
    e1 = jnp.where(emask, logits, NEG)
    m1 = jnp.max(e1, axis=1, keepdims=True)
    i1 = jnp.min(jnp.where(e1 == m1, el, big), axis=1, keepdims=True)
    e2 = jnp.where(emask & (el != i1), logits, NEG)
    m2 = jnp.max(e2, axis=1, keepdims=True)
    i2 = jnp.min(jnp.where(e2 == m2, el, big), axis=1, keepdims=True)
    tt = jnp.exp(m2 - m1)
    p1 = 1.0 / (1.0 + tt)
    w1 = gw * p1
    w2 = gw * (tt * p1)
    oh1 = el == i1
    oh2 = el == i2
    cnt = (oh1 | oh2).astype(F32)
    r_i = lax.broadcasted_iota(I32, (tm, tm), 0)
    c_i = lax.broadcasted_iota(I32, (tm, tm), 1)
    before = _dot((r_i > c_i).astype(BF16), cnt.astype(BF16)) + carry_s[...]
    rank1 = jnp.sum(jnp.where(oh1, before, 0.0), axis=1, keepdims=True)
    rank2 = jnp.sum(jnp.where(oh2, before, 0.0), axis=1, keepdims=True)
    carry_s[...] = carry_s[...] + jnp.sum(cnt, axis=0, keepdims=True)
    meta = jnp.zeros_like(logits)
    for pos, val in enumerate((i1, i2, w1, w2, rank1, rank2)):
        meta = jnp.where(lane_i == pos, val, meta)
    meta_ref[...] = meta
    meta_t_ref[...] = meta.T[0:meta_t_ref.shape[0], :]
    cnt_ref[...] = jnp.broadcast_to(carry_s[...], cnt_ref.shape)


def _router(x, nw, sc, sh, wr, *, rows_per_batch):
    t, d = x.shape
    tm = TOKEN_TILE
    tiles_per_batch = rows_per_batch // tm
    return pl.pallas_call(
        _router_kernel,
        out_shape=(jax.ShapeDtypeStruct((t * TOK_SUB, LANES), U32),
                   jax.ShapeDtypeStruct((t, LANES), F32),
                   jax.ShapeDtypeStruct((8, t), F32),
                   jax.ShapeDtypeStruct((8, LANES), F32)),
        grid=(t // tm,),
        in_specs=[pl.BlockSpec((tm, d), lambda i: (i, 0)),
                  pl.BlockSpec((1, d), lambda i: (0, 0)),
                  pl.BlockSpec((1, 1, d), lambda i: (i // tiles_per_batch, 0, 0)),
                  pl.BlockSpec((1, 1, d), lambda i: (i // tiles_per_batch, 0, 0)),
                  pl.BlockSpec((d, LANES), lambda i: (0, 0))],
        out_specs=(pl.BlockSpec((tm * TOK_SUB, LANES), lambda i: (i, 0)),
                   pl.BlockSpec((tm, LANES), lambda i: (i, 0)),
                   pl.BlockSpec((8, tm), lambda i: (0, i)),
                   pl.BlockSpec((8, LANES), lambda i: (0, 0))),
        scratch_shapes=[pltpu.VMEM((1, LANES), F32), pltpu.VMEM((d, LANES), BF16),
                        pltpu.VMEM((d, LANES), BF16)],
        compiler_params=_cparams(("arbitrary",)),
        name="moe_router",
    )(x, nw, sc, sh, wr)


def _slot_kernel(pstart_ref, mt_ref, o_ref):
    eid = mt_ref[0:2, :]
    start = jnp.zeros(eid.shape, I32)
    for e in range(N_EXPERTS):
        start = jnp.where(eid == float(e), pstart_ref[e], start)
    o_ref[...] = start + mt_ref[4:6, :].astype(I32)


def _slots(pstart, meta_t):
    t = meta_t.shape[1]
    return pl.pallas_call(
        _slot_kernel,
        out_shape=jax.ShapeDtypeStruct((2, t), I32),
        grid_spec=pltpu.PrefetchScalarGridSpec(
            num_scalar_prefetch=1,
            grid=(1,),
            in_specs=[pl.BlockSpec(meta_t.shape, lambda i, ps: (0, 0))],
            out_specs=pl.BlockSpec((2, t), lambda i, ps: (0, 0))),
        compiler_params=_cparams(("arbitrary",)),
        name="moe_slots",
    )(pstart, meta_t)


def _token_tile(ref, r):
    return ref.at[pl.ds(pl.multiple_of(r * TOK_SUB, TOK_SUB), TOK_SUB)]


def _dispatch_kernel(dest_ref, zflag_ref, h_ref, xs_ref, zbuf, sem, zsem):
    tm = h_ref.shape[0] // TOK_SUB
    base = pl.program_id(0) * tm
    ntok = pl.num_programs(0) * tm
    tb = zbuf.shape[0]

    @pl.when(pl.program_id(0) == 0)
    def _():
        zbuf[...] = jnp.zeros_like(zbuf)

        def zcopy(b):
            return pltpu.make_async_copy(zbuf, xs_ref.at[pl.ds(pl.multiple_of(b * tb, tb), tb)], zsem)

        def zstart(b, carry):
            @pl.when(zflag_ref[b] == 1)
            def _():
                zcopy(b).start()
            return carry

        def zwait(b, carry):
            @pl.when(zflag_ref[b] == 1)
            def _():
                zcopy(b).wait()
            return carry

        nblk = xs_ref.shape[0] // tb
        lax.fori_loop(0, nblk, zstart, 0)
        lax.fori_loop(0, nblk, zwait, 0)

    def copy(t, k):
        return pltpu.make_async_copy(_token_tile(h_ref, t),
                                     _token_tile(xs_ref, dest_ref[k * ntok + base + t]), sem)

    def issue(t, carry):
        copy(t, 0).start(priority=0)
        copy(t, 1).start(priority=1)
        return carry

    def drain(t, carry):
        copy(t, 0).wait()
        copy(t, 1).wait()
        return carry

    lax.fori_loop(0, tm, issue, 0, unroll=DMA_UNROLL)
    lax.fori_loop(0, tm, drain, 0, unroll=DMA_UNROLL)


def _dispatch(dest, zflag, h, n_slots):
    t = h.shape[0] // TOK_SUB
    tm = TOKEN_TILE
    return pl.pallas_call(
        _dispatch_kernel,
        out_shape=jax.ShapeDtypeStruct((n_slots * TOK_SUB, LANES), U32),
        grid_spec=pltpu.PrefetchScalarGridSpec(
            num_scalar_prefetch=2,
            grid=(t // tm,),
            in_specs=[pl.BlockSpec((tm * TOK_SUB, LANES), lambda i, dest, zf: (i, 0))],
            out_specs=pl.BlockSpec(memory_space=pl.ANY),
            scratch_shapes=[pltpu.VMEM((MOE_TB * TOK_SUB, LANES), U32), pltpu.SemaphoreType.DMA(()),
                            pltpu.SemaphoreType.DMA(())]),
        compiler_params=_cparams(("arbitrary",)),
        name="moe_dispatch",
    )(dest, zflag, h)


def _ffn_kernel(be_ref, nxt_ref, slot_ref, nu_ref, xs_ref, w1_hbm, w3_hbm, w2_hbm, o_ref,
                wb1, wb3, wb2, w1_s, w3_s, w2_s, sem, *, layer):
    i = pl.program_id(0)

    def fetch(e, s):
        return (pltpu.make_async_copy(w1_hbm.at[layer, e], wb1.at[s], sem.at[s, 0]),
                pltpu.make_async_copy(w3_hbm.at[layer, e], wb3.at[s], sem.at[s, 1]),
                pltpu.make_async_copy(w2_hbm.at[layer, e], wb2.at[s], sem.at[s, 2]))

    @pl.when(i < nu_ref[0])
    def _():
        e = be_ref[i]
        s = slot_ref[i]

        @pl.when((i == 0) | (e != be_ref[jnp.maximum(i - 1, 0)]))
        def _():
            @pl.when(i == 0)
            def _():
                for cp in fetch(e, s):
                    cp.start()

            for cp in fetch(e, s):
                cp.wait()

            @pl.when(nxt_ref[i] >= 0)
            def _():
                for cp in fetch(nxt_ref[i], 1 - s):
                    cp.start()

            w1_s[...] = wb1[s].astype(BF16)
            w3_s[...] = wb3[s].astype(BF16)
            w2_s[...] = wb2[s].astype(BF16)

        lo, hi = _unpack_halves(_load_token_tiles(xs_ref))
        xl, xh = lo.astype(BF16), hi.astype(BF16)
        half = xl.shape[1]
        a = _dot(xl, w1_s[0:half, :]) + _dot(xh, w1_s[half:2 * half, :])
        b = _dot(xl, w3_s[0:half, :]) + _dot(xh, w3_s[half:2 * half, :])
        hmid = _silu(a) * b
        _store_token_tiles(o_ref, _pack_halves(_dot(hmid.astype(BF16), w2_s[...]).astype(BF16)))

    @pl.when(i >= nu_ref[0])
    def _():
        o_ref[...] = jnp.zeros_like(o_ref)


def _expert_ffn(blk_expert, blk_next, blk_slot, n_used, xs, w1, w3, w2, layer):
    d, f = w1.shape[2], w1.shape[3]
    rows = MOE_TB * TOK_SUB
    nb = xs.shape[0] // rows
    hbm = pl.BlockSpec(memory_space=pl.ANY)
    return pl.pallas_call(
        functools.partial(_ffn_kernel, layer=layer),
        out_shape=jax.ShapeDtypeStruct(xs.shape, U32),
        grid_spec=pltpu.PrefetchScalarGridSpec(
            num_scalar_prefetch=4,
            grid=(nb,),
            in_specs=[pl.BlockSpec((rows, LANES),
                                   lambda i, be, nx, sl, nu: (jnp.minimum(i, nu[0] - 1), 0)),
                      hbm, hbm, hbm],
            out_specs=pl.BlockSpec((rows, LANES), lambda i, be, nx, sl, nu: (i, 0)),
            scratch_shapes=[pltpu.VMEM((2, d, f), F32), pltpu.VMEM((2, d, f), F32),
                            pltpu.VMEM((2, f, d), F32),
                            pltpu.VMEM((d, f), BF16), pltpu.VMEM((d, f), BF16),
                            pltpu.VMEM((f, d), BF16),
                            pltpu.SemaphoreType.DMA((2, 3))]),
        compiler_params=_cparams(("arbitrary",)),
        name="moe_expert_ffn",
    )(blk_expert, blk_next, blk_slot, n_used, xs, w1, w3, w2)


def _combine_kernel(dest_ref, x_ref, meta_ref, g_ref, fnw_ref, ys_ref, o_ref, buf, sem, *, final):
    tm = x_ref.shape[0]
    i = pl.program_id(0)
    nsteps = pl.num_programs(0)
    ntok = nsteps * tm

    def copy(tile, t, k):
        half = tile % 2
        return pltpu.make_async_copy(_token_tile(ys_ref, dest_ref[k * ntok + tile * tm + t]),
                                     _token_tile(buf.at[half, k], t), sem.at[half])

    def issue(tile):
        def body(t, carry):
            copy(tile, t, 0).start(priority=0)
            copy(tile, t, 1).start(priority=1)
            return carry
        lax.fori_loop(0, tm, body, 0, unroll=DMA_UNROLL)

    def drain(tile):
        def body(t, carry):
            copy(tile, t, 0).wait()
            copy(tile, t, 1).wait()
            return carry
        lax.fori_loop(0, tm, body, 0, unroll=DMA_UNROLL)

    @pl.when(i == 0)
    def _():
        issue(i)

    @pl.when(i + 1 < nsteps)
    def _():
        issue(i + 1)

    drain(i)
    cur = i % 2
    meta = meta_ref[...]
    w1, w2 = meta[:, 2:3], meta[:, 3:4]
    lo1, hi1 = _unpack_halves(_load_token_tiles(buf.at[cur, 0]))
    lo2, hi2 = _unpack_halves(_load_token_tiles(buf.at[cur, 1]))
    y = jnp.concatenate([w1 * lo1 + w2 * lo2, w1 * hi1 + w2 * hi2], axis=1)
    xn = x_ref[...] + g_ref[0] * y
    if final:
        ms = jnp.mean(xn * xn, axis=-1, keepdims=True)
        xn = xn * lax.rsqrt(ms + EPS) * fnw_ref[...]
    o_ref[...] = xn


def _combine(dest, x, meta, g, fnw, ys, *, rows_per_batch, final):
    t, d = x.shape
    tm = TOKEN_TILE
    tiles_per_batch = rows_per_batch // tm
    return pl.pallas_call(
        functools.partial(_combine_kernel, final=final),
        out_shape=jax.ShapeDtypeStruct((t, d), F32),
        grid_spec=pltpu.PrefetchScalarGridSpec(
            num_scalar_prefetch=1,
            grid=(t // tm,),
            in_specs=[pl.BlockSpec((tm, d), lambda i, dest: (i, 0)),
                      pl.BlockSpec((tm, LANES), lambda i, dest: (i, 0)),
                      pl.BlockSpec((1, 1, d), lambda i, dest: (i // tiles_per_batch, 0, 0)),
                      pl.BlockSpec((1, d), lambda i, dest: (0, 0)),
                      pl.BlockSpec(memory_space=pl.ANY)],
            out_specs=pl.BlockSpec((tm, d), lambda i, dest: (i, 0)),
            scratch_shapes=[pltpu.VMEM((2, 2, tm * TOK_SUB, LANES), U32),
                            pltpu.SemaphoreType.DMA((2,))]),
        compiler_params=_cparams(("arbitrary",)),
        name="moe_combine",
    )(dest, x, meta, g, fnw, ys)


def _hier_moe(x, mod_sc, mod_sh, mod_g, nw, w_group, w_expert, w1, w3, w2, layer, fnw, *,
              rows_per_batch, final):
    t, d = x.shape
    a = 2 * t
    tb = MOE_TB
    wr = jnp.concatenate([w_group, w_expert], axis=1)
    wr = jnp.pad(wr, ((0, 0), (0, LANES - wr.shape[1])))
    h, meta, meta_t, cnt = _router(x, nw, mod_sc, mod_sh, wr, rows_per_batch=rows_per_batch)
    ne = N_EXPERTS
    counts = cnt[0, MOE_GROUPS:MOE_GROUPS + ne].astype(I32)
    padded = ((counts + tb - 1) // tb) * tb
    pend = jnp.cumsum(padded)
    pstart = pend - padded
    dest = _slots(pstart, meta_t).reshape(a)
    nb = (a + ne * (tb - 1) + tb - 1) // tb
    n_used = (pend[-1] // tb).astype(I32)
    blk = jnp.arange(nb, dtype=I32)
    be = jnp.minimum(jnp.sum((pend[None, :] <= (blk * tb)[:, None]).astype(I32), axis=1), ne - 1)
    seg_last = jnp.any((pend[None, :] == ((blk + 1) * tb)[:, None]) & (padded[None, :] > 0), axis=1)
    zflag = (seg_last | (blk >= n_used)).astype(I32)
    nonempty = counts > 0
    seg = jnp.cumsum(nonempty.astype(I32)) - 1
    later = lax.cummin(jnp.where(nonempty, jnp.arange(ne, dtype=I32), ne), axis=0, reverse=True)
    nxt_e = jnp.concatenate([later[1:], jnp.full((1,), ne, I32)])
    nxt_e = jnp.where(nxt_e == ne, -1, nxt_e)
    xs = _dispatch(dest, zflag, h, nb * tb)
    ys = _expert_ffn(be, nxt_e[be], seg[be] % 2, n_used.reshape(1), xs, w1, w3, w2, layer)
    return _combine(dest, x, meta, mod_g, fnw, ys, rows_per_batch=rows_per_batch, final=final)


def kernel(x, c, ada_w, ada_b, norm_mix, norm_ffn, ssd_w_in, ssd_conv_w, ssd_conv_b, ssd_a_log,
           ssd_dt_bias, ssd_d, ssd_norm_w, ssd_w_out, na_w_qkv, na_rpb, na_w_o,
           moe_w_group, moe_w_expert, moe_w1, moe_w3, moe_w2, final_norm):
    batch, seq, d = x.shape
    depth = ada_w.shape[0]
    xt = x.reshape(batch * seq, d)
    c_pad = jnp.pad(c, ((0, 8 - batch), (0, 0)))
    mod = _ada(c_pad, ada_w, ada_b)[:, :batch]
    fnw = final_norm.reshape(1, d)
    for i in range(depth):
        sh1, sc1, g1, sh2, sc2, g2 = [mod[i, :, k * d:(k + 1) * d].reshape(batch, 1, d)
                                      for k in range(6)]
        j = i // 2
        nw = norm_mix[i].reshape(1, d)
        if i % 2 == 0:
            xt = _ssd_mixer(xt, sc1, sh1, g1, nw, ssd_w_in[j], ssd_conv_w[j], ssd_conv_b[j],
                            ssd_a_log[j], ssd_dt_bias[j], ssd_d[j], ssd_norm_w[j], ssd_w_out[j],
                            batch=batch, seq=seq)
        else:
            xt = _na_mixer(xt, sc1, sh1, g1, nw, na_w_qkv[j], na_rpb[j], na_w_o[j],
                           batch=batch, seq=seq)
        xt = _hier_moe(xt, sc2, sh2, g2, norm_ffn[i].reshape(1, d), moe_w_group[i], moe_w_expert[i],
                       moe_w1, moe_w3, moe_w2, i, fnw, rows_per_batch=seq,
                       final=(i == depth - 1))
    return xt.reshape(batch, seq, d)
```

```python
import functools

import jax
import jax.numpy as jnp
from jax import lax
from jax.experimental import pallas as pl
from jax.experimental.pallas import tpu as pltpu

F32 = jnp.float32
BF16 = jnp.bfloat16
I32 = jnp.int32

EPS = 1e-6
NEG = -1e30
LOG2_E = 1.4426950408889634

D_MODEL = 2048
GRID_W = 64
SSD_HEAD_DIM = 64
SSD_GROUPS = 8
SSD_HEADS_PER_GROUP = 8
SSD_STATE = 128
SSD_CONV = 5
D_INNER = 2 * D_MODEL
GROUP_W = SSD_HEADS_PER_GROUP * SSD_HEAD_DIM
SSD_Q = 128
CONV_HALO = 16
NA_HEAD_DIM = 64
NA_HEADS = D_MODEL // NA_HEAD_DIM
NA_PAIRS = NA_HEADS // 2
WIN_H = 8
WIN_W = 16
NA_ROWS = 4
MOE_GROUPS = 4
MOE_EPG = 8
N_EXPERTS = MOE_GROUPS * MOE_EPG
MOE_D_FF = D_MODEL // 4
MOE_TB = 256
DMA_UNROLL = 8
VMEM_LIMIT = 56 * 1024 * 1024
LANES = 128

MM_ROWS = 1024
MM_COLS = 1024
RESID_COLS = 512
RESID_LHS_ELEMS = 4 * 1024 * 1024
ADA_COLS = 2048
TOKEN_TILE = 256


def _cparams(sem):
    return pltpu.CompilerParams(dimension_semantics=sem, vmem_limit_bytes=VMEM_LIMIT)


def _silu(v):
    return v * pl.reciprocal(1.0 + jnp.exp(-v), approx=True)


def _softplus(v):
    return jnp.maximum(v, 0.0) + jnp.log1p(jnp.exp(-jnp.abs(v)))


def _split3(v):
    hi = v.astype(BF16)
    r1 = v - hi.astype(F32)
    mid = r1.astype(BF16)
    lo = (r1 - mid.astype(F32)).astype(BF16)
    return hi, mid, lo


def _dot(a, b):
    return jnp.dot(a, b, preferred_element_type=F32)


def _dot_nt(a, b):
    return lax.dot_general(a, b, (((1,), (1,)), ((), ())), preferred_element_type=F32)


def _dot3_left(v, sel):
    hi, mid, lo = _split3(v)
    return _dot(hi, sel) + _dot(mid, sel) + _dot(lo, sel)


def _normmod(x, nw, sc, sh):
    ms = jnp.mean(x * x, axis=-1, keepdims=True)
    return (x * lax.rsqrt(ms + EPS) * nw) * (1.0 + sc) + sh


def _split2(v):
    hi = v.astype(BF16)
    return hi, (v - hi.astype(F32)).astype(BF16)


def _dot_split(a, b_hi, b_lo):
    a_hi, a_lo = _split2(a)
    return _dot(a_hi, b_hi) + (_dot(a_lo, b_hi) + _dot(a_hi, b_lo))


def _ada_kernel(c_ref, w_ref, b_ref, o_ref):
    c = c_ref[...]
    o_ref[0] = _dot_split(c / (1.0 + jnp.exp(-c)), *_split2(w_ref[0])) + b_ref[0]


def _ada(c_pad, ada_w, ada_b):
    depth, d, n = ada_w.shape
    tn = ADA_COLS
    return pl.pallas_call(
        _ada_kernel,
        out_shape=jax.ShapeDtypeStruct((depth, 8, n), F32),
        grid=(depth, n // tn),
        in_specs=[pl.BlockSpec((8, d), lambda i, j: (0, 0)),
                  pl.BlockSpec((1, d, tn), lambda i, j: (i, 0, j)),
                  pl.BlockSpec((1, 1, tn), lambda i, j: (i, 0, j))],
        out_specs=pl.BlockSpec((1, 8, tn), lambda i, j: (i, 0, j)),
        compiler_params=_cparams(("arbitrary", "arbitrary")),
        name="ada_mod",
    )(c_pad, ada_w, ada_b.reshape(depth, 1, n))


def _nm_mm_kernel(x_ref, nw_ref, sc_ref, sh_ref, w_ref, *rest, pair_major, tail, lead_tiles,
                  lead_scale):
    if tail:
        wt_ref, o_ref, ot_ref, h_ref = rest
    else:
        o_ref, h_ref = rest

    @pl.when(pl.program_id(1) == 0)
    def _():
        h_ref[...] = _normmod(x_ref[...], nw_ref[...], sc_ref[0], sh_ref[0]).astype(BF16)

    r = _dot(h_ref[...], w_ref[...].astype(BF16))
    if lead_tiles:
        r = r * jnp.where(pl.program_id(1) < lead_tiles, lead_scale, 1.0)
    if pair_major:
        for c in range(o_ref.shape[0]):
            o_ref[c] = r[:, c * LANES:(c + 1) * LANES].astype(o_ref.dtype)
    else:
        o_ref[...] = r.astype(o_ref.dtype)

    if tail:
        @pl.when(pl.program_id(1) == pl.num_programs(1) - 1)
        def _():
            ot_ref[...] = _dot(h_ref[...], wt_ref[...].astype(BF16))


def _nm_matmul(x, nw, sc, sh, w, *, ncols, tn, out_dtype, rows_per_batch, pair_major=False,
               tail_cols=0, lead_cols=0, lead_scale=1.0):
    t, d = x.shape
    tm = MM_ROWS
    tiles_per_batch = rows_per_batch // tm
    if pair_major:
        out_shape = jax.ShapeDtypeStruct((ncols // LANES, t, LANES), out_dtype)
        out_spec = pl.BlockSpec((tn // LANES, tm, LANES), lambda i, j: (j, i, 0))
    else:
        out_shape = jax.ShapeDtypeStruct((t, ncols), out_dtype)
        out_spec = pl.BlockSpec((tm, tn), lambda i, j: (i, j))
    in_specs = [pl.BlockSpec((tm, d), lambda i, j: (i, 0)),
                pl.BlockSpec((1, d), lambda i, j: (0, 0)),
                pl.BlockSpec((1, 1, d), lambda i, j: (i // tiles_per_batch, 0, 0)),
                pl.BlockSpec((1, 1, d), lambda i, j: (i // tiles_per_batch, 0, 0)),
                pl.BlockSpec((d, tn), lambda i, j: (0, j))]
    operands = [x, nw, sc, sh, w]
    if tail_cols:
        tail_blk = ncols // tail_cols
        in_specs.append(pl.BlockSpec((d, tail_cols), lambda i, j: (0, tail_blk)))
        operands.append(w)
        out_shape = (out_shape, jax.ShapeDtypeStruct((t, tail_cols), F32))
        out_spec = (out_spec, pl.BlockSpec((tm, tail_cols), lambda i, j: (i, 0)))
    return pl.pallas_call(
        functools.partial(_nm_mm_kernel, pair_major=pair_major, tail=bool(tail_cols),
                          lead_tiles=lead_cols // tn, lead_scale=lead_scale),
        out_shape=out_shape,
        grid=(t // tm, ncols // tn),
        in_specs=in_specs,
        out_specs=out_spec,
        scratch_shapes=[pltpu.VMEM((tm, d), BF16)],
        compiler_params=_cparams(("arbitrary", "arbitrary")),
        name="norm_mod_matmul",
    )(*operands)


def _mm_resid_kernel(a_ref, w_ref, x_ref, g_ref, o_ref, *, pair_major):
    if pair_major:
        a = jnp.concatenate([a_ref[c] for c in range(a_ref.shape[0])], axis=1)
    else:
        a = a_ref[...]
    o_ref[...] = x_ref[...] + g_ref[0] * _dot(a, w_ref[...].astype(BF16))


def _mm_resid(a, w, x, g, *, rows_per_batch, pair_major=False):
    t, n = x.shape
    k = w.shape[0]
    tn = RESID_COLS
    tm = RESID_LHS_ELEMS // k
    tiles_per_batch = rows_per_batch // tm
    if pair_major:
        a_spec = pl.BlockSpec((k // LANES, tm, LANES), lambda i, j: (0, i, 0))
    else:
        a_spec = pl.BlockSpec((tm, k), lambda i, j: (i, 0))
    return pl.pallas_call(
        functools.partial(_mm_resid_kernel, pair_major=pair_major),
        out_shape=jax.ShapeDtypeStruct((t, n), F32),
        grid=(t // tm, n // tn),
        in_specs=[a_spec,
                  pl.BlockSpec((k, tn), lambda i, j: (0, j)),
                  pl.BlockSpec((tm, tn), lambda i, j: (i, j)),
                  pl.BlockSpec((1, 1, tn), lambda i, j: (i // tiles_per_batch, 0, j))],
        out_specs=pl.BlockSpec((tm, tn), lambda i, j: (i, j)),
        compiler_params=_cparams(("arbitrary", "arbitrary")),
        name="matmul_resid",
    )(a, w, x, g)


def _ssd_kernel(z_ref, x_ref, b_ref, c_ref, dtr_ref,
                cwx_ref, cwb_ref, cwc_ref, cbx_ref, cbb_ref, cbc_ref,
                alr_ref, dbr_ref, dsk_ref, nw_ref,
                o_ref,
                xc_s, bc_s, cc_s, yacc_s, st_s, cv_s):
    seq = x_ref.shape[0]
    q = SSD_Q
    nc = seq // q
    halo = CONV_HALO
    nrow = 2 * SSD_HEADS_PER_GROUP

    def conv_piece(j, base, src_ref, w_ref, bias_ref, dst_ref, lo, stage):
        cols = slice(lo, lo + LANES)
        pstart = pl.multiple_of(jnp.maximum(base - halo, 0), halo)
        nstart = pl.multiple_of(jnp.minimum(base + q, seq - halo), halo)
        stage[0:halo, :] = jnp.where(j > 0, src_ref[pl.ds(pstart, halo), cols].astype(F32), 0.0)
        stage[halo:halo + q, :] = src_ref[pl.ds(base, q), cols].astype(F32)
        stage[halo + q:, :] = jnp.where(j < nc - 1, src_ref[pl.ds(nstart, halo), cols].astype(F32), 0.0)
        acc = jnp.broadcast_to(bias_ref[:, cols], (q, LANES))
        for k in range(SSD_CONV):
            first = halo - SSD_CONV // 2 + k
            acc = acc + w_ref[k:k + 1, cols] * stage[first:first + q, :]
        dst_ref[pl.ds(base, q), cols] = _silu(acc).astype(BF16)

    def conv_chunk(j, carry):
        base = pl.multiple_of(j * q, q)
        npx = GROUP_W // LANES
        for i in range(npx):
            conv_piece(j, base, x_ref, cwx_ref, cbx_ref, xc_s, i * LANES, cv_s.at[i])
        conv_piece(j, base, b_ref, cwb_ref, cbb_ref, bc_s, 0, cv_s.at[npx])
        conv_piece(j, base, c_ref, cwc_ref, cbc_ref, cc_s, 0, cv_s.at[npx + 1])
        return carry

    lax.fori_loop(0, nc, conv_chunk, 0)

    row_i = lax.broadcasted_iota(I32, (q, q), 0)
    col_i = lax.broadcasted_iota(I32, (q, q), 1)
    lower = row_i >= col_i
    upper = row_i <= col_i
    lower_b = lower.astype(BF16)
    upper_b = upper.astype(BF16)
    left =lax.broadcasted_iota(I32, (q, LANES), 1) < SSD_HEAD_DIM
    a_row = -jnp.exp(alr_ref[...])
    pad_rows = jnp.zeros((LANES - nrow, q), F32)

    def scan_pass(direction):
        hoff = direction * SSD_HEADS_PER_GROUP
        mask = lower if direction == 0 else upper
        tri = upper_b if direction == 0 else lower_b
        edge = q - 1 if direction == 0 else 0
        st_s[...] = jnp.zeros_like(st_s)

        def chunk(t, carry):
            c = t if direction == 0 else nc - 1 - t
            base = pl.multiple_of(c * q, q)
            rows = pl.ds(base, q)
            dt_r = _softplus(dtr_ref[c] + dbr_ref[...])
            cum_r = _dot3_left(dt_r * a_row, tri) * LOG2_E
            cum_c = jnp.concatenate([cum_r, pad_rows], axis=0).T
            decdt_r = jnp.exp2(cum_r[:, edge:edge + 1] - cum_r) * dt_r
            src_r = cum_r - jnp.log2(dt_r)

            bm = bc_s[rows, :]
            cm = cc_s[rows, :]
            cb = _dot_nt(cm, bm)
            bm_t = bm.astype(F32).T
            y_off_all = _dot(cm, st_s[...].astype(BF16)) if direction == 1 else None

            ssq = jnp.zeros((q, 1), F32)
            for pp in range(SSD_HEADS_PER_GROUP // 2):
                cols = slice(pp * LANES, (pp + 1) * LANES)
                xcb = xc_s[rows, cols]
                zero_b = jnp.zeros_like(xcb)
                st_in = st_s[:, cols]
                lhs_y, lhs_s, scales = [], [], []
                for par in range(2):
                    j = hoff + 2 * pp + par
                    cum_b = jnp.broadcast_to(cum_c[:, j:j + 1], (q, q))
                    lmat = jnp.exp2(jnp.where(mask, cum_b - src_r[j:j + 1, :], NEG))
                    lhs_y.append((cb * lmat).astype(BF16))
                    lhs_s.append((bm_t * decdt_r[j:j + 1, :]).astype(BF16))
                    scales.append(jnp.exp2(cum_b))
                x_rhs = jnp.concatenate([jnp.where(left, xcb, zero_b), jnp.where(left, zero_b, xcb)],
                                        axis=0)
                sc_tile = jnp.where(left, scales[0], scales[1])
                y_off = _dot(cm, st_in.astype(BF16)) if y_off_all is None else y_off_all[:, cols]
                y = _dot(jnp.concatenate(lhs_y, axis=1), x_rhs) + y_off * sc_tile
                st_s[:, cols] = (st_in * sc_tile[edge:edge + 1, :]
                                 + _dot(jnp.concatenate(lhs_s, axis=1), x_rhs))
                if direction == 0:
                    yacc_s[rows, cols] = y
                else:
                    total = yacc_s[rows, cols] + y + xcb.astype(F32) * dsk_ref[:, cols]
                    gated = total * _silu(z_ref[rows, cols].astype(F32))
                    ssq = ssq + jnp.sum(gated * gated, axis=-1, keepdims=True)
                    yacc_s[rows, cols] = gated
            if direction == 1:
                inv = lax.rsqrt(ssq * (1.0 / GROUP_W) + EPS)
                o_ref[rows, :] = (yacc_s[rows, :] * inv * nw_ref[...]).astype(BF16)
            return carry

        lax.fori_loop(0, nc, chunk, 0, unroll=16)

    scan_pass(0)
    scan_pass(1)


def _ssd_core(zx, dt_row, conv_w, conv_b, al_row, db_row, dskip, norm_w, *, batch, seq):
    g = SSD_GROUPS
    nc = seq // SSD_Q
    xb = D_INNER // GROUP_W
    bb = (2 * D_INNER) // SSD_STATE
    cb = bb + g
    cwb = D_INNER // SSD_STATE
    cwc = cwb + g
    return pl.pallas_call(
        _ssd_kernel,
        out_shape=jax.ShapeDtypeStruct((batch * seq, D_INNER), BF16),
        grid=(batch, g),
        in_specs=[
            pl.BlockSpec((seq, GROUP_W), lambda b, i: (b, i)),
            pl.BlockSpec((seq, GROUP_W), lambda b, i: (b, xb + i)),
            pl.BlockSpec((seq, SSD_STATE), lambda b, i: (b, bb + i)),
            pl.BlockSpec((seq, SSD_STATE), lambda b, i: (b, cb + i)),
            pl.BlockSpec((None, None, nc, 2 * SSD_HEADS_PER_GROUP, SSD_Q), lambda b, i: (b, i, 0, 0, 0)),
            pl.BlockSpec((SSD_CONV, GROUP_W), lambda b, i: (0, i)),
            pl.BlockSpec((SSD_CONV, SSD_STATE), lambda b, i: (0, cwb + i)),
            pl.BlockSpec((SSD_CONV, SSD_STATE), lambda b, i: (0, cwc + i)),
            pl.BlockSpec((1, GROUP_W), lambda b, i: (0, i)),
            pl.BlockSpec((1, SSD_STATE), lambda b, i: (0, cwb + i)),
            pl.BlockSpec((1, SSD_STATE), lambda b, i: (0, cwc + i)),
            pl.BlockSpec((None, 2 * SSD_HEADS_PER_GROUP, 1), lambda b, i: (i, 0, 0)),
            pl.BlockSpec((None, 2 * SSD_HEADS_PER_GROUP, 1), lambda b, i: (i, 0, 0)),
            pl.BlockSpec((1, GROUP_W), lambda b, i: (0, i)),
            pl.BlockSpec((1, GROUP_W), lambda b, i: (0, i)),
        ],
        out_specs=pl.BlockSpec((seq, GROUP_W), lambda b, i: (b, i)),
        scratch_shapes=[pltpu.VMEM((seq, GROUP_W), BF16),
                        pltpu.VMEM((seq, SSD_STATE), BF16),
                        pltpu.VMEM((seq, SSD_STATE), BF16),
                        pltpu.VMEM((seq, GROUP_W), F32),
                        pltpu.VMEM((SSD_STATE, GROUP_W), F32),
                        pltpu.VMEM((GROUP_W // LANES + 2, SSD_Q + 2 * CONV_HALO, LANES), F32)],
        compiler_params=_cparams(("arbitrary", "arbitrary")),
        name="ssd_core",
    )(zx, zx, zx, zx, dt_row, conv_w, conv_w, conv_w, conv_b, conv_b, conv_b,
      al_row, db_row, dskip, norm_w)


def _ssd_mixer(x, mod_sc, mod_sh, mod_g, nw, w_in, conv_w, conv_b, a_log, dt_bias, d_skip, norm_w,
               w_out, *, batch, seq):
    g, r = SSD_GROUPS, SSD_HEADS_PER_GROUP
    conv_dim = conv_w.shape[1]
    zx, dt_raw = _nm_matmul(x, nw, mod_sc, mod_sh, w_in, ncols=D_INNER + conv_dim, tn=MM_COLS,
                            out_dtype=BF16, rows_per_batch=seq, tail_cols=2 * g * r)
    nc = seq // SSD_Q
    dt_row = dt_raw.reshape(batch, nc, SSD_Q, 2, g, r).transpose(0, 4, 1, 3, 5, 2)
    dt_row = dt_row.reshape(batch, g, nc, 2 * r, SSD_Q)

    def row_form(p):
        return p.reshape(2, g, r).transpose(1, 0, 2).reshape(g, 2 * r, 1)

    yn = _ssd_core(zx, dt_row, conv_w, conv_b.reshape(1, conv_dim), row_form(a_log), row_form(dt_bias),
                   jnp.repeat(d_skip, SSD_HEAD_DIM).reshape(1, D_INNER), norm_w.reshape(1, D_INNER),
                   batch=batch, seq=seq)
    return _mm_resid(yn, w_out, x, mod_g, rows_per_batch=seq)


def _bias_table_kernel(rpb_ref, o_ref):
    lane = lax.broadcasted_iota(I32, (GRID_W, LANES), 1)
    j = lax.broadcasted_iota(I32, (GRID_W, LANES), 0)
    c = lane & (GRID_W - 1)
    c0 = jnp.clip(j - WIN_W // 2, 0, GRID_W - WIN_W)
    win = (c >= c0) & (c < c0 + WIN_W)
    left = lane < GRID_W

    n_off = o_ref.shape[0]
    for par in range(2):
        for dy in range(2 * WIN_H - 2):
            halves = []
            for sub in range(2):
                row = rpb_ref[par, dy + sub:dy + sub + 1, :]
                shift = (sub * GRID_W - (WIN_W - 1)) % LANES
                halves.append(pltpu.roll(jnp.broadcast_to(row, (GRID_W, LANES)), shift, 1,
                                         stride=1, stride_axis=0))
            tile = jnp.where(win, jnp.where(left, halves[0], halves[1]) * LOG2_E, NEG)
            for m in range(WIN_H // 2):
                dy0 = dy - 2 * m
                if 0 <= dy0 < n_off:
                    o_ref[dy0, 0, par * GRID_W:(par + 1) * GRID_W, m * LANES:(m + 1) * LANES] = tile


def _na_bias_table(rpb):
    h, ndy, ndx = rpb.shape
    rpb_p = jnp.pad(rpb, ((0, 0), (0, 2 * WIN_H - ndy), (0, LANES - ndx)))
    return pl.pallas_call(
        _bias_table_kernel,
        out_shape=jax.ShapeDtypeStruct((WIN_H, h // 2, 2 * GRID_W, WIN_H * GRID_W), F32),
        grid=(h // 2,),
        in_specs=[pl.BlockSpec((2, 2 * WIN_H, LANES), lambda p: (p, 0, 0))],
        out_specs=pl.BlockSpec((WIN_H, 1, 2 * GRID_W, WIN_H * GRID_W), lambda p: (0, p, 0, 0)),
        compiler_params=_cparams(("arbitrary",)),
        name="na_bias_table",
    )(rpb_p)


def _na_kernel(q_ref, kv_hbm, bias_ref, o_ref, kc_s, vc_s, s_s, p_s, r_s, sem, *, n_row_blocks):
    rb = pl.program_id(2)
    npairs = q_ref.shape[0]
    blk = NA_ROWS * GRID_W
    nkeys = WIN_H * GRID_W
    n_sec, n_batch = pl.num_programs(0), pl.num_programs(1)
    step = (pl.program_id(0) * n_batch + pl.program_id(1)) * n_row_blocks + rb
    nsteps = n_sec * n_batch * n_row_blocks

    def fetch(st, half):
        r = st % n_row_blocks
        hb = st // n_row_blocks
        tok0 = ((hb % n_batch) * n_row_blocks + jnp.clip(r - 1, 0, n_row_blocks - 3)) * blk
        tok0 = pl.multiple_of(tok0, blk)

        def one(sec, dst, j):
            first_pair = (sec * n_sec + hb // n_batch) * npairs
            return pltpu.make_async_copy(
                kv_hbm.at[pl.ds(first_pair, npairs), pl.ds(tok0, 3 * blk), :], dst.at[half],
                sem.at[half, j])
        return one(1, kc_s, 0), one(2, vc_s, 1)

    cur = step % 2

    @pl.when(step == 0)
    def _():
        for cp in fetch(step, cur):
            cp.start()

    @pl.when(step + 1 < nsteps)
    def _():
        for cp in fetch(step + 1, 1 - cur):
            cp.start()

    for cp in fetch(step, cur):
        cp.wait()
    first = rb == 0
    last = rb == n_row_blocks - 1
    edge = first | last
    lane = lax.broadcasted_iota(I32, (GRID_W, LANES), 1)
    left = lane < NA_HEAD_DIM

    def window(qi):
        off = jnp.where(first, 0, jnp.where(last, blk, qi * GRID_W))
        li = jnp.where(edge, NA_ROWS - 1 - qi, NA_ROWS - 1)
        return pl.multiple_of(off, GRID_W), li

    def pair_body(pp, carry):
        for qi in range(NA_ROWS):
            off, li = window(qi)
            q2 = q_ref[pp, qi * GRID_W:(qi + 1) * GRID_W, :]
            zero = jnp.zeros_like(q2)
            qs = jnp.concatenate([jnp.where(left, q2, zero), jnp.where(left, zero, q2)], axis=0)
            kw = kc_s[cur, pp, pl.ds(off, nkeys), :]
            s_s[qi] = _dot_nt(qs, kw) + bias_ref[li, pp]
        for qi in range(NA_ROWS):
            s = s_s[qi]
            p = jnp.exp2(s - jnp.max(s, axis=-1, keepdims=True))
            r_s[qi] = 1.0 / jnp.sum(p, axis=-1, keepdims=True)
            p_s[qi] = p.astype(BF16)
        for qi in range(NA_ROWS):
            off, _ = window(qi)
            pv = _dot(p_s[qi], vc_s[cur, pp, pl.ds(off, nkeys), :]) * r_s[qi]
            o = jnp.where(left, pv[0:GRID_W], pv[GRID_W:2 * GRID_W])
            o_ref[pp, qi * GRID_W:(qi + 1) * GRID_W, :] = o.astype(BF16)
        return carry

    lax.fori_loop(0, npairs, pair_body, 0, unroll=4)


def _na_attention(qkv_t, bias_tab, *, batch, seq):
    t = batch * seq
    blk = NA_ROWS * GRID_W
    nrb = seq // blk
    hp = NA_PAIRS // 2
    nsec = NA_PAIRS // hp

    return pl.pallas_call(
        functools.partial(_na_kernel, n_row_blocks=nrb),
        out_shape=jax.ShapeDtypeStruct((NA_PAIRS, t, LANES), BF16),
        grid=(nsec, batch, nrb),
        in_specs=[pl.BlockSpec((hp, blk, LANES), lambda hh, b, r: (hh, b * nrb + r, 0)),
                  pl.BlockSpec(memory_space=pl.ANY),
                  pl.BlockSpec((NA_ROWS, hp, 2 * GRID_W, WIN_H * GRID_W),
                               lambda hh, b, r: (jnp.where(r == 0, 1, 0), hh, 0, 0))],
        out_specs=pl.BlockSpec((hp, blk, LANES), lambda hh, b, r: (hh, b * nrb + r, 0)),
        scratch_shapes=[pltpu.VMEM((2, hp, 3 * blk, LANES), BF16),
                        pltpu.VMEM((2, hp, 3 * blk, LANES), BF16),
                        pltpu.VMEM((NA_ROWS, 2 * GRID_W, WIN_H * GRID_W), F32),
                        pltpu.VMEM((NA_ROWS, 2 * GRID_W, WIN_H * GRID_W), BF16),
                        pltpu.VMEM((NA_ROWS, 2 * GRID_W, 1), F32),
                        pltpu.SemaphoreType.DMA((2, 2))],
        compiler_params=_cparams(("arbitrary", "arbitrary", "arbitrary")),
        name="na_attention",
    )(qkv_t, qkv_t, bias_tab)


def _na_mixer(x, mod_sc, mod_sh, mod_g, nw, w_qkv, rpb, w_o, *, batch, seq):
    qkv_t = _nm_matmul(x, nw, mod_sc, mod_sh, w_qkv, ncols=3 * D_MODEL, tn=MM_COLS,
                       out_dtype=BF16, rows_per_batch=seq, pair_major=True,
                       lead_cols=D_MODEL, lead_scale=NA_HEAD_DIM ** -0.5 * LOG2_E)
    o_t = _na_attention(qkv_t, _na_bias_table(rpb), batch=batch, seq=seq)
    return _mm_resid(o_t, w_o, x, mod_g, rows_per_batch=seq, pair_major=True)


U32 = jnp.uint32


def _pack_halves(vb):
    n = vb.shape[1] // 2
    bits = pltpu.bitcast(vb.astype(F32), U32)
    return (bits[:, :n] >> 16) | bits[:, n:]


def _unpack_halves(w):
    return pltpu.bitcast(w << 16, F32), pltpu.bitcast(w & U32(0xFFFF0000), F32)


TOK_SUB = (D_MODEL // 2) // LANES


def _store_token_tiles(ref, packed):
    rows = packed.shape[0]
    for s in range(TOK_SUB):
        ref[pl.ds(s, rows, stride=TOK_SUB), :] = packed[:, s * LANES:(s + 1) * LANES]


def _load_token_tiles(ref):
    rows = ref.shape[0] // TOK_SUB
    return jnp.concatenate([ref[pl.ds(s, rows, stride=TOK_SUB), :] for s in range(TOK_SUB)], axis=1)


def _router_kernel(x_ref, nw_ref, sc_ref, sh_ref, wr_ref, h_ref, meta_ref, meta_t_ref, cnt_ref,
                   carry_s, whi_s, wlo_s):
    @pl.when(pl.program_id(0) == 0)
    def _():
        carry_s[...] = jnp.zeros_like(carry_s)
        whi_s[...], wlo_s[...] = _split2(wr_ref[...])

    h = _normmod(x_ref[...], nw_ref[...], sc_ref[0], sh_ref[0])
    _store_token_tiles(h_ref, _pack_halves(h.astype(BF16)))
    logits = _dot_split(h, whi_s[...], wlo_s[...])
    tm = logits.shape[0]
    lane_i = lax.broadcasted_iota(I32, logits.shape, 1)
    lane = lane_i.astype(F32)
    big = 1e9
    gl = jnp.where(lane_i < MOE_GROUPS, logits, NEG)
    gmax = jnp.max(gl, axis=1, keepdims=True)
    gsel = jnp.min(jnp.where(gl == gmax, lane, big), axis=1, keepdims=True)
    gw = 1.0 / jnp.sum(jnp.exp(gl - gmax), axis=1, keepdims=True)
    el = lane - MOE_GROUPS
    lo = gsel * MOE_EPG
    emask = (el >= lo) & (el < lo + MOE_EPG)
    e1 = jnp.where(emask, logits, NEG)
    m1 = jnp.max(e1, axis=1, keepdims=True)
    i1 = jnp.min(jnp.where(e1 == m1, el, big), axis=1, keepdims=True)
    e2 = jnp.where(emask & (el != i1), logits, NEG)
    m2 = jnp.max(e2, axis=1, keepdims=True)
    i2 = jnp.min(jnp.where(e2 == m2, el, big), axis=1, keepdims=True)
    tt = jnp.exp(m2 - m1)
    p1 = 1.0 / (1.0 + tt)
    w1 = gw * p1
    w2 = gw * (tt * p1)
    oh1 = el == i1
    oh2 = el == i2
    cnt = (oh1 | oh2).astype(F32)
    r_i = lax.broadcasted_iota(I32, (tm, tm), 0)
    c_i = lax.broadcasted_iota(I32, (tm, tm), 1)
    before = _dot((r_i > c_i).astype(BF16), cnt.astype(BF16)) + carry_s[...]
    rank1 = jnp.sum(jnp.where(oh1, before, 0.0), axis=1, keepdims=True)
    rank2 = jnp.sum(jnp.where(oh2, before, 0.0), axis=1, keepdims=True)
    carry_s[...] = carry_s[...] + jnp.sum(cnt, axis=0, keepdims=True)
    meta = jnp.zeros_like(logits)
    for pos, val in enumerate((i1, i2, w1, w2, rank1, rank2)):
        meta = jnp.where(lane_i == pos, val, meta)
    meta_ref[...] = meta
    meta_t_ref[...] = meta.T[0:meta_t_ref.shape[0], :]
    cnt_ref[...] = jnp.broadcast_to(carry_s[...], cnt_ref.shape)


def _router(x, nw, sc, sh, wr, *, rows_per_batch):
    t, d = x.shape
    tm = TOKEN_TILE
    tiles_per_batch = rows_per_batch // tm
    return pl.pallas_call(
        _router_kernel,
        out_shape=(jax.ShapeDtypeStruct((t * TOK_SUB, LANES), U32),
                   jax.ShapeDtypeStruct((t, LANES), F32),
                   jax.ShapeDtypeStruct((8, t), F32),
                   jax.ShapeDtypeStruct((8, LANES), F32)),
        grid=(t // tm,),
        in_specs=[pl.BlockSpec((tm, d), lambda i: (i, 0)),
                  pl.BlockSpec((1, d), lambda i: (0, 0)),
                  pl.BlockSpec((1, 1, d), lambda i: (i // tiles_per_batch, 0, 0)),
                  pl.BlockSpec((1, 1, d), lambda i: (i // tiles_per_batch, 0, 0)),
                  pl.BlockSpec((d, LANES), lambda i: (0, 0))],
        out_specs=(pl.BlockSpec((tm * TOK_SUB, LANES), lambda i: (i, 0)),
                   pl.BlockSpec((tm, LANES), lambda i: (i, 0)),
                   pl.BlockSpec((8, tm), lambda i: (0, i)),
                   pl.BlockSpec((8, LANES), lambda i: (0, 0))),
        scratch_shapes=[pltpu.VMEM((1, LANES), F32), pltpu.VMEM((d, LANES), BF16),
                        pltpu.VMEM((d, LANES), BF16)],
        compiler_params=_cparams(("arbitrary",)),
        name="moe_router",
    )(x, nw, sc, sh, wr)


def _slot_kernel(pstart_ref, mt_ref, o_ref):
    eid = mt_ref[0:2, :]
    start = jnp.zeros(eid.shape, I32)
    for e in range(N_EXPERTS):
        start = jnp.where(eid == float(e), pstart_ref[e], start)
    o_ref[...] = start + mt_ref[4:6, :].astype(I32)


def _slots(pstart, meta_t):
    t = meta_t.shape[1]
    return pl.pallas_call(
        _slot_kernel,
        out_shape=jax.ShapeDtypeStruct((2, t), I32),
        grid_spec=pltpu.PrefetchScalarGridSpec(
            num_scalar_prefetch=1,
            grid=(1,),
            in_specs=[pl.BlockSpec(meta_t.shape, lambda i, ps: (0, 0))],
            out_specs=pl.BlockSpec((2, t), lambda i, ps: (0, 0))),
        compiler_params=_cparams(("arbitrary",)),
        name="moe_slots",
    )(pstart, meta_t)


def _token_tile(ref, r):
    return ref.at[pl.ds(pl.multiple_of(r * TOK_SUB, TOK_SUB), TOK_SUB)]


def _dispatch_kernel(dest_ref, zflag_ref, h_ref, xs_ref, zbuf, sem, zsem):
    tm = h_ref.shape[0] // TOK_SUB
    base = pl.program_id(0) * tm
    ntok = pl.num_programs(0) * tm
    tb = zbuf.shape[0]

    @pl.when(pl.program_id(0) == 0)
    def _():
        zbuf[...] = jnp.zeros_like(zbuf)

        def zcopy(b):
            return pltpu.make_async_copy(zbuf, xs_ref.at[pl.ds(pl.multiple_of(b * tb, tb), tb)], zsem)

        def zstart(b, carry):
            @pl.when(zflag_ref[b] == 1)
            def _():
                zcopy(b).start()
            return carry

        def zwait(b, carry):
            @pl.when(zflag_ref[b] == 1)
            def _():
                zcopy(b).wait()
            return carry

        nblk = xs_ref.shape[0] // tb
        lax.fori_loop(0, nblk, zstart, 0)
        lax.fori_loop(0, nblk, zwait, 0)

    def copy(t, k):
        return pltpu.make_async_copy(_token_tile(h_ref, t),
                                     _token_tile(xs_ref, dest_ref[k * ntok + base + t]), sem)

    def issue(t, carry):
        copy(t, 0).start(priority=0)
        copy(t, 1).start(priority=1)
        return carry

    def drain(t, carry):
        copy(t, 0).wait()
        copy(t, 1).wait()
        return carry

    lax.fori_loop(0, tm, issue, 0, unroll=DMA_UNROLL)
    lax.fori_loop(0, tm, drain, 0, unroll=DMA_UNROLL)


def _dispatch(dest, zflag, h, n_slots):
    t = h.shape[0] // TOK_SUB
    tm = TOKEN_TILE
    return pl.pallas_call(
        _dispatch_kernel,
        out_shape=jax.ShapeDtypeStruct((n_slots * TOK_SUB, LANES), U32),
        grid_spec=pltpu.PrefetchScalarGridSpec(
            num_scalar_prefetch=2,
            grid=(t // tm,),
            in_specs=[pl.BlockSpec((tm * TOK_SUB, LANES), lambda i, dest, zf: (i, 0))],
            out_specs=pl.BlockSpec(memory_space=pl.ANY),
            scratch_shapes=[pltpu.VMEM((MOE_TB * TOK_SUB, LANES), U32), pltpu.SemaphoreType.DMA(()),
                            pltpu.SemaphoreType.DMA(())]),
        compiler_params=_cparams(("arbitrary",)),
        name="moe_dispatch",
    )(dest, zflag, h)


def _ffn_kernel(be_ref, nxt_ref, slot_ref, nu_ref, xs_ref, w1_hbm, w3_hbm, w2_hbm, o_ref,
                wb1, wb3, wb2, w1_s, w3_s, w2_s, sem, *, layer):
    i = pl.program_id(0)

    def fetch(e, s):
        return (pltpu.make_async_copy(w1_hbm.at[layer, e], wb1.at[s], sem.at[s, 0]),
                pltpu.make_async_copy(w3_hbm.at[layer, e], wb3.at[s], sem.at[s, 1]),
                pltpu.make_async_copy(w2_hbm.at[layer, e], wb2.at[s], sem.at[s, 2]))

    @pl.when(i < nu_ref[0])
    def _():
        e = be_ref[i]
        s = slot_ref[i]

        @pl.when((i == 0) | (e != be_ref[jnp.maximum(i - 1, 0)]))
        def _():
            @pl.when(i == 0)
            def _():
                for cp in fetch(e, s):
                    cp.start()

            for cp in fetch(e, s):
                cp.wait()

            @pl.when(nxt_ref[i] >= 0)
            def _():
                for cp in fetch(nxt_ref[i], 1 - s):
                    cp.start()

            w1_s[...] = wb1[s].astype(BF16)
            w3_s[...] = wb3[s].astype(BF16)
            w2_s[...] = wb2[s].astype(BF16)

        lo, hi = _unpack_halves(_load_token_tiles(xs_ref))
        xl, xh = lo.astype(BF16), hi.astype(BF16)
        half = xl.shape[1]
        a = _dot(xl, w1_s[0:half, :]) + _dot(xh, w1_s[half:2 * half, :])
        b = _dot(xl, w3_s[0:half, :]) + _dot(xh, w3_s[half:2 * half, :])
        hmid = _silu(a) * b
        _store_token_tiles(o_ref, _pack_halves(_dot(hmid.astype(BF16), w2_s[...]).astype(BF16)))

    @pl.when(i >= nu_ref[0])
    def _():
        o_ref[...] = jnp.zeros_like(o_ref)


def _expert_ffn(blk_expert, blk_next, blk_slot, n_used, xs, w1, w3, w2, layer):
    d, f = w1.shape[2], w1.shape[3]
    rows = MOE_TB * TOK_SUB
    nb = xs.shape[0] // rows
    hbm = pl.BlockSpec(memory_space=pl.ANY)
    return pl.pallas_call(
        functools.partial(_ffn_kernel, layer=layer),
        out_shape=jax.ShapeDtypeStruct(xs.shape, U32),
        grid_spec=pltpu.PrefetchScalarGridSpec(
            num_scalar_prefetch=4,
            grid=(nb,),
            in_specs=[pl.BlockSpec((rows, LANES),
                                   lambda i, be, nx, sl, nu: (jnp.minimum(i, nu[0] - 1), 0)),
                      hbm, hbm, hbm],
            out_specs=pl.BlockSpec((rows, LANES), lambda i, be, nx, sl, nu: (i, 0)),
            scratch_shapes=[pltpu.VMEM((2, d, f), F32), pltpu.VMEM((2, d, f), F32),
                            pltpu.VMEM((2, f, d), F32),
                            pltpu.VMEM((d, f), BF16), pltpu.VMEM((d, f), BF16),
                            pltpu.VMEM((f, d), BF16),
                            pltpu.SemaphoreType.DMA((2, 3))]),
        compiler_params=_cparams(("arbitrary",)),
        name="moe_expert_ffn",
    )(blk_expert, blk_next, blk_slot, n_used, xs, w1, w3, w2)


def _combine_kernel(dest_ref, x_ref, meta_ref, g_ref, fnw_ref, ys_ref, o_ref, buf, sem, *, final):
    tm = x_ref.shape[0]
    i = pl.program_id(0)
    nsteps = pl.num_programs(0)
    ntok = nsteps * tm

    def copy(tile, t, k):
        half = tile % 2
        return pltpu.make_async_copy(_token_tile(ys_ref, dest_ref[k * ntok + tile * tm + t]),
                                     _token_tile(buf.at[half, k], t), sem.at[half])

    def issue(tile):
        def body(t, carry):
            copy(tile, t, 0).start(priority=0)
            copy(tile, t, 1).start(priority=1)
            return carry
        lax.fori_loop(0, tm, body, 0, unroll=DMA_UNROLL)

    def drain(tile):
        def body(t, carry):
            copy(tile, t, 0).wait()
            copy(tile, t, 1).wait()
            return carry
        lax.fori_loop(0, tm, body, 0, unroll=DMA_UNROLL)

    @pl.when(i == 0)
    def _():
        issue(i)

    @pl.when(i + 1 < nsteps)
    def _():
        issue(i + 1)

    drain(i)
    cur = i % 2
    meta = meta_ref[...]
    w1, w2 = meta[:, 2:3], meta[:, 3:4]
    lo1, hi1 = _unpack_halves(_load_token_tiles(buf.at[cur, 0]))
    lo2, hi2 = _unpack_halves(_load_token_tiles(buf.at[cur, 1]))
    y = jnp.concatenate([w1 * lo1 + w2 * lo2, w1 * hi1 + w2 * hi2], axis=1)
    xn = x_ref[...] + g_ref[0] * y
    if final:
        ms = jnp.mean(xn * xn, axis=-1, keepdims=True)
        xn = xn * lax.rsqrt(ms + EPS) * fnw_ref[...]
    o_ref[...] = xn


def _combine(dest, x, meta, g, fnw, ys, *, rows_per_batch, final):
    t, d = x.shape
    tm = TOKEN_TILE
    tiles_per_batch = rows_per_batch // tm
    return pl.pallas_call(
        functools.partial(_combine_kernel, final=final),
        out_shape=jax.ShapeDtypeStruct((t, d), F32),
        grid_spec=pltpu.PrefetchScalarGridSpec(
            num_scalar_prefetch=1,
            grid=(t // tm,),
            in_specs=[pl.BlockSpec((tm, d), lambda i, dest: (i, 0)),
                      pl.BlockSpec((tm, LANES), lambda i, dest: (i, 0)),
                      pl.BlockSpec((1, 1, d), lambda i, dest: (i // tiles_per_batch, 0, 0)),
                      pl.BlockSpec((1, d), lambda i, dest: (0, 0)),
                      pl.BlockSpec(memory_space=pl.ANY)],
            out_specs=pl.BlockSpec((tm, d), lambda i, dest: (i, 0)),
            scratch_shapes=[pltpu.VMEM((2, 2, tm * TOK_SUB, LANES), U32),
                            pltpu.SemaphoreType.DMA((2,))]),
        compiler_params=_cparams(("arbitrary",)),
        name="moe_combine",
    )(dest, x, meta, g, fnw, ys)


def _hier_moe(x, mod_sc, mod_sh, mod_g, nw, w_group, w_expert, w1, w3, w2, layer, fnw, *,
              rows_per_batch, final):
    t, d = x.shape
    a = 2 * t
    tb = MOE_TB
    wr = jnp.concatenate([w_group, w_expert], axis=1)
    wr = jnp.pad(wr, ((0, 0), (0, LANES - wr.shape[1])))
    h, meta, meta_t, cnt = _router(x, nw, mod_sc, mod_sh, wr, rows_per_batch=rows_per_batch)
    ne = N_EXPERTS
    counts = cnt[0, MOE_GROUPS:MOE_GROUPS + ne].astype(I32)
    padded = ((counts + tb - 1) // tb) * tb
    pend = jnp.cumsum(padded)
    pstart = pend - padded
    dest = _slots(pstart, meta_t).reshape(a)
    nb = (a + ne * (tb - 1) + tb - 1) // tb
    n_used = (pend[-1] // tb).astype(I32)
    blk = jnp.arange(nb, dtype=I32)
    be = jnp.minimum(jnp.sum((pend[None, :] <= (blk * tb)[:, None]).astype(I32), axis=1), ne - 1)
    seg_last = jnp.any((pend[None, :] == ((blk + 1) * tb)[:, None]) & (padded[None, :] > 0), axis=1)
    zflag = (seg_last | (blk >= n_used)).astype(I32)
    nonempty = counts > 0
    seg = jnp.cumsum(nonempty.astype(I32)) - 1
    later = lax.cummin(jnp.where(nonempty, jnp.arange(ne, dtype=I32), ne), axis=0, reverse=True)
    nxt_e = jnp.concatenate([later[1:], jnp.full((1,), ne, I32)])
    nxt_e = jnp.where(nxt_e == ne, -1, nxt_e)
    xs = _dispatch(dest, zflag, h, nb * tb)
    ys = _expert_ffn(be, nxt_e[be], seg[be] % 2, n_used.reshape(1), xs, w1, w3, w2, layer)
    return _combine(dest, x, meta, mod_g, fnw, ys, rows_per_batch=rows_per_batch, final=final)


def kernel(x, c, ada_w, ada_b, norm_mix, norm_ffn, ssd_w_in, ssd_conv_w, ssd_conv_b, ssd_a_log,
           ssd_dt_bias, ssd_d, ssd_norm_w, ssd_w_out, na_w_qkv, na_rpb, na_w_o,
           moe_w_group, moe_w_expert, moe_w1, moe_w3, moe_w2, final_norm):
    batch, seq, d = x.shape
    depth = ada_w.shape[0]
    xt = x.reshape(batch * seq, d)
    c_pad = jnp.pad(c, ((0, 8 - batch), (0, 0)))
    mod = _ada(c_pad, ada_w, ada_b)[:, :batch]
    fnw = final_norm.reshape(1, d)
    for i in range(depth):
        sh1, sc1, g1, sh2, sc2, g2 = [mod[i, :, k * d:(k + 1) * d].reshape(batch, 1, d)
                                      for k in range(6)]
        j = i // 2
        nw = norm_mix[i].reshape(1, d)
        if i % 2 == 0:
            xt = _ssd_mixer(xt, sc1, sh1, g1, nw, ssd_w_in[j], ssd_conv_w[j], ssd_conv_b[j],
                            ssd_a_log[j], ssd_dt_bias[j], ssd_d[j], ssd_norm_w[j], ssd_w_out[j],
                            batch=batch, seq=seq)
        else:
            xt = _na_mixer(xt, sc1, sh1, g1, nw, na_w_qkv[j], na_rpb[j], na_w_o[j],
                           batch=batch, seq=seq)
        xt = _hier_moe(xt, sc2, sh2, g2, norm_ffn[i].reshape(1, d), moe_w_group[i], moe_w_expert[i],
                       moe_w1, moe_w3, moe_w2, i, fnw, rows_per_batch=seq,
                       final=(i == depth - 1))
    return xt.reshape(batch, seq, d)
```

```python
import functools

import jax
import jax.numpy as jnp
from jax import lax
from jax.experimental import pallas as pl
from jax.experimental.pallas import tpu as pltpu

F32 = jnp.float32
BF16 = jnp.bfloat16
I32 = jnp.int32

EPS = 1e-6
NEG = -1e30
LOG2_E = 1.4426950408889634

D_MODEL = 2048
GRID_W = 64
SSD_HEAD_DIM = 64
SSD_GROUPS = 8
SSD_HEADS_PER_GROUP = 8
SSD_STATE = 128
SSD_CONV = 5
D_INNER = 2 * D_MODEL
GROUP_W = SSD_HEADS_PER_GROUP * SSD_HEAD_DIM
SSD_Q = 128
CONV_HALO = 16
NA_HEAD_DIM = 64
NA_HEADS = D_MODEL // NA_HEAD_DIM
NA_PAIRS = NA_HEADS // 2
WIN_H = 8
WIN_W = 16
NA_ROWS = 4
NA_STAGE = 2
MOE_GROUPS = 4
MOE_EPG = 8
N_EXPERTS = MOE_GROUPS * MOE_EPG
MOE_D_FF = D_MODEL // 4
MOE_TB = 256
DMA_UNROLL = 8
VMEM_LIMIT = 56 * 1024 * 1024
LANES = 128

MM_ROWS = 1024
MM_COLS = 1024
RESID_COLS = 512
RESID_LHS_ELEMS = 4 * 1024 * 1024
ADA_COLS = 2048
TOKEN_TILE = 256


def _cparams(sem):
    return pltpu.CompilerParams(dimension_semantics=sem, vmem_limit_bytes=VMEM_LIMIT)


def _silu(v):
    return v * pl.reciprocal(1.0 + jnp.exp(-v), approx=True)


def _softplus(v):
    return jnp.maximum(v, 0.0) + jnp.log1p(jnp.exp(-jnp.abs(v)))


def _split3(v):
    hi = v.astype(BF16)
    r1 = v - hi.astype(F32)
    mid = r1.astype(BF16)
    lo = (r1 - mid.astype(F32)).astype(BF16)
    return hi, mid, lo


def _dot(a, b):
    return jnp.dot(a, b, preferred_element_type=F32)


def _dot_nt(a, b):
    return lax.dot_general(a, b, (((1,), (1,)), ((), ())), preferred_element_type=F32)


def _dot3_left(v, sel):
    hi, mid, lo = _split3(v)
    return _dot(hi, sel) + _dot(mid, sel) + _dot(lo, sel)


def _normmod(x, nw, sc, sh):
    ms = jnp.mean(x * x, axis=-1, keepdims=True)
    return (x * lax.rsqrt(ms + EPS) * nw) * (1.0 + sc) + sh


def _split2(v):
    hi = v.astype(BF16)
    return hi, (v - hi.astype(F32)).astype(BF16)


def _dot_split(a, b_hi, b_lo):
    a_hi, a_lo = _split2(a)
    return _dot(a_hi, b_hi) + (_dot(a_lo, b_hi) + _dot(a_hi, b_lo))


def _ada_kernel(c_ref, w_ref, b_ref, o_ref):
    c = c_ref[...]
    o_ref[0] = _dot_split(c / (1.0 + jnp.exp(-c)), *_split2(w_ref[0])) + b_ref[0]


def _ada(c_pad, ada_w, ada_b):
    depth, d, n = ada_w.shape
    tn = ADA_COLS
    return pl.pallas_call(
        _ada_kernel,
        out_shape=jax.ShapeDtypeStruct((depth, 8, n), F32),
        grid=(depth, n // tn),
        in_specs=[pl.BlockSpec((8, d), lambda i, j: (0, 0)),
                  pl.BlockSpec((1, d, tn), lambda i, j: (i, 0, j)),
                  pl.BlockSpec((1, 1, tn), lambda i, j: (i, 0, j))],
        out_specs=pl.BlockSpec((1, 8, tn), lambda i, j: (i, 0, j)),
        compiler_params=_cparams(("arbitrary", "arbitrary")),
        name="ada_mod",
    )(c_pad, ada_w, ada_b.reshape(depth, 1, n))


def _nm_mm_kernel(x_ref, nw_ref, sc_ref, sh_ref, w_ref, *rest, pair_major, tail, lead_tiles,
                  lead_scale):
    if tail:
        wt_ref, o_ref, ot_ref, h_ref = rest
    else:
        o_ref, h_ref = rest

    @pl.when(pl.program_id(1) == 0)
    def _():
        h_ref[...] = _normmod(x_ref[...], nw_ref[...], sc_ref[0], sh_ref[0]).astype(BF16)

    r = _dot(h_ref[...], w_ref[...].astype(BF16))
    if lead_tiles:
        r = r * jnp.where(pl.program_id(1) < lead_tiles, lead_scale, 1.0)
    if pair_major:
        for c in range(o_ref.shape[0]):
            o_ref[c] = r[:, c * LANES:(c + 1) * LANES].astype(o_ref.dtype)
    else:
        o_ref[...] = r.astype(o_ref.dtype)

    if tail:
        @pl.when(pl.program_id(1) == pl.num_programs(1) - 1)
        def _():
            ot_ref[...] = _dot(h_ref[...], wt_ref[...].astype(BF16))


def _nm_matmul(x, nw, sc, sh, w, *, ncols, tn, out_dtype, rows_per_batch, pair_major=False,
               tail_cols=0, lead_cols=0, lead_scale=1.0):
    t, d = x.shape
    tm = MM_ROWS
    tiles_per_batch = rows_per_batch // tm
    if pair_major:
        out_shape = jax.ShapeDtypeStruct((ncols // LANES, t, LANES), out_dtype)
        out_spec = pl.BlockSpec((tn // LANES, tm, LANES), lambda i, j: (j, i, 0))
    else:
        out_shape = jax.ShapeDtypeStruct((t, ncols), out_dtype)
        out_spec = pl.BlockSpec((tm, tn), lambda i, j: (i, j))
    in_specs = [pl.BlockSpec((tm, d), lambda i, j: (i, 0)),
                pl.BlockSpec((1, d), lambda i, j: (0, 0)),
                pl.BlockSpec((1, 1, d), lambda i, j: (i // tiles_per_batch, 0, 0)),
                pl.BlockSpec((1, 1, d), lambda i, j: (i // tiles_per_batch, 0, 0)),
                pl.BlockSpec((d, tn), lambda i, j: (0, j))]
    operands = [x, nw, sc, sh, w]
    if tail_cols:
        tail_blk = ncols // tail_cols
        in_specs.append(pl.BlockSpec((d, tail_cols), lambda i, j: (0, tail_blk)))
        operands.append(w)
        out_shape = (out_shape, jax.ShapeDtypeStruct((t, tail_cols), F32))
        out_spec = (out_spec, pl.BlockSpec((tm, tail_cols), lambda i, j: (i, 0)))
    return pl.pallas_call(
        functools.partial(_nm_mm_kernel, pair_major=pair_major, tail=bool(tail_cols),
                          lead_tiles=lead_cols // tn, lead_scale=lead_scale),
        out_shape=out_shape,
        grid=(t // tm, ncols // tn),
        in_specs=in_specs,
        out_specs=out_spec,
        scratch_shapes=[pltpu.VMEM((tm, d), BF16)],
        compiler_params=_cparams(("arbitrary", "arbitrary")),
        name="norm_mod_matmul",
    )(*operands)


def _mm_resid_kernel(a_ref, w_ref, x_ref, g_ref, o_ref, *, pair_major):
    if pair_major:
        a = jnp.concatenate([a_ref[c] for c in range(a_ref.shape[0])], axis=1)
    else:
        a = a_ref[...]
    o_ref[...] = x_ref[...] + g_ref[0] * _dot(a, w_ref[...].astype(BF16))


def _mm_resid(a, w, x, g, *, rows_per_batch, pair_major=False):
    t, n = x.shape
    k = w.shape[0]
    tn = RESID_COLS
    tm = RESID_LHS_ELEMS // k
    tiles_per_batch = rows_per_batch // tm
    if pair_major:
        a_spec = pl.BlockSpec((k // LANES, tm, LANES), lambda i, j: (0, i, 0))
    else:
        a_spec = pl.BlockSpec((tm, k), lambda i, j: (i, 0))
    return pl.pallas_call(
        functools.partial(_mm_resid_kernel, pair_major=pair_major),
        out_shape=jax.ShapeDtypeStruct((t, n), F32),
        grid=(t // tm, n // tn),
        in_specs=[a_spec,
                  pl.BlockSpec((k, tn), lambda i, j: (0, j)),
                  pl.BlockSpec((tm, tn), lambda i, j: (i, j)),
                  pl.BlockSpec((1, 1, tn), lambda i, j: (i // tiles_per_batch, 0, j))],
        out_specs=pl.BlockSpec((tm, tn), lambda i, j: (i, j)),
        compiler_params=_cparams(("arbitrary", "arbitrary")),
        name="matmul_resid",
    )(a, w, x, g)


def _ssd_kernel(z_ref, x_ref, b_ref, c_ref, dtr_ref,
                cwx_ref, cwb_ref, cwc_ref, cbx_ref, cbb_ref, cbc_ref,
                alr_ref, dbr_ref, dsk_ref, nw_ref,
                o_ref,
                xc_s, bc_s, cc_s, yacc_s, st_s, cv_s):
    seq = x_ref.shape[0]
    q = SSD_Q
    nc = seq // q
    halo = CONV_HALO
    nrow = 2 * SSD_HEADS_PER_GROUP

    def conv_piece(j, base, src_ref, w_ref, bias_ref, dst_ref, lo, stage):
        cols = slice(lo, lo + LANES)
        pstart = pl.multiple_of(jnp.maximum(base - halo, 0), halo)
        nstart = pl.multiple_of(jnp.minimum(base + q, seq - halo), halo)
        stage[0:halo, :] = jnp.where(j > 0, src_ref[pl.ds(pstart, halo), cols].astype(F32), 0.0)
        stage[halo:halo + q, :] = src_ref[pl.ds(base, q), cols].astype(F32)
        stage[halo + q:, :] = jnp.where(j < nc - 1, src_ref[pl.ds(nstart, halo), cols].astype(F32), 0.0)
        acc = jnp.broadcast_to(bias_ref[:, cols], (q, LANES))
        for k in range(SSD_CONV):
            first = halo - SSD_CONV // 2 + k
            acc = acc + w_ref[k:k + 1, cols] * stage[first:first + q, :]
        dst_ref[pl.ds(base, q), cols] = _silu(acc).astype(BF16)

    def conv_chunk(j, carry):
        base = pl.multiple_of(j * q, q)
        npx = GROUP_W // LANES
        for i in range(npx):
            conv_piece(j, base, x_ref, cwx_ref, cbx_ref, xc_s, i * LANES, cv_s.at[i])
        conv_piece(j, base, b_ref, cwb_ref, cbb_ref, bc_s, 0, cv_s.at[npx])
        conv_piece(j, base, c_ref, cwc_ref, cbc_ref, cc_s, 0, cv_s.at[npx + 1])
        return carry

    lax.fori_loop(0, nc, conv_chunk, 0)

    row_i = lax.broadcasted_iota(I32, (q, q), 0)
    col_i = lax.broadcasted_iota(I32, (q, q), 1)
    lower = row_i >= col_i
    upper = row_i <= col_i
    lower_b = lower.astype(BF16)
    upper_b = upper.astype(BF16)
    left =lax.broadcasted_iota(I32, (q, LANES), 1) < SSD_HEAD_DIM
    a_row = -jnp.exp(alr_ref[...])
    pad_rows = jnp.zeros((LANES - nrow, q), F32)

    def scan_pass(direction):
        hoff = direction * SSD_HEADS_PER_GROUP
        mask = lower if direction == 0 else upper
        tri = upper_b if direction == 0 else lower_b
        edge = q - 1 if direction == 0 else 0
        st_s[...] = jnp.zeros_like(st_s)

        def chunk(t, carry):
            c = t if direction == 0 else nc - 1 - t
            base = pl.multiple_of(c * q, q)
            rows = pl.ds(base, q)
            dt_r = _softplus(dtr_ref[c] + dbr_ref[...])
            cum_r = _dot3_left(dt_r * a_row, tri) * LOG2_E
            cum_c = jnp.concatenate([cum_r, pad_rows], axis=0).T
            decdt_r = jnp.exp2(cum_r[:, edge:edge + 1] - cum_r) * dt_r
            src_r = cum_r - jnp.log2(dt_r)

            bm = bc_s[rows, :]
            cm = cc_s[rows, :]
            cb = _dot_nt(cm, bm)
            bm_t = bm.astype(F32).T
            y_off_all = _dot(cm, st_s[...].astype(BF16)) if direction == 1 else None

            ssq = jnp.zeros((q, 1), F32)
            for pp in range(SSD_HEADS_PER_GROUP // 2):
                cols = slice(pp * LANES, (pp + 1) * LANES)
                xcb = xc_s[rows, cols]
                zero_b = jnp.zeros_like(xcb)
                st_in = st_s[:, cols]
                lhs_y, lhs_s, scales = [], [], []
                for par in range(2):
                    j = hoff + 2 * pp + par
                    cum_b = jnp.broadcast_to(cum_c[:, j:j + 1], (q, q))
                    lmat = jnp.exp2(jnp.where(mask, cum_b - src_r[j:j + 1, :], NEG))
                    lhs_y.append((cb * lmat).astype(BF16))
                    lhs_s.append((bm_t * decdt_r[j:j + 1, :]).astype(BF16))
                    scales.append(jnp.exp2(cum_b))
                x_rhs = jnp.concatenate([jnp.where(left, xcb, zero_b), jnp.where(left, zero_b, xcb)],
                                        axis=0)
                sc_tile = jnp.where(left, scales[0], scales[1])
                y_off = _dot(cm, st_in.astype(BF16)) if y_off_all is None else y_off_all[:, cols]
                y = _dot(jnp.concatenate(lhs_y, axis=1), x_rhs) + y_off * sc_tile
                st_s[:, cols] = (st_in * sc_tile[edge:edge + 1, :]
                                 + _dot(jnp.concatenate(lhs_s, axis=1), x_rhs))
                if direction == 0:
                    yacc_s[rows, cols] = y
                else:
                    total = yacc_s[rows, cols] + y + xcb.astype(F32) * dsk_ref[:, cols]
                    gated = total * _silu(z_ref[rows, cols].astype(F32))
                    ssq = ssq + jnp.sum(gated * gated, axis=-1, keepdims=True)
                    yacc_s[rows, cols] = gated
            if direction == 1:
                inv = lax.rsqrt(ssq * (1.0 / GROUP_W) + EPS)
                o_ref[rows, :] = (yacc_s[rows, :] * inv * nw_ref[...]).astype(BF16)
            return carry

        lax.fori_loop(0, nc, chunk, 0, unroll=16)

    scan_pass(0)
    scan_pass(1)


def _ssd_core(zx, dt_row, conv_w, conv_b, al_row, db_row, dskip, norm_w, *, batch, seq):
    g = SSD_GROUPS
    nc = seq // SSD_Q
    xb = D_INNER // GROUP_W
    bb = (2 * D_INNER) // SSD_STATE
    cb = bb + g
    cwb = D_INNER // SSD_STATE
    cwc = cwb + g
    return pl.pallas_call(
        _ssd_kernel,
        out_shape=jax.ShapeDtypeStruct((batch * seq, D_INNER), BF16),
        grid=(batch, g),
        in_specs=[
            pl.BlockSpec((seq, GROUP_W), lambda b, i: (b, i)),
            pl.BlockSpec((seq, GROUP_W), lambda b, i: (b, xb + i)),
            pl.BlockSpec((seq, SSD_STATE), lambda b, i: (b, bb + i)),
            pl.BlockSpec((seq, SSD_STATE), lambda b, i: (b, cb + i)),
            pl.BlockSpec((None, None, nc, 2 * SSD_HEADS_PER_GROUP, SSD_Q), lambda b, i: (b, i, 0, 0, 0)),
            pl.BlockSpec((SSD_CONV, GROUP_W), lambda b, i: (0, i)),
            pl.BlockSpec((SSD_CONV, SSD_STATE), lambda b, i: (0, cwb + i)),
            pl.BlockSpec((SSD_CONV, SSD_STATE), lambda b, i: (0, cwc + i)),
            pl.BlockSpec((1, GROUP_W), lambda b, i: (0, i)),
            pl.BlockSpec((1, SSD_STATE), lambda b, i: (0, cwb + i)),
            pl.BlockSpec((1, SSD_STATE), lambda b, i: (0, cwc + i)),
            pl.BlockSpec((None, 2 * SSD_HEADS_PER_GROUP, 1), lambda b, i: (i, 0, 0)),
            pl.BlockSpec((None, 2 * SSD_HEADS_PER_GROUP, 1), lambda b, i: (i, 0, 0)),
            pl.BlockSpec((1, GROUP_W), lambda b, i: (0, i)),
            pl.BlockSpec((1, GROUP_W), lambda b, i: (0, i)),
        ],
        out_specs=pl.BlockSpec((seq, GROUP_W), lambda b, i: (b, i)),
        scratch_shapes=[pltpu.VMEM((seq, GROUP_W), BF16),
                        pltpu.VMEM((seq, SSD_STATE), BF16),
                        pltpu.VMEM((seq, SSD_STATE), BF16),
                        pltpu.VMEM((seq, GROUP_W), F32),
                        pltpu.VMEM((SSD_STATE, GROUP_W), F32),
                        pltpu.VMEM((GROUP_W // LANES + 2, SSD_Q + 2 * CONV_HALO, LANES), F32)],
        compiler_params=_cparams(("arbitrary", "arbitrary")),
        name="ssd_core",
    )(zx, zx, zx, zx, dt_row, conv_w, conv_w, conv_w, conv_b, conv_b, conv_b,
      al_row, db_row, dskip, norm_w)


def _ssd_mixer(x, mod_sc, mod_sh, mod_g, nw, w_in, conv_w, conv_b, a_log, dt_bias, d_skip, norm_w,
               w_out, *, batch, seq):
    g, r = SSD_GROUPS, SSD_HEADS_PER_GROUP
    conv_dim = conv_w.shape[1]
    zx, dt_raw = _nm_matmul(x, nw, mod_sc, mod_sh, w_in, ncols=D_INNER + conv_dim, tn=MM_COLS,
                            out_dtype=BF16, rows_per_batch=seq, tail_cols=2 * g * r)
    nc = seq // SSD_Q
    dt_row = dt_raw.reshape(batch, nc, SSD_Q, 2, g, r).transpose(0, 4, 1, 3, 5, 2)
    dt_row = dt_row.reshape(batch, g, nc, 2 * r, SSD_Q)

    def row_form(p):
        return p.reshape(2, g, r).transpose(1, 0, 2).reshape(g, 2 * r, 1)

    yn = _ssd_core(zx, dt_row, conv_w, conv_b.reshape(1, conv_dim), row_form(a_log), row_form(dt_bias),
                   jnp.repeat(d_skip, SSD_HEAD_DIM).reshape(1, D_INNER), norm_w.reshape(1, D_INNER),
                   batch=batch, seq=seq)
    return _mm_resid(yn, w_out, x, mod_g, rows_per_batch=seq)


def _bias_table_kernel(rpb_ref, o_ref):
    lane = lax.broadcasted_iota(I32, (GRID_W, LANES), 1)
    j = lax.broadcasted_iota(I32, (GRID_W, LANES), 0)
    c = lane & (GRID_W - 1)
    c0 = jnp.clip(j - WIN_W // 2, 0, GRID_W - WIN_W)
    win = (c >= c0) & (c < c0 + WIN_W)
    left = lane < GRID_W

    n_off = o_ref.shape[0]
    for par in range(2):
        for dy in range(2 * WIN_H - 2):
            halves = []
            for sub in range(2):
                row = rpb_ref[par, dy + sub:dy + sub + 1, :]
                shift = (sub * GRID_W - (WIN_W - 1)) % LANES
                halves.append(pltpu.roll(jnp.broadcast_to(row, (GRID_W, LANES)), shift, 1,
                                         stride=1, stride_axis=0))
            tile = jnp.where(win, jnp.where(left, halves[0], halves[1]) * LOG2_E, NEG)
            for m in range(WIN_H // 2):
                dy0 = dy - 2 * m
                if 0 <= dy0 < n_off:
                    o_ref[dy0, 0, par * GRID_W:(par + 1) * GRID_W, m * LANES:(m + 1) * LANES] = tile


def _na_bias_table(rpb):
    h, ndy, ndx = rpb.shape
    rpb_p = jnp.pad(rpb, ((0, 0), (0, 2 * WIN_H - ndy), (0, LANES - ndx)))
    return pl.pallas_call(
        _bias_table_kernel,
        out_shape=jax.ShapeDtypeStruct((WIN_H, h // 2, 2 * GRID_W, WIN_H * GRID_W), F32),
        grid=(h // 2,),
        in_specs=[pl.BlockSpec((2, 2 * WIN_H, LANES), lambda p: (p, 0, 0))],
        out_specs=pl.BlockSpec((WIN_H, 1, 2 * GRID_W, WIN_H * GRID_W), lambda p: (0, p, 0, 0)),
        compiler_params=_cparams(("arbitrary",)),
        name="na_bias_table",
    )(rpb_p)


def _na_kernel(q_ref, kv_hbm, bias_ref, o_ref, kc_s, vc_s, s_s, p_s, r_s, sem, *, n_row_blocks):
    rb = pl.program_id(2)
    npairs = q_ref.shape[0]
    blk = NA_ROWS * GRID_W
    nkeys = WIN_H * GRID_W
    n_sec, n_batch = pl.num_programs(0), pl.num_programs(1)
    step = (pl.program_id(0) * n_batch + pl.program_id(1)) * n_row_blocks + rb
    nsteps = n_sec * n_batch * n_row_blocks

    def fetch(st, half):
        r = st % n_row_blocks
        hb = st // n_row_blocks
        tok0 = ((hb % n_batch) * n_row_blocks + jnp.clip(r - 1, 0, n_row_blocks - 3)) * blk
        tok0 = pl.multiple_of(tok0, blk)

        def one(sec, dst, j):
            first_pair = (sec * n_sec + hb // n_batch) * npairs
            return pltpu.make_async_copy(
                kv_hbm.at[pl.ds(first_pair, npairs), pl.ds(tok0, 3 * blk), :], dst.at[half],
                sem.at[half, j])
        return one(1, kc_s, 0), one(2, vc_s, 1)

    cur = step % 2

    @pl.when(step == 0)
    def _():
        for cp in fetch(step, cur):
            cp.start()

    @pl.when(step + 1 < nsteps)
    def _():
        for cp in fetch(step + 1, 1 - cur):
            cp.start()

    for cp in fetch(step, cur):
        cp.wait()
    first = rb == 0
    last = rb == n_row_blocks - 1
    edge = first | last
    lane = lax.broadcasted_iota(I32, (GRID_W, LANES), 1)
    left = lane < NA_HEAD_DIM

    def window(qi):
        off = jnp.where(first, 0, jnp.where(last, blk, qi * GRID_W))
        li = jnp.where(edge, NA_ROWS - 1 - qi, NA_ROWS - 1)
        return pl.multiple_of(off, GRID_W), li

    def pair_body(pp, carry):
        for q0 in range(0, NA_ROWS, NA_STAGE):
            rows_q = range(q0, q0 + NA_STAGE)
            for qi in rows_q:
                off, li = window(qi)
                q2 = q_ref[pp, qi * GRID_W:(qi + 1) * GRID_W, :]
                zero = jnp.zeros_like(q2)
                qs = jnp.concatenate([jnp.where(left, q2, zero), jnp.where(left, zero, q2)], axis=0)
                kw = kc_s[cur, pp, pl.ds(off, nkeys), :]
                s_s[qi] = _dot_nt(qs, kw) + bias_ref[li, pp]
            for qi in rows_q:
                s = s_s[qi]
                p = jnp.exp2(s - jnp.max(s, axis=-1, keepdims=True))
                r_s[qi] = 1.0 / jnp.sum(p, axis=-1, keepdims=True)
                p_s[qi] = p.astype(BF16)
            for qi in rows_q:
                off, _ = window(qi)
                pv = _dot(p_s[qi], vc_s[cur, pp, pl.ds(off, nkeys), :]) * r_s[qi]
                o = jnp.where(left, pv[0:GRID_W], pv[GRID_W:2 * GRID_W])
                o_ref[pp, qi * GRID_W:(qi + 1) * GRID_W, :] = o.astype(BF16)
        return carry

    lax.fori_loop(0, npairs, pair_body, 0, unroll=4)


def _na_attention(qkv_t, bias_tab, *, batch, seq):
    t = batch * seq
    blk = NA_ROWS * GRID_W
    nrb = seq // blk
    hp = NA_PAIRS // 2
    nsec = NA_PAIRS // hp

    return pl.pallas_call(
        functools.partial(_na_kernel, n_row_blocks=nrb),
        out_shape=jax.ShapeDtypeStruct((NA_PAIRS, t, LANES), BF16),
        grid=(nsec, batch, nrb),
        in_specs=[pl.BlockSpec((hp, blk, LANES), lambda hh, b, r: (hh, b * nrb + r, 0)),
                  pl.BlockSpec(memory_space=pl.ANY),
                  pl.BlockSpec((NA_ROWS, hp, 2 * GRID_W, WIN_H * GRID_W),
                               lambda hh, b, r: (jnp.where(r == 0, 1, 0), hh, 0, 0))],
        out_specs=pl.BlockSpec((hp, blk, LANES), lambda hh, b, r: (hh, b * nrb + r, 0)),
        scratch_shapes=[pltpu.VMEM((2, hp, 3 * blk, LANES), BF16),
                        pltpu.VMEM((2, hp, 3 * blk, LANES), BF16),
                        pltpu.VMEM((NA_ROWS, 2 * GRID_W, WIN_H * GRID_W), F32),
                        pltpu.VMEM((NA_ROWS, 2 * GRID_W, WIN_H * GRID_W), BF16),
                        pltpu.VMEM((NA_ROWS, 2 * GRID_W, 1), F32),
                        pltpu.SemaphoreType.DMA((2, 2))],
        compiler_params=_cparams(("arbitrary", "arbitrary", "arbitrary")),
        name="na_attention",
    )(qkv_t, qkv_t, bias_tab)


def _na_mixer(x, mod_sc, mod_sh, mod_g, nw, w_qkv, rpb, w_o, *, batch, seq):
    qkv_t = _nm_matmul(x, nw, mod_sc, mod_sh, w_qkv, ncols=3 * D_MODEL, tn=MM_COLS,
                       out_dtype=BF16, rows_per_batch=seq, pair_major=True,
                       lead_cols=D_MODEL, lead_scale=NA_HEAD_DIM ** -0.5 * LOG2_E)
    o_t = _na_attention(qkv_t, _na_bias_table(rpb), batch=batch, seq=seq)
    return _mm_resid(o_t, w_o, x, mod_g, rows_per_batch=seq, pair_major=True)


U32 = jnp.uint32


def _pack_halves(vb):
    n = vb.shape[1] // 2
    bits = pltpu.bitcast(vb.astype(F32), U32)
    return (bits[:, :n] >> 16) | bits[:, n:]


def _unpack_halves(w):
    return pltpu.bitcast(w << 16, F32), pltpu.bitcast(w & U32(0xFFFF0000), F32)


TOK_SUB = (D_MODEL // 2) // LANES


def _store_token_tiles(ref, packed):
    rows = packed.shape[0]
    for s in range(TOK_SUB):
        ref[pl.ds(s, rows, stride=TOK_SUB), :] = packed[:, s * LANES:(s + 1) * LANES]


def _load_token_tiles(ref):
    rows = ref.shape[0] // TOK_SUB
    return jnp.concatenate([ref[pl.ds(s, rows, stride=TOK_SUB), :] for s in range(TOK_SUB)], axis=1)


def _router_kernel(x_ref, nw_ref, sc_ref, sh_ref, wr_ref, h_ref, meta_ref, meta_t_ref, cnt_ref,
                   carry_s, whi_s, wlo_s):
    @pl.when(pl.program_id(0) == 0)
    def _():
        carry_s[...] = jnp.zeros_like(carry_s)
        whi_s[...], wlo_s[...] = _split2(wr_ref[...])

    h = _normmod(x_ref[...], nw_ref[...], sc_ref[0], sh_ref[0])
    _store_token_tiles(h_ref, _pack_halves(h.astype(BF16)))
    logits = _dot_split(h, whi_s[...], wlo_s[...])
    tm = logits.shape[0]
    lane_i = lax.broadcasted_iota(I32, logits.shape, 1)
    lane = lane_i.astype(F32)
    big = 1e9
    gl = jnp.where(lane_i < MOE_GROUPS, logits, NEG)
    gmax = jnp.max(gl, axis=1, keepdims=True)
    gsel = jnp.min(jnp.where(gl == gmax, lane, big), axis=1, keepdims=True)
    gw = 1.0 / jnp.sum(jnp.exp(gl - gmax), axis=1, keepdims=True)
    el = lane - MOE_GROUPS
    lo = gsel * MOE_EPG
    emask = (el >= lo) & (el < lo + MOE_EPG)
    e1 = jnp.where(emask, logits, NEG)
    m1 = jnp.max(e1, axis=1, keepdims=True)
    i1 = jnp.min(jnp.where(e1 == m1, el, big), axis=1, keepdims=True)
    e2 = jnp.where(emask & (el != i1), logits, NEG)
    m2 = jnp.max(e2, axis=1, keepdims=True)
    i2 = jnp.min(jnp.where(e2 == m2, el, big), axis=1, keepdims=True)
    tt = jnp.exp(m2 - m1)
    p1 = 1.0 / (1.0 + tt)
    w1 = gw * p1
    w2 = gw * (tt * p1)
    oh1 = el == i1
    oh2 = el == i2
    cnt = (oh1 | oh2).astype(F32)
    r_i = lax.broadcasted_iota(I32, (tm, tm), 0)
    c_i = lax.broadcasted_iota(I32, (tm, tm), 1)
    before = _dot((r_i > c_i).astype(BF16), cnt.astype(BF16)) + carry_s[...]
    rank1 = jnp.sum(jnp.where(oh1, before, 0.0), axis=1, keepdims=True)
    rank2 = jnp.sum(jnp.where(oh2, before, 0.0), axis=1, keepdims=True)
    carry_s[...] = carry_s[...] + jnp.sum(cnt, axis=0, keepdims=True)
    meta = jnp.zeros_like(logits)
    for pos, val in enumerate((i1, i2, w1, w2, rank1, rank2)):
        meta = jnp.where(lane_i == pos, val, meta)
    meta_ref[...] = meta
    meta_t_ref[...] = meta.T[0:meta_t_ref.shape[0], :]
    cnt_ref[...] = jnp.broadcast_to(carry_s[...], cnt_ref.shape)


def _router(x, nw, sc, sh, wr, *, rows_per_batch):
    t, d = x.shape
    tm = TOKEN_TILE
    tiles_per_batch = rows_per_batch // tm
    return pl.pallas_call(
        _router_kernel,
        out_shape=(jax.ShapeDtypeStruct((t * TOK_SUB, LANES), U32),
                   jax.ShapeDtypeStruct((t, LANES), F32),
                   jax.ShapeDtypeStruct((8, t), F32),
                   jax.ShapeDtypeStruct((8, LANES), F32)),
        grid=(t // tm,),
        in_specs=[pl.BlockSpec((tm, d), lambda i: (i, 0)),
                  pl.BlockSpec((1, d), lambda i: (0, 0)),
                  pl.BlockSpec((1, 1, d), lambda i: (i // tiles_per_batch, 0, 0)),
                  pl.BlockSpec((1, 1, d), lambda i: (i // tiles_per_batch, 0, 0)),
                  pl.BlockSpec((d, LANES), lambda i: (0, 0))],
        out_specs=(pl.BlockSpec((tm * TOK_SUB, LANES), lambda i: (i, 0)),
                   pl.BlockSpec((tm, LANES), lambda i: (i, 0)),
                   pl.BlockSpec((8, tm), lambda i: (0, i)),
                   pl.BlockSpec((8, LANES), lambda i: (0, 0))),
        scratch_shapes=[pltpu.VMEM((1, LANES), F32), pltpu.VMEM((d, LANES), BF16),
                        pltpu.VMEM((d, LANES), BF16)],
        compiler_params=_cparams(("arbitrary",)),
        name="moe_router",
    )(x, nw, sc, sh, wr)


def _slot_kernel(pstart_ref, mt_ref, o_ref):
    eid = mt_ref[0:2, :]
    start = jnp.zeros(eid.shape, I32)
    for e in range(N_EXPERTS):
        start = jnp.where(eid == float(e), pstart_ref[e], start)
    o_ref[...] = start + mt_ref[4:6, :].astype(I32)


def _slots(pstart, meta_t):
    t = meta_t.shape[1]
    return pl.pallas_call(
        _slot_kernel,
        out_shape=jax.ShapeDtypeStruct((2, t), I32),
        grid_spec=pltpu.PrefetchScalarGridSpec(
            num_scalar_prefetch=1,
            grid=(1,),
            in_specs=[pl.BlockSpec(meta_t.shape, lambda i, ps: (0, 0))],
            out_specs=pl.BlockSpec((2, t), lambda i, ps: (0, 0))),
        compiler_params=_cparams(("arbitrary",)),
        name="moe_slots",
    )(pstart, meta_t)


def _token_tile(ref, r):
    return ref.at[pl.ds(pl.multiple_of(r * TOK_SUB, TOK_SUB), TOK_SUB)]


def _dispatch_kernel(dest_ref, zflag_ref, h_ref, xs_ref, zbuf, sem, zsem):
    tm = h_ref.shape[0] // TOK_SUB
    base = pl.program_id(0) * tm
    ntok = pl.num_programs(0) * tm
    tb = zbuf.shape[0]

    @pl.when(pl.program_id(0) == 0)
    def _():
        zbuf[...] = jnp.zeros_like(zbuf)

        def zcopy(b):
            return pltpu.make_async_copy(zbuf, xs_ref.at[pl.ds(pl.multiple_of(b * tb, tb), tb)], zsem)

        def zstart(b, carry):
            @pl.when(zflag_ref[b] == 1)
            def _():
                zcopy(b).start()
            return carry

        def zwait(b, carry):
            @pl.when(zflag_ref[b] == 1)
            def _():
                zcopy(b).wait()
            return carry

        nblk = xs_ref.shape[0] // tb
        lax.fori_loop(0, nblk, zstart, 0)
        lax.fori_loop(0, nblk, zwait, 0)

    def copy(t, k):
        return pltpu.make_async_copy(_token_tile(h_ref, t),
                                     _token_tile(xs_ref, dest_ref[k * ntok + base + t]), sem)

    def issue(t, carry):
        copy(t, 0).start(priority=0)
        copy(t, 1).start(priority=1)
        return carry

    def drain(t, carry):
        copy(t, 0).wait()
        copy(t, 1).wait()
        return carry

    lax.fori_loop(0, tm, issue, 0, unroll=DMA_UNROLL)
    lax.fori_loop(0, tm, drain, 0, unroll=DMA_UNROLL)


def _dispatch(dest, zflag, h, n_slots):
    t = h.shape[0] // TOK_SUB
    tm = TOKEN_TILE
    return pl.pallas_call(
        _dispatch_kernel,
        out_shape=jax.ShapeDtypeStruct((n_slots * TOK_SUB, LANES), U32),
        grid_spec=pltpu.PrefetchScalarGridSpec(
            num_scalar_prefetch=2,
            grid=(t // tm,),
            in_specs=[pl.BlockSpec((tm * TOK_SUB, LANES), lambda i, dest, zf: (i, 0))],
            out_specs=pl.BlockSpec(memory_space=pl.ANY),
            scratch_shapes=[pltpu.VMEM((MOE_TB * TOK_SUB, LANES), U32), pltpu.SemaphoreType.DMA(()),
                            pltpu.SemaphoreType.DMA(())]),
        compiler_params=_cparams(("arbitrary",)),
        name="moe_dispatch",
    )(dest, zflag, h)


def _ffn_kernel(be_ref, nxt_ref, slot_ref, nu_ref, xs_ref, w1_hbm, w3_hbm, w2_hbm, o_ref,
                wb1, wb3, wb2, w1_s, w3_s, w2_s, sem, *, layer):
    i = pl.program_id(0)

    def fetch(e, s):
        return (pltpu.make_async_copy(w1_hbm.at[layer, e], wb1.at[s], sem.at[s, 0]),
                pltpu.make_async_copy(w3_hbm.at[layer, e], wb3.at[s], sem.at[s, 1]),
                pltpu.make_async_copy(w2_hbm.at[layer, e], wb2.at[s], sem.at[s, 2]))

    @pl.when(i < nu_ref[0])
    def _():
        e = be_ref[i]
        s = slot_ref[i]

        @pl.when((i == 0) | (e != be_ref[jnp.maximum(i - 1, 0)]))
        def _():
            @pl.when(i == 0)
            def _():
                for cp in fetch(e, s):
                    cp.start()

            for cp in fetch(e, s):
                cp.wait()

            @pl.when(nxt_ref[i] >= 0)
            def _():
                for cp in fetch(nxt_ref[i], 1 - s):
                    cp.start()

            w1_s[...] = wb1[s].astype(BF16)
            w3_s[...] = wb3[s].astype(BF16)
            w2_s[...] = wb2[s].astype(BF16)

        lo, hi = _unpack_halves(_load_token_tiles(xs_ref))
        xl, xh = lo.astype(BF16), hi.astype(BF16)
        half = xl.shape[1]
        a = _dot(xl, w1_s[0:half, :]) + _dot(xh, w1_s[half:2 * half, :])
        b = _dot(xl, w3_s[0:half, :]) + _dot(xh, w3_s[half:2 * half, :])
        hmid = _silu(a) * b
        _store_token_tiles(o_ref, _pack_halves(_dot(hmid.astype(BF16), w2_s[...]).astype(BF16)))

    @pl.when(i >= nu_ref[0])
    def _():
        o_ref[...] = jnp.zeros_like(o_ref)


def _expert_ffn(blk_expert, blk_next, blk_slot, n_used, xs, w1, w3, w2, layer):
    d, f = w1.shape[2], w1.shape[3]
    rows = MOE_TB * TOK_SUB
    nb = xs.shape[0] // rows
    hbm = pl.BlockSpec(memory_space=pl.ANY)
    return pl.pallas_call(
        functools.partial(_ffn_kernel, layer=layer),
        out_shape=jax.ShapeDtypeStruct(xs.shape, U32),
        grid_spec=pltpu.PrefetchScalarGridSpec(
            num_scalar_prefetch=4,
            grid=(nb,),
            in_specs=[pl.BlockSpec((rows, LANES),
                                   lambda i, be, nx, sl, nu: (jnp.minimum(i, nu[0] - 1), 0)),
                      hbm, hbm, hbm],
            out_specs=pl.BlockSpec((rows, LANES), lambda i, be, nx, sl, nu: (i, 0)),
            scratch_shapes=[pltpu.VMEM((2, d, f), F32), pltpu.VMEM((2, d, f), F32),
                            pltpu.VMEM((2, f, d), F32),
                            pltpu.VMEM((d, f), BF16), pltpu.VMEM((d, f), BF16),
                            pltpu.VMEM((f, d), BF16),
                            pltpu.SemaphoreType.DMA((2, 3))]),
        compiler_params=_cparams(("arbitrary",)),
        name="moe_expert_ffn",
    )(blk_expert, blk_next, blk_slot, n_used, xs, w1, w3, w2)


def _combine_kernel(dest_ref, x_ref, meta_ref, g_ref, fnw_ref, ys_ref, o_ref, buf, sem, *, final):
    tm = x_ref.shape[0]
    i = pl.program_id(0)
    nsteps = pl.num_programs(0)
    ntok = nsteps * tm

    def copy(tile, t, k):
        half = tile % 2
        return pltpu.make_async_copy(_token_tile(ys_ref, dest_ref[k * ntok + tile * tm + t]),
                                     _token_tile(buf.at[half, k], t), sem.at[half])

    def issue(tile):
        def body(t, carry):
            copy(tile, t, 0).start(priority=0)
            copy(tile, t, 1).start(priority=1)
            return carry
        lax.fori_loop(0, tm, body, 0, unroll=DMA_UNROLL)

    def drain(tile):
        def body(t, carry):
            copy(tile, t, 0).wait()
            copy(tile, t, 1).wait()
            return carry
        lax.fori_loop(0, tm, body, 0, unroll=DMA_UNROLL)

    @pl.when(i == 0)
    def _():
        issue(i)

    @pl.when(i + 1 < nsteps)
    def _():
        issue(i + 1)

    drain(i)
    cur = i % 2
    meta = meta_ref[...]
    w1, w2 = meta[:, 2:3], meta[:, 3:4]
    lo1, hi1 = _unpack_halves(_load_token_tiles(buf.at[cur, 0]))
    lo2, hi2 = _unpack_halves(_load_token_tiles(buf.at[cur, 1]))
    y = jnp.concatenate([w1 * lo1 + w2 * lo2, w1 * hi1 + w2 * hi2], axis=1)
    xn = x_ref[...] + g_ref[0] * y
    if final:
        ms = jnp.mean(xn * xn, axis=-1, keepdims=True)
        xn = xn * lax.rsqrt(ms + EPS) * fnw_ref[...]
    o_ref[...] = xn


def _combine(dest, x, meta, g, fnw, ys, *, rows_per_batch, final):
    t, d = x.shape
    tm = TOKEN_TILE
    tiles_per_batch = rows_per_batch // tm
    return pl.pallas_call(
        functools.partial(_combine_kernel, final=final),
        out_shape=jax.ShapeDtypeStruct((t, d), F32),
        grid_spec=pltpu.PrefetchScalarGridSpec(
            num_scalar_prefetch=1,
            grid=(t // tm,),
            in_specs=[pl.BlockSpec((tm, d), lambda i, dest: (i, 0)),
                      pl.BlockSpec((tm, LANES), lambda i, dest: (i, 0)),
                      pl.BlockSpec((1, 1, d), lambda i, dest: (i // tiles_per_batch, 0, 0)),
                      pl.BlockSpec((1, d), lambda i, dest: (0, 0)),
                      pl.BlockSpec(memory_space=pl.ANY)],
            out_specs=pl.BlockSpec((tm, d), lambda i, dest: (i, 0)),
            scratch_shapes=[pltpu.VMEM((2, 2, tm * TOK_SUB, LANES), U32),
                            pltpu.SemaphoreType.DMA((2,))]),
        compiler_params=_cparams(("arbitrary",)),
        name="moe_combine",
    )(dest, x, meta, g, fnw, ys)


def _hier_moe(x, mod_sc, mod_sh, mod_g, nw, w_group, w_expert, w1, w3, w2, layer, fnw, *,
              rows_per_batch, final):
    t, d = x.shape
    a = 2 * t
    tb = MOE_TB
    wr = jnp.concatenate([w_group, w_expert], axis=1)
    wr = jnp.pad(wr, ((0, 0), (0, LANES - wr.shape[1])))
    h, meta, meta_t, cnt = _router(x, nw, mod_sc, mod_sh, wr, rows_per_batch=rows_per_batch)
    ne = N_EXPERTS
    counts = cnt[0, MOE_GROUPS:MOE_GROUPS + ne].astype(I32)
    padded = ((counts + tb - 1) // tb) * tb
    pend = jnp.cumsum(padded)
    pstart = pend - padded
    dest = _slots(pstart, meta_t).reshape(a)
    nb = (a + ne * (tb - 1) + tb - 1) // tb
    n_used = (pend[-1] // tb).astype(I32)
    blk = jnp.arange(nb, dtype=I32)
    be = jnp.minimum(jnp.sum((pend[None, :] <= (blk * tb)[:, None]).astype(I32), axis=1), ne - 1)
    seg_last = jnp.any((pend[None, :] == ((blk + 1) * tb)[:, None]) & (padded[None, :] > 0), axis=1)
    zflag = (seg_last | (blk >= n_used)).astype(I32)
    nonempty = counts > 0
    seg = jnp.cumsum(nonempty.astype(I32)) - 1
    later = lax.cummin(jnp.where(nonempty, jnp.arange(ne, dtype=I32), ne), axis=0, reverse=True)
    nxt_e = jnp.concatenate([later[1:], jnp.full((1,), ne, I32)])
    nxt_e = jnp.where(nxt_e == ne, -1, nxt_e)
    xs = _dispatch(dest, zflag, h, nb * tb)
    ys = _expert_ffn(be, nxt_e[be], seg[be] % 2, n_used.reshape(1), xs, w1, w3, w2, layer)
    return _combine(dest, x, meta, mod_g, fnw, ys, rows_per_batch=rows_per_batch, final=final)


def kernel(x, c, ada_w, ada_b, norm_mix, norm_ffn, ssd_w_in, ssd_conv_w, ssd_conv_b, ssd_a_log,
           ssd_dt_bias, ssd_d, ssd_norm_w, ssd_w_out, na_w_qkv, na_rpb, na_w_o,
           moe_w_group, moe_w_expert, moe_w1, moe_w3, moe_w2, final_norm):
    batch, seq, d = x.shape
    depth = ada_w.shape[0]
    xt = x.reshape(batch * seq, d)
    c_pad = jnp.pad(c, ((0, 8 - batch), (0, 0)))
    mod = _ada(c_pad, ada_w, ada_b)[:, :batch]
    fnw = final_norm.reshape(1, d)
    for i in range(depth):
        sh1, sc1, g1, sh2, sc2, g2 = [mod[i, :, k * d:(k + 1) * d].reshape(batch, 1, d)
                                      for k in range(6)]
        j = i // 2
        nw = norm_mix[i].reshape(1, d)
        if i % 2 == 0:
            xt = _ssd_mixer(xt, sc1, sh1, g1, nw, ssd_w_in[j], ssd_conv_w[j], ssd_conv_b[j],
                            ssd_a_log[j], ssd_dt_bias[j], ssd_d[j], ssd_norm_w[j], ssd_w_out[j],
                            batch=batch, seq=seq)
        else:
            xt = _na_mixer(xt, sc1, sh1, g1, nw, na_w_qkv[j], na_rpb[j], na_w_o[j],
                           batch=batch, seq=seq)
        xt = _hier_moe(xt, sc2, sh2, g2, norm_ffn[i].reshape(1, d), moe_w_group[i], moe_w_expert[i],
                       moe_w1, moe_w3, moe_w2, i, fnw, rows_per_batch=seq,
                       final=(i == depth - 1))
    return xt.reshape(batch, seq, d)
```

```python
import functools

import jax
import jax.numpy as jnp
from jax import lax
from jax.experimental import pallas as pl
from jax.experimental.pallas import tpu as pltpu

F32 = jnp.float32
BF16 = jnp.bfloat16
I32 = jnp.int32

EPS = 1e-6
NEG = -1e30
LOG2_E = 1.4426950408889634

D_MODEL = 2048
GRID_W = 64
SSD_HEAD_DIM = 64
SSD_GROUPS = 8
SSD_HEADS_PER_GROUP = 8
SSD_STATE = 128
SSD_CONV = 5
D_INNER = 2 * D_MODEL
GROUP_W = SSD_HEADS_PER_GROUP * SSD_HEAD_DIM
SSD_Q = 128
CONV_HALO = 16
NA_HEAD_DIM = 64
NA_HEADS = D_MODEL // NA_HEAD_DIM
NA_PAIRS = NA_HEADS // 2
WIN_H = 8
WIN_W = 16
NA_ROWS = 4
NA_STAGE = 2
MOE_GROUPS = 4
MOE_EPG = 8
N_EXPERTS = MOE_GROUPS * MOE_EPG
MOE_D_FF = D_MODEL // 4
MOE_TB = 256
DMA_UNROLL = 8
VMEM_LIMIT = 60 * 1024 * 1024
LANES = 128

MM_ROWS = 1024
MM_COLS = 1024
RESID_ROWS = 2048
RESID_W_ELEMS = 1024 * 1024
ADA_COLS = 2048
TOKEN_TILE = 256


def _cparams(sem):
    return pltpu.CompilerParams(dimension_semantics=sem, vmem_limit_bytes=VMEM_LIMIT)


def _silu(v):
    return v * pl.reciprocal(1.0 + jnp.exp(-v), approx=True)


def _softplus(v):
    return jnp.maximum(v, 0.0) + jnp.log1p(jnp.exp(-jnp.abs(v)))


def _split3(v):
    hi = v.astype(BF16)
    r1 = v - hi.astype(F32)
    mid = r1.astype(BF16)
    lo = (r1 - mid.astype(F32)).astype(BF16)
    return hi, mid, lo


def _dot(a, b):
    return jnp.dot(a, b, preferred_element_type=F32)


def _dot_nt(a, b):
    return lax.dot_general(a, b, (((1,), (1,)), ((), ())), preferred_element_type=F32)


def _dot3_left(v, sel):
    hi, mid, lo = _split3(v)
    return _dot(hi, sel) + _dot(mid, sel) + _dot(lo, sel)


def _normmod(x, nw, sc, sh):
    ms = jnp.mean(x * x, axis=-1, keepdims=True)
    return (x * lax.rsqrt(ms + EPS) * nw) * (1.0 + sc) + sh


def _split2(v):
    hi = v.astype(BF16)
    return hi, (v - hi.astype(F32)).astype(BF16)


def _dot_split(a, b_hi, b_lo):
    a_hi, a_lo = _split2(a)
    return _dot(a_hi, b_hi) + (_dot(a_lo, b_hi) + _dot(a_hi, b_lo))


def _ada_kernel(c_ref, w_ref, b_ref, o_ref):
    c = c_ref[...]
    o_ref[0] = _dot_split(c / (1.0 + jnp.exp(-c)), *_split2(w_ref[0])) + b_ref[0]


def _ada(c_pad, ada_w, ada_b):
    depth, d, n = ada_w.shape
    tn = ADA_COLS
    return pl.pallas_call(
        _ada_kernel,
        out_shape=jax.ShapeDtypeStruct((depth, 8, n), F32),
        grid=(depth, n // tn),
        in_specs=[pl.BlockSpec((8, d), lambda i, j: (0, 0)),
                  pl.BlockSpec((1, d, tn), lambda i, j: (i, 0, j)),
                  pl.BlockSpec((1, 1, tn), lambda i, j: (i, 0, j))],
        out_specs=pl.BlockSpec((1, 8, tn), lambda i, j: (i, 0, j)),
        compiler_params=_cparams(("arbitrary", "arbitrary")),
        name="ada_mod",
    )(c_pad, ada_w, ada_b.reshape(depth, 1, n))


def _nm_mm_kernel(x_ref, nw_ref, sc_ref, sh_ref, w_ref, *rest, pair_major, tail, lead_tiles,
                  lead_scale):
    if tail:
        wt_ref, o_ref, ot_ref, h_ref = rest
    else:
        o_ref, h_ref = rest

    @pl.when(pl.program_id(1) == 0)
    def _():
        h_ref[...] = _normmod(x_ref[...], nw_ref[...], sc_ref[0], sh_ref[0]).astype(BF16)

    r = _dot(h_ref[...], w_ref[...].astype(BF16))
    if lead_tiles:
        r = r * jnp.where(pl.program_id(1) < lead_tiles, lead_scale, 1.0)
    if pair_major:
        for c in range(o_ref.shape[0]):
            o_ref[c] = r[:, c * LANES:(c + 1) * LANES].astype(o_ref.dtype)
    else:
        o_ref[...] = r.astype(o_ref.dtype)

    if tail:
        @pl.when(pl.program_id(1) == pl.num_programs(1) - 1)
        def _():
            ot_ref[...] = _dot(h_ref[...], wt_ref[...].astype(BF16))


def _nm_matmul(x, nw, sc, sh, w, *, ncols, tn, out_dtype, rows_per_batch, pair_major=False,
               tail_cols=0, lead_cols=0, lead_scale=1.0):
    t, d = x.shape
    tm = MM_ROWS
    tiles_per_batch = rows_per_batch // tm
    if pair_major:
        out_shape = jax.ShapeDtypeStruct((ncols // LANES, t, LANES), out_dtype)
        out_spec = pl.BlockSpec((tn // LANES, tm, LANES), lambda i, j: (j, i, 0))
    else:
        out_shape = jax.ShapeDtypeStruct((t, ncols), out_dtype)
        out_spec = pl.BlockSpec((tm, tn), lambda i, j: (i, j))
    in_specs = [pl.BlockSpec((tm, d), lambda i, j: (i, 0)),
                pl.BlockSpec((1, d), lambda i, j: (0, 0)),
                pl.BlockSpec((1, 1, d), lambda i, j: (i // tiles_per_batch, 0, 0)),
                pl.BlockSpec((1, 1, d), lambda i, j: (i // tiles_per_batch, 0, 0)),
                pl.BlockSpec((d, tn), lambda i, j: (0, j))]
    operands = [x, nw, sc, sh, w]
    if tail_cols:
        tail_blk = ncols // tail_cols
        in_specs.append(pl.BlockSpec((d, tail_cols), lambda i, j: (0, tail_blk)))
        operands.append(w)
        out_shape = (out_shape, jax.ShapeDtypeStruct((t, tail_cols), F32))
        out_spec = (out_spec, pl.BlockSpec((tm, tail_cols), lambda i, j: (i, 0)))
    return pl.pallas_call(
        functools.partial(_nm_mm_kernel, pair_major=pair_major, tail=bool(tail_cols),
                          lead_tiles=lead_cols // tn, lead_scale=lead_scale),
        out_shape=out_shape,
        grid=(t // tm, ncols // tn),
        in_specs=in_specs,
        out_specs=out_spec,
        scratch_shapes=[pltpu.VMEM((tm, d), BF16)],
        compiler_params=_cparams(("arbitrary", "arbitrary")),
        name="norm_mod_matmul",
    )(*operands)


def _mm_resid_kernel(a_ref, w_ref, x_ref, g_ref, o_ref, *, pair_major):
    if pair_major:
        a = jnp.concatenate([a_ref[c] for c in range(a_ref.shape[0])], axis=1)
    else:
        a = a_ref[...]
    o_ref[...] = x_ref[...] + g_ref[0] * _dot(a, w_ref[...].astype(BF16))


def _mm_resid(a, w, x, g, *, rows_per_batch, pair_major=False):
    t, n = x.shape
    k = w.shape[0]
    tm = RESID_ROWS
    tn = RESID_W_ELEMS // k
    tiles_per_batch = rows_per_batch // tm
    if pair_major:
        a_spec = pl.BlockSpec((k // LANES, tm, LANES), lambda i, j: (0, i, 0))
    else:
        a_spec = pl.BlockSpec((tm, k), lambda i, j: (i, 0))
    return pl.pallas_call(
        functools.partial(_mm_resid_kernel, pair_major=pair_major),
        out_shape=jax.ShapeDtypeStruct((t, n), F32),
        grid=(t // tm, n // tn),
        in_specs=[a_spec,
                  pl.BlockSpec((k, tn), lambda i, j: (0, j)),
                  pl.BlockSpec((tm, tn), lambda i, j: (i, j)),
                  pl.BlockSpec((1, 1, tn), lambda i, j: (i // tiles_per_batch, 0, j))],
        out_specs=pl.BlockSpec((tm, tn), lambda i, j: (i, j)),
        compiler_params=_cparams(("arbitrary", "arbitrary")),
        name="matmul_resid",
    )(a, w, x, g)


def _ssd_kernel(z_ref, x_ref, b_ref, c_ref, dtr_ref,
                cwx_ref, cwb_ref, cwc_ref, cbx_ref, cbb_ref, cbc_ref,
                alr_ref, dbr_ref, dsk_ref, nw_ref,
                o_ref,
                xc_s, bc_s, cc_s, yacc_s, st_s, cv_s):
    seq = x_ref.shape[0]
    q = SSD_Q
    nc = seq // q
    halo = CONV_HALO
    nrow = 2 * SSD_HEADS_PER_GROUP

    def conv_piece(j, base, src_ref, w_ref, bias_ref, dst_ref, lo, stage):
        cols = slice(lo, lo + LANES)
        pstart = pl.multiple_of(jnp.maximum(base - halo, 0), halo)
        nstart = pl.multiple_of(jnp.minimum(base + q, seq - halo), halo)
        stage[0:halo, :] = jnp.where(j > 0, src_ref[pl.ds(pstart, halo), cols].astype(F32), 0.0)
        stage[halo:halo + q, :] = src_ref[pl.ds(base, q), cols].astype(F32)
        stage[halo + q:, :] = jnp.where(j < nc - 1, src_ref[pl.ds(nstart, halo), cols].astype(F32), 0.0)
        acc = jnp.broadcast_to(bias_ref[:, cols], (q, LANES))
        for k in range(SSD_CONV):
            first = halo - SSD_CONV // 2 + k
            acc = acc + w_ref[k:k + 1, cols] * stage[first:first + q, :]
        dst_ref[pl.ds(base, q), cols] = _silu(acc).astype(BF16)

    def conv_chunk(j, carry):
        base = pl.multiple_of(j * q, q)
        npx = GROUP_W // LANES
        for i in range(npx):
            conv_piece(j, base, x_ref, cwx_ref, cbx_ref, xc_s, i * LANES, cv_s.at[i])
        conv_piece(j, base, b_ref, cwb_ref, cbb_ref, bc_s, 0, cv_s.at[npx])
        conv_piece(j, base, c_ref, cwc_ref, cbc_ref, cc_s, 0, cv_s.at[npx + 1])
        return carry

    lax.fori_loop(0, nc, conv_chunk, 0)

    row_i = lax.broadcasted_iota(I32, (q, q), 0)
    col_i = lax.broadcasted_iota(I32, (q, q), 1)
    lower = row_i >= col_i
    upper = row_i <= col_i
    lower_b = lower.astype(BF16)
    upper_b = upper.astype(BF16)
    left =lax.broadcasted_iota(I32, (q, LANES), 1) < SSD_HEAD_DIM
    a_row = -jnp.exp(alr_ref[...])
    pad_rows = jnp.zeros((LANES - nrow, q), F32)

    def scan_pass(direction):
        hoff = direction * SSD_HEADS_PER_GROUP
        mask = lower if direction == 0 else upper
        tri = upper_b if direction == 0 else lower_b
        edge = q - 1 if direction == 0 else 0
        st_s[...] = jnp.zeros_like(st_s)

        def chunk(t, carry):
            c = t if direction == 0 else nc - 1 - t
            base = pl.multiple_of(c * q, q)
            rows = pl.ds(base, q)
            dt_r = _softplus(dtr_ref[c] + dbr_ref[...])
            cum_r = _dot3_left(dt_r * a_row, tri) * LOG2_E
            cum_c = jnp.concatenate([cum_r, pad_rows], axis=0).T
            decdt_r = jnp.exp2(cum_r[:, edge:edge + 1] - cum_r) * dt_r
            src_r = cum_r - jnp.log2(dt_r)

            bm = bc_s[rows, :]
            cm = cc_s[rows, :]
            cb = _dot_nt(cm, bm)
            bm_t = bm.astype(F32).T
            y_off_all = _dot(cm, st_s[...].astype(BF16)) if direction == 1 else None

            ssq = jnp.zeros((q, 1), F32)
            for pp in range(SSD_HEADS_PER_GROUP // 2):
                cols = slice(pp * LANES, (pp + 1) * LANES)
                xcb = xc_s[rows, cols]
                zero_b = jnp.zeros_like(xcb)
                st_in = st_s[:, cols]
                lhs_y, lhs_s, scales = [], [], []
                for par in range(2):
                    j = hoff + 2 * pp + par
                    cum_b = jnp.broadcast_to(cum_c[:, j:j + 1], (q, q))
                    lmat = jnp.exp2(jnp.where(mask, cum_b - src_r[j:j + 1, :], NEG))
                    lhs_y.append((cb * lmat).astype(BF16))
                    lhs_s.append((bm_t * decdt_r[j:j + 1, :]).astype(BF16))
                    scales.append(jnp.exp2(cum_b))
                x_rhs = jnp.concatenate([jnp.where(left, xcb, zero_b), jnp.where(left, zero_b, xcb)],
                                        axis=0)
                sc_tile = jnp.where(left, scales[0], scales[1])
                y_off = _dot(cm, st_in.astype(BF16)) if y_off_all is None else y_off_all[:, cols]
                y = _dot(jnp.concatenate(lhs_y, axis=1), x_rhs) + y_off * sc_tile
                st_s[:, cols] = (st_in * sc_tile[edge:edge + 1, :]
                                 + _dot(jnp.concatenate(lhs_s, axis=1), x_rhs))
                if direction == 0:
                    yacc_s[rows, cols] = y
                else:
                    total = yacc_s[rows, cols] + y + xcb.astype(F32) * dsk_ref[:, cols]
                    gated = total * _silu(z_ref[rows, cols].astype(F32))
                    ssq = ssq + jnp.sum(gated * gated, axis=-1, keepdims=True)
                    yacc_s[rows, cols] = gated
            if direction == 1:
                inv = lax.rsqrt(ssq * (1.0 / GROUP_W) + EPS)
                o_ref[rows, :] = (yacc_s[rows, :] * inv * nw_ref[...]).astype(BF16)
            return carry

        lax.fori_loop(0, nc, chunk, 0, unroll=16)

    scan_pass(0)
    scan_pass(1)


def _ssd_core(zx, dt_row, conv_w, conv_b, al_row, db_row, dskip, norm_w, *, batch, seq):
    g = SSD_GROUPS
    nc = seq // SSD_Q
    xb = D_INNER // GROUP_W
    bb = (2 * D_INNER) // SSD_STATE
    cb = bb + g
    cwb = D_INNER // SSD_STATE
    cwc = cwb + g
    return pl.pallas_call(
        _ssd_kernel,
        out_shape=jax.ShapeDtypeStruct((batch * seq, D_INNER), BF16),
        grid=(batch, g),
        in_specs=[
            pl.BlockSpec((seq, GROUP_W), lambda b, i: (b, i)),
            pl.BlockSpec((seq, GROUP_W), lambda b, i: (b, xb + i)),
            pl.BlockSpec((seq, SSD_STATE), lambda b, i: (b, bb + i)),
            pl.BlockSpec((seq, SSD_STATE), lambda b, i: (b, cb + i)),
            pl.BlockSpec((None, None, nc, 2 * SSD_HEADS_PER_GROUP, SSD_Q), lambda b, i: (b, i, 0, 0, 0)),
            pl.BlockSpec((SSD_CONV, GROUP_W), lambda b, i: (0, i)),
            pl.BlockSpec((SSD_CONV, SSD_STATE), lambda b, i: (0, cwb + i)),
            pl.BlockSpec((SSD_CONV, SSD_STATE), lambda b, i: (0, cwc + i)),
            pl.BlockSpec((1, GROUP_W), lambda b, i: (0, i)),
            pl.BlockSpec((1, SSD_STATE), lambda b, i: (0, cwb + i)),
            pl.BlockSpec((1, SSD_STATE), lambda b, i: (0, cwc + i)),
            pl.BlockSpec((None, 2 * SSD_HEADS_PER_GROUP, 1), lambda b, i: (i, 0, 0)),
            pl.BlockSpec((None, 2 * SSD_HEADS_PER_GROUP, 1), lambda b, i: (i, 0, 0)),
            pl.BlockSpec((1, GROUP_W), lambda b, i: (0, i)),
            pl.BlockSpec((1, GROUP_W), lambda b, i: (0, i)),
        ],
        out_specs=pl.BlockSpec((seq, GROUP_W), lambda b, i: (b, i)),
        scratch_shapes=[pltpu.VMEM((seq, GROUP_W), BF16),
                        pltpu.VMEM((seq, SSD_STATE), BF16),
                        pltpu.VMEM((seq, SSD_STATE), BF16),
                        pltpu.VMEM((seq, GROUP_W), F32),
                        pltpu.VMEM((SSD_STATE, GROUP_W), F32),
                        pltpu.VMEM((GROUP_W // LANES + 2, SSD_Q + 2 * CONV_HALO, LANES), F32)],
        compiler_params=_cparams(("arbitrary", "arbitrary")),
        name="ssd_core",
    )(zx, zx, zx, zx, dt_row, conv_w, conv_w, conv_w, conv_b, conv_b, conv_b,
      al_row, db_row, dskip, norm_w)


def _ssd_mixer(x, mod_sc, mod_sh, mod_g, nw, w_in, conv_w, conv_b, a_log, dt_bias, d_skip, norm_w,
               w_out, *, batch, seq):
    g, r = SSD_GROUPS, SSD_HEADS_PER_GROUP
    conv_dim = conv_w.shape[1]
    zx, dt_raw = _nm_matmul(x, nw, mod_sc, mod_sh, w_in, ncols=D_INNER + conv_dim, tn=MM_COLS,
                            out_dtype=BF16, rows_per_batch=seq, tail_cols=2 * g * r)
    nc = seq // SSD_Q
    dt_row = dt_raw.reshape(batch, nc, SSD_Q, 2, g, r).transpose(0, 4, 1, 3, 5, 2)
    dt_row = dt_row.reshape(batch, g, nc, 2 * r, SSD_Q)

    def row_form(p):
        return p.reshape(2, g, r).transpose(1, 0, 2).reshape(g, 2 * r, 1)

    yn = _ssd_core(zx, dt_row, conv_w, conv_b.reshape(1, conv_dim), row_form(a_log), row_form(dt_bias),
                   jnp.repeat(d_skip, SSD_HEAD_DIM).reshape(1, D_INNER), norm_w.reshape(1, D_INNER),
                   batch=batch, seq=seq)
    return _mm_resid(yn, w_out, x, mod_g, rows_per_batch=seq)


def _bias_table_kernel(rpb_ref, o_ref):
    lane = lax.broadcasted_iota(I32, (GRID_W, LANES), 1)
    j = lax.broadcasted_iota(I32, (GRID_W, LANES), 0)
    c = lane & (GRID_W - 1)
    c0 = jnp.clip(j - WIN_W // 2, 0, GRID_W - WIN_W)
    win = (c >= c0) & (c < c0 + WIN_W)
    left = lane < GRID_W

    n_off = o_ref.shape[0]
    for par in range(2):
        for dy in range(2 * WIN_H - 2):
            halves = []
            for sub in range(2):
                row = rpb_ref[par, dy + sub:dy + sub + 1, :]
                shift = (sub * GRID_W - (WIN_W - 1)) % LANES
                halves.append(pltpu.roll(jnp.broadcast_to(row, (GRID_W, LANES)), shift, 1,
                                         stride=1, stride_axis=0))
            tile = jnp.where(win, jnp.where(left, halves[0], halves[1]) * LOG2_E, NEG)
            for m in range(WIN_H // 2):
                dy0 = dy - 2 * m
                if 0 <= dy0 < n_off:
                    o_ref[dy0, 0, par * GRID_W:(par + 1) * GRID_W, m * LANES:(m + 1) * LANES] = tile


def _na_bias_table(rpb):
    h, ndy, ndx = rpb.shape
    rpb_p = jnp.pad(rpb, ((0, 0), (0, 2 * WIN_H - ndy), (0, LANES - ndx)))
    return pl.pallas_call(
        _bias_table_kernel,
        out_shape=jax.ShapeDtypeStruct((WIN_H, h // 2, 2 * GRID_W, WIN_H * GRID_W), F32),
        grid=(h // 2,),
        in_specs=[pl.BlockSpec((2, 2 * WIN_H, LANES), lambda p: (p, 0, 0))],
        out_specs=pl.BlockSpec((WIN_H, 1, 2 * GRID_W, WIN_H * GRID_W), lambda p: (0, p, 0, 0)),
        compiler_params=_cparams(("arbitrary",)),
        name="na_bias_table",
    )(rpb_p)


def _na_kernel(q_ref, kv_hbm, bias_ref, o_ref, kc_s, vc_s, s_s, p_s, r_s, sem, *, n_row_blocks):
    rb = pl.program_id(2)
    npairs = q_ref.shape[0]
    blk = NA_ROWS * GRID_W
    nkeys = WIN_H * GRID_W
    n_sec, n_batch = pl.num_programs(0), pl.num_programs(1)
    step = (pl.program_id(0) * n_batch + pl.program_id(1)) * n_row_blocks + rb
    nsteps = n_sec * n_batch * n_row_blocks

    def fetch(st, half):
        r = st % n_row_blocks
        hb = st // n_row_blocks
        tok0 = ((hb % n_batch) * n_row_blocks + jnp.clip(r - 1, 0, n_row_blocks - 3)) * blk
        tok0 = pl.multiple_of(tok0, blk)

        def one(sec, dst, j):
            first_pair = (sec * n_sec + hb // n_batch) * npairs
            return pltpu.make_async_copy(
                kv_hbm.at[pl.ds(first_pair, npairs), pl.ds(tok0, 3 * blk), :], dst.at[half],
                sem.at[half, j])
        return one(1, kc_s, 0), one(2, vc_s, 1)

    cur = step % 2

    @pl.when(step == 0)
    def _():
        for cp in fetch(step, cur):
            cp.start()

    @pl.when(step + 1 < nsteps)
    def _():
        for cp in fetch(step + 1, 1 - cur):
            cp.start()

    for cp in fetch(step, cur):
        cp.wait()
    first = rb == 0
    last = rb == n_row_blocks - 1
    edge = first | last
    lane = lax.broadcasted_iota(I32, (GRID_W, LANES), 1)
    left = lane < NA_HEAD_DIM

    def window(qi):
        off = jnp.where(first, 0, jnp.where(last, blk, qi * GRID_W))
        li = jnp.where(edge, NA_ROWS - 1 - qi, NA_ROWS - 1)
        return pl.multiple_of(off, GRID_W), li

    def pair_body(pp, carry):
        for q0 in range(0, NA_ROWS, NA_STAGE):
            rows_q = range(q0, q0 + NA_STAGE)
            for qi in rows_q:
                off, li = window(qi)
                q2 = q_ref[pp, qi * GRID_W:(qi + 1) * GRID_W, :]
                zero = jnp.zeros_like(q2)
                qs = jnp.concatenate([jnp.where(left, q2, zero), jnp.where(left, zero, q2)], axis=0)
                kw = kc_s[cur, pp, pl.ds(off, nkeys), :]
                s_s[qi] = _dot_nt(qs, kw) + bias_ref[li, pp]
            for qi in rows_q:
                s = s_s[qi]
                p = jnp.exp2(s - jnp.max(s, axis=-1, keepdims=True))
                r_s[qi] = 1.0 / jnp.sum(p, axis=-1, keepdims=True)
                p_s[qi] = p.astype(BF16)
            for qi in rows_q:
                off, _ = window(qi)
                pv = _dot(p_s[qi], vc_s[cur, pp, pl.ds(off, nkeys), :]) * r_s[qi]
                o = jnp.where(left, pv[0:GRID_W], pv[GRID_W:2 * GRID_W])
                o_ref[pp, qi * GRID_W:(qi + 1) * GRID_W, :] = o.astype(BF16)
        return carry

    lax.fori_loop(0, npairs, pair_body, 0, unroll=4)


def _na_attention(qkv_t, bias_tab, *, batch, seq):
    t = batch * seq
    blk = NA_ROWS * GRID_W
    nrb = seq // blk
    hp = NA_PAIRS // 2
    nsec = NA_PAIRS // hp

    return pl.pallas_call(
        functools.partial(_na_kernel, n_row_blocks=nrb),
        out_shape=jax.ShapeDtypeStruct((NA_PAIRS, t, LANES), BF16),
        grid=(nsec, batch, nrb),
        in_specs=[pl.BlockSpec((hp, blk, LANES), lambda hh, b, r: (hh, b * nrb + r, 0)),
                  pl.BlockSpec(memory_space=pl.ANY),
                  pl.BlockSpec((NA_ROWS, hp, 2 * GRID_W, WIN_H * GRID_W),
                               lambda hh, b, r: (jnp.where(r == 0, 1, 0), hh, 0, 0))],
        out_specs=pl.BlockSpec((hp, blk, LANES), lambda hh, b, r: (hh, b * nrb + r, 0)),
        scratch_shapes=[pltpu.VMEM((2, hp, 3 * blk, LANES), BF16),
                        pltpu.VMEM((2, hp, 3 * blk, LANES), BF16),
                        pltpu.VMEM((NA_ROWS, 2 * GRID_W, WIN_H * GRID_W), F32),
                        pltpu.VMEM((NA_ROWS, 2 * GRID_W, WIN_H * GRID_W), BF16),
                        pltpu.VMEM((NA_ROWS, 2 * GRID_W, 1), F32),
                        pltpu.SemaphoreType.DMA((2, 2))],
        compiler_params=_cparams(("arbitrary", "arbitrary", "arbitrary")),
        name="na_attention",
    )(qkv_t, qkv_t, bias_tab)


def _na_mixer(x, mod_sc, mod_sh, mod_g, nw, w_qkv, rpb, w_o, *, batch, seq):
    qkv_t = _nm_matmul(x, nw, mod_sc, mod_sh, w_qkv, ncols=3 * D_MODEL, tn=MM_COLS,
                       out_dtype=BF16, rows_per_batch=seq, pair_major=True,
                       lead_cols=D_MODEL, lead_scale=NA_HEAD_DIM ** -0.5 * LOG2_E)
    o_t = _na_attention(qkv_t, _na_bias_table(rpb), batch=batch, seq=seq)
    return _mm_resid(o_t, w_o, x, mod_g, rows_per_batch=seq, pair_major=True)


U32 = jnp.uint32


def _pack_halves(vb):
    n = vb.shape[1] // 2
    bits = pltpu.bitcast(vb.astype(F32), U32)
    return (bits[:, :n] >> 16) | bits[:, n:]


def _unpack_halves(w):
    return pltpu.bitcast(w << 16, F32), pltpu.bitcast(w & U32(0xFFFF0000), F32)


TOK_SUB = (D_MODEL // 2) // LANES


def _store_token_tiles(ref, packed):
    rows = packed.shape[0]
    for s in range(TOK_SUB):
        ref[pl.ds(s, rows, stride=TOK_SUB), :] = packed[:, s * LANES:(s + 1) * LANES]


def _load_token_tiles(ref):
    rows = ref.shape[0] // TOK_SUB
    return jnp.concatenate([ref[pl.ds(s, rows, stride=TOK_SUB), :] for s in range(TOK_SUB)], axis=1)


def _router_kernel(x_ref, nw_ref, sc_ref, sh_ref, wr_ref, h_ref, meta_ref, meta_t_ref, cnt_ref,
                   carry_s, whi_s, wlo_s):
    @pl.when(pl.program_id(0) == 0)
    def _():
        carry_s[...] = jnp.zeros_like(carry_s)
        whi_s[...], wlo_s[...] = _split2(wr_ref[...])

    h = _normmod(x_ref[...], nw_ref[...], sc_ref[0], sh_ref[0])
    _store_token_tiles(h_ref, _pack_halves(h.astype(BF16)))
    logits = _dot_split(h, whi_s[...], wlo_s[...])
    tm = logits.shape[0]
    lane_i = lax.broadcasted_iota(I32, logits.shape, 1)
    lane = lane_i.astype(F32)
    big = 1e9
    gl = jnp.where(lane_i < MOE_GROUPS, logits, NEG)
    gmax = jnp.max(gl, axis=1, keepdims=True)
    gsel = jnp.min(jnp.where(gl == gmax, lane, big), axis=1, keepdims=True)
    gw = 1.0 / jnp.sum(jnp.exp(gl - gmax), axis=1, keepdims=True)
    el = lane - MOE_GROUPS
    lo = gsel * MOE_EPG
    emask = (el >= lo) & (el < lo + MOE_EPG)
    e1 = jnp.where(emask, logits, NEG)
    m1 = jnp.max(e1, axis=1, keepdims=True)
    i1 = jnp.min(jnp.where(e1 == m1, el, big), axis=1, keepdims=True)
    e2 = jnp.where(emask & (el != i1), logits, NEG)
    m2 = jnp.max(e2, axis=1, keepdims=True)
    i2 = jnp.min(jnp.where(e2 == m2, el, big), axis=1, keepdims=True)
    tt = jnp.exp(m2 - m1)
    p1 = 1.0 / (1.0 + tt)
    w1 = gw * p1
    w2 = gw * (tt * p1)
    oh1 = el == i1
    oh2 = el == i2
    cnt = (oh1 | oh2).astype(F32)
    r_i = lax.broadcasted_iota(I32, (tm, tm), 0)
    c_i = lax.broadcasted_iota(I32, (tm, tm), 1)
    before = _dot((r_i > c_i).astype(BF16), cnt.astype(BF16)) + carry_s[...]
    rank1 = jnp.sum(jnp.where(oh1, before, 0.0), axis=1, keepdims=True)
    rank2 = jnp.sum(jnp.where(oh2, before, 0.0), axis=1, keepdims=True)
    carry_s[...] = carry_s[...] + jnp.sum(cnt, axis=0, keepdims=True)
    meta = jnp.zeros_like(logits)
    for pos, val in enumerate((i1, i2, w1, w2, rank1, rank2)):
        meta = jnp.where(lane_i == pos, val, meta)
    meta_ref[...] = meta
    meta_t_ref[...] = meta.T[0:meta_t_ref.shape[0], :]
    cnt_ref[...] = jnp.broadcast_to(carry_s[...], cnt_ref.shape)


def _router(x, nw, sc, sh, wr, *, rows_per_batch):
    t, d = x.shape
    tm = TOKEN_TILE
    tiles_per_batch = rows_per_batch // tm
    return pl.pallas_call(
        _router_kernel,
        out_shape=(jax.ShapeDtypeStruct((t * TOK_SUB, LANES), U32),
                   jax.ShapeDtypeStruct((t, LANES), F32),
                   jax.ShapeDtypeStruct((8, t), F32),
                   jax.ShapeDtypeStruct((8, LANES), F32)),
        grid=(t // tm,),
        in_specs=[pl.BlockSpec((tm, d), lambda i: (i, 0)),
                  pl.BlockSpec((1, d), lambda i: (0, 0)),
                  pl.BlockSpec((1, 1, d), lambda i: (i // tiles_per_batch, 0, 0)),
                  pl.BlockSpec((1, 1, d), lambda i: (i // tiles_per_batch, 0, 0)),
                  pl.BlockSpec((d, LANES), lambda i: (0, 0))],
        out_specs=(pl.BlockSpec((tm * TOK_SUB, LANES), lambda i: (i, 0)),
                   pl.BlockSpec((tm, LANES), lambda i: (i, 0)),
                   pl.BlockSpec((8, tm), lambda i: (0, i)),
                   pl.BlockSpec((8, LANES), lambda i: (0, 0))),
        scratch_shapes=[pltpu.VMEM((1, LANES), F32), pltpu.VMEM((d, LANES), BF16),
                        pltpu.VMEM((d, LANES), BF16)],
        compiler_params=_cparams(("arbitrary",)),
        name="moe_router",
    )(x, nw, sc, sh, wr)


def _slot_kernel(pstart_ref, mt_ref, o_ref):
    eid = mt_ref[0:2, :]
    start = jnp.zeros(eid.shape, I32)
    for e in range(N_EXPERTS):
        start = jnp.where(eid == float(e), pstart_ref[e], start)
    o_ref[...] = start + mt_ref[4:6, :].astype(I32)


def _slots(pstart, meta_t):
    t = meta_t.shape[1]
    return pl.pallas_call(
        _slot_kernel,
        out_shape=jax.ShapeDtypeStruct((2, t), I32),
        grid_spec=pltpu.PrefetchScalarGridSpec(
            num_scalar_prefetch=1,
            grid=(1,),
            in_specs=[pl.BlockSpec(meta_t.shape, lambda i, ps: (0, 0))],
            out_specs=pl.BlockSpec((2, t), lambda i, ps: (0, 0))),
        compiler_params=_cparams(("arbitrary",)),
        name="moe_slots",
    )(pstart, meta_t)


def _token_tile(ref, r):
    return ref.at[pl.ds(pl.multiple_of(r * TOK_SUB, TOK_SUB), TOK_SUB)]


def _dispatch_kernel(dest_ref, zflag_ref, h_ref, xs_ref, zbuf, sem, zsem):
    tm = h_ref.shape[0] // TOK_SUB
    base = pl.program_id(0) * tm
    ntok = pl.num_programs(0) * tm
    tb = zbuf.shape[0]

    @pl.when(pl.program_id(0) == 0)
    def _():
        zbuf[...] = jnp.zeros_like(zbuf)

        def zcopy(b):
            return pltpu.make_async_copy(zbuf, xs_ref.at[pl.ds(pl.multiple_of(b * tb, tb), tb)], zsem)

        def zstart(b, carry):
            @pl.when(zflag_ref[b] == 1)
            def _():
                zcopy(b).start()
            return carry

        def zwait(b, carry):
            @pl.when(zflag_ref[b] == 1)
            def _():
                zcopy(b).wait()
            return carry

        nblk = xs_ref.shape[0] // tb
        lax.fori_loop(0, nblk, zstart, 0)
        lax.fori_loop(0, nblk, zwait, 0)

    def copy(t, k):
        return pltpu.make_async_copy(_token_tile(h_ref, t),
                                     _token_tile(xs_ref, dest_ref[k * ntok + base + t]), sem)

    def issue(t, carry):
        copy(t, 0).start(priority=0)
        copy(t, 1).start(priority=1)
        return carry

    def drain(t, carry):
        copy(t, 0).wait()
        copy(t, 1).wait()
        return carry

    lax.fori_loop(0, tm, issue, 0, unroll=DMA_UNROLL)
    lax.fori_loop(0, tm, drain, 0, unroll=DMA_UNROLL)


def _dispatch(dest, zflag, h, n_slots):
    t = h.shape[0] // TOK_SUB
    tm = TOKEN_TILE
    return pl.pallas_call(
        _dispatch_kernel,
        out_shape=jax.ShapeDtypeStruct((n_slots * TOK_SUB, LANES), U32),
        grid_spec=pltpu.PrefetchScalarGridSpec(
            num_scalar_prefetch=2,
            grid=(t // tm,),
            in_specs=[pl.BlockSpec((tm * TOK_SUB, LANES), lambda i, dest, zf: (i, 0))],
            out_specs=pl.BlockSpec(memory_space=pl.ANY),
            scratch_shapes=[pltpu.VMEM((MOE_TB * TOK_SUB, LANES), U32), pltpu.SemaphoreType.DMA(()),
                            pltpu.SemaphoreType.DMA(())]),
        compiler_params=_cparams(("arbitrary",)),
        name="moe_dispatch",
    )(dest, zflag, h)


def _ffn_kernel(be_ref, nxt_ref, slot_ref, nu_ref, xs_ref, w1_hbm, w3_hbm, w2_hbm, o_ref,
                wb1, wb3, wb2, w1_s, w3_s, w2_s, sem, *, layer):
    i = pl.program_id(0)

    def fetch(e, s):
        return (pltpu.make_async_copy(w1_hbm.at[layer, e], wb1.at[s], sem.at[s, 0]),
                pltpu.make_async_copy(w3_hbm.at[layer, e], wb3.at[s], sem.at[s, 1]),
                pltpu.make_async_copy(w2_hbm.at[layer, e], wb2.at[s], sem.at[s, 2]))

    @pl.when(i < nu_ref[0])
    def _():
        e = be_ref[i]
        s = slot_ref[i]

        @pl.when((i == 0) | (e != be_ref[jnp.maximum(i - 1, 0)]))
        def _():
            @pl.when(i == 0)
            def _():
                for cp in fetch(e, s):
                    cp.start()

            for cp in fetch(e, s):
                cp.wait()

            @pl.when(nxt_ref[i] >= 0)
            def _():
                for cp in fetch(nxt_ref[i], 1 - s):
                    cp.start()

            w1_s[...] = wb1[s].astype(BF16)
            w3_s[...] = wb3[s].astype(BF16)
            w2_s[...] = wb2[s].astype(BF16)

        lo, hi = _unpack_halves(_load_token_tiles(xs_ref))
        xl, xh = lo.astype(BF16), hi.astype(BF16)
        half = xl.shape[1]
        a = _dot(xl, w1_s[0:half, :]) + _dot(xh, w1_s[half:2 * half, :])
        b = _dot(xl, w3_s[0:half, :]) + _dot(xh, w3_s[half:2 * half, :])
        hmid = _silu(a) * b
        _store_token_tiles(o_ref, _pack_halves(_dot(hmid.astype(BF16), w2_s[...]).astype(BF16)))

    @pl.when(i >= nu_ref[0])
    def _():
        o_ref[...] = jnp.zeros_like(o_ref)


def _expert_ffn(blk_expert, blk_next, blk_slot, n_used, xs, w1, w3, w2, layer):
    d, f = w1.shape[2], w1.shape[3]
    rows = MOE_TB * TOK_SUB
    nb = xs.shape[0] // rows
    hbm = pl.BlockSpec(memory_space=pl.ANY)
    return pl.pallas_call(
        functools.partial(_ffn_kernel, layer=layer),
        out_shape=jax.ShapeDtypeStruct(xs.shape, U32),
        grid_spec=pltpu.PrefetchScalarGridSpec(
            num_scalar_prefetch=4,
            grid=(nb,),
            in_specs=[pl.BlockSpec((rows, LANES),
                                   lambda i, be, nx, sl, nu: (jnp.minimum(i, nu[0] - 1), 0)),
                      hbm, hbm, hbm],
            out_specs=pl.BlockSpec((rows, LANES), lambda i, be, nx, sl, nu: (i, 0)),
            scratch_shapes=[pltpu.VMEM((2, d, f), F32), pltpu.VMEM((2, d, f), F32),
                            pltpu.VMEM((2, f, d), F32),
                            pltpu.VMEM((d, f), BF16), pltpu.VMEM((d, f), BF16),
                            pltpu.VMEM((f, d), BF16),
                            pltpu.SemaphoreType.DMA((2, 3))]),
        compiler_params=_cparams(("arbitrary",)),
        name="moe_expert_ffn",
    )(blk_expert, blk_next, blk_slot, n_used, xs, w1, w3, w2)


def _combine_kernel(dest_ref, x_ref, meta_ref, g_ref, fnw_ref, ys_ref, o_ref, buf, sem, *, final):
    tm = x_ref.shape[0]
    i = pl.program_id(0)
    nsteps = pl.num_programs(0)
    ntok = nsteps * tm

    def copy(tile, t, k):
        half = tile % 2
        return pltpu.make_async_copy(_token_tile(ys_ref, dest_ref[k * ntok + tile * tm + t]),
                                     _token_tile(buf.at[half, k], t), sem.at[half])

    def issue(tile):
        def body(t, carry):
            copy(tile, t, 0).start(priority=0)
            copy(tile, t, 1).start(priority=1)
            return carry
        lax.fori_loop(0, tm, body, 0, unroll=DMA_UNROLL)

    def drain(tile):
        def body(t, carry):
            copy(tile, t, 0).wait()
            copy(tile, t, 1).wait()
            return carry
        lax.fori_loop(0, tm, body, 0, unroll=DMA_UNROLL)

    @pl.when(i == 0)
    def _():
        issue(i)

    @pl.when(i + 1 < nsteps)
    def _():
        issue(i + 1)

    drain(i)
    cur = i % 2
    meta = meta_ref[...]
    w1, w2 = meta[:, 2:3], meta[:, 3:4]
    lo1, hi1 = _unpack_halves(_load_token_tiles(buf.at[cur, 0]))
    lo2, hi2 = _unpack_halves(_load_token_tiles(buf.at[cur, 1]))
    y = jnp.concatenate([w1 * lo1 + w2 * lo2, w1 * hi1 + w2 * hi2], axis=1)
    xn = x_ref[...] + g_ref[0] * y
    if final:
        ms = jnp.mean(xn * xn, axis=-1, keepdims=True)
        xn = xn * lax.rsqrt(ms + EPS) * fnw_ref[...]
    o_ref[...] = xn


def _combine(dest, x, meta, g, fnw, ys, *, rows_per_batch, final):
    t, d = x.shape
    tm = TOKEN_TILE
    tiles_per_batch = rows_per_batch // tm
    return pl.pallas_call(
        functools.partial(_combine_kernel, final=final),
        out_shape=jax.ShapeDtypeStruct((t, d), F32),
        grid_spec=pltpu.PrefetchScalarGridSpec(
            num_scalar_prefetch=1,
            grid=(t // tm,),
            in_specs=[pl.BlockSpec((tm, d), lambda i, dest: (i, 0)),
                      pl.BlockSpec((tm, LANES), lambda i, dest: (i, 0)),
                      pl.BlockSpec((1, 1, d), lambda i, dest: (i // tiles_per_batch, 0, 0)),
                      pl.BlockSpec((1, d), lambda i, dest: (0, 0)),
                      pl.BlockSpec(memory_space=pl.ANY)],
            out_specs=pl.BlockSpec((tm, d), lambda i, dest: (i, 0)),
            scratch_shapes=[pltpu.VMEM((2, 2, tm * TOK_SUB, LANES), U32),
                            pltpu.SemaphoreType.DMA((2,))]),
        compiler_params=_cparams(("arbitrary",)),
        name="moe_combine",
    )(dest, x, meta, g, fnw, ys)


def _hier_moe(x, mod_sc, mod_sh, mod_g, nw, w_group, w_expert, w1, w3, w2, layer, fnw, *,
              rows_per_batch, final):
    t, d = x.shape
    a = 2 * t
    tb = MOE_TB
    wr = jnp.concatenate([w_group, w_expert], axis=1)
    wr = jnp.pad(wr, ((0, 0), (0, LANES - wr.shape[1])))
    h, meta, meta_t, cnt = _router(x, nw, mod_sc, mod_sh, wr, rows_per_batch=rows_per_batch)
    ne = N_EXPERTS
    counts = cnt[0, MOE_GROUPS:MOE_GROUPS + ne].astype(I32)
    padded = ((counts + tb - 1) // tb) * tb
    pend = jnp.cumsum(padded)
    pstart = pend - padded
    dest = _slots(pstart, meta_t).reshape(a)
    nb = (a + ne * (tb - 1) + tb - 1) // tb
    n_used = (pend[-1] // tb).astype(I32)
    blk = jnp.arange(nb, dtype=I32)
    be = jnp.minimum(jnp.sum((pend[None, :] <= (blk * tb)[:, None]).astype(I32), axis=1), ne - 1)
    seg_last = jnp.any((pend[None, :] == ((blk + 1) * tb)[:, None]) & (padded[None, :] > 0), axis=1)
    zflag = (seg_last | (blk >= n_used)).astype(I32)
    nonempty = counts > 0
    seg = jnp.cumsum(nonempty.astype(I32)) - 1
    later = lax.cummin(jnp.where(nonempty, jnp.arange(ne, dtype=I32), ne), axis=0, reverse=True)
    nxt_e = jnp.concatenate([later[1:], jnp.full((1,), ne, I32)])
    nxt_e = jnp.where(nxt_e == ne, -1, nxt_e)
    xs = _dispatch(dest, zflag, h, nb * tb)
    ys = _expert_ffn(be, nxt_e[be], seg[be] % 2, n_used.reshape(1), xs, w1, w3, w2, layer)
    return _combine(dest, x, meta, mod_g, fnw, ys, rows_per_batch=rows_per_batch, final=final)


def kernel(x, c, ada_w, ada_b, norm_mix, norm_ffn, ssd_w_in, ssd_conv_w, ssd_conv_b, ssd_a_log,
           ssd_dt_bias, ssd_d, ssd_norm_w, ssd_w_out, na_w_qkv, na_rpb, na_w_o,
           moe_w_group, moe_w_expert, moe_w1, moe_w3, moe_w2, final_norm):
    batch, seq, d = x.shape
    depth = ada_w.shape[0]
    xt = x.reshape(batch * seq, d)
    c_pad = jnp.pad(c, ((0, 8 - batch), (0, 0)))
    mod = _ada(c_pad, ada_w, ada_b)[:, :batch]
    fnw = final_norm.reshape(1, d)
    for i in range(depth):
        sh1, sc1, g1, sh2, sc2, g2 = [mod[i, :, k * d:(k + 1) * d].reshape(batch, 1, d)
                                      for k in range(6)]
        j = i // 2
        nw = norm_mix[i].reshape(1, d)
        if i % 2 == 0:
            xt = _ssd_mixer(xt, sc1, sh1, g1, nw, ssd_w_in[j], ssd_conv_w[j], ssd_conv_b[j],
                            ssd_a_log[j], ssd_dt_bias[j], ssd_d[j], ssd_norm_w[j], ssd_w_out[j],
                            batch=batch, seq=seq)
        else:
            xt = _na_mixer(xt, sc1, sh1, g1, nw, na_w_qkv[j], na_rpb[j], na_w_o[j],
                           batch=batch, seq=seq)
        xt = _hier_moe(xt, sc2, sh2, g2, norm_ffn[i].reshape(1, d), moe_w_group[i], moe_w_expert[i],
                       moe_w1, moe_w3, moe_w2, i, fnw, rows_per_batch=seq,
                       final=(i == depth - 1))
    return xt.reshape(batch, seq, d)
```

```python
import functools

import jax
import jax.numpy as jnp
from jax import lax
from jax.experimental import pallas as pl
from jax.experimental.pallas import tpu as pltpu

F32 = jnp.float32
BF16 = jnp.bfloat16
I32 = jnp.int32

EPS = 1e-6
NEG = -1e30
LOG2_E = 1.4426950408889634

D_MODEL = 2048
GRID_W = 64
SSD_HEAD_DIM = 64
SSD_GROUPS = 8
SSD_HEADS_PER_GROUP = 8
SSD_STATE = 128
SSD_CONV = 5
D_INNER = 2 * D_MODEL
GROUP_W = SSD_HEADS_PER_GROUP * SSD_HEAD_DIM
SSD_Q = 128
CONV_HALO = 16
NA_HEAD_DIM = 64
NA_HEADS = D_MODEL // NA_HEAD_DIM
NA_PAIRS = NA_HEADS // 2
WIN_H = 8
WIN_W = 16
NA_ROWS = 4
MOE_GROUPS = 4
MOE_EPG = 8
N_EXPERTS = MOE_GROUPS * MOE_EPG
MOE_D_FF = D_MODEL // 4
MOE_TB = 256
DMA_UNROLL = 8
VMEM_LIMIT = 56 * 1024 * 1024
LANES = 128

MM_ROWS = 1024
MM_COLS = 1024
RESID_COLS = 512
W_RING = 3
RESID_LHS_ELEMS = 4 * 1024 * 1024
ADA_COLS = 2048
TOKEN_TILE = 256


def _cparams(sem):
    return pltpu.CompilerParams(dimension_semantics=sem, vmem_limit_bytes=VMEM_LIMIT)


def _silu(v):
    return v * pl.reciprocal(1.0 + jnp.exp(-v), approx=True)


def _softplus(v):
    return jnp.maximum(v, 0.0) + jnp.log1p(jnp.exp(-jnp.abs(v)))


def _split3(v):
    hi = v.astype(BF16)
    r1 = v - hi.astype(F32)
    mid = r1.astype(BF16)
    lo = (r1 - mid.astype(F32)).astype(BF16)
    return hi, mid, lo


def _dot(a, b):
    return jnp.dot(a, b, preferred_element_type=F32)


def _dot_nt(a, b):
    return lax.dot_general(a, b, (((1,), (1,)), ((), ())), preferred_element_type=F32)


def _dot3_left(v, sel):
    hi, mid, lo = _split3(v)
    return _dot(hi, sel) + _dot(mid, sel) + _dot(lo, sel)


def _normmod(x, nw, sc, sh):
    ms = jnp.mean(x * x, axis=-1, keepdims=True)
    return (x * lax.rsqrt(ms + EPS) * nw) * (1.0 + sc) + sh


def _split2(v):
    hi = v.astype(BF16)
    return hi, (v - hi.astype(F32)).astype(BF16)


def _dot_split(a, b_hi, b_lo):
    a_hi, a_lo = _split2(a)
    return _dot(a_hi, b_hi) + (_dot(a_lo, b_hi) + _dot(a_hi, b_lo))


def _ada_kernel(c_ref, w_ref, b_ref, o_ref):
    c = c_ref[...]
    o_ref[0] = _dot_split(c / (1.0 + jnp.exp(-c)), *_split2(w_ref[0])) + b_ref[0]


def _ada(c_pad, ada_w, ada_b):
    depth, d, n = ada_w.shape
    tn = ADA_COLS
    return pl.pallas_call(
        _ada_kernel,
        out_shape=jax.ShapeDtypeStruct((depth, 8, n), F32),
        grid=(depth, n // tn),
        in_specs=[pl.BlockSpec((8, d), lambda i, j: (0, 0)),
                  pl.BlockSpec((1, d, tn), lambda i, j: (i, 0, j)),
                  pl.BlockSpec((1, 1, tn), lambda i, j: (i, 0, j))],
        out_specs=pl.BlockSpec((1, 8, tn), lambda i, j: (i, 0, j)),
        compiler_params=_cparams(("arbitrary", "arbitrary")),
        name="ada_mod",
    )(c_pad, ada_w, ada_b.reshape(depth, 1, n))


def _nm_mm_kernel(x_ref, nw_ref, sc_ref, sh_ref, w_ref, *rest, pair_major, tail, lead_tiles,
                  lead_scale):
    if tail:
        wt_ref, o_ref, ot_ref, h_ref = rest
    else:
        o_ref, h_ref = rest

    @pl.when(pl.program_id(1) == 0)
    def _():
        h_ref[...] = _normmod(x_ref[...], nw_ref[...], sc_ref[0], sh_ref[0]).astype(BF16)

    r = _dot(h_ref[...], w_ref[...].astype(BF16))
    if lead_tiles:
        r = r * jnp.where(pl.program_id(1) < lead_tiles, lead_scale, 1.0)
    if pair_major:
        for c in range(o_ref.shape[0]):
            o_ref[c] = r[:, c * LANES:(c + 1) * LANES].astype(o_ref.dtype)
    else:
        o_ref[...] = r.astype(o_ref.dtype)

    if tail:
        @pl.when(pl.program_id(1) == pl.num_programs(1) - 1)
        def _():
            ot_ref[...] = _dot(h_ref[...], wt_ref[...].astype(BF16))


def _nm_matmul(x, nw, sc, sh, w, *, ncols, tn, out_dtype, rows_per_batch, pair_major=False,
               tail_cols=0, lead_cols=0, lead_scale=1.0):
    t, d = x.shape
    tm = MM_ROWS
    tiles_per_batch = rows_per_batch // tm
    if pair_major:
        out_shape = jax.ShapeDtypeStruct((ncols // LANES, t, LANES), out_dtype)
        out_spec = pl.BlockSpec((tn // LANES, tm, LANES), lambda i, j: (j, i, 0))
    else:
        out_shape = jax.ShapeDtypeStruct((t, ncols), out_dtype)
        out_spec = pl.BlockSpec((tm, tn), lambda i, j: (i, j))
    in_specs = [pl.BlockSpec((tm, d), lambda i, j: (i, 0)),
                pl.BlockSpec((1, d), lambda i, j: (0, 0)),
                pl.BlockSpec((1, 1, d), lambda i, j: (i // tiles_per_batch, 0, 0)),
                pl.BlockSpec((1, 1, d), lambda i, j: (i // tiles_per_batch, 0, 0)),
                pl.BlockSpec((d, tn), lambda i, j: (0, j))]
    operands = [x, nw, sc, sh, w]
    if tail_cols:
        tail_blk = ncols // tail_cols
        in_specs.append(pl.BlockSpec((d, tail_cols), lambda i, j: (0, tail_blk)))
        operands.append(w)
        out_shape = (out_shape, jax.ShapeDtypeStruct((t, tail_cols), F32))
        out_spec = (out_spec, pl.BlockSpec((tm, tail_cols), lambda i, j: (i, 0)))
    return pl.pallas_call(
        functools.partial(_nm_mm_kernel, pair_major=pair_major, tail=bool(tail_cols),
                          lead_tiles=lead_cols // tn, lead_scale=lead_scale),
        out_shape=out_shape,
        grid=(t // tm, ncols // tn),
        in_specs=in_specs,
        out_specs=out_spec,
        scratch_shapes=[pltpu.VMEM((tm, d), BF16)],
        compiler_params=_cparams(("arbitrary", "arbitrary")),
        name="norm_mod_matmul",
    )(*operands)


def _mm_resid_kernel(a_ref, w_hbm, x_ref, g_ref, o_ref, wbuf, sem, *, pair_major):
    nj = pl.num_programs(1)
    nsteps = pl.num_programs(0) * nj
    s = pl.program_id(0) * nj + pl.program_id(1)
    tn = o_ref.shape[1]

    def fetch(st):
        col = pl.multiple_of((st % nj) * tn, tn)
        slot = st % W_RING
        return pltpu.make_async_copy(w_hbm.at[:, pl.ds(col, tn)], wbuf.at[slot], sem.at[slot])

    @pl.when(s == 0)
    def _():
        for d in range(W_RING - 1):
            fetch(d).start()

    @pl.when(s + W_RING - 1 < nsteps)
    def _():
        fetch(s + W_RING - 1).start()

    fetch(s).wait()
    if pair_major:
        a = jnp.concatenate([a_ref[c] for c in range(a_ref.shape[0])], axis=1)
    else:
        a = a_ref[...]
    o_ref[...] = x_ref[...] + g_ref[0] * _dot(a, wbuf[s % W_RING].astype(BF16))


def _mm_resid(a, w, x, g, *, rows_per_batch, pair_major=False):
    t, n = x.shape
    k = w.shape[0]
    tn = RESID_COLS
    tm = RESID_LHS_ELEMS // k
    tiles_per_batch = rows_per_batch // tm
    if pair_major:
        a_spec = pl.BlockSpec((k // LANES, tm, LANES), lambda i, j: (0, i, 0))
    else:
        a_spec = pl.BlockSpec((tm, k), lambda i, j: (i, 0))
    return pl.pallas_call(
        functools.partial(_mm_resid_kernel, pair_major=pair_major),
        out_shape=jax.ShapeDtypeStruct((t, n), F32),
        grid=(t // tm, n // tn),
        in_specs=[a_spec,
                  pl.BlockSpec(memory_space=pl.ANY),
                  pl.BlockSpec((tm, tn), lambda i, j: (i, j)),
                  pl.BlockSpec((1, 1, tn), lambda i, j: (i // tiles_per_batch, 0, j))],
        out_specs=pl.BlockSpec((tm, tn), lambda i, j: (i, j)),
        scratch_shapes=[pltpu.VMEM((W_RING, k, tn), F32), pltpu.SemaphoreType.DMA((W_RING,))],
        compiler_params=_cparams(("arbitrary", "arbitrary")),
        name="matmul_resid",
    )(a, w, x, g)


def _ssd_kernel(z_ref, x_ref, b_ref, c_ref, dtr_ref,
                cwx_ref, cwb_ref, cwc_ref, cbx_ref, cbb_ref, cbc_ref,
                alr_ref, dbr_ref, dsk_ref, nw_ref,
                o_ref,
                xc_s, bc_s, cc_s, yacc_s, st_s, cv_s):
    seq = x_ref.shape[0]
    q = SSD_Q
    nc = seq // q
    halo = CONV_HALO
    nrow = 2 * SSD_HEADS_PER_GROUP

    def conv_piece(j, base, src_ref, w_ref, bias_ref, dst_ref, lo, stage):
        cols = slice(lo, lo + LANES)
        pstart = pl.multiple_of(jnp.maximum(base - halo, 0), halo)
        nstart = pl.multiple_of(jnp.minimum(base + q, seq - halo), halo)
        stage[0:halo, :] = jnp.where(j > 0, src_ref[pl.ds(pstart, halo), cols].astype(F32), 0.0)
        stage[halo:halo + q, :] = src_ref[pl.ds(base, q), cols].astype(F32)
        stage[halo + q:, :] = jnp.where(j < nc - 1, src_ref[pl.ds(nstart, halo), cols].astype(F32), 0.0)
        acc = jnp.broadcast_to(bias_ref[:, cols], (q, LANES))
        for k in range(SSD_CONV):
            first = halo - SSD_CONV // 2 + k
            acc = acc + w_ref[k:k + 1, cols] * stage[first:first + q, :]
        dst_ref[pl.ds(base, q), cols] = _silu(acc).astype(BF16)

    def conv_chunk(j, carry):
        base = pl.multiple_of(j * q, q)
        npx = GROUP_W // LANES
        for i in range(npx):
            conv_piece(j, base, x_ref, cwx_ref, cbx_ref, xc_s, i * LANES, cv_s.at[i])
        conv_piece(j, base, b_ref, cwb_ref, cbb_ref, bc_s, 0, cv_s.at[npx])
        conv_piece(j, base, c_ref, cwc_ref, cbc_ref, cc_s, 0, cv_s.at[npx + 1])
        return carry

    lax.fori_loop(0, nc, conv_chunk, 0)

    row_i = lax.broadcasted_iota(I32, (q, q), 0)
    col_i = lax.broadcasted_iota(I32, (q, q), 1)
    lower = row_i >= col_i
    upper = row_i <= col_i
    lower_b = lower.astype(BF16)
    upper_b = upper.astype(BF16)
    left =lax.broadcasted_iota(I32, (q, LANES), 1) < SSD_HEAD_DIM
    a_row = -jnp.exp(alr_ref[...])
    pad_rows = jnp.zeros((LANES - nrow, q), F32)

    def scan_pass(direction):
        hoff = direction * SSD_HEADS_PER_GROUP
        mask = lower if direction == 0 else upper
        tri = upper_b if direction == 0 else lower_b
        edge = q - 1 if direction == 0 else 0
        st_s[...] = jnp.zeros_like(st_s)

        def chunk(t, carry):
            c = t if direction == 0 else nc - 1 - t
            base = pl.multiple_of(c * q, q)
            rows = pl.ds(base, q)
            dt_r = _softplus(dtr_ref[c] + dbr_ref[...])
            cum_r = _dot3_left(dt_r * a_row, tri) * LOG2_E
            cum_c = jnp.concatenate([cum_r, pad_rows], axis=0).T
            decdt_r = jnp.exp2(cum_r[:, edge:edge + 1] - cum_r) * dt_r
            src_r = cum_r - jnp.log2(dt_r)

            bm = bc_s[rows, :]
            cm = cc_s[rows, :]
            cb = _dot_nt(cm, bm)
            bm_t = bm.astype(F32).T
            y_off_all = _dot(cm, st_s[...].astype(BF16)) if direction == 1 else None

            ssq = jnp.zeros((q, 1), F32)
            for pp in range(SSD_HEADS_PER_GROUP // 2):
                cols = slice(pp * LANES, (pp + 1) * LANES)
                xcb = xc_s[rows, cols]
                zero_b = jnp.zeros_like(xcb)
                st_in = st_s[:, cols]
                lhs_y, lhs_s, scales = [], [], []
                for par in range(2):
                    j = hoff + 2 * pp + par
                    cum_b = jnp.broadcast_to(cum_c[:, j:j + 1], (q, q))
                    lmat = jnp.exp2(jnp.where(mask, cum_b - src_r[j:j + 1, :], NEG))
                    lhs_y.append((cb * lmat).astype(BF16))
                    lhs_s.append((bm_t * decdt_r[j:j + 1, :]).astype(BF16))
                    scales.append(jnp.exp2(cum_b))
                x_rhs = jnp.concatenate([jnp.where(left, xcb, zero_b), jnp.where(left, zero_b, xcb)],
                                        axis=0)
                sc_tile = jnp.where(left, scales[0], scales[1])
                y_off = _dot(cm, st_in.astype(BF16)) if y_off_all is None else y_off_all[:, cols]
                y = _dot(jnp.concatenate(lhs_y, axis=1), x_rhs) + y_off * sc_tile
                st_s[:, cols] = (st_in * sc_tile[edge:edge + 1, :]
                                 + _dot(jnp.concatenate(lhs_s, axis=1), x_rhs))
                if direction == 0:
                    yacc_s[rows, cols] = y
                else:
                    total = yacc_s[rows, cols] + y + xcb.astype(F32) * dsk_ref[:, cols]
                    gated = total * _silu(z_ref[rows, cols].astype(F32))
                    ssq = ssq + jnp.sum(gated * gated, axis=-1, keepdims=True)
                    yacc_s[rows, cols] = gated
            if direction == 1:
                inv = lax.rsqrt(ssq * (1.0 / GROUP_W) + EPS)
                o_ref[rows, :] = (yacc_s[rows, :] * inv * nw_ref[...]).astype(BF16)
            return carry

        lax.fori_loop(0, nc, chunk, 0, unroll=16)

    scan_pass(0)
    scan_pass(1)


def _ssd_core(zx, dt_row, conv_w, conv_b, al_row, db_row, dskip, norm_w, *, batch, seq):
    g = SSD_GROUPS
    nc = seq // SSD_Q
    xb = D_INNER // GROUP_W
    bb = (2 * D_INNER) // SSD_STATE
    cb = bb + g
    cwb = D_INNER // SSD_STATE
    cwc = cwb + g
    return pl.pallas_call(
        _ssd_kernel,
        out_shape=jax.ShapeDtypeStruct((batch * seq, D_INNER), BF16),
        grid=(batch, g),
        in_specs=[
            pl.BlockSpec((seq, GROUP_W), lambda b, i: (b, i)),
            pl.BlockSpec((seq, GROUP_W), lambda b, i: (b, xb + i)),
            pl.BlockSpec((seq, SSD_STATE), lambda b, i: (b, bb + i)),
            pl.BlockSpec((seq, SSD_STATE), lambda b, i: (b, cb + i)),
            pl.BlockSpec((None, None, nc, 2 * SSD_HEADS_PER_GROUP, SSD_Q), lambda b, i: (b, i, 0, 0, 0)),
            pl.BlockSpec((SSD_CONV, GROUP_W), lambda b, i: (0, i)),
            pl.BlockSpec((SSD_CONV, SSD_STATE), lambda b, i: (0, cwb + i)),
            pl.BlockSpec((SSD_CONV, SSD_STATE), lambda b, i: (0, cwc + i)),
            pl.BlockSpec((1, GROUP_W), lambda b, i: (0, i)),
            pl.BlockSpec((1, SSD_STATE), lambda b, i: (0, cwb + i)),
            pl.BlockSpec((1, SSD_STATE), lambda b, i: (0, cwc + i)),
            pl.BlockSpec((None, 2 * SSD_HEADS_PER_GROUP, 1), lambda b, i: (i, 0, 0)),
            pl.BlockSpec((None, 2 * SSD_HEADS_PER_GROUP, 1), lambda b, i: (i, 0, 0)),
            pl.BlockSpec((1, GROUP_W), lambda b, i: (0, i)),
            pl.BlockSpec((1, GROUP_W), lambda b, i: (0, i)),
        ],
        out_specs=pl.BlockSpec((seq, GROUP_W), lambda b, i: (b, i)),
        scratch_shapes=[pltpu.VMEM((seq, GROUP_W), BF16),
                        pltpu.VMEM((seq, SSD_STATE), BF16),
                        pltpu.VMEM((seq, SSD_STATE), BF16),
                        pltpu.VMEM((seq, GROUP_W), F32),
                        pltpu.VMEM((SSD_STATE, GROUP_W), F32),
                        pltpu.VMEM((GROUP_W // LANES + 2, SSD_Q + 2 * CONV_HALO, LANES), F32)],
        compiler_params=_cparams(("arbitrary", "arbitrary")),
        name="ssd_core",
    )(zx, zx, zx, zx, dt_row, conv_w, conv_w, conv_w, conv_b, conv_b, conv_b,
      al_row, db_row, dskip, norm_w)


def _ssd_mixer(x, mod_sc, mod_sh, mod_g, nw, w_in, conv_w, conv_b, a_log, dt_bias, d_skip, norm_w,
               w_out, *, batch, seq):
    g, r = SSD_GROUPS, SSD_HEADS_PER_GROUP
    conv_dim = conv_w.shape[1]
    zx, dt_raw = _nm_matmul(x, nw, mod_sc, mod_sh, w_in, ncols=D_INNER + conv_dim, tn=MM_COLS,
                            out_dtype=BF16, rows_per_batch=seq, tail_cols=2 * g * r)
    nc = seq // SSD_Q
    dt_row = dt_raw.reshape(batch, nc, SSD_Q, 2, g, r).transpose(0, 4, 1, 3, 5, 2)
    dt_row = dt_row.reshape(batch, g, nc, 2 * r, SSD_Q)

    def row_form(p):
        return p.reshape(2, g, r).transpose(1, 0, 2).reshape(g, 2 * r, 1)

    yn = _ssd_core(zx, dt_row, conv_w, conv_b.reshape(1, conv_dim), row_form(a_log), row_form(dt_bias),
                   jnp.repeat(d_skip, SSD_HEAD_DIM).reshape(1, D_INNER), norm_w.reshape(1, D_INNER),
                   batch=batch, seq=seq)
    return _mm_resid(yn, w_out, x, mod_g, rows_per_batch=seq)


def _bias_table_kernel(rpb_ref, o_ref):
    lane = lax.broadcasted_iota(I32, (GRID_W, LANES), 1)
    j = lax.broadcasted_iota(I32, (GRID_W, LANES), 0)
    c = lane & (GRID_W - 1)
    c0 = jnp.clip(j - WIN_W // 2, 0, GRID_W - WIN_W)
    win = (c >= c0) & (c < c0 + WIN_W)
    left = lane < GRID_W

    n_off = o_ref.shape[0]
    for par in range(2):
        for dy in range(2 * WIN_H - 2):
            halves = []
            for sub in range(2):
                row = rpb_ref[par, dy + sub:dy + sub + 1, :]
                shift = (sub * GRID_W - (WIN_W - 1)) % LANES
                halves.append(pltpu.roll(jnp.broadcast_to(row, (GRID_W, LANES)), shift, 1,
                                         stride=1, stride_axis=0))
            tile = jnp.where(win, jnp.where(left, halves[0], halves[1]) * LOG2_E, NEG)
            for m in range(WIN_H // 2):
                dy0 = dy - 2 * m
                if 0 <= dy0 < n_off:
                    o_ref[dy0, 0, par * GRID_W:(par + 1) * GRID_W, m * LANES:(m + 1) * LANES] = tile


def _na_bias_table(rpb):
    h, ndy, ndx = rpb.shape
    rpb_p = jnp.pad(rpb, ((0, 0), (0, 2 * WIN_H - ndy), (0, LANES - ndx)))
    return pl.pallas_call(
        _bias_table_kernel,
        out_shape=jax.ShapeDtypeStruct((WIN_H, h // 2, 2 * GRID_W, WIN_H * GRID_W), F32),
        grid=(h // 2,),
        in_specs=[pl.BlockSpec((2, 2 * WIN_H, LANES), lambda p: (p, 0, 0))],
        out_specs=pl.BlockSpec((WIN_H, 1, 2 * GRID_W, WIN_H * GRID_W), lambda p: (0, p, 0, 0)),
        compiler_params=_cparams(("arbitrary",)),
        name="na_bias_table",
    )(rpb_p)


def _na_kernel(q_ref, kv_hbm, bias_ref, o_ref, kc_s, vc_s, s_s, p_s, r_s, sem, *, n_row_blocks):
    rb = pl.program_id(2)
    npairs = q_ref.shape[0]
    blk = NA_ROWS * GRID_W
    nkeys = WIN_H * GRID_W
    n_sec, n_batch = pl.num_programs(0), pl.num_programs(1)
    step = (pl.program_id(0) * n_batch + pl.program_id(1)) * n_row_blocks + rb
    nsteps = n_sec * n_batch * n_row_blocks

    def fetch(st, half):
        r = st % n_row_blocks
        hb = st // n_row_blocks
        tok0 = ((hb % n_batch) * n_row_blocks + jnp.clip(r - 1, 0, n_row_blocks - 3)) * blk
        tok0 = pl.multiple_of(tok0, blk)

        def one(sec, dst, j):
            first_pair = (sec * n_sec + hb // n_batch) * npairs
            return pltpu.make_async_copy(
                kv_hbm.at[pl.ds(first_pair, npairs), pl.ds(tok0, 3 * blk), :], dst.at[half],
                sem.at[half, j])
        return one(1, kc_s, 0), one(2, vc_s, 1)

    cur = step % 2

    @pl.when(step == 0)
    def _():
        for cp in fetch(step, cur):
            cp.start()

    @pl.when(step + 1 < nsteps)
    def _():
        for cp in fetch(step + 1, 1 - cur):
            cp.start()

    for cp in fetch(step, cur):
        cp.wait()
    first = rb == 0
    last = rb == n_row_blocks - 1
    edge = first | last
    lane = lax.broadcasted_iota(I32, (GRID_W, LANES), 1)
    left = lane < NA_HEAD_DIM

    def window(qi):
        off = jnp.where(first, 0, jnp.where(last, blk, qi * GRID_W))
        li = jnp.where(edge, NA_ROWS - 1 - qi, NA_ROWS - 1)
        return pl.multiple_of(off, GRID_W), li

    def pair_body(pp, carry):
        for qi in range(NA_ROWS):
            off, li = window(qi)
            q2 = q_ref[pp, qi * GRID_W:(qi + 1) * GRID_W, :]
            zero = jnp.zeros_like(q2)
            qs = jnp.concatenate([jnp.where(left, q2, zero), jnp.where(left, zero, q2)], axis=0)
            kw = kc_s[cur, pp, pl.ds(off, nkeys), :]
            s_s[qi] = _dot_nt(qs, kw) + bias_ref[li, pp]
        for qi in range(NA_ROWS):
            s = s_s[qi]
            p = jnp.exp2(s - jnp.max(s, axis=-1, keepdims=True))
            r_s[qi] = 1.0 / jnp.sum(p, axis=-1, keepdims=True)
            p_s[qi] = p.astype(BF16)
        for qi in range(NA_ROWS):
            off, _ = window(qi)
            pv = _dot(p_s[qi], vc_s[cur, pp, pl.ds(off, nkeys), :]) * r_s[qi]
            o = jnp.where(left, pv[0:GRID_W], pv[GRID_W:2 * GRID_W])
            o_ref[pp, qi * GRID_W:(qi + 1) * GRID_W, :] = o.astype(BF16)
        return carry

    lax.fori_loop(0, npairs, pair_body, 0, unroll=4)


def _na_attention(qkv_t, bias_tab, *, batch, seq):
    t = batch * seq
    blk = NA_ROWS * GRID_W
    nrb = seq // blk
    hp = NA_PAIRS // 2
    nsec = NA_PAIRS // hp

    return pl.pallas_call(
        functools.partial(_na_kernel, n_row_blocks=nrb),
        out_shape=jax.ShapeDtypeStruct((NA_PAIRS, t, LANES), BF16),
        grid=(nsec, batch, nrb),
        in_specs=[pl.BlockSpec((hp, blk, LANES), lambda hh, b, r: (hh, b * nrb + r, 0)),
                  pl.BlockSpec(memory_space=pl.ANY),
                  pl.BlockSpec((NA_ROWS, hp, 2 * GRID_W, WIN_H * GRID_W),
                               lambda hh, b, r: (jnp.where(r == 0, 1, 0), hh, 0, 0))],
        out_specs=pl.BlockSpec((hp, blk, LANES), lambda hh, b, r: (hh, b * nrb + r, 0)),
        scratch_shapes=[pltpu.VMEM((2, hp, 3 * blk, LANES), BF16),
                        pltpu.VMEM((2, hp, 3 * blk, LANES), BF16),
                        pltpu.VMEM((NA_ROWS, 2 * GRID_W, WIN_H * GRID_W), F32),
                        pltpu.VMEM((NA_ROWS, 2 * GRID_W, WIN_H * GRID_W), BF16),
                        pltpu.VMEM((NA_ROWS, 2 * GRID_W, 1), F32),
                        pltpu.SemaphoreType.DMA((2, 2))],
        compiler_params=_cparams(("arbitrary", "arbitrary", "arbitrary")),
        name="na_attention",
    )(qkv_t, qkv_t, bias_tab)


def _na_mixer(x, mod_sc, mod_sh, mod_g, nw, w_qkv, rpb, w_o, *, batch, seq):
    qkv_t = _nm_matmul(x, nw, mod_sc, mod_sh, w_qkv, ncols=3 * D_MODEL, tn=MM_COLS,
                       out_dtype=BF16, rows_per_batch=seq, pair_major=True,
                       lead_cols=D_MODEL, lead_scale=NA_HEAD_DIM ** -0.5 * LOG2_E)
    o_t = _na_attention(qkv_t, _na_bias_table(rpb), batch=batch, seq=seq)
    return _mm_resid(o_t, w_o, x, mod_g, rows_per_batch=seq, pair_major=True)


U32 = jnp.uint32


def _pack_halves(vb):
    n = vb.shape[1] // 2
    bits = pltpu.bitcast(vb.astype(F32), U32)
    return (bits[:, :n] >> 16) | bits[:, n:]


def _unpack_halves(w):
    return pltpu.bitcast(w << 16, F32), pltpu.bitcast(w & U32(0xFFFF0000), F32)


TOK_SUB = (D_MODEL // 2) // LANES


def _store_token_tiles(ref, packed):
    rows = packed.shape[0]
    for s in range(TOK_SUB):
        ref[pl.ds(s, rows, stride=TOK_SUB), :] = packed[:, s * LANES:(s + 1) * LANES]


def _load_token_tiles(ref):
    rows = ref.shape[0] // TOK_SUB
    return jnp.concatenate([ref[pl.ds(s, rows, stride=TOK_SUB), :] for s in range(TOK_SUB)], axis=1)


def _router_kernel(x_ref, nw_ref, sc_ref, sh_ref, wr_ref, h_ref, meta_ref, meta_t_ref, cnt_ref,
                   carry_s, whi_s, wlo_s):
    @pl.when(pl.program_id(0) == 0)
    def _():
        carry_s[...] = jnp.zeros_like(carry_s)
        whi_s[...], wlo_s[...] = _split2(wr_ref[...])

    h = _normmod(x_ref[...], nw_ref[...], sc_ref[0], sh_ref[0])
    _store_token_tiles(h_ref, _pack_halves(h.astype(BF16)))
    logits = _dot_split(h, whi_s[...], wlo_s[...])
    tm = logits.shape[0]
    lane_i = lax.broadcasted_iota(I32, logits.shape, 1)
    lane = lane_i.astype(F32)
    big = 1e9
    gl = jnp.where(lane_i < MOE_GROUPS, logits, NEG)
    gmax = jnp.max(gl, axis=1, keepdims=True)
    gsel = jnp.min(jnp.where(gl == gmax, lane, big), axis=1, keepdims=True)
    gw = 1.0 / jnp.sum(jnp.exp(gl - gmax), axis=1, keepdims=True)
    el = lane - MOE_GROUPS
    lo = gsel * MOE_EPG
    emask = (el >= lo) & (el < lo + MOE_EPG)
    e1 = jnp.where(emask, logits, NEG)
    m1 = jnp.max(e1, axis=1, keepdims=True)
    i1 = jnp.min(jnp.where(e1 == m1, el, big), axis=1, keepdims=True)
    e2 = jnp.where(emask & (el != i1), logits, NEG)
    m2 = jnp.max(e2, axis=1, keepdims=True)
    i2 = jnp.min(jnp.where(e2 == m2, el, big), axis=1, keepdims=True)
    tt = jnp.exp(m2 - m1)
    p1 = 1.0 / (1.0 + tt)
    w1 = gw * p1
    w2 = gw * (tt * p1)
    oh1 = el == i1
    oh2 = el == i2
    cnt = (oh1 | oh2).astype(F32)
    r_i = lax.broadcasted_iota(I32, (tm, tm), 0)
    c_i = lax.broadcasted_iota(I32, (tm, tm), 1)
    before = _dot((r_i > c_i).astype(BF16), cnt.astype(BF16)) + carry_s[...]
    rank1 = jnp.sum(jnp.where(oh1, before, 0.0), axis=1, keepdims=True)
    rank2 = jnp.sum(jnp.where(oh2, before, 0.0), axis=1, keepdims=True)
    carry_s[...] = carry_s[...] + jnp.sum(cnt, axis=0, keepdims=True)
    meta = jnp.zeros_like(logits)
    for pos, val in enumerate((i1, i2, w1, w2, rank1, rank2)):
        meta = jnp.where(lane_i == pos, val, meta)
    meta_ref[...] = meta
    meta_t_ref[...] = meta.T[0:meta_t_ref.shape[0], :]
    cnt_ref[...] = jnp.broadcast_to(carry_s[...], cnt_ref.shape)


def _router(x, nw, sc, sh, wr, *, rows_per_batch):
    t, d = x.shape
    tm = TOKEN_TILE
    tiles_per_batch = rows_per_batch // tm
    return pl.pallas_call(
        _router_kernel,
        out_shape=(jax.ShapeDtypeStruct((t * TOK_SUB, LANES), U32),
                   jax.ShapeDtypeStruct((t, LANES), F32),
                   jax.ShapeDtypeStruct((8, t), F32),
                   jax.ShapeDtypeStruct((8, LANES), F32)),
        grid=(t // tm,),
        in_specs=[pl.BlockSpec((tm, d), lambda i: (i, 0)),
                  pl.BlockSpec((1, d), lambda i: (0, 0)),
                  pl.BlockSpec((1, 1, d), lambda i: (i // tiles_per_batch, 0, 0)),
                  pl.BlockSpec((1, 1, d), lambda i: (i // tiles_per_batch, 0, 0)),
                  pl.BlockSpec((d, LANES), lambda i: (0, 0))],
        out_specs=(pl.BlockSpec((tm * TOK_SUB, LANES), lambda i: (i, 0)),
                   pl.BlockSpec((tm, LANES), lambda i: (i, 0)),
                   pl.BlockSpec((8, tm), lambda i: (0, i)),
                   pl.BlockSpec((8, LANES), lambda i: (0, 0))),
        scratch_shapes=[pltpu.VMEM((1, LANES), F32), pltpu.VMEM((d, LANES), BF16),
                        pltpu.VMEM((d, LANES), BF16)],
        compiler_params=_cparams(("arbitrary",)),
        name="moe_router",
    )(x, nw, sc, sh, wr)


def _slot_kernel(pstart_ref, mt_ref, o_ref):
    eid = mt_ref[0:2, :]
    start = jnp.zeros(eid.shape, I32)
    for e in range(N_EXPERTS):
        start = jnp.where(eid == float(e), pstart_ref[e], start)
    o_ref[...] = start + mt_ref[4:6, :].astype(I32)


def _slots(pstart, meta_t):
    t = meta_t.shape[1]
    return pl.pallas_call(
        _slot_kernel,
        out_shape=jax.ShapeDtypeStruct((2, t), I32),
        grid_spec=pltpu.PrefetchScalarGridSpec(
            num_scalar_prefetch=1,
            grid=(1,),
            in_specs=[pl.BlockSpec(meta_t.shape, lambda i, ps: (0, 0))],
            out_specs=pl.BlockSpec((2, t), lambda i, ps: (0, 0))),
        compiler_params=_cparams(("arbitrary",)),
        name="moe_slots",
    )(pstart, meta_t)


def _token_tile(ref, r):
    return ref.at[pl.ds(pl.multiple_of(r * TOK_SUB, TOK_SUB), TOK_SUB)]


def _dispatch_kernel(dest_ref, zflag_ref, h_ref, xs_ref, zbuf, sem, zsem):
    tm = h_ref.shape[0] // TOK_SUB
    base = pl.program_id(0) * tm
    ntok = pl.num_programs(0) * tm
    tb = zbuf.shape[0]

    @pl.when(pl.program_id(0) == 0)
    def _():
        zbuf[...] = jnp.zeros_like(zbuf)

        def zcopy(b):
            return pltpu.make_async_copy(zbuf, xs_ref.at[pl.ds(pl.multiple_of(b * tb, tb), tb)], zsem)

        def zstart(b, carry):
            @pl.when(zflag_ref[b] == 1)
            def _():
                zcopy(b).start()
            return carry

        def zwait(b, carry):
            @pl.when(zflag_ref[b] == 1)
            def _():
                zcopy(b).wait()
            return carry

        nblk = xs_ref.shape[0] // tb
        lax.fori_loop(0, nblk, zstart, 0)
        lax.fori_loop(0, nblk, zwait, 0)

    def copy(t, k):
        return pltpu.make_async_copy(_token_tile(h_ref, t),
                                     _token_tile(xs_ref, dest_ref[k * ntok + base + t]), sem)

    def issue(t, carry):
        copy(t, 0).start(priority=0)
        copy(t, 1).start(priority=1)
        return carry

    def drain(t, carry):
        copy(t, 0).wait()
        copy(t, 1).wait()
        return carry

    lax.fori_loop(0, tm, issue, 0, unroll=DMA_UNROLL)
    lax.fori_loop(0, tm, drain, 0, unroll=DMA_UNROLL)


def _dispatch(dest, zflag, h, n_slots):
    t = h.shape[0] // TOK_SUB
    tm = TOKEN_TILE
    return pl.pallas_call(
        _dispatch_kernel,
        out_shape=jax.ShapeDtypeStruct((n_slots * TOK_SUB, LANES), U32),
        grid_spec=pltpu.PrefetchScalarGridSpec(
            num_scalar_prefetch=2,
            grid=(t // tm,),
            in_specs=[pl.BlockSpec((tm * TOK_SUB, LANES), lambda i, dest, zf: (i, 0))],
            out_specs=pl.BlockSpec(memory_space=pl.ANY),
            scratch_shapes=[pltpu.VMEM((MOE_TB * TOK_SUB, LANES), U32), pltpu.SemaphoreType.DMA(()),
                            pltpu.SemaphoreType.DMA(())]),
        compiler_params=_cparams(("arbitrary",)),
        name="moe_dispatch",
    )(dest, zflag, h)


def _ffn_kernel(be_ref, nxt_ref, slot_ref, nu_ref, xs_ref, w1_hbm, w3_hbm, w2_hbm, o_ref,
                wb1, wb3, wb2, w1_s, w3_s, w2_s, sem, *, layer):
    i = pl.program_id(0)

    def fetch(e, s):
        return (pltpu.make_async_copy(w1_hbm.at[layer, e], wb1.at[s], sem.at[s, 0]),
                pltpu.make_async_copy(w3_hbm.at[layer, e], wb3.at[s], sem.at[s, 1]),
                pltpu.make_async_copy(w2_hbm.at[layer, e], wb2.at[s], sem.at[s, 2]))

    @pl.when(i < nu_ref[0])
    def _():
        e = be_ref[i]
        s = slot_ref[i]

        @pl.when((i == 0) | (e != be_ref[jnp.maximum(i - 1, 0)]))
        def _():
            @pl.when(i == 0)
            def _():
                for cp in fetch(e, s):
                    cp.start()

            for cp in fetch(e, s):
                cp.wait()

            @pl.when(nxt_ref[i] >= 0)
            def _():
                for cp in fetch(nxt_ref[i], 1 - s):
                    cp.start()

            w1_s[...] = wb1[s].astype(BF16)
            w3_s[...] = wb3[s].astype(BF16)
            w2_s[...] = wb2[s].astype(BF16)

        lo, hi = _unpack_halves(_load_token_tiles(xs_ref))
        xl, xh = lo.astype(BF16), hi.astype(BF16)
        half = xl.shape[1]
        a = _dot(xl, w1_s[0:half, :]) + _dot(xh, w1_s[half:2 * half, :])
        b = _dot(xl, w3_s[0:half, :]) + _dot(xh, w3_s[half:2 * half, :])
        hmid = _silu(a) * b
        _store_token_tiles(o_ref, _pack_halves(_dot(hmid.astype(BF16), w2_s[...]).astype(BF16)))

    @pl.when(i >= nu_ref[0])
    def _():
        o_ref[...] = jnp.zeros_like(o_ref)


def _expert_ffn(blk_expert, blk_next, blk_slot, n_used, xs, w1, w3, w2, layer):
    d, f = w1.shape[2], w1.shape[3]
    rows = MOE_TB * TOK_SUB
    nb = xs.shape[0] // rows
    hbm = pl.BlockSpec(memory_space=pl.ANY)
    return pl.pallas_call(
        functools.partial(_ffn_kernel, layer=layer),
        out_shape=jax.ShapeDtypeStruct(xs.shape, U32),
        grid_spec=pltpu.PrefetchScalarGridSpec(
            num_scalar_prefetch=4,
            grid=(nb,),
            in_specs=[pl.BlockSpec((rows, LANES),
                                   lambda i, be, nx, sl, nu: (jnp.minimum(i, nu[0] - 1), 0)),
                      hbm, hbm, hbm],
            out_specs=pl.BlockSpec((rows, LANES), lambda i, be, nx, sl, nu: (i, 0)),
            scratch_shapes=[pltpu.VMEM((2, d, f), F32), pltpu.VMEM((2, d, f), F32),
                            pltpu.VMEM((2, f, d), F32),
                            pltpu.VMEM((d, f), BF16), pltpu.VMEM((d, f), BF16),
                            pltpu.VMEM((f, d), BF16),
                            pltpu.SemaphoreType.DMA((2, 3))]),
        compiler_params=_cparams(("arbitrary",)),
        name="moe_expert_ffn",
    )(blk_expert, blk_next, blk_slot, n_used, xs, w1, w3, w2)


def _combine_kernel(dest_ref, x_ref, meta_ref, g_ref, fnw_ref, ys_ref, o_ref, buf, sem, *, final):
    tm = x_ref.shape[0]
    i = pl.program_id(0)
    nsteps = pl.num_programs(0)
    ntok = nsteps * tm

    def copy(tile, t, k):
        half = tile % 2
        return pltpu.make_async_copy(_token_tile(ys_ref, dest_ref[k * ntok + tile * tm + t]),
                                     _token_tile(buf.at[half, k], t), sem.at[half])

    def issue(tile):
        def body(t, carry):
            copy(tile, t, 0).start(priority=0)
            copy(tile, t, 1).start(priority=1)
            return carry
        lax.fori_loop(0, tm, body, 0, unroll=DMA_UNROLL)

    def drain(tile):
        def body(t, carry):
            copy(tile, t, 0).wait()
            copy(tile, t, 1).wait()
            return carry
        lax.fori_loop(0, tm, body, 0, unroll=DMA_UNROLL)

    @pl.when(i == 0)
    def _():
        issue(i)

    @pl.when(i + 1 < nsteps)
    def _():
        issue(i + 1)

    drain(i)
    cur = i % 2
    meta = meta_ref[...]
    w1, w2 = meta[:, 2:3], meta[:, 3:4]
    lo1, hi1 = _unpack_halves(_load_token_tiles(buf.at[cur, 0]))
    lo2, hi2 = _unpack_halves(_load_token_tiles(buf.at[cur, 1]))
    y = jnp.concatenate([w1 * lo1 + w2 * lo2, w1 * hi1 + w2 * hi2], axis=1)
    xn = x_ref[...] + g_ref[0] * y
    if final:
        ms = jnp.mean(xn * xn, axis=-1, keepdims=True)
        xn = xn * lax.rsqrt(ms + EPS) * fnw_ref[...]
    o_ref[...] = xn


def _combine(dest, x, meta, g, fnw, ys, *, rows_per_batch, final):
    t, d = x.shape
    tm = TOKEN_TILE
    tiles_per_batch = rows_per_batch // tm
    return pl.pallas_call(
        functools.partial(_combine_kernel, final=final),
        out_shape=jax.ShapeDtypeStruct((t, d), F32),
        grid_spec=pltpu.PrefetchScalarGridSpec(
            num_scalar_prefetch=1,
            grid=(t // tm,),
            in_specs=[pl.BlockSpec((tm, d), lambda i, dest: (i, 0)),
                      pl.BlockSpec((tm, LANES), lambda i, dest: (i, 0)),
                      pl.BlockSpec((1, 1, d), lambda i, dest: (i // tiles_per_batch, 0, 0)),
                      pl.BlockSpec((1, d), lambda i, dest: (0, 0)),
                      pl.BlockSpec(memory_space=pl.ANY)],
            out_specs=pl.BlockSpec((tm, d), lambda i, dest: (i, 0)),
            scratch_shapes=[pltpu.VMEM((2, 2, tm * TOK_SUB, LANES), U32),
                            pltpu.SemaphoreType.DMA((2,))]),
        compiler_params=_cparams(("arbitrary",)),
        name="moe_combine",
    )(dest, x, meta, g, fnw, ys)


def _hier_moe(x, mod_sc, mod_sh, mod_g, nw, w_group, w_expert, w1, w3, w2, layer, fnw, *,
              rows_per_batch, final):
    t, d = x.shape
    a = 2 * t
    tb = MOE_TB
    wr = jnp.concatenate([w_group, w_expert], axis=1)
    wr = jnp.pad(wr, ((0, 0), (0, LANES - wr.shape[1])))
    h, meta, meta_t, cnt = _router(x, nw, mod_sc, mod_sh, wr, rows_per_batch=rows_per_batch)
    ne = N_EXPERTS
    counts = cnt[0, MOE_GROUPS:MOE_GROUPS + ne].astype(I32)
    padded = ((counts + tb - 1) // tb) * tb
    pend = jnp.cumsum(padded)
    pstart = pend - padded
    dest = _slots(pstart, meta_t).reshape(a)
    nb = (a + ne * (tb - 1) + tb - 1) // tb
    n_used = (pend[-1] // tb).astype(I32)
    blk = jnp.arange(nb, dtype=I32)
    be = jnp.minimum(jnp.sum((pend[None, :] <= (blk * tb)[:, None]).astype(I32), axis=1), ne - 1)
    seg_last = jnp.any((pend[None, :] == ((blk + 1) * tb)[:, None]) & (padded[None, :] > 0), axis=1)
    zflag = (seg_last | (blk >= n_used)).astype(I32)
    nonempty = counts > 0
    seg = jnp.cumsum(nonempty.astype(I32)) - 1
    later = lax.cummin(jnp.where(nonempty, jnp.arange(ne, dtype=I32), ne), axis=0, reverse=True)
    nxt_e = jnp.concatenate([later[1:], jnp.full((1,), ne, I32)])
    nxt_e = jnp.where(nxt_e == ne, -1, nxt_e)
    xs = _dispatch(dest, zflag, h, nb * tb)
    ys = _expert_ffn(be, nxt_e[be], seg[be] % 2, n_used.reshape(1), xs, w1, w3, w2, layer)
    return _combine(dest, x, meta, mod_g, fnw, ys, rows_per_batch=rows_per_batch, final=final)


def kernel(x, c, ada_w, ada_b, norm_mix, norm_ffn, ssd_w_in, ssd_conv_w, ssd_conv_b, ssd_a_log,
           ssd_dt_bias, ssd_d, ssd_norm_w, ssd_w_out, na_w_qkv, na_rpb, na_w_o,
           moe_w_group, moe_w_expert, moe_w1, moe_w3, moe_w2, final_norm):
    batch, seq, d = x.shape
    depth = ada_w.shape[0]
    xt = x.reshape(batch * seq, d)
    c_pad = jnp.pad(c, ((0, 8 - batch), (0, 0)))
    mod = _ada(c_pad, ada_w, ada_b)[:, :batch]
    fnw = final_norm.reshape(1, d)
    for i in range(depth):
        sh1, sc1, g1, sh2, sc2, g2 = [mod[i, :, k * d:(k + 1) * d].reshape(batch, 1, d)
                                      for k in range(6)]
        j = i // 2
        nw = norm_mix[i].reshape(1, d)
        if i % 2 == 0:
            xt = _ssd_mixer(xt, sc1, sh1, g1, nw, ssd_w_in[j], ssd_conv_w[j], ssd_conv_b[j],
                            ssd_a_log[j], ssd_dt_bias[j], ssd_d[j], ssd_norm_w[j], ssd_w_out[j],
                            batch=batch, seq=seq)
        else:
            xt = _na_mixer(xt, sc1, sh1, g1, nw, na_w_qkv[j], na_rpb[j], na_w_o[j],
                           batch=batch, seq=seq)
        xt = _hier_moe(xt, sc2, sh2, g2, norm_ffn[i].reshape(1, d), moe_w_group[i], moe_w_expert[i],
                       moe_w1, moe_w3, moe_w2, i, fnw, rows_per_batch=seq,
                       final=(i == depth - 1))
    return xt.reshape(batch, seq, d)
```
